```python
import jax, jax.numpy as jnp
from jax import lax
import numpy as np

D_MODEL = 1024
BATCH = 8
SEQ = 2048
DEPTH = 4

HEAD_DIM = D_MODEL // 16
CONV_HEADS = 6
POOL_GROUPS = 4
SGU_HEADS = 6
CONV_W = CONV_HEADS * HEAD_DIM
POOL_W = POOL_GROUPS * HEAD_DIM
SGU_W = SGU_HEADS * HEAD_DIM
D_MIX = CONV_W + POOL_W + SGU_W
D_IN = 3 * CONV_W + POOL_W + 2 * SGU_W
CONV_K = 3
POOL_WINDOWS = (2, 4, 8, 16)
CHUNK = 128
N_EXPERTS = 32
N_GROUPS = 4
EXPERTS_PER_GROUP = N_EXPERTS // N_GROUPS
TOP_K = 2
D_FF = D_MODEL // 2
ROW_BLOCK = 128
ADA_CHUNKS = 6
DEEPNORM_ALPHA = (2 * DEPTH) ** 0.25
DEEPNORM_BETA = (8 * DEPTH) ** -0.25
LN_EPS = 1e-5

kernel_name = "hybrid_conv_pool_sgu_grouped_moe_deepnorm"


def layer_norm(x, g, b):
    xf = x.astype(jnp.float32)
    mu = xf.mean(-1, keepdims=True)
    var = jnp.square(xf - mu).mean(-1, keepdims=True)
    return ((xf - mu) * lax.rsqrt(var + LN_EPS) * g + b).astype(x.dtype)


def short_conv(z, w, b):
    s = z.shape[1]
    zp = jnp.pad(z, ((0, 0), (CONV_K - 1, 0), (0, 0)))
    return w[0] * zp[:, 0:s] + w[1] * zp[:, 1:s + 1] + w[2] * zp[:, 2:s + 2] + b


def pool_mixer(p, w, scale):
    bsz, s, _ = p.shape
    cs = jnp.cumsum(p.astype(jnp.float32), axis=1)
    cs = jnp.pad(cs, ((0, 0), (1, 0), (0, 0)))
    t = jnp.arange(s)
    outs = []
    for gi, win in enumerate(POOL_WINDOWS):
        cg = cs[..., gi * HEAD_DIM:(gi + 1) * HEAD_DIM]
        upper = cg[:, 1:]
        lower = jnp.pad(cg[:, :s - win + 1], ((0, 0), (win - 1, 0), (0, 0)))
        count = jnp.minimum(t + 1, win).astype(jnp.float32)[None, :, None]
        outs.append((upper - lower) / count)
    pooled = jnp.concatenate(outs, axis=-1).astype(p.dtype) - p
    mixed = jnp.einsum('bsgc,gcd->bsgd', pooled.reshape(bsz, s, POOL_GROUPS, HEAD_DIM), w)
    return mixed.reshape(bsz, s, POOL_W) * scale


def spatial_gating(u, v, ln_g, ln_b, w_s, b_s):
    bsz, s, _ = v.shape
    v = layer_norm(v, ln_g, ln_b)
    vh = v.reshape(bsz, s // CHUNK, CHUNK, SGU_HEADS, HEAD_DIM)
    mask = jnp.tril(jnp.ones((CHUNK, CHUNK), dtype=bool))
    ws = jnp.where(mask[None], w_s, jnp.zeros_like(w_s))
    mixed = jnp.einsum('hts,bnshd->bnthd', ws, vh) + b_s.T[None, None, :, :, None]
    return u * mixed.reshape(bsz, s, SGU_W)


def token_mixer(h, w_in, conv_w, conv_b, pool_w, pool_scale, sgu_ln_g, sgu_ln_b, sgu_w, sgu_b, w_out):
    z = h @ w_in
    splits = [CONV_W, 2 * CONV_W, 3 * CONV_W, 3 * CONV_W + POOL_W, 3 * CONV_W + POOL_W + SGU_W]
    gb, gc, xc, p, u, v = jnp.split(z, splits, axis=-1)
    y_conv = gb * short_conv(gc * xc, conv_w, conv_b)
    y_pool = pool_mixer(p, pool_w, pool_scale)
    y_sgu = spatial_gating(u, v, sgu_ln_g, sgu_ln_b, sgu_w, sgu_b)
    y = jnp.concatenate([y_conv, y_pool, y_sgu], axis=-1)
    return y @ w_out


def expert_swiglu(args):
    xb, wg, wu, wd = args
    return (jax.nn.silu(xb @ wg) * (xb @ wu)) @ wd


def grouped_moe(h, w_router, b_router, w_gate, w_up, w_down):
    bsz, s, d = h.shape
    hf = h.reshape(-1, d)
    n_tok = hf.shape[0]
    n_assign = n_tok * TOP_K
    logits = (hf @ w_router + b_router).astype(jnp.float32)
    probs = jax.nn.softmax(logits, axis=-1)
    grouped = probs.reshape(n_tok, N_GROUPS, EXPERTS_PER_GROUP)
    group_score = lax.top_k(grouped, TOP_K)[0].sum(-1)
    g_sel = jnp.argmax(group_score, axis=-1)
    in_group = jnp.take_along_axis(grouped, g_sel[:, None, None], axis=1)[:, 0]
    top_p, top_i = lax.top_k(in_group, TOP_K)
    expert_idx = g_sel[:, None] * EXPERTS_PER_GROUP + top_i
    weights = top_p / top_p.sum(-1, keepdims=True)
    flat_e = expert_idx.reshape(-1).astype(jnp.int32)
    order = jnp.argsort(flat_e)
    sorted_e = flat_e[order]
    tok = order // TOP_K
    sizes = jnp.bincount(flat_e, length=N_EXPERTS).astype(jnp.int32)
    padded = ((sizes + ROW_BLOCK - 1) // ROW_BLOCK) * ROW_BLOCK
    pad_end = jnp.cumsum(padded)
    pad_start = pad_end - padded
    start = jnp.cumsum(sizes) - sizes
    rank = jnp.arange(n_assign, dtype=jnp.int32) - start[sorted_e]
    dest = pad_start[sorted_e] + rank
    n_blocks = n_assign // ROW_BLOCK + N_EXPERTS
    buf = jnp.zeros((n_blocks * ROW_BLOCK, d), h.dtype).at[dest].set(hf[tok])
    block_start = jnp.arange(n_blocks, dtype=jnp.int32) * ROW_BLOCK
    block_e = jnp.minimum(jnp.searchsorted(pad_end, block_start, side='right'), N_EXPERTS - 1)
    xb = buf.reshape(n_blocks, ROW_BLOCK, d)
    yb = lax.map(lambda a: expert_swiglu((a[0], w_gate[a[1]], w_up[a[1]], w_down[a[1]])), (xb, block_e))
    ys = yb.reshape(n_blocks * ROW_BLOCK, d)[dest]
    w_sorted = weights.reshape(-1)[order].astype(h.dtype)
    out = jnp.zeros_like(hf).at[tok].add(ys * w_sorted[:, None])
    return out.reshape(bsz, s, d)


def setup_inputs(seed: int = 0) -> dict:
    key = jax.random.key(seed)
    ks = jax.random.split(key, 24)
    f32 = jnp.float32
    nrm = lambda k, shape, sc: jax.random.normal(k, shape, f32) * sc
    L, D = DEPTH, D_MODEL
    return {
        "x": nrm(ks[0], (BATCH, SEQ, D), 1.0),
        "c": nrm(ks[1], (BATCH, D), 1.0),
        "w_ada": nrm(ks[2], (L, D, ADA_CHUNKS * D), 0.5 * D ** -0.5),
        "b_ada": nrm(ks[3], (L, ADA_CHUNKS * D), 0.01),
        "w_in": nrm(ks[4], (L, D, D_IN), D ** -0.5),
        "conv_w": nrm(ks[5], (L, CONV_K, CONV_W), CONV_K ** -0.5),
        "conv_b": nrm(ks[6], (L, CONV_W), 0.01),
        "pool_w": nrm(ks[7], (L, POOL_GROUPS, HEAD_DIM, HEAD_DIM), HEAD_DIM ** -0.5),
        "pool_scale": 1.0 + nrm(ks[8], (L, POOL_W), 0.1),
        "sgu_ln_g": 1.0 + nrm(ks[9], (L, SGU_W), 0.01),
        "sgu_ln_b": nrm(ks[10], (L, SGU_W), 0.01),
        "sgu_w": nrm(ks[11], (L, SGU_HEADS, CHUNK, CHUNK), CHUNK ** -0.5),
        "sgu_b": 1.0 + nrm(ks[12], (L, SGU_HEADS, CHUNK), 0.01),
        "w_out": nrm(ks[13], (L, D_MIX, D), DEEPNORM_BETA * D_MIX ** -0.5),
        "ln1_g": 1.0 + nrm(ks[14], (L, D), 0.01),
        "ln1_b": nrm(ks[15], (L, D), 0.01),
        "w_router": nrm(ks[16], (D, N_EXPERTS), D ** -0.5),
        "b_router": nrm(ks[17], (N_EXPERTS,), 0.01),
        "w_gate": nrm(ks[18], (L, N_EXPERTS, D, D_FF), D ** -0.5),
        "w_up": nrm(ks[19], (L, N_EXPERTS, D, D_FF), D ** -0.5),
        "w_down": nrm(ks[20], (L, N_EXPERTS, D_FF, D), DEEPNORM_BETA * D_FF ** -0.5),
        "ln2_g": 1.0 + nrm(ks[21], (L, D), 0.01),
        "ln2_b": nrm(ks[22], (L, D), 0.01),
    }


def reference(x, c, w_ada, b_ada, w_in, conv_w, conv_b, pool_w, pool_scale, sgu_ln_g, sgu_ln_b,
              sgu_w, sgu_b, w_out, ln1_g, ln1_b, w_router, b_router, w_gate, w_up, w_down,
              ln2_g, ln2_b):
    c_act = jax.nn.silu(c)
    for l in range(DEPTH):
        ada = c_act @ w_ada[l] + b_ada[l]
        sh1, sc1, g1, sh2, sc2, g2 = [a[:, None, :] for a in jnp.split(ada, ADA_CHUNKS, axis=-1)]
        h = x * (1.0 + sc1) + sh1
        y = token_mixer(h, w_in[l], conv_w[l], conv_b[l], pool_w[l], pool_scale[l],
                        sgu_ln_g[l], sgu_ln_b[l], sgu_w[l], sgu_b[l], w_out[l])
        x = layer_norm(DEEPNORM_ALPHA * x + g1 * y, ln1_g[l], ln1_b[l])
        h = x * (1.0 + sc2) + sh2
        y = grouped_moe(h, w_router, b_router, w_gate[l], w_up[l], w_down[l])
        x = layer_norm(DEEPNORM_ALPHA * x + g2 * y, ln2_g[l], ln2_b[l])
    return x
```

```python
import functools

import jax
import jax.numpy as jnp
from jax import lax
from jax.experimental import pallas as pl
from jax.experimental.pallas import tpu as pltpu

D_MODEL = 1024
HEAD_DIM = D_MODEL // 16
CONV_W = 6 * HEAD_DIM
POOL_W = 4 * HEAD_DIM
SGU_W = 6 * HEAD_DIM
D_IN = 3 * CONV_W + POOL_W + 2 * SGU_W
CONV_K = 3
POOL_WINDOWS = (2, 4, 8, 16)
CHUNK = 128
N_EXPERTS = 32
N_GROUPS = 4
EXPERTS_PER_GROUP = N_EXPERTS // N_GROUPS
TOP_K = 2
D_FF = D_MODEL // 2
ADA_CHUNKS = 6
LN_EPS = 1e-5

OFF_GB = 0
OFF_GC = CONV_W
OFF_P = 3 * CONV_W
OFF_U = OFF_P + POOL_W
OFF_V = OFF_U + SGU_W
YOFF_POOL = CONV_W
YOFF_SGU = CONV_W + POOL_W

LANES = 128
CONV_HALO = 8
POOL_HALO = 16
SEQ_TILE = 512
ROW_BLOCK = 256
TOK_TILE = 512
ADA_TILE = 1536
VMEM_LIMIT = 56 * 1024 * 1024

F32 = jnp.float32
BF16 = jnp.bfloat16


def _layer_norm(r, g, b):
    mu = jnp.mean(r, axis=-1, keepdims=True)
    d = r - mu
    var = jnp.mean(d * d, axis=-1, keepdims=True)
    return d * lax.rsqrt(var + LN_EPS) * g + b


def _ada_kernel(c_ref, w_ref, b_ref, o_ref):
    c = c_ref[...]
    c_act = (c * jax.nn.sigmoid(c)).astype(BF16)
    o_ref[0] = jnp.dot(c_act, w_ref[0].astype(BF16), preferred_element_type=F32) + b_ref[0]


def _ada_call(c, w_ada, b_ada):
    depth, d, n = w_ada.shape
    bsz = c.shape[0]
    return pl.pallas_call(
        _ada_kernel,
        grid=(depth, n // ADA_TILE),
        in_specs=[
            pl.BlockSpec((bsz, d), lambda l, j: (0, 0)),
            pl.BlockSpec((1, d, ADA_TILE), lambda l, j: (l, 0, j)),
            pl.BlockSpec((1, 1, ADA_TILE), lambda l, j: (l, 0, j)),
        ],
        out_specs=pl.BlockSpec((1, bsz, ADA_TILE), lambda l, j: (l, 0, j)),
        out_shape=jax.ShapeDtypeStruct((depth, bsz, n), F32),
        compiler_params=pltpu.CompilerParams(
            dimension_semantics=("arbitrary", "arbitrary"), vmem_limit_bytes=VMEM_LIMIT),
        name="ada",
    )(c, w_ada, b_ada.reshape(depth, 1, n))


def _top2_of_group(pg, sub_iota):
    big = float(EXPERTS_PER_GROUP)
    m1 = jnp.max(pg, axis=0, keepdims=True)
    i1 = jnp.min(jnp.where(pg == m1, sub_iota, big), axis=0, keepdims=True)
    rest = jnp.where(sub_iota == i1, -1.0, pg)
    m2 = jnp.max(rest, axis=0, keepdims=True)
    i2 = jnp.min(jnp.where(rest == m2, sub_iota, big), axis=0, keepdims=True)
    return m1, i1, m2, i2


def _mixer_kernel(x_ref, ada_ref, win_ref, wout_ref, convw_ref, convb_ref, poolw_ref, pscale_ref,
                  lng_ref, lnb_ref, sguw_ref, sgub_ref, ln1g_ref, ln1b_ref, wr_ref, br_ref,
                  x1_ref, h2_ref, e_ref, wt_ref, rank_ref, cnt_ref,
                  gx_scr, p_scr, y_scr, cnt_scr, *, ts, alpha):
    b = pl.program_id(0)
    j = pl.program_id(1)

    @pl.when(j == 0)
    def _():
        gx_scr[0:CONV_HALO, :] = jnp.zeros((CONV_HALO, CONV_W), F32)
        p_scr[0:POOL_HALO, :] = jnp.zeros((POOL_HALO, POOL_W), F32)

    @pl.when((b == 0) & (j == 0))
    def _():
        cnt_scr[...] = jnp.zeros_like(cnt_scr)

    x = x_ref[0]
    ada = ada_ref[0]
    sh1, sc1, g1 = ada[0:1], ada[1:2], ada[2:3]
    sh2, sc2, g2 = ada[3:4], ada[4:5], ada[5:6]
    hb = (x * (1.0 + sc1) + sh1).astype(BF16)

    def proj(lo, hi):
        return jnp.dot(hb, win_ref[:, lo:hi], preferred_element_type=F32)

    gcxc = proj(OFF_GC, OFF_P)
    g = gcxc[:, :CONV_W] * gcxc[:, CONV_W:]
    gx_scr[CONV_HALO:CONV_HALO + ts, :] = g
    cw = convw_ref[...]
    conv = (cw[0:1] * gx_scr[CONV_HALO - 2:CONV_HALO - 2 + ts, :]
            + cw[1:2] * gx_scr[CONV_HALO - 1:CONV_HALO - 1 + ts, :]
            + cw[2:3] * g + convb_ref[...])
    gx_scr[0:CONV_HALO, :] = g[ts - CONV_HALO:ts, :]
    y_scr[:, 0:CONV_W] = (proj(OFF_GB, OFF_GC) * conv).astype(BF16)

    p = proj(OFF_P, OFF_U)
    p_scr[POOL_HALO:POOL_HALO + ts, :] = p

    def shifted(k, lo):
        return p_scr[POOL_HALO - k:POOL_HALO - k + ts, lo:lo + LANES]

    s2 = p[:, 0:LANES] + shifted(1, 0)
    s4 = s2 + shifted(2, 0) + shifted(3, 0)
    acc = p[:, LANES:2 * LANES]
    for k in range(1, 8):
        acc = acc + shifted(k, LANES)
    s8 = acc
    for k in range(8, 16):
        acc = acc + shifted(k, LANES)
    s16 = acc
    p_scr[0:POOL_HALO, :] = p[ts - POOL_HALO:ts, :]
    lane = lax.broadcasted_iota(jnp.int32, (ts, LANES), 1)
    tpos = (lax.broadcasted_iota(jnp.int32, (ts, LANES), 0) + (j * ts + 1)).astype(F32)
    lo_half = lane < HEAD_DIM
    cnt_a = jnp.minimum(tpos, jnp.where(lo_half, float(POOL_WINDOWS[0]), float(POOL_WINDOWS[1])))
    cnt_b = jnp.minimum(tpos, jnp.where(lo_half, float(POOL_WINDOWS[2]), float(POOL_WINDOWS[3])))
    pooled = jnp.concatenate([jnp.where(lo_half, s2, s4) / cnt_a,
                              jnp.where(lo_half, s8, s16) / cnt_b], axis=1) - p
    mixed = jnp.dot(pooled.astype(BF16), poolw_ref[...], preferred_element_type=F32)
    y_scr[:, YOFF_POOL:YOFF_POOL + POOL_W] = (mixed * pscale_ref[...]).astype(BF16)

    v = proj(OFF_V, D_IN)
    vnb = _layer_norm(v, lng_ref[...], lnb_ref[...]).astype(BF16)
    u = proj(OFF_U, OFF_V)
    row_c = lax.broadcasted_iota(jnp.int32, (CHUNK, 2 * CHUNK), 0)
    col_c = lax.broadcasted_iota(jnp.int32, (CHUNK, 2 * CHUNK), 1)
    causal = (col_c & (CHUNK - 1)) <= row_c
    lo_lanes = lax.broadcasted_iota(jnp.int32, (CHUNK, LANES), 1) < HEAD_DIM
    zero_b = jnp.zeros((CHUNK, LANES), BF16)
    for hp in range(SGU_W // LANES):
        wl = jnp.where(causal, sguw_ref[hp], 0.0).astype(BF16)
        bias = sgub_ref[hp]
        for ci in range(ts // CHUNK):
            rs = slice(ci * CHUNK, (ci + 1) * CHUNK)
            vc = vnb[rs, hp * LANES:(hp + 1) * LANES]
            rhs = jnp.concatenate([jnp.where(lo_lanes, vc, zero_b),
                                   jnp.where(lo_lanes, zero_b, vc)], axis=0)
            mixed_c = jnp.dot(wl, rhs, preferred_element_type=F32) + bias
            y_scr[rs, YOFF_SGU + hp * LANES:YOFF_SGU + (hp + 1) * LANES] = (
                u[rs, hp * LANES:(hp + 1) * LANES] * mixed_c).astype(BF16)

    yo = jnp.dot(y_scr[...], wout_ref[...], preferred_element_type=F32)
    x1 = _layer_norm(alpha * x + g1 * yo, ln1g_ref[...], ln1b_ref[...])
    x1_ref[0] = x1
    h2 = x1 * (1.0 + sc2) + sh2
    h2_ref[0] = h2

    logits = jnp.dot(h2.astype(BF16), wr_ref[...], preferred_element_type=F32) + br_ref[...]
    lt = logits.T[0:N_EXPERTS, :]
    ex = jnp.exp(lt - jnp.max(lt, axis=0, keepdims=True))
    probs = ex / jnp.sum(ex, axis=0, keepdims=True)
    sub_iota = lax.broadcasted_iota(jnp.int32, (EXPERTS_PER_GROUP, ts), 0).astype(F32)
    best = None
    for gi in range(N_GROUPS):
        m1, i1, m2, i2 = _top2_of_group(
            probs[gi * EXPERTS_PER_GROUP:(gi + 1) * EXPERTS_PER_GROUP, :], sub_iota)
        score = m1 + m2
        cand = (score, m1, m2, i1 + float(gi * EXPERTS_PER_GROUP), i2 + float(gi * EXPERTS_PER_GROUP))
        if best is None:
            best = cand
        else:
            better = score > best[0]
            best = tuple(jnp.where(better, c, o) for c, o in zip(cand, best))
    _, p1, p2, e1, e2 = best
    den = p1 + p2
    wt_ref[0] = jnp.concatenate([p1 / den, p2 / den], axis=0)
    e_ref[0] = jnp.concatenate([e1, e2], axis=0).astype(jnp.int32)

    ex_iota = lax.broadcasted_iota(jnp.int32, (N_EXPERTS, ts), 0).astype(F32)
    oh1 = ex_iota == e1
    oh2 = ex_iota == e2
    either = jnp.where(oh1 | oh2, 1.0, 0.0)
    before = (lax.broadcasted_iota(jnp.int32, (ts, ts), 0)
              < lax.broadcasted_iota(jnp.int32, (ts, ts), 1))
    tri = jnp.where(before, 1.0, 0.0).astype(BF16)
    seen = jnp.dot(either.astype(BF16), tri, preferred_element_type=F32) + cnt_scr[:, 0:1]
    r1 = jnp.sum(jnp.where(oh1, seen, 0.0), axis=0, keepdims=True)
    r2 = jnp.sum(jnp.where(oh2, seen, 0.0), axis=0, keepdims=True)
    rank_ref[0] = jnp.concatenate([r1, r2], axis=0).astype(jnp.int32)
    cnt_new = cnt_scr[...] + jnp.sum(either, axis=1, keepdims=True)
    cnt_scr[...] = cnt_new
    cnt_ref[...] = cnt_new


def _mixer_call(x, ada_l, w_in, w_out, conv_w, conv_b, poolw_bd, pool_scale, ln_g, ln_b,
                sguw_pair, sgub_pair, ln1_g, ln1_b, wr_pad, br_pad, alpha):
    bsz, s, d = x.shape
    ts = SEQ_TILE
    nt = s // ts
    n_tiles = bsz * nt
    const2 = lambda b, j: (0, 0)
    const3 = lambda b, j: (0, 0, 0)
    tile3 = lambda b, j: (b, j, 0)
    meta3 = lambda b, j: (b * nt + j, 0, 0)
    kern = functools.partial(_mixer_kernel, ts=ts, alpha=alpha)
    return pl.pallas_call(
        kern,
        grid=(bsz, nt),
        in_specs=[
            pl.BlockSpec((1, ts, d), tile3),
            pl.BlockSpec((1, ADA_CHUNKS, d), lambda b, j: (b, 0, 0)),
            pl.BlockSpec((d, D_IN), const2),
            pl.BlockSpec((d, d), const2),
            pl.BlockSpec((CONV_K, CONV_W), const2),
            pl.BlockSpec((1, CONV_W), const2),
            pl.BlockSpec((POOL_W, POOL_W), const2),
            pl.BlockSpec((1, POOL_W), const2),
            pl.BlockSpec((1, SGU_W), const2),
            pl.BlockSpec((1, SGU_W), const2),
            pl.BlockSpec((SGU_W // LANES, CHUNK, 2 * CHUNK), const3),
            pl.BlockSpec((SGU_W // LANES, CHUNK, LANES), const3),
            pl.BlockSpec((1, d), const2),
            pl.BlockSpec((1, d), const2),
            pl.BlockSpec((d, LANES), const2),
            pl.BlockSpec((1, LANES), const2),
        ],
        out_specs=[
            pl.BlockSpec((1, ts, d), tile3),
            pl.BlockSpec((1, ts, d), tile3),
            pl.BlockSpec((1, TOP_K, ts), meta3),
            pl.BlockSpec((1, TOP_K, ts), meta3),
            pl.BlockSpec((1, TOP_K, ts), meta3),
            pl.BlockSpec((N_EXPERTS, LANES), const2),
        ],
        out_shape=[
            jax.ShapeDtypeStruct((bsz, s, d), F32),
            jax.ShapeDtypeStruct((bsz, s, d), F32),
            jax.ShapeDtypeStruct((n_tiles, TOP_K, ts), jnp.int32),
            jax.ShapeDtypeStruct((n_tiles, TOP_K, ts), F32),
            jax.ShapeDtypeStruct((n_tiles, TOP_K, ts), jnp.int32),
            jax.ShapeDtypeStruct((N_EXPERTS, LANES), F32),
        ],
        scratch_shapes=[
            pltpu.VMEM((CONV_HALO + ts, CONV_W), F32),
            pltpu.VMEM((POOL_HALO + ts, POOL_W), F32),
            pltpu.VMEM((ts, d), BF16),
            pltpu.VMEM((N_EXPERTS, LANES), F32),
        ],
        compiler_params=pltpu.CompilerParams(
            dimension_semantics=("arbitrary", "arbitrary"), vmem_limit_bytes=VMEM_LIMIT),
        name="mixer_router",
    )(x, ada_l, w_in, w_out, conv_w, conv_b, poolw_bd, pool_scale, ln_g, ln_b,
      sguw_pair, sgub_pair, ln1_g, ln1_b, wr_pad, br_pad)


def _row_copy(src_ref, src_row, dst_ref, dst_row, sem):
    return pltpu.make_async_copy(src_ref.at[pl.ds(src_row, 1), :], dst_ref.at[pl.ds(dst_row, 1), :], sem)


def _dispatch_kernel(seg_ref, e_ref, rank_ref, h_ref, xs_in_ref, xs_ref, sem, *, tt):
    del xs_in_ref

    def issue(t, carry):
        for k in range(TOP_K):
            dst = seg_ref[e_ref[0, k, t]] + rank_ref[0, k, t]
            _row_copy(h_ref, t, xs_ref, dst, sem).start()
        return carry

    lax.fori_loop(0, tt, issue, 0, unroll=8)
    for k in range(TOP_K):
        pltpu.make_async_copy(h_ref, xs_ref.at[pl.ds(0, tt), :], sem).wait()


def _dispatch_call(seg_start, e_idx, rank, h2_flat, xs_zero):
    n_tok, d = h2_flat.shape
    tt = TOK_TILE
    return pl.pallas_call(
        functools.partial(_dispatch_kernel, tt=tt),
        grid_spec=pltpu.PrefetchScalarGridSpec(
            num_scalar_prefetch=1,
            grid=(n_tok // tt,),
            in_specs=[
                pl.BlockSpec((1, TOP_K, tt), lambda i, seg: (i, 0, 0), memory_space=pltpu.SMEM),
                pl.BlockSpec((1, TOP_K, tt), lambda i, seg: (i, 0, 0), memory_space=pltpu.SMEM),
                pl.BlockSpec((tt, d), lambda i, seg: (i, 0)),
                pl.BlockSpec(memory_space=pl.ANY),
            ],
            out_specs=pl.BlockSpec(memory_space=pl.ANY),
            scratch_shapes=[pltpu.SemaphoreType.DMA(())],
        ),
        out_shape=jax.ShapeDtypeStruct(xs_zero.shape, xs_zero.dtype),
        input_output_aliases={4: 0},
        compiler_params=pltpu.CompilerParams(
            dimension_semantics=("arbitrary",), vmem_limit_bytes=VMEM_LIMIT),
        name="dispatch",
    )(seg_start, e_idx, rank, h2_flat, xs_zero)


def _ffn_kernel(be_ref, nb_ref, xs_ref, wg_ref, wu_ref, wd_ref, ys_ref, wg_scr, wu_scr, wd_scr):
    i = pl.program_id(0)
    prev = be_ref[jnp.maximum(i - 1, 0)]
    active = i < nb_ref[0]

    @pl.when(active & ((i == 0) | (be_ref[i] != prev)))
    def _():
        wg_scr[...] = wg_ref[0].astype(BF16)
        wu_scr[...] = wu_ref[0].astype(BF16)
        wd_scr[...] = wd_ref[0].astype(BF16)

    @pl.when(active)
    def _():
        xb = xs_ref[...].astype(BF16)
        gate = jnp.dot(xb, wg_scr[...], preferred_element_type=F32)
        up = jnp.dot(xb, wu_scr[...], preferred_element_type=F32)
        act = (gate * jax.nn.sigmoid(gate) * up).astype(BF16)
        ys_ref[...] = jnp.dot(act, wd_scr[...], preferred_element_type=F32)

    @pl.when(jnp.logical_not(active))
    def _():
        ys_ref[...] = jnp.zeros_like(ys_ref)


def _ffn_call(block_e, n_used, xs, w_gate, w_up, w_down):
    n_rows, d = xs.shape
    n_blocks = n_rows // ROW_BLOCK
    f = w_gate.shape[-1]
    row_map = lambda i, be, nb: (i, 0)
    w_map = lambda i, be, nb: (be[i], 0, 0)
    return pl.pallas_call(
        _ffn_kernel,
        grid_spec=pltpu.PrefetchScalarGridSpec(
            num_scalar_prefetch=2,
            grid=(n_blocks,),
            in_specs=[
                pl.BlockSpec((ROW_BLOCK, d), row_map),
                pl.BlockSpec((1, d, f), w_map),
                pl.BlockSpec((1, d, f), w_map),
                pl.BlockSpec((1, f, d), w_map),
            ],
            out_specs=pl.BlockSpec((ROW_BLOCK, d), row_map),
            scratch_shapes=[
                pltpu.VMEM((d, f), BF16),
                pltpu.VMEM((d, f), BF16),
                pltpu.VMEM((f, d), BF16),
            ],
        ),
        out_shape=jax.ShapeDtypeStruct((n_rows, d), F32),
        compiler_params=pltpu.CompilerParams(
            dimension_semantics=("arbitrary",), vmem_limit_bytes=VMEM_LIMIT),
        name="expert_ffn",
    )(block_e, n_used, xs, w_gate, w_up, w_down)


def _combine_kernel(seg_ref, e_ref, rank_ref, ys_ref, wt_ref, x1_ref, ada_ref, g_ref, b_ref,
                    o_ref, buf, sem, *, tt, alpha):
    def issue(t, carry):
        for k in range(TOP_K):
            src = seg_ref[e_ref[0, k, t]] + rank_ref[0, k, t]
            _row_copy(ys_ref, src, buf.at[k], t, sem).start()
        return carry

    lax.fori_loop(0, tt, issue, 0, unroll=8)
    for k in range(TOP_K):
        pltpu.make_async_copy(ys_ref.at[pl.ds(0, tt), :], buf.at[k], sem).wait()

    wt = wt_ref[...]
    y = wt[:, 0:1] * buf[0] + wt[:, 1:2] * buf[1]
    g2 = ada_ref[0][5:6]
    o_ref[...] = _layer_norm(alpha * x1_ref[...] + g2 * y, g_ref[...], b_ref[...])


def _combine_call(seg_start, e_idx, rank, ys, wt_tok, x1_flat, ada_l, ln_g, ln_b, alpha, seq):
    n_tok, d = x1_flat.shape
    tt = TOK_TILE
    per_seq = seq // tt
    return pl.pallas_call(
        functools.partial(_combine_kernel, tt=tt, alpha=alpha),
        grid_spec=pltpu.PrefetchScalarGridSpec(
            num_scalar_prefetch=1,
            grid=(n_tok // tt,),
            in_specs=[
                pl.BlockSpec((1, TOP_K, tt), lambda i, seg: (i, 0, 0), memory_space=pltpu.SMEM),
                pl.BlockSpec((1, TOP_K, tt), lambda i, seg: (i, 0, 0), memory_space=pltpu.SMEM),
                pl.BlockSpec(memory_space=pl.ANY),
                pl.BlockSpec((tt, TOP_K), lambda i, seg: (i, 0)),
                pl.BlockSpec((tt, d), lambda i, seg: (i, 0)),
                pl.BlockSpec((1, ADA_CHUNKS, d), lambda i, seg: (i // per_seq, 0, 0)),
                pl.BlockSpec((1, d), lambda i, seg: (0, 0)),
                pl.BlockSpec((1, d), lambda i, seg: (0, 0)),
            ],
            out_specs=pl.BlockSpec((tt, d), lambda i, seg: (i, 0)),
            scratch_shapes=[
                pltpu.VMEM((TOP_K, tt, d), F32),
                pltpu.SemaphoreType.DMA(()),
            ],
        ),
        out_shape=jax.ShapeDtypeStruct((n_tok, d), F32),
        compiler_params=pltpu.CompilerParams(
            dimension_semantics=("arbitrary",), vmem_limit_bytes=VMEM_LIMIT),
        name="combine_ln",
    )(seg_start, e_idx, rank, ys, wt_tok, x1_flat, ada_l, ln_g, ln_b)


def kernel(x, c, w_ada, b_ada, w_in, conv_w, conv_b, pool_w, pool_scale, sgu_ln_g, sgu_ln_b,
           sgu_w, sgu_b, w_out, ln1_g, ln1_b, w_router, b_router, w_gate, w_up, w_down,
           ln2_g, ln2_b):
    bsz, seq, d = x.shape
    depth = w_ada.shape[0]
    n_tok = bsz * seq
    alpha = (2 * depth) ** 0.25
    assert d == D_MODEL and seq % SEQ_TILE == 0 and SEQ_TILE % CHUNK == 0
    assert SEQ_TILE == TOK_TILE and n_tok % TOK_TILE == 0

    ada = _ada_call(c, w_ada, b_ada).reshape(depth, bsz, ADA_CHUNKS, d)

    w_in_b = w_in.astype(BF16)
    w_out_b = w_out.astype(BF16)
    eye_g = jnp.eye(POOL_W // HEAD_DIM, dtype=F32)
    poolw_bd = jnp.einsum('lgcd,gh->lgchd', pool_w, eye_g).reshape(depth, POOL_W, POOL_W).astype(BF16)
    n_pairs = SGU_W // LANES
    sguw_pair = sgu_w.reshape(depth, n_pairs, 2, CHUNK, CHUNK).transpose(0, 1, 3, 2, 4).reshape(
        depth, n_pairs, CHUNK, 2 * CHUNK)
    sgub_pair = jnp.repeat(sgu_b.transpose(0, 2, 1), HEAD_DIM, axis=-1).reshape(
        depth, CHUNK, n_pairs, LANES).transpose(0, 2, 1, 3)
    wr_pad = jnp.pad(w_router, ((0, 0), (0, LANES - N_EXPERTS))).astype(BF16)
    br_pad = jnp.pad(b_router, (0, LANES - N_EXPERTS)).reshape(1, LANES)

    n_blocks = (n_tok * TOP_K) // ROW_BLOCK + N_EXPERTS
    n_rows = n_blocks * ROW_BLOCK
    block_start = jnp.arange(n_blocks, dtype=jnp.int32) * ROW_BLOCK

    for l in range(depth):
        x1, h2, e_idx, wts, rank, counts = _mixer_call(
            x, ada[l], w_in_b[l], w_out_b[l], conv_w[l], conv_b[l].reshape(1, -1), poolw_bd[l],
            pool_scale[l].reshape(1, -1), sgu_ln_g[l].reshape(1, -1), sgu_ln_b[l].reshape(1, -1),
            sguw_pair[l], sgub_pair[l], ln1_g[l].reshape(1, -1), ln1_b[l].reshape(1, -1),
            wr_pad, br_pad, alpha)
        sizes = counts[:, 0].astype(jnp.int32)
        padded = ((sizes + ROW_BLOCK - 1) // ROW_BLOCK) * ROW_BLOCK
        seg_end = jnp.cumsum(padded)
        seg_start = (seg_end - padded).astype(jnp.int32)
        block_e = jnp.minimum(
            jnp.sum(block_start[:, None] >= seg_end[None, :], axis=1), N_EXPERTS - 1).astype(jnp.int32)
        n_used = (seg_end[-1:] // ROW_BLOCK).astype(jnp.int32)
        block_e = jnp.where(block_start < seg_end[-1], block_e, block_e[jnp.maximum(n_used[0] - 1, 0)])

        xs = _dispatch_call(seg_start, e_idx, rank, h2.reshape(n_tok, d), jnp.zeros((n_rows, d), F32))
        ys = _ffn_call(block_e, n_used, xs, w_gate[l], w_up[l], w_down[l])
        wt_tok = wts.transpose(0, 2, 1).reshape(n_tok, TOP_K)
        x = _combine_call(seg_start, e_idx, rank, ys, wt_tok, x1.reshape(n_tok, d), ada[l],
                          ln2_g[l].reshape(1, -1), ln2_b[l].reshape(1, -1), alpha, seq).reshape(bsz, seq, d)
    return x
```

```python
import functools

import jax
import jax.numpy as jnp
from jax import lax
from jax.experimental import pallas as pl
from jax.experimental.pallas import tpu as pltpu

D_MODEL = 1024
HEAD_DIM = D_MODEL // 16
CONV_W = 6 * HEAD_DIM
POOL_W = 4 * HEAD_DIM
SGU_W = 6 * HEAD_DIM
D_IN = 3 * CONV_W + POOL_W + 2 * SGU_W
CONV_K = 3
POOL_WINDOWS = (2, 4, 8, 16)
CHUNK = 128
N_EXPERTS = 32
N_GROUPS = 4
EXPERTS_PER_GROUP = N_EXPERTS // N_GROUPS
TOP_K = 2
D_FF = D_MODEL // 2
ADA_CHUNKS = 6
LN_EPS = 1e-5

OFF_GB = 0
OFF_GC = CONV_W
OFF_P = 3 * CONV_W
OFF_U = OFF_P + POOL_W
OFF_V = OFF_U + SGU_W
YOFF_POOL = CONV_W
YOFF_SGU = CONV_W + POOL_W

LANES = 128
CONV_HALO = 8
POOL_HALO = 16
SEQ_TILE = 512
ROW_BLOCK = 256
TOK_TILE = 512
ADA_TILE = 1536
VMEM_LIMIT = 56 * 1024 * 1024

F32 = jnp.float32
BF16 = jnp.bfloat16


def _layer_norm(r, g, b):
    mu = jnp.mean(r, axis=-1, keepdims=True)
    d = r - mu
    var = jnp.mean(d * d, axis=-1, keepdims=True)
    return d * lax.rsqrt(var + LN_EPS) * g + b


def _ada_kernel(c_ref, w_ref, b_ref, o_ref):
    c = c_ref[...]
    c_act = (c * jax.nn.sigmoid(c)).astype(BF16)
    o_ref[0] = jnp.dot(c_act, w_ref[0].astype(BF16), preferred_element_type=F32) + b_ref[0]


def _ada_call(c, w_ada, b_ada):
    depth, d, n = w_ada.shape
    bsz = c.shape[0]
    return pl.pallas_call(
        _ada_kernel,
        grid=(depth, n // ADA_TILE),
        in_specs=[
            pl.BlockSpec((bsz, d), lambda l, j: (0, 0)),
            pl.BlockSpec((1, d, ADA_TILE), lambda l, j: (l, 0, j)),
            pl.BlockSpec((1, 1, ADA_TILE), lambda l, j: (l, 0, j)),
        ],
        out_specs=pl.BlockSpec((1, bsz, ADA_TILE), lambda l, j: (l, 0, j)),
        out_shape=jax.ShapeDtypeStruct((depth, bsz, n), F32),
        compiler_params=pltpu.CompilerParams(
            dimension_semantics=("arbitrary", "arbitrary"), vmem_limit_bytes=VMEM_LIMIT),
        name="ada",
    )(c, w_ada, b_ada.reshape(depth, 1, n))


def _top2_of_group(pg, sub_iota):
    big = float(EXPERTS_PER_GROUP)
    m1 = jnp.max(pg, axis=0, keepdims=True)
    i1 = jnp.min(jnp.where(pg == m1, sub_iota, big), axis=0, keepdims=True)
    rest = jnp.where(sub_iota == i1, -1.0, pg)
    m2 = jnp.max(rest, axis=0, keepdims=True)
    i2 = jnp.min(jnp.where(rest == m2, sub_iota, big), axis=0, keepdims=True)
    return m1, i1, m2, i2


def _mixer_kernel(x_ref, ada_ref, win_ref, wout_ref, convw_ref, convb_ref, poolw_ref, pscale_ref,
                  lng_ref, lnb_ref, sguw_ref, sgub_ref, ln1g_ref, ln1b_ref, wr_ref, br_ref,
                  x1_ref, h2_ref, e_ref, wt_ref, rank_ref, cnt_ref,
                  gx_scr, p_scr, y_scr, cnt_scr, *, ts, alpha):
    b = pl.program_id(0)
    j = pl.program_id(1)

    @pl.when(j == 0)
    def _():
        gx_scr[0:CONV_HALO, :] = jnp.zeros((CONV_HALO, CONV_W), F32)
        p_scr[0:POOL_HALO, :] = jnp.zeros((POOL_HALO, POOL_W), F32)

    @pl.when((b == 0) & (j == 0))
    def _():
        cnt_scr[...] = jnp.zeros_like(cnt_scr)

    x = x_ref[0]
    ada = ada_ref[0]
    sh1, sc1, g1 = ada[0:1], ada[1:2], ada[2:3]
    sh2, sc2, g2 = ada[3:4], ada[4:5], ada[5:6]
    hb = (x * (1.0 + sc1) + sh1).astype(BF16)

    def proj(lo, hi):
        return jnp.dot(hb, win_ref[:, lo:hi], preferred_element_type=F32)

    gcxc = proj(OFF_GC, OFF_P)
    g = gcxc[:, :CONV_W] * gcxc[:, CONV_W:]
    gx_scr[CONV_HALO:CONV_HALO + ts, :] = g
    cw = convw_ref[...]
    conv = (cw[0:1] * gx_scr[CONV_HALO - 2:CONV_HALO - 2 + ts, :]
            + cw[1:2] * gx_scr[CONV_HALO - 1:CONV_HALO - 1 + ts, :]
            + cw[2:3] * g + convb_ref[...])
    gx_scr[0:CONV_HALO, :] = g[ts - CONV_HALO:ts, :]
    y_scr[:, 0:CONV_W] = (proj(OFF_GB, OFF_GC) * conv).astype(BF16)

    p = proj(OFF_P, OFF_U)
    p_scr[POOL_HALO:POOL_HALO + ts, :] = p

    def shifted(k, lo):
        return p_scr[POOL_HALO - k:POOL_HALO - k + ts, lo:lo + LANES]

    s2 = p[:, 0:LANES] + shifted(1, 0)
    s4 = s2 + shifted(2, 0) + shifted(3, 0)
    acc = p[:, LANES:2 * LANES]
    for k in range(1, 8):
        acc = acc + shifted(k, LANES)
    s8 = acc
    for k in range(8, 16):
        acc = acc + shifted(k, LANES)
    s16 = acc
    p_scr[0:POOL_HALO, :] = p[ts - POOL_HALO:ts, :]
    lane = lax.broadcasted_iota(jnp.int32, (ts, LANES), 1)
    tpos = (lax.broadcasted_iota(jnp.int32, (ts, LANES), 0) + (j * ts + 1)).astype(F32)
    lo_half = lane < HEAD_DIM
    cnt_a = jnp.minimum(tpos, jnp.where(lo_half, float(POOL_WINDOWS[0]), float(POOL_WINDOWS[1])))
    cnt_b = jnp.minimum(tpos, jnp.where(lo_half, float(POOL_WINDOWS[2]), float(POOL_WINDOWS[3])))
    pooled = jnp.concatenate([jnp.where(lo_half, s2, s4) / cnt_a,
                              jnp.where(lo_half, s8, s16) / cnt_b], axis=1) - p
    mixed = jnp.dot(pooled.astype(BF16), poolw_ref[...], preferred_element_type=F32)
    y_scr[:, YOFF_POOL:YOFF_POOL + POOL_W] = (mixed * pscale_ref[...]).astype(BF16)

    v = proj(OFF_V, D_IN)
    vnb = _layer_norm(v, lng_ref[...], lnb_ref[...]).astype(BF16)
    u = proj(OFF_U, OFF_V)
    row_c = lax.broadcasted_iota(jnp.int32, (CHUNK, 2 * CHUNK), 0)
    col_c = lax.broadcasted_iota(jnp.int32, (CHUNK, 2 * CHUNK), 1)
    causal = (col_c & (CHUNK - 1)) <= row_c
    lo_lanes = lax.broadcasted_iota(jnp.int32, (CHUNK, LANES), 1) < HEAD_DIM
    zero_b = jnp.zeros((CHUNK, LANES), BF16)
    for hp in range(SGU_W // LANES):
        wl = jnp.where(causal, sguw_ref[hp], 0.0).astype(BF16)
        bias = sgub_ref[hp]
        for ci in range(ts // CHUNK):
            rs = slice(ci * CHUNK, (ci + 1) * CHUNK)
            vc = vnb[rs, hp * LANES:(hp + 1) * LANES]
            rhs = jnp.concatenate([jnp.where(lo_lanes, vc, zero_b),
                                   jnp.where(lo_lanes, zero_b, vc)], axis=0)
            mixed_c = jnp.dot(wl, rhs, preferred_element_type=F32) + bias
            y_scr[rs, YOFF_SGU + hp * LANES:YOFF_SGU + (hp + 1) * LANES] = (
                u[rs, hp * LANES:(hp + 1) * LANES] * mixed_c).astype(BF16)

    yo = jnp.dot(y_scr[...], wout_ref[...], preferred_element_type=F32)
    x1 = _layer_norm(alpha * x + g1 * yo, ln1g_ref[...], ln1b_ref[...])
    x1_ref[0] = x1
    h2 = x1 * (1.0 + sc2) + sh2
    h2_ref[0] = h2

    logits = jnp.dot(h2.astype(BF16), wr_ref[...], preferred_element_type=F32) + br_ref[...]
    lt = logits.T[0:N_EXPERTS, :]
    ex = jnp.exp(lt - jnp.max(lt, axis=0, keepdims=True))
    probs = ex / jnp.sum(ex, axis=0, keepdims=True)
    sub_iota = lax.broadcasted_iota(jnp.int32, (EXPERTS_PER_GROUP, ts), 0).astype(F32)
    best = None
    for gi in range(N_GROUPS):
        m1, i1, m2, i2 = _top2_of_group(
            probs[gi * EXPERTS_PER_GROUP:(gi + 1) * EXPERTS_PER_GROUP, :], sub_iota)
        score = m1 + m2
        cand = (score, m1, m2, i1 + float(gi * EXPERTS_PER_GROUP), i2 + float(gi * EXPERTS_PER_GROUP))
        if best is None:
            best = cand
        else:
            better = score > best[0]
            best = tuple(jnp.where(better, c, o) for c, o in zip(cand, best))
    _, p1, p2, e1, e2 = best
    den = p1 + p2
    wt_ref[0] = jnp.concatenate([p1 / den, p2 / den], axis=0)
    e_ref[0] = jnp.concatenate([e1, e2], axis=0).astype(jnp.int32)

    ex_iota = lax.broadcasted_iota(jnp.int32, (N_EXPERTS, ts), 0).astype(F32)
    oh1 = ex_iota == e1
    oh2 = ex_iota == e2
    either = jnp.where(oh1 | oh2, 1.0, 0.0)
    before = (lax.broadcasted_iota(jnp.int32, (ts, ts), 0)
              < lax.broadcasted_iota(jnp.int32, (ts, ts), 1))
    tri = jnp.where(before, 1.0, 0.0).astype(BF16)
    seen = jnp.dot(either.astype(BF16), tri, preferred_element_type=F32) + cnt_scr[:, 0:1]
    r1 = jnp.sum(jnp.where(oh1, seen, 0.0), axis=0, keepdims=True)
    r2 = jnp.sum(jnp.where(oh2, seen, 0.0), axis=0, keepdims=True)
    rank_ref[0] = jnp.concatenate([r1, r2], axis=0).astype(jnp.int32)
    cnt_new = cnt_scr[...] + jnp.sum(either, axis=1, keepdims=True)
    cnt_scr[...] = cnt_new
    cnt_ref[...] = cnt_new


def _mixer_call(x, ada_l, w_in, w_out, conv_w, conv_b, poolw_bd, pool_scale, ln_g, ln_b,
                sguw_pair, sgub_pair, ln1_g, ln1_b, wr_pad, br_pad, alpha):
    bsz, s, d = x.shape
    ts = SEQ_TILE
    nt = s // ts
    n_tiles = bsz * nt
    const2 = lambda b, j: (0, 0)
    const3 = lambda b, j: (0, 0, 0)
    tile3 = lambda b, j: (b, j, 0)
    meta3 = lambda b, j: (b * nt + j, 0, 0)
    kern = functools.partial(_mixer_kernel, ts=ts, alpha=alpha)
    return pl.pallas_call(
        kern,
        grid=(bsz, nt),
        in_specs=[
            pl.BlockSpec((1, ts, d), tile3),
            pl.BlockSpec((1, ADA_CHUNKS, d), lambda b, j: (b, 0, 0)),
            pl.BlockSpec((d, D_IN), const2),
            pl.BlockSpec((d, d), const2),
            pl.BlockSpec((CONV_K, CONV_W), const2),
            pl.BlockSpec((1, CONV_W), const2),
            pl.BlockSpec((POOL_W, POOL_W), const2),
            pl.BlockSpec((1, POOL_W), const2),
            pl.BlockSpec((1, SGU_W), const2),
            pl.BlockSpec((1, SGU_W), const2),
            pl.BlockSpec((SGU_W // LANES, CHUNK, 2 * CHUNK), const3),
            pl.BlockSpec((SGU_W // LANES, CHUNK, LANES), const3),
            pl.BlockSpec((1, d), const2),
            pl.BlockSpec((1, d), const2),
            pl.BlockSpec((d, LANES), const2),
            pl.BlockSpec((1, LANES), const2),
        ],
        out_specs=[
            pl.BlockSpec((1, ts, d), tile3),
            pl.BlockSpec((1, ts, d), tile3),
            pl.BlockSpec((1, TOP_K, ts), meta3),
            pl.BlockSpec((1, TOP_K, ts), meta3),
            pl.BlockSpec((1, TOP_K, ts), meta3),
            pl.BlockSpec((N_EXPERTS, LANES), const2),
        ],
        out_shape=[
            jax.ShapeDtypeStruct((bsz, s, d), F32),
            jax.ShapeDtypeStruct((bsz, s, d), F32),
            jax.ShapeDtypeStruct((n_tiles, TOP_K, ts), jnp.int32),
            jax.ShapeDtypeStruct((n_tiles, TOP_K, ts), F32),
            jax.ShapeDtypeStruct((n_tiles, TOP_K, ts), jnp.int32),
            jax.ShapeDtypeStruct((N_EXPERTS, LANES), F32),
        ],
        scratch_shapes=[
            pltpu.VMEM((CONV_HALO + ts, CONV_W), F32),
            pltpu.VMEM((POOL_HALO + ts, POOL_W), F32),
            pltpu.VMEM((ts, d), BF16),
            pltpu.VMEM((N_EXPERTS, LANES), F32),
        ],
        compiler_params=pltpu.CompilerParams(
            dimension_semantics=("arbitrary", "arbitrary"), vmem_limit_bytes=VMEM_LIMIT),
        name="mixer_router",
    )(x, ada_l, w_in, w_out, conv_w, conv_b, poolw_bd, pool_scale, ln_g, ln_b,
      sguw_pair, sgub_pair, ln1_g, ln1_b, wr_pad, br_pad)


def _row_copy(src_ref, src_row, dst_ref, dst_row, sem):
    return pltpu.make_async_copy(src_ref.at[pl.ds(src_row, 1), :], dst_ref.at[pl.ds(dst_row, 1), :], sem)


def _dispatch_kernel(seg_ref, e_ref, rank_ref, h_ref, xs_in_ref, xs_ref, sem, *, tt):
    del xs_in_ref

    def issue(t, carry):
        for k in range(TOP_K):
            dst = seg_ref[e_ref[0, k, t]] + rank_ref[0, k, t]
            _row_copy(h_ref, t, xs_ref, dst, sem).start()
        return carry

    lax.fori_loop(0, tt, issue, 0, unroll=8)
    for k in range(TOP_K):
        pltpu.make_async_copy(h_ref, xs_ref.at[pl.ds(0, tt), :], sem).wait()


def _dispatch_call(seg_start, e_idx, rank, h2_flat, xs_zero):
    n_tok, d = h2_flat.shape
    tt = TOK_TILE
    return pl.pallas_call(
        functools.partial(_dispatch_kernel, tt=tt),
        grid_spec=pltpu.PrefetchScalarGridSpec(
            num_scalar_prefetch=1,
            grid=(n_tok // tt,),
            in_specs=[
                pl.BlockSpec((1, TOP_K, tt), lambda i, seg: (i, 0, 0), memory_space=pltpu.SMEM),
                pl.BlockSpec((1, TOP_K, tt), lambda i, seg: (i, 0, 0), memory_space=pltpu.SMEM),
                pl.BlockSpec((tt, d), lambda i, seg: (i, 0)),
                pl.BlockSpec(memory_space=pl.ANY),
            ],
            out_specs=pl.BlockSpec(memory_space=pl.ANY),
            scratch_shapes=[pltpu.SemaphoreType.DMA(())],
        ),
        out_shape=jax.ShapeDtypeStruct(xs_zero.shape, xs_zero.dtype),
        input_output_aliases={4: 0},
        compiler_params=pltpu.CompilerParams(
            dimension_semantics=("arbitrary",), vmem_limit_bytes=VMEM_LIMIT),
        name="dispatch",
    )(seg_start, e_idx, rank, h2_flat, xs_zero)


def _ffn_kernel(be_ref, nb_ref, xs_ref, wg_ref, wu_ref, wd_ref, ys_ref, wg_scr, wu_scr, wd_scr):
    i = pl.program_id(0)
    prev = be_ref[jnp.maximum(i - 1, 0)]
    active = i < nb_ref[0]

    @pl.when(active & ((i == 0) | (be_ref[i] != prev)))
    def _():
        wg_scr[...] = wg_ref[0, 0].astype(BF16)
        wu_scr[...] = wu_ref[0, 0].astype(BF16)
        wd_scr[...] = wd_ref[0, 0].astype(BF16)

    @pl.when(active)
    def _():
        xb = xs_ref[...].astype(BF16)
        gate = jnp.dot(xb, wg_scr[...], preferred_element_type=F32)
        up = jnp.dot(xb, wu_scr[...], preferred_element_type=F32)
        act = (gate * jax.nn.sigmoid(gate) * up).astype(BF16)
        ys_ref[...] = jnp.dot(act, wd_scr[...], preferred_element_type=F32)

    @pl.when(jnp.logical_not(active))
    def _():
        ys_ref[...] = jnp.zeros_like(ys_ref)


def _ffn_call(layer, block_e, n_used, xs, w_gate, w_up, w_down):
    n_rows, d = xs.shape
    n_blocks = n_rows // ROW_BLOCK
    f = w_gate.shape[-1]
    row_map = lambda i, be, nb: (i, 0)
    w_map = lambda i, be, nb: (layer, be[i], 0, 0)
    return pl.pallas_call(
        _ffn_kernel,
        grid_spec=pltpu.PrefetchScalarGridSpec(
            num_scalar_prefetch=2,
            grid=(n_blocks,),
            in_specs=[
                pl.BlockSpec((ROW_BLOCK, d), row_map),
                pl.BlockSpec((1, 1, d, f), w_map),
                pl.BlockSpec((1, 1, d, f), w_map),
                pl.BlockSpec((1, 1, f, d), w_map),
            ],
            out_specs=pl.BlockSpec((ROW_BLOCK, d), row_map),
            scratch_shapes=[
                pltpu.VMEM((d, f), BF16),
                pltpu.VMEM((d, f), BF16),
                pltpu.VMEM((f, d), BF16),
            ],
        ),
        out_shape=jax.ShapeDtypeStruct((n_rows, d), F32),
        compiler_params=pltpu.CompilerParams(
            dimension_semantics=("arbitrary",), vmem_limit_bytes=VMEM_LIMIT),
        name="expert_ffn",
    )(block_e, n_used, xs, w_gate, w_up, w_down)


def _combine_kernel(seg_ref, e_ref, rank_ref, ys_ref, wt_ref, x1_ref, ada_ref, g_ref, b_ref,
                    o_ref, buf, sem, *, tt, alpha):
    def issue(t, carry):
        for k in range(TOP_K):
            src = seg_ref[e_ref[0, k, t]] + rank_ref[0, k, t]
            _row_copy(ys_ref, src, buf.at[k], t, sem).start()
        return carry

    lax.fori_loop(0, tt, issue, 0, unroll=8)
    for k in range(TOP_K):
        pltpu.make_async_copy(ys_ref.at[pl.ds(0, tt), :], buf.at[k], sem).wait()

    wt = wt_ref[...]
    y = wt[:, 0:1] * buf[0] + wt[:, 1:2] * buf[1]
    g2 = ada_ref[0][5:6]
    o_ref[...] = _layer_norm(alpha * x1_ref[...] + g2 * y, g_ref[...], b_ref[...])


def _combine_call(seg_start, e_idx, rank, ys, wt_tok, x1_flat, ada_l, ln_g, ln_b, alpha, seq):
    n_tok, d = x1_flat.shape
    tt = TOK_TILE
    per_seq = seq // tt
    return pl.pallas_call(
        functools.partial(_combine_kernel, tt=tt, alpha=alpha),
        grid_spec=pltpu.PrefetchScalarGridSpec(
            num_scalar_prefetch=1,
            grid=(n_tok // tt,),
            in_specs=[
                pl.BlockSpec((1, TOP_K, tt), lambda i, seg: (i, 0, 0), memory_space=pltpu.SMEM),
                pl.BlockSpec((1, TOP_K, tt), lambda i, seg: (i, 0, 0), memory_space=pltpu.SMEM),
                pl.BlockSpec(memory_space=pl.ANY),
                pl.BlockSpec((tt, TOP_K), lambda i, seg: (i, 0)),
                pl.BlockSpec((tt, d), lambda i, seg: (i, 0)),
                pl.BlockSpec((1, ADA_CHUNKS, d), lambda i, seg: (i // per_seq, 0, 0)),
                pl.BlockSpec((1, d), lambda i, seg: (0, 0)),
                pl.BlockSpec((1, d), lambda i, seg: (0, 0)),
            ],
            out_specs=pl.BlockSpec((tt, d), lambda i, seg: (i, 0)),
            scratch_shapes=[
                pltpu.VMEM((TOP_K, tt, d), F32),
                pltpu.SemaphoreType.DMA(()),
            ],
        ),
        out_shape=jax.ShapeDtypeStruct((n_tok, d), F32),
        compiler_params=pltpu.CompilerParams(
            dimension_semantics=("arbitrary",), vmem_limit_bytes=VMEM_LIMIT),
        name="combine_ln",
    )(seg_start, e_idx, rank, ys, wt_tok, x1_flat, ada_l, ln_g, ln_b)


def kernel(x, c, w_ada, b_ada, w_in, conv_w, conv_b, pool_w, pool_scale, sgu_ln_g, sgu_ln_b,
           sgu_w, sgu_b, w_out, ln1_g, ln1_b, w_router, b_router, w_gate, w_up, w_down,
           ln2_g, ln2_b):
    bsz, seq, d = x.shape
    depth = w_ada.shape[0]
    n_tok = bsz * seq
    alpha = (2 * depth) ** 0.25
    assert d == D_MODEL and seq % SEQ_TILE == 0 and SEQ_TILE % CHUNK == 0
    assert SEQ_TILE == TOK_TILE and n_tok % TOK_TILE == 0

    ada = _ada_call(c, w_ada, b_ada).reshape(depth, bsz, ADA_CHUNKS, d)

    w_in_b = w_in.astype(BF16)
    w_out_b = w_out.astype(BF16)
    eye_g = jnp.eye(POOL_W // HEAD_DIM, dtype=F32)
    poolw_bd = jnp.einsum('lgcd,gh->lgchd', pool_w, eye_g).reshape(depth, POOL_W, POOL_W).astype(BF16)
    n_pairs = SGU_W // LANES
    sguw_pair = sgu_w.reshape(depth, n_pairs, 2, CHUNK, CHUNK).transpose(0, 1, 3, 2, 4).reshape(
        depth, n_pairs, CHUNK, 2 * CHUNK)
    sgub_pair = jnp.repeat(sgu_b.transpose(0, 2, 1), HEAD_DIM, axis=-1).reshape(
        depth, CHUNK, n_pairs, LANES).transpose(0, 2, 1, 3)
    wr_pad = jnp.pad(w_router, ((0, 0), (0, LANES - N_EXPERTS))).astype(BF16)
    br_pad = jnp.pad(b_router, (0, LANES - N_EXPERTS)).reshape(1, LANES)

    n_blocks = (n_tok * TOP_K) // ROW_BLOCK + N_EXPERTS
    n_rows = n_blocks * ROW_BLOCK
    block_start = jnp.arange(n_blocks, dtype=jnp.int32) * ROW_BLOCK

    for l in range(depth):
        x1, h2, e_idx, wts, rank, counts = _mixer_call(
            x, ada[l], w_in_b[l], w_out_b[l], conv_w[l], conv_b[l].reshape(1, -1), poolw_bd[l],
            pool_scale[l].reshape(1, -1), sgu_ln_g[l].reshape(1, -1), sgu_ln_b[l].reshape(1, -1),
            sguw_pair[l], sgub_pair[l], ln1_g[l].reshape(1, -1), ln1_b[l].reshape(1, -1),
            wr_pad, br_pad, alpha)
        sizes = counts[:, 0].astype(jnp.int32)
        padded = ((sizes + ROW_BLOCK - 1) // ROW_BLOCK) * ROW_BLOCK
        seg_end = jnp.cumsum(padded)
        seg_start = (seg_end - padded).astype(jnp.int32)
        block_e = jnp.minimum(
            jnp.sum(block_start[:, None] >= seg_end[None, :], axis=1), N_EXPERTS - 1).astype(jnp.int32)
        n_used = (seg_end[-1:] // ROW_BLOCK).astype(jnp.int32)
        block_e = jnp.where(block_start < seg_end[-1], block_e, block_e[jnp.maximum(n_used[0] - 1, 0)])

        xs = _dispatch_call(seg_start, e_idx, rank, h2.reshape(n_tok, d), jnp.zeros((n_rows, d), F32))
        ys = _ffn_call(l, block_e, n_used, xs, w_gate, w_up, w_down)
        wt_tok = wts.transpose(0, 2, 1).reshape(n_tok, TOP_K)
        x = _combine_call(seg_start, e_idx, rank, ys, wt_tok, x1.reshape(n_tok, d), ada[l],
                          ln2_g[l].reshape(1, -1), ln2_b[l].reshape(1, -1), alpha, seq).reshape(bsz, seq, d)
    return x
```

```python
import functools

import jax
import jax.numpy as jnp
from jax import lax
from jax.experimental import pallas as pl
from jax.experimental.pallas import tpu as pltpu

D_MODEL = 1024
HEAD_DIM = D_MODEL // 16
CONV_W = 6 * HEAD_DIM
POOL_W = 4 * HEAD_DIM
SGU_W = 6 * HEAD_DIM
D_IN = 3 * CONV_W + POOL_W + 2 * SGU_W
CONV_K = 3
POOL_WINDOWS = (2, 4, 8, 16)
CHUNK = 128
N_EXPERTS = 32
N_GROUPS = 4
EXPERTS_PER_GROUP = N_EXPERTS // N_GROUPS
TOP_K = 2
D_FF = D_MODEL // 2
ADA_CHUNKS = 6
LN_EPS = 1e-5

OFF_GB = 0
OFF_GC = CONV_W
OFF_P = 3 * CONV_W
OFF_U = OFF_P + POOL_W
OFF_V = OFF_U + SGU_W
YOFF_POOL = CONV_W
YOFF_SGU = CONV_W + POOL_W

LANES = 128
SUBLANES = 8
CONV_HALO = 8
POOL_HALO = 16
SEQ_TILE = 512
ROW_BLOCK = 256
ADA_TILE = 1536
VMEM_LIMIT = 60 * 1024 * 1024

SLOTS = TOP_K * SEQ_TILE + N_EXPERTS * SUBLANES
TILE_GROUPS = SLOTS // SUBLANES
BLOCK_GROUPS = ROW_BLOCK // SUBLANES
WAIT_BITS = TILE_GROUPS.bit_length()
XS_W = D_MODEL + LANES
META_OFF, META_N8, META_BASE, META_END = 0, 1, 2, 3

F32 = jnp.float32
BF16 = jnp.bfloat16


def _layer_norm(r, g, b):
    mu = jnp.mean(r, axis=-1, keepdims=True)
    d = r - mu
    var = jnp.mean(d * d, axis=-1, keepdims=True)
    return d * lax.rsqrt(var + LN_EPS) * g + b


def _rows(ref, start, size):
    return ref.at[pl.ds(pl.multiple_of(start, SUBLANES), size), :]


def _wait_groups(src_ref, dst_ref, groups, sem):
    for k in range(WAIT_BITS):
        size = SUBLANES << k

        @pl.when(((groups >> k) & 1) == 1)
        def _():
            pltpu.make_async_copy(_rows(src_ref, 0, size), _rows(dst_ref, 0, size), sem).wait()


def _ada_kernel(c_ref, w_ref, b_ref, o_ref):
    c = c_ref[...]
    c_act = (c * jax.nn.sigmoid(c)).astype(BF16)
    o_ref[0] = jnp.dot(c_act, w_ref[0].astype(BF16), preferred_element_type=F32) + b_ref[0]


def _ada_call(c, w_ada, b_ada):
    depth, d, n = w_ada.shape
    bsz = c.shape[0]
    return pl.pallas_call(
        _ada_kernel,
        grid=(depth, n // ADA_TILE),
        in_specs=[
            pl.BlockSpec((bsz, d), lambda l, j: (0, 0)),
            pl.BlockSpec((1, d, ADA_TILE), lambda l, j: (l, 0, j)),
            pl.BlockSpec((1, 1, ADA_TILE), lambda l, j: (l, 0, j)),
        ],
        out_specs=pl.BlockSpec((1, bsz, ADA_TILE), lambda l, j: (l, 0, j)),
        out_shape=jax.ShapeDtypeStruct((depth, bsz, n), F32),
        compiler_params=pltpu.CompilerParams(
            dimension_semantics=("arbitrary", "arbitrary"), vmem_limit_bytes=VMEM_LIMIT),
        name="ada",
    )(c, w_ada, b_ada.reshape(depth, 1, n))


def _top2_of_group(pg, sub_iota):
    big = float(EXPERTS_PER_GROUP)
    m1 = jnp.max(pg, axis=0, keepdims=True)
    i1 = jnp.min(jnp.where(pg == m1, sub_iota, big), axis=0, keepdims=True)
    rest = jnp.where(sub_iota == i1, -1.0, pg)
    m2 = jnp.max(rest, axis=0, keepdims=True)
    i2 = jnp.min(jnp.where(rest == m2, sub_iota, big), axis=0, keepdims=True)
    return m1, i1, m2, i2


def _split3(w):
    hi = w.astype(BF16)
    r1 = w - hi.astype(F32)
    mid = r1.astype(BF16)
    lo = (r1 - mid.astype(F32)).astype(BF16)
    return hi, mid, lo


def _mixer_kernel(x_ref, ada_ref, win_ref, wout_ref, convw_ref, convb_ref, poolw_ref, pscale_ref,
                  lng_ref, lnb_ref, sguw_ref, sgub_ref, ln1g_ref, ln1b_ref, wr_ref, br_ref,
                  x1_ref, pos_ref, meta_ref, xs_ref,
                  gx_scr, p_scr, y_scr, base_scr, *, ts, alpha):
    b = pl.program_id(0)
    j = pl.program_id(1)

    @pl.when(j == 0)
    def _():
        gx_scr[0:CONV_HALO, :] = jnp.zeros((CONV_HALO, CONV_W), F32)
        p_scr[0:POOL_HALO, :] = jnp.zeros((POOL_HALO, POOL_W), F32)

    @pl.when((b == 0) & (j == 0))
    def _():
        base_scr[...] = jnp.zeros_like(base_scr)

    x = x_ref[0]
    ada = ada_ref[0]
    sh1, sc1, g1 = ada[0:1], ada[1:2], ada[2:3]
    sh2, sc2 = ada[3:4], ada[4:5]
    hb = (x * (1.0 + sc1) + sh1).astype(BF16)

    def proj(lo, hi):
        return jnp.dot(hb, win_ref[:, lo:hi], preferred_element_type=F32)

    gcxc = proj(OFF_GC, OFF_P)
    g = gcxc[:, :CONV_W] * gcxc[:, CONV_W:]
    gx_scr[CONV_HALO:CONV_HALO + ts, :] = g
    cw = convw_ref[...]
    conv = (cw[0:1] * gx_scr[CONV_HALO - 2:CONV_HALO - 2 + ts, :]
            + cw[1:2] * gx_scr[CONV_HALO - 1:CONV_HALO - 1 + ts, :]
            + cw[2:3] * g + convb_ref[...])
    gx_scr[0:CONV_HALO, :] = g[ts - CONV_HALO:ts, :]
    y_scr[:, 0:CONV_W] = (proj(OFF_GB, OFF_GC) * conv).astype(BF16)

    p = proj(OFF_P, OFF_U)
    p_scr[POOL_HALO:POOL_HALO + ts, :] = p

    def shifted(k, lo):
        return p_scr[POOL_HALO - k:POOL_HALO - k + ts, lo:lo + LANES]

    s2 = p[:, 0:LANES] + shifted(1, 0)
    s4 = s2 + shifted(2, 0) + shifted(3, 0)
    acc = p[:, LANES:2 * LANES]
    for k in range(1, 8):
        acc = acc + shifted(k, LANES)
    s8 = acc
    for k in range(8, 16):
        acc = acc + shifted(k, LANES)
    s16 = acc
    p_scr[0:POOL_HALO, :] = p[ts - POOL_HALO:ts, :]
    lane = lax.broadcasted_iota(jnp.int32, (ts, LANES), 1)
    tpos = (lax.broadcasted_iota(jnp.int32, (ts, LANES), 0) + (j * ts + 1)).astype(F32)
    lo_half = lane < HEAD_DIM
    cnt_a = jnp.minimum(tpos, jnp.where(lo_half, float(POOL_WINDOWS[0]), float(POOL_WINDOWS[1])))
    cnt_b = jnp.minimum(tpos, jnp.where(lo_half, float(POOL_WINDOWS[2]), float(POOL_WINDOWS[3])))
    pooled = jnp.concatenate([jnp.where(lo_half, s2, s4) / cnt_a,
                              jnp.where(lo_half, s8, s16) / cnt_b], axis=1) - p
    mixed = jnp.dot(pooled.astype(BF16), poolw_ref[...], preferred_element_type=F32)
    y_scr[:, YOFF_POOL:YOFF_POOL + POOL_W] = (mixed * pscale_ref[...]).astype(BF16)

    v = proj(OFF_V, D_IN)
    vnb = _layer_norm(v, lng_ref[...], lnb_ref[...]).astype(BF16)
    u = proj(OFF_U, OFF_V)
    row_c = lax.broadcasted_iota(jnp.int32, (CHUNK, 2 * CHUNK), 0)
    col_c = lax.broadcasted_iota(jnp.int32, (CHUNK, 2 * CHUNK), 1)
    causal = (col_c & (CHUNK - 1)) <= row_c
    lo_lanes = lax.broadcasted_iota(jnp.int32, (CHUNK, LANES), 1) < HEAD_DIM
    zero_b = jnp.zeros((CHUNK, LANES), BF16)
    for hp in range(SGU_W // LANES):
        wl = jnp.where(causal, sguw_ref[hp], 0.0).astype(BF16)
        bias = sgub_ref[hp]
        for ci in range(ts // CHUNK):
            rs = slice(ci * CHUNK, (ci + 1) * CHUNK)
            vc = vnb[rs, hp * LANES:(hp + 1) * LANES]
            rhs = jnp.concatenate([jnp.where(lo_lanes, vc, zero_b),
                                   jnp.where(lo_lanes, zero_b, vc)], axis=0)
            mixed_c = jnp.dot(wl, rhs, preferred_element_type=F32) + bias
            y_scr[rs, YOFF_SGU + hp * LANES:YOFF_SGU + (hp + 1) * LANES] = (
                u[rs, hp * LANES:(hp + 1) * LANES] * mixed_c).astype(BF16)

    yo = jnp.dot(y_scr[...], wout_ref[...], preferred_element_type=F32)
    x1 = _layer_norm(alpha * x + g1 * yo, ln1g_ref[...], ln1b_ref[...])
    x1_ref[0] = x1
    h2b = (x1 * (1.0 + sc2) + sh2).astype(BF16)

    logits = jnp.dot(h2b, wr_ref[...], preferred_element_type=F32) + br_ref[...]
    lt = logits.T[0:N_EXPERTS, :]
    ex = jnp.exp(lt - jnp.max(lt, axis=0, keepdims=True))
    probs = ex / jnp.sum(ex, axis=0, keepdims=True)
    sub_iota = lax.broadcasted_iota(jnp.int32, (EXPERTS_PER_GROUP, ts), 0).astype(F32)
    best = None
    for gi in range(N_GROUPS):
        m1, i1, m2, i2 = _top2_of_group(
            probs[gi * EXPERTS_PER_GROUP:(gi + 1) * EXPERTS_PER_GROUP, :], sub_iota)
        score = m1 + m2
        cand = (score, m1, m2, i1 + float(gi * EXPERTS_PER_GROUP), i2 + float(gi * EXPERTS_PER_GROUP))
        if best is None:
            best = cand
        else:
            better = score > best[0]
            best = tuple(jnp.where(better, c, o) for c, o in zip(cand, best))
    _, p1, p2, e1, e2 = best
    den = p1 + p2
    w1 = p1 / den
    w2 = p2 / den

    ex_iota = lax.broadcasted_iota(jnp.int32, (N_EXPERTS, ts), 0).astype(F32)
    oh1 = ex_iota == e1
    oh2 = ex_iota == e2
    either = jnp.where(oh1 | oh2, 1.0, 0.0)
    before = (lax.broadcasted_iota(jnp.int32, (ts, ts), 0)
              < lax.broadcasted_iota(jnp.int32, (ts, ts), 1))
    tri = jnp.where(before, 1.0, 0.0).astype(BF16)
    seen = jnp.dot(either.astype(BF16), tri, preferred_element_type=F32)
    n_col = jnp.sum(either, axis=1, keepdims=True)
    n8_col = jnp.ceil(n_col * (1.0 / SUBLANES)) * float(SUBLANES)
    lower = (lax.broadcasted_iota(jnp.int32, (N_EXPERTS, N_EXPERTS), 1)
             < lax.broadcasted_iota(jnp.int32, (N_EXPERTS, N_EXPERTS), 0))
    off_col = jnp.dot(jnp.where(lower, 1.0, 0.0).astype(BF16),
                      jnp.broadcast_to(n8_col, (N_EXPERTS, LANES)).astype(BF16),
                      preferred_element_type=F32)[:, 0:1]
    slot = off_col + seen
    pos1 = jnp.sum(jnp.where(oh1, slot, 0.0), axis=0, keepdims=True)
    pos2 = jnp.sum(jnp.where(oh2, slot, 0.0), axis=0, keepdims=True)
    pos_ref[0] = jnp.concatenate([pos1, pos2], axis=0).astype(jnp.int32)

    base_col = base_scr[:, 0:1]
    end_col = base_col + n8_col
    base_scr[...] = jnp.broadcast_to(end_col, (N_EXPERTS, LANES))
    mlane = lax.broadcasted_iota(jnp.int32, (N_EXPERTS, LANES), 1)
    meta = jnp.where(mlane == META_OFF, off_col,
                     jnp.where(mlane == META_N8, n8_col,
                               jnp.where(mlane == META_BASE, base_col,
                                         jnp.where(mlane == META_END, end_col, 0.0)))).astype(jnp.int32)
    meta_ref[0] = meta

    slot_iota = lax.broadcasted_iota(jnp.int32, (SLOTS, ts), 0).astype(F32)
    sel1 = slot_iota == pos1
    sel2 = slot_iota == pos2
    p1f = jnp.where(sel1, 1.0, 0.0)
    p2f = jnp.where(sel2, 1.0, 0.0)
    perm = (p1f + p2f).astype(BF16)
    perm_k = jnp.concatenate([p1f.astype(BF16), p2f.astype(BF16)], axis=1)
    wcat = jnp.concatenate([w1, w2], axis=1)
    hi, mid, lo = (piece.astype(F32) for piece in _split3(wcat))
    wrow = lax.broadcasted_iota(jnp.int32, (LANES, TOP_K * ts), 0)
    wmat = jnp.where(wrow == 0, hi, jnp.where(wrow == 1, mid, jnp.where(wrow == 2, lo, 0.0))).astype(BF16)
    xs_ref[:, 0:D_MODEL] = jnp.dot(perm, h2b, preferred_element_type=F32)
    xs_ref[:, D_MODEL:XS_W] = lax.dot_general(perm_k, wmat, (((1,), (1,)), ((), ())),
                                              preferred_element_type=F32)


def _mixer_call(x, ada_l, w_in, w_out, conv_w, conv_b, poolw_bd, pool_scale, ln_g, ln_b,
                sguw_pair, sgub_pair, ln1_g, ln1_b, wr_pad, br_pad, alpha):
    bsz, s, d = x.shape
    ts = SEQ_TILE
    nt = s // ts
    n_tiles = bsz * nt
    const2 = lambda b, j: (0, 0)
    const3 = lambda b, j: (0, 0, 0)
    tile3 = lambda b, j: (b, j, 0)
    meta3 = lambda b, j: (b * nt + j, 0, 0)
    once = pl.Buffered(1)
    kern = functools.partial(_mixer_kernel, ts=ts, alpha=alpha)
    return pl.pallas_call(
        kern,
        grid=(bsz, nt),
        in_specs=[
            pl.BlockSpec((1, ts, d), tile3),
            pl.BlockSpec((1, ADA_CHUNKS, d), lambda b, j: (b, 0, 0)),
            pl.BlockSpec((d, D_IN), const2, pipeline_mode=once),
            pl.BlockSpec((d, d), const2, pipeline_mode=once),
            pl.BlockSpec((CONV_K, CONV_W), const2),
            pl.BlockSpec((1, CONV_W), const2),
            pl.BlockSpec((POOL_W, POOL_W), const2),
            pl.BlockSpec((1, POOL_W), const2),
            pl.BlockSpec((1, SGU_W), const2),
            pl.BlockSpec((1, SGU_W), const2),
            pl.BlockSpec((SGU_W // LANES, CHUNK, 2 * CHUNK), const3),
            pl.BlockSpec((SGU_W // LANES, CHUNK, LANES), const3),
            pl.BlockSpec((1, d), const2),
            pl.BlockSpec((1, d), const2),
            pl.BlockSpec((d, LANES), const2),
            pl.BlockSpec((1, LANES), const2),
        ],
        out_specs=[
            pl.BlockSpec((1, ts, d), tile3),
            pl.BlockSpec((1, TOP_K, ts), meta3),
            pl.BlockSpec((1, N_EXPERTS, LANES), meta3),
            pl.BlockSpec((SLOTS, XS_W), lambda b, j: (b * nt + j, 0)),
        ],
        out_shape=[
            jax.ShapeDtypeStruct((bsz, s, d), F32),
            jax.ShapeDtypeStruct((n_tiles, TOP_K, ts), jnp.int32),
            jax.ShapeDtypeStruct((n_tiles, N_EXPERTS, LANES), jnp.int32),
            jax.ShapeDtypeStruct((n_tiles * SLOTS, XS_W), F32),
        ],
        scratch_shapes=[
            pltpu.VMEM((CONV_HALO + ts, CONV_W), F32),
            pltpu.VMEM((POOL_HALO + ts, POOL_W), F32),
            pltpu.VMEM((ts, d), BF16),
            pltpu.VMEM((N_EXPERTS, LANES), F32),
        ],
        compiler_params=pltpu.CompilerParams(
            dimension_semantics=("arbitrary", "arbitrary"), vmem_limit_bytes=VMEM_LIMIT),
        name="mixer_router",
    )(x, ada_l, w_in, w_out, conv_w, conv_b, poolw_bd, pool_scale, ln_g, ln_b,
      sguw_pair, sgub_pair, ln1_g, ln1_b, wr_pad, br_pad)


def _ffn_kernel(src_ref, be_ref, nb_ref, xs_ref, wg_ref, wu_ref, wd_ref, ys_ref,
                xbuf, wg_scr, wu_scr, wd_scr, sem):
    i = pl.program_id(0)
    n_used = nb_ref[0]
    cur = i % 2
    prev = be_ref[jnp.maximum(i - 1, 0)]
    active = i < n_used

    def gather(step, slot):
        for r in range(BLOCK_GROUPS):
            pltpu.make_async_copy(_rows(xs_ref, src_ref[step * BLOCK_GROUPS + r], SUBLANES),
                                  xbuf.at[slot, pl.ds(r * SUBLANES, SUBLANES), :], sem.at[slot]).start()

    @pl.when((i == 0) & active)
    def _():
        gather(0, 0)

    @pl.when(i + 1 < n_used)
    def _():
        gather(i + 1, 1 - cur)

    @pl.when(active & ((i == 0) | (be_ref[i] != prev)))
    def _():
        wg_scr[...] = wg_ref[0, 0].astype(BF16)
        wu_scr[...] = wu_ref[0, 0].astype(BF16)
        wd_scr[...] = wd_ref[0, 0].astype(BF16)

    @pl.when(active)
    def _():
        pltpu.make_async_copy(_rows(xs_ref, 0, ROW_BLOCK), xbuf.at[cur], sem.at[cur]).wait()
        xb = xbuf[cur, :, 0:D_MODEL].astype(BF16)
        wparts = xbuf[cur, :, D_MODEL:XS_W]
        wrow = (wparts[:, 0:1] + wparts[:, 1:2]) + wparts[:, 2:3]
        gate = jnp.dot(xb, wg_scr[...], preferred_element_type=F32)
        up = jnp.dot(xb, wu_scr[...], preferred_element_type=F32)
        act = (gate * jax.nn.sigmoid(gate) * up).astype(BF16)
        ys_ref[...] = jnp.dot(act, wd_scr[...], preferred_element_type=F32) * wrow

    @pl.when(jnp.logical_not(active))
    def _():
        ys_ref[...] = jnp.zeros_like(ys_ref)


def _ffn_call(layer, group_src, block_e, n_used, xs, w_gate, w_up, w_down):
    n_steps = block_e.shape[0]
    d, f = w_gate.shape[-2:]
    w_map = lambda i, src, be, nb: (layer, be[i], 0, 0)
    return pl.pallas_call(
        _ffn_kernel,
        grid_spec=pltpu.PrefetchScalarGridSpec(
            num_scalar_prefetch=3,
            grid=(n_steps,),
            in_specs=[
                pl.BlockSpec(memory_space=pl.ANY),
                pl.BlockSpec((1, 1, d, f), w_map),
                pl.BlockSpec((1, 1, d, f), w_map),
                pl.BlockSpec((1, 1, f, d), w_map),
            ],
            out_specs=pl.BlockSpec((ROW_BLOCK, d), lambda i, src, be, nb: (i, 0)),
            scratch_shapes=[
                pltpu.VMEM((2, ROW_BLOCK, XS_W), F32),
                pltpu.VMEM((d, f), BF16),
                pltpu.VMEM((d, f), BF16),
                pltpu.VMEM((f, d), BF16),
                pltpu.SemaphoreType.DMA((2,)),
            ],
        ),
        out_shape=jax.ShapeDtypeStruct((n_steps * ROW_BLOCK, d), F32),
        compiler_params=pltpu.CompilerParams(
            dimension_semantics=("arbitrary",), vmem_limit_bytes=VMEM_LIMIT),
        name="expert_ffn",
    )(group_src, block_e, n_used, xs, w_gate, w_up, w_down)


def _combine_kernel(src_ref, ng_ref, ys_ref, pos_ref, x1_ref, ada_ref, g_ref, b_ref,
                    o_ref, buf, sem, *, ts, alpha):
    i = pl.program_id(0)
    n = pl.num_programs(0)
    cur = i % 2

    def gather(tile, slot):
        def issue(s, carry):
            pltpu.make_async_copy(_rows(ys_ref, src_ref[tile * TILE_GROUPS + s], SUBLANES),
                                  buf.at[slot, pl.ds(pl.multiple_of(s * SUBLANES, SUBLANES), SUBLANES), :],
                                  sem.at[slot]).start()
            return carry

        lax.fori_loop(0, ng_ref[tile], issue, 0)

    @pl.when(i == 0)
    def _():
        buf[...] = jnp.zeros_like(buf)
        gather(0, 0)

    @pl.when(i + 1 < n)
    def _():
        gather(i + 1, 1 - cur)

    _wait_groups(ys_ref, buf.at[cur], ng_ref[i], sem.at[cur])

    pos = pos_ref[...]
    slot_iota = lax.broadcasted_iota(jnp.int32, (ts, SLOTS), 1)
    pick = (slot_iota == pos[:, 0:1]) | (slot_iota == pos[:, 1:2])
    unsort = jnp.where(pick, 1.0, 0.0).astype(BF16)
    y = jnp.dot(unsort, buf[cur].astype(BF16), preferred_element_type=F32)
    g2 = ada_ref[0][5:6]
    o_ref[...] = _layer_norm(alpha * x1_ref[...] + g2 * y, g_ref[...], b_ref[...])


def _combine_call(group_src, n_groups, ys, pos_tok, x1_flat, ada_l, ln_g, ln_b, alpha, seq):
    n_tok, d = x1_flat.shape
    ts = SEQ_TILE
    n_tiles = n_tok // ts
    per_seq = seq // ts
    return pl.pallas_call(
        functools.partial(_combine_kernel, ts=ts, alpha=alpha),
        grid_spec=pltpu.PrefetchScalarGridSpec(
            num_scalar_prefetch=2,
            grid=(n_tiles,),
            in_specs=[
                pl.BlockSpec(memory_space=pl.ANY),
                pl.BlockSpec((ts, TOP_K), lambda i, src, ng: (i, 0)),
                pl.BlockSpec((ts, d), lambda i, src, ng: (i, 0)),
                pl.BlockSpec((1, ADA_CHUNKS, d), lambda i, src, ng: (i // per_seq, 0, 0)),
                pl.BlockSpec((1, d), lambda i, src, ng: (0, 0)),
                pl.BlockSpec((1, d), lambda i, src, ng: (0, 0)),
            ],
            out_specs=pl.BlockSpec((ts, d), lambda i, src, ng: (i, 0)),
            scratch_shapes=[
                pltpu.VMEM((2, SLOTS, d), F32),
                pltpu.SemaphoreType.DMA((2,)),
            ],
        ),
        out_shape=jax.ShapeDtypeStruct((n_tok, d), F32),
        compiler_params=pltpu.CompilerParams(
            dimension_semantics=("arbitrary",), vmem_limit_bytes=VMEM_LIMIT),
        name="combine_ln",
    )(group_src, n_groups, ys, pos_tok, x1_flat, ada_l, ln_g, ln_b)


def kernel(x, c, w_ada, b_ada, w_in, conv_w, conv_b, pool_w, pool_scale, sgu_ln_g, sgu_ln_b,
           sgu_w, sgu_b, w_out, ln1_g, ln1_b, w_router, b_router, w_gate, w_up, w_down,
           ln2_g, ln2_b):
    bsz, seq, d = x.shape
    depth = w_ada.shape[0]
    n_tok = bsz * seq
    n_tiles = n_tok // SEQ_TILE
    alpha = (2 * depth) ** 0.25
    assert d == D_MODEL and seq % SEQ_TILE == 0 and SEQ_TILE % CHUNK == 0

    ada = _ada_call(c, w_ada, b_ada).reshape(depth, bsz, ADA_CHUNKS, d)

    w_in_b = w_in.astype(BF16)
    w_out_b = w_out.astype(BF16)
    eye_g = jnp.eye(POOL_W // HEAD_DIM, dtype=F32)
    poolw_bd = jnp.einsum('lgcd,gh->lgchd', pool_w, eye_g).reshape(depth, POOL_W, POOL_W).astype(BF16)
    n_pairs = SGU_W // LANES
    sguw_pair = sgu_w.reshape(depth, n_pairs, 2, CHUNK, CHUNK).transpose(0, 1, 3, 2, 4).reshape(
        depth, n_pairs, CHUNK, 2 * CHUNK)
    sgub_pair = jnp.repeat(sgu_b.transpose(0, 2, 1), HEAD_DIM, axis=-1).reshape(
        depth, CHUNK, n_pairs, LANES).transpose(0, 2, 1, 3)
    wr_pad = jnp.pad(w_router, ((0, 0), (0, LANES - N_EXPERTS))).astype(BF16)
    br_pad = jnp.pad(b_router, (0, LANES - N_EXPERTS)).reshape(1, LANES)

    max_rows = n_tok * TOP_K + (SUBLANES - 1) * N_EXPERTS * n_tiles
    n_steps = -(-max_rows // ROW_BLOCK) + N_EXPERTS
    step = jnp.arange(n_steps, dtype=jnp.int32)
    zero_group = SLOTS - SUBLANES

    for l in range(depth):
        x1, pos, meta, xs = _mixer_call(
            x, ada[l], w_in_b[l], w_out_b[l], conv_w[l], conv_b[l].reshape(1, -1), poolw_bd[l],
            pool_scale[l].reshape(1, -1), sgu_ln_g[l].reshape(1, -1), sgu_ln_b[l].reshape(1, -1),
            sguw_pair[l], sgub_pair[l], ln1_g[l].reshape(1, -1), ln1_b[l].reshape(1, -1),
            wr_pad, br_pad, alpha)
        off = meta[:, :, META_OFF]
        n8 = meta[:, :, META_N8]
        base = meta[:, :, META_BASE]
        end = meta[:, :, META_END]
        seg_rows = end[n_tiles - 1]
        seg_blocks = (seg_rows + ROW_BLOCK - 1) // ROW_BLOCK
        blk_end = jnp.cumsum(seg_blocks)
        blk_start = blk_end - seg_blocks
        n_used = blk_end[-1:].astype(jnp.int32)
        active = step < n_used[0]
        block_e = jnp.minimum(jnp.sum(step[:, None] >= blk_end[None, :], axis=1), N_EXPERTS - 1)
        unit = ((step - blk_start[block_e]) * BLOCK_GROUPS)[:, None] + jnp.arange(BLOCK_GROUPS)[None, :]
        end_u = (end.T // SUBLANES)[block_e]
        tile_of = jnp.sum(end_u[:, None, :] <= unit[:, :, None], axis=2)
        valid = (tile_of < n_tiles) & active[:, None]
        tile_c = jnp.minimum(tile_of, n_tiles - 1)
        chunk_off = jnp.take_along_axis(off.T[block_e], tile_c, axis=1)
        chunk_base = jnp.take_along_axis(base.T[block_e], tile_c, axis=1)
        src = tile_c * SLOTS + chunk_off + unit * SUBLANES - chunk_base
        ffn_src = jnp.where(valid, src, zero_group).astype(jnp.int32).reshape(-1)
        block_e = jnp.where(active, block_e, block_e[jnp.maximum(n_used[0] - 1, 0)]).astype(jnp.int32)
        grp = jnp.arange(TILE_GROUPS, dtype=jnp.int32)
        chunk_end_u = (off + n8) // SUBLANES
        exp_of = jnp.minimum(jnp.sum(chunk_end_u[:, None, :] <= grp[None, :, None], axis=2), N_EXPERTS - 1)
        seg_row0 = (blk_start * ROW_BLOCK)[None, :] + base - off
        comb_src = (jnp.take_along_axis(seg_row0, exp_of, axis=1) + grp[None, :] * SUBLANES)
        n_groups = chunk_end_u[:, N_EXPERTS - 1].astype(jnp.int32)
        comb_src = jnp.where(grp[None, :] < n_groups[:, None], comb_src, 0).astype(jnp.int32).reshape(-1)

        ys = _ffn_call(l, ffn_src, block_e, n_used, xs, w_gate, w_up, w_down)
        pos_tok = pos.transpose(0, 2, 1).reshape(n_tok, TOP_K)
        x = _combine_call(comb_src, n_groups, ys, pos_tok, x1.reshape(n_tok, d), ada[l],
                          ln2_g[l].reshape(1, -1), ln2_b[l].reshape(1, -1), alpha, seq).reshape(bsz, seq, d)
    return x
```

```python
import functools

import jax
import jax.numpy as jnp
from jax import lax
from jax.experimental import pallas as pl
from jax.experimental.pallas import tpu as pltpu

D_MODEL = 1024
HEAD_DIM = D_MODEL // 16
CONV_W = 6 * HEAD_DIM
POOL_W = 4 * HEAD_DIM
SGU_W = 6 * HEAD_DIM
D_IN = 3 * CONV_W + POOL_W + 2 * SGU_W
CONV_K = 3
POOL_WINDOWS = (2, 4, 8, 16)
CHUNK = 128
N_EXPERTS = 32
N_GROUPS = 4
EXPERTS_PER_GROUP = N_EXPERTS // N_GROUPS
TOP_K = 2
D_FF = D_MODEL // 2
ADA_CHUNKS = 6
LN_EPS = 1e-5

OFF_GB = 0
OFF_GC = CONV_W
OFF_P = 3 * CONV_W
OFF_U = OFF_P + POOL_W
OFF_V = OFF_U + SGU_W
YOFF_POOL = CONV_W
YOFF_SGU = CONV_W + POOL_W

LANES = 128
SUBLANES = 8
CONV_HALO = 8
POOL_HALO = 16
SEQ_TILE = 512
ROW_BLOCK = 256
ADA_TILE = 1536
VMEM_LIMIT = 60 * 1024 * 1024

SLOTS = TOP_K * SEQ_TILE + N_EXPERTS * SUBLANES
TILE_GROUPS = SLOTS // SUBLANES
BLOCK_GROUPS = ROW_BLOCK // SUBLANES
WAIT_BITS = TILE_GROUPS.bit_length()
PACK_W = D_MODEL // 2
XS_W = PACK_W + LANES
META_OFF, META_N8, META_BASE, META_END = 0, 1, 2, 3

F32 = jnp.float32
BF16 = jnp.bfloat16


def _layer_norm(r, g, b):
    mu = jnp.mean(r, axis=-1, keepdims=True)
    d = r - mu
    var = jnp.mean(d * d, axis=-1, keepdims=True)
    return d * lax.rsqrt(var + LN_EPS) * g + b


def _rows(ref, start, size):
    return ref.at[pl.ds(pl.multiple_of(start, SUBLANES), size), :]


def _pack_bf16_pairs(v):
    bits = pltpu.bitcast(v, jnp.int32)
    return bits[:, 0:PACK_W] | lax.shift_right_logical(bits[:, PACK_W:2 * PACK_W], 16)


def _unpack_bf16_pairs(words):
    first = pltpu.bitcast(words & jnp.int32(-65536), F32).astype(BF16)
    second = pltpu.bitcast(lax.shift_left(words, 16), F32).astype(BF16)
    return first, second


def _wait_groups(src_ref, dst_ref, groups, sem):
    for k in range(WAIT_BITS):
        size = SUBLANES << k

        @pl.when(((groups >> k) & 1) == 1)
        def _():
            pltpu.make_async_copy(_rows(src_ref, 0, size), _rows(dst_ref, 0, size), sem).wait()


def _ada_kernel(c_ref, w_ref, b_ref, o_ref):
    c = c_ref[...]
    c_act = (c * jax.nn.sigmoid(c)).astype(BF16)
    o_ref[0] = jnp.dot(c_act, w_ref[0].astype(BF16), preferred_element_type=F32) + b_ref[0]


def _ada_call(c, w_ada, b_ada):
    depth, d, n = w_ada.shape
    bsz = c.shape[0]
    return pl.pallas_call(
        _ada_kernel,
        grid=(depth, n // ADA_TILE),
        in_specs=[
            pl.BlockSpec((bsz, d), lambda l, j: (0, 0)),
            pl.BlockSpec((1, d, ADA_TILE), lambda l, j: (l, 0, j)),
            pl.BlockSpec((1, 1, ADA_TILE), lambda l, j: (l, 0, j)),
        ],
        out_specs=pl.BlockSpec((1, bsz, ADA_TILE), lambda l, j: (l, 0, j)),
        out_shape=jax.ShapeDtypeStruct((depth, bsz, n), F32),
        compiler_params=pltpu.CompilerParams(
            dimension_semantics=("arbitrary", "arbitrary"), vmem_limit_bytes=VMEM_LIMIT),
        name="ada",
    )(c, w_ada, b_ada.reshape(depth, 1, n))


def _top2_of_group(pg, sub_iota):
    big = float(EXPERTS_PER_GROUP)
    m1 = jnp.max(pg, axis=0, keepdims=True)
    i1 = jnp.min(jnp.where(pg == m1, sub_iota, big), axis=0, keepdims=True)
    rest = jnp.where(sub_iota == i1, -1.0, pg)
    m2 = jnp.max(rest, axis=0, keepdims=True)
    i2 = jnp.min(jnp.where(rest == m2, sub_iota, big), axis=0, keepdims=True)
    return m1, i1, m2, i2


def _split3(w):
    hi = w.astype(BF16)
    r1 = w - hi.astype(F32)
    mid = r1.astype(BF16)
    lo = (r1 - mid.astype(F32)).astype(BF16)
    return hi, mid, lo


def _mixer_kernel(x_ref, ada_ref, win_ref, wout_ref, convw_ref, convb_ref, poolw_ref, pscale_ref,
                  lng_ref, lnb_ref, sguw_ref, sgub_ref, ln1g_ref, ln1b_ref, wr_ref, br_ref,
                  x1_ref, pos_ref, meta_ref, xs_ref,
                  gx_scr, p_scr, y_scr, base_scr, *, ts, alpha):
    b = pl.program_id(0)
    j = pl.program_id(1)

    @pl.when(j == 0)
    def _():
        gx_scr[0:CONV_HALO, :] = jnp.zeros((CONV_HALO, CONV_W), F32)
        p_scr[0:POOL_HALO, :] = jnp.zeros((POOL_HALO, POOL_W), F32)

    @pl.when((b == 0) & (j == 0))
    def _():
        base_scr[...] = jnp.zeros_like(base_scr)

    x = x_ref[0]
    ada = ada_ref[0]
    sh1, sc1, g1 = ada[0:1], ada[1:2], ada[2:3]
    sh2, sc2 = ada[3:4], ada[4:5]
    hb = (x * (1.0 + sc1) + sh1).astype(BF16)

    def proj(lo, hi):
        return jnp.dot(hb, win_ref[:, lo:hi], preferred_element_type=F32)

    gcxc = proj(OFF_GC, OFF_P)
    g = gcxc[:, :CONV_W] * gcxc[:, CONV_W:]
    gx_scr[CONV_HALO:CONV_HALO + ts, :] = g
    cw = convw_ref[...]
    conv = (cw[0:1] * gx_scr[CONV_HALO - 2:CONV_HALO - 2 + ts, :]
            + cw[1:2] * gx_scr[CONV_HALO - 1:CONV_HALO - 1 + ts, :]
            + cw[2:3] * g + convb_ref[...])
    gx_scr[0:CONV_HALO, :] = g[ts - CONV_HALO:ts, :]
    y_scr[:, 0:CONV_W] = (proj(OFF_GB, OFF_GC) * conv).astype(BF16)

    p = proj(OFF_P, OFF_U)
    p_scr[POOL_HALO:POOL_HALO + ts, :] = p

    def shifted(k, lo):
        return p_scr[POOL_HALO - k:POOL_HALO - k + ts, lo:lo + LANES]

    s2 = p[:, 0:LANES] + shifted(1, 0)
    s4 = s2 + shifted(2, 0) + shifted(3, 0)
    acc = p[:, LANES:2 * LANES]
    for k in range(1, 8):
        acc = acc + shifted(k, LANES)
    s8 = acc
    for k in range(8, 16):
        acc = acc + shifted(k, LANES)
    s16 = acc
    p_scr[0:POOL_HALO, :] = p[ts - POOL_HALO:ts, :]
    lane = lax.broadcasted_iota(jnp.int32, (ts, LANES), 1)
    tpos = (lax.broadcasted_iota(jnp.int32, (ts, LANES), 0) + (j * ts + 1)).astype(F32)
    lo_half = lane < HEAD_DIM
    cnt_a = jnp.minimum(tpos, jnp.where(lo_half, float(POOL_WINDOWS[0]), float(POOL_WINDOWS[1])))
    cnt_b = jnp.minimum(tpos, jnp.where(lo_half, float(POOL_WINDOWS[2]), float(POOL_WINDOWS[3])))
    pooled = jnp.concatenate([jnp.where(lo_half, s2, s4) / cnt_a,
                              jnp.where(lo_half, s8, s16) / cnt_b], axis=1) - p
    mixed = jnp.dot(pooled.astype(BF16), poolw_ref[...], preferred_element_type=F32)
    y_scr[:, YOFF_POOL:YOFF_POOL + POOL_W] = (mixed * pscale_ref[...]).astype(BF16)

    v = proj(OFF_V, D_IN)
    vnb = _layer_norm(v, lng_ref[...], lnb_ref[...]).astype(BF16)
    u = proj(OFF_U, OFF_V)
    row_c = lax.broadcasted_iota(jnp.int32, (CHUNK, 2 * CHUNK), 0)
    col_c = lax.broadcasted_iota(jnp.int32, (CHUNK, 2 * CHUNK), 1)
    causal = (col_c & (CHUNK - 1)) <= row_c
    lo_lanes = lax.broadcasted_iota(jnp.int32, (CHUNK, LANES), 1) < HEAD_DIM
    zero_b = jnp.zeros((CHUNK, LANES), BF16)
    for hp in range(SGU_W // LANES):
        wl = jnp.where(causal, sguw_ref[hp], 0.0).astype(BF16)
        bias = sgub_ref[hp]
        for ci in range(ts // CHUNK):
            rs = slice(ci * CHUNK, (ci + 1) * CHUNK)
            vc = vnb[rs, hp * LANES:(hp + 1) * LANES]
            rhs = jnp.concatenate([jnp.where(lo_lanes, vc, zero_b),
                                   jnp.where(lo_lanes, zero_b, vc)], axis=0)
            mixed_c = jnp.dot(wl, rhs, preferred_element_type=F32) + bias
            y_scr[rs, YOFF_SGU + hp * LANES:YOFF_SGU + (hp + 1) * LANES] = (
                u[rs, hp * LANES:(hp + 1) * LANES] * mixed_c).astype(BF16)

    yo = jnp.dot(y_scr[...], wout_ref[...], preferred_element_type=F32)
    x1 = _layer_norm(alpha * x + g1 * yo, ln1g_ref[...], ln1b_ref[...])
    x1_ref[0] = x1
    h2b = (x1 * (1.0 + sc2) + sh2).astype(BF16)

    logits = jnp.dot(h2b, wr_ref[...], preferred_element_type=F32) + br_ref[...]
    lt = logits.T[0:N_EXPERTS, :]
    ex = jnp.exp(lt - jnp.max(lt, axis=0, keepdims=True))
    probs = ex / jnp.sum(ex, axis=0, keepdims=True)
    sub_iota = lax.broadcasted_iota(jnp.int32, (EXPERTS_PER_GROUP, ts), 0).astype(F32)
    best = None
    for gi in range(N_GROUPS):
        m1, i1, m2, i2 = _top2_of_group(
            probs[gi * EXPERTS_PER_GROUP:(gi + 1) * EXPERTS_PER_GROUP, :], sub_iota)
        score = m1 + m2
        cand = (score, m1, m2, i1 + float(gi * EXPERTS_PER_GROUP), i2 + float(gi * EXPERTS_PER_GROUP))
        if best is None:
            best = cand
        else:
            better = score > best[0]
            best = tuple(jnp.where(better, c, o) for c, o in zip(cand, best))
    _, p1, p2, e1, e2 = best
    den = p1 + p2
    w1 = p1 / den
    w2 = p2 / den

    ex_iota = lax.broadcasted_iota(jnp.int32, (N_EXPERTS, ts), 0).astype(F32)
    oh1 = ex_iota == e1
    oh2 = ex_iota == e2
    either = jnp.where(oh1 | oh2, 1.0, 0.0)
    before = (lax.broadcasted_iota(jnp.int32, (ts, ts), 0)
              < lax.broadcasted_iota(jnp.int32, (ts, ts), 1))
    tri = jnp.where(before, 1.0, 0.0).astype(BF16)
    seen = jnp.dot(either.astype(BF16), tri, preferred_element_type=F32)
    n_col = jnp.sum(either, axis=1, keepdims=True)
    n8_col = jnp.ceil(n_col * (1.0 / SUBLANES)) * float(SUBLANES)
    lower = (lax.broadcasted_iota(jnp.int32, (N_EXPERTS, N_EXPERTS), 1)
             < lax.broadcasted_iota(jnp.int32, (N_EXPERTS, N_EXPERTS), 0))
    off_col = jnp.dot(jnp.where(lower, 1.0, 0.0).astype(BF16),
                      jnp.broadcast_to(n8_col, (N_EXPERTS, LANES)).astype(BF16),
                      preferred_element_type=F32)[:, 0:1]
    slot = off_col + seen
    pos1 = jnp.sum(jnp.where(oh1, slot, 0.0), axis=0, keepdims=True)
    pos2 = jnp.sum(jnp.where(oh2, slot, 0.0), axis=0, keepdims=True)
    pos_ref[0] = jnp.concatenate([pos1, pos2], axis=0).astype(jnp.int32)

    base_col = base_scr[:, 0:1]
    end_col = base_col + n8_col
    base_scr[...] = jnp.broadcast_to(end_col, (N_EXPERTS, LANES))
    mlane = lax.broadcasted_iota(jnp.int32, (N_EXPERTS, LANES), 1)
    meta = jnp.where(mlane == META_OFF, off_col,
                     jnp.where(mlane == META_N8, n8_col,
                               jnp.where(mlane == META_BASE, base_col,
                                         jnp.where(mlane == META_END, end_col, 0.0)))).astype(jnp.int32)
    meta_ref[0] = meta

    slot_iota = lax.broadcasted_iota(jnp.int32, (SLOTS, ts), 0).astype(F32)
    sel1 = slot_iota == pos1
    sel2 = slot_iota == pos2
    p1f = jnp.where(sel1, 1.0, 0.0)
    p2f = jnp.where(sel2, 1.0, 0.0)
    perm = (p1f + p2f).astype(BF16)
    perm_k = jnp.concatenate([p1f.astype(BF16), p2f.astype(BF16)], axis=1)
    wcat = jnp.concatenate([w1, w2], axis=1)
    hi, mid, lo = (piece.astype(F32) for piece in _split3(wcat))
    wrow = lax.broadcasted_iota(jnp.int32, (LANES, TOP_K * ts), 0)
    wmat = jnp.where(wrow == 0, hi, jnp.where(wrow == 1, mid, jnp.where(wrow == 2, lo, 0.0))).astype(BF16)
    xs_ref[:, 0:PACK_W] = _pack_bf16_pairs(jnp.dot(perm, h2b, preferred_element_type=F32))
    xs_ref[:, PACK_W:XS_W] = pltpu.bitcast(
        lax.dot_general(perm_k, wmat, (((1,), (1,)), ((), ())), preferred_element_type=F32), jnp.int32)


def _mixer_call(x, ada_l, w_in, w_out, conv_w, conv_b, poolw_bd, pool_scale, ln_g, ln_b,
                sguw_pair, sgub_pair, ln1_g, ln1_b, wr_pad, br_pad, alpha):
    bsz, s, d = x.shape
    ts = SEQ_TILE
    nt = s // ts
    n_tiles = bsz * nt
    const2 = lambda b, j: (0, 0)
    const3 = lambda b, j: (0, 0, 0)
    tile3 = lambda b, j: (b, j, 0)
    meta3 = lambda b, j: (b * nt + j, 0, 0)
    once = pl.Buffered(1)
    kern = functools.partial(_mixer_kernel, ts=ts, alpha=alpha)
    return pl.pallas_call(
        kern,
        grid=(bsz, nt),
        in_specs=[
            pl.BlockSpec((1, ts, d), tile3),
            pl.BlockSpec((1, ADA_CHUNKS, d), lambda b, j: (b, 0, 0)),
            pl.BlockSpec((d, D_IN), const2, pipeline_mode=once),
            pl.BlockSpec((d, d), const2, pipeline_mode=once),
            pl.BlockSpec((CONV_K, CONV_W), const2),
            pl.BlockSpec((1, CONV_W), const2),
            pl.BlockSpec((POOL_W, POOL_W), const2),
            pl.BlockSpec((1, POOL_W), const2),
            pl.BlockSpec((1, SGU_W), const2),
            pl.BlockSpec((1, SGU_W), const2),
            pl.BlockSpec((SGU_W // LANES, CHUNK, 2 * CHUNK), const3),
            pl.BlockSpec((SGU_W // LANES, CHUNK, LANES), const3),
            pl.BlockSpec((1, d), const2),
            pl.BlockSpec((1, d), const2),
            pl.BlockSpec((d, LANES), const2),
            pl.BlockSpec((1, LANES), const2),
        ],
        out_specs=[
            pl.BlockSpec((1, ts, d), tile3),
            pl.BlockSpec((1, TOP_K, ts), meta3),
            pl.BlockSpec((1, N_EXPERTS, LANES), meta3),
            pl.BlockSpec((SLOTS, XS_W), lambda b, j: (b * nt + j, 0)),
        ],
        out_shape=[
            jax.ShapeDtypeStruct((bsz, s, d), F32),
            jax.ShapeDtypeStruct((n_tiles, TOP_K, ts), jnp.int32),
            jax.ShapeDtypeStruct((n_tiles, N_EXPERTS, LANES), jnp.int32),
            jax.ShapeDtypeStruct((n_tiles * SLOTS, XS_W), jnp.int32),
        ],
        scratch_shapes=[
            pltpu.VMEM((CONV_HALO + ts, CONV_W), F32),
            pltpu.VMEM((POOL_HALO + ts, POOL_W), F32),
            pltpu.VMEM((ts, d), BF16),
            pltpu.VMEM((N_EXPERTS, LANES), F32),
        ],
        compiler_params=pltpu.CompilerParams(
            dimension_semantics=("arbitrary", "arbitrary"), vmem_limit_bytes=VMEM_LIMIT),
        name="mixer_router",
    )(x, ada_l, w_in, w_out, conv_w, conv_b, poolw_bd, pool_scale, ln_g, ln_b,
      sguw_pair, sgub_pair, ln1_g, ln1_b, wr_pad, br_pad)


def _ffn_kernel(src_ref, b0_ref, nb_ref, tot_ref, xs_ref, wg_ref, wu_ref, wd_ref, ys_ref,
                xbuf, obuf, wg_scr, wu_scr, wd_scr, gsem, osem, *, n_blocks):
    e = pl.program_id(0)
    b0 = b0_ref[e]
    nb = nb_ref[e]
    n_used = tot_ref[0]

    def gather(g, slot):
        for r in range(BLOCK_GROUPS):
            pltpu.make_async_copy(_rows(xs_ref, src_ref[g * BLOCK_GROUPS + r], SUBLANES),
                                  xbuf.at[slot, pl.ds(r * SUBLANES, SUBLANES), :], gsem.at[slot]).start()

    def out_copy(g, slot):
        return pltpu.make_async_copy(obuf.at[slot], _rows(ys_ref, g * ROW_BLOCK, ROW_BLOCK), osem.at[slot])

    @pl.when(e == 0)
    def _():
        gather(0, 0)

    @pl.when(nb > 0)
    def _():
        wg_scr[...] = wg_ref[0, 0].astype(BF16)
        wu_scr[...] = wu_ref[0, 0].astype(BF16)
        wd_scr[...] = wd_ref[0, 0].astype(BF16)

    def block(k, carry):
        g = b0 + k
        slot = g % 2

        @pl.when(g + 1 < n_used)
        def _():
            gather(g + 1, 1 - slot)

        pltpu.make_async_copy(_rows(xs_ref, 0, ROW_BLOCK), xbuf.at[slot], gsem.at[slot]).wait()

        @pl.when(g >= 2)
        def _():
            out_copy(g - 2, slot).wait()

        xb = jnp.concatenate(_unpack_bf16_pairs(xbuf[slot, :, 0:PACK_W]), axis=1)
        wparts = pltpu.bitcast(xbuf[slot, :, PACK_W:XS_W], F32)
        wrow = (wparts[:, 0:1] + wparts[:, 1:2]) + wparts[:, 2:3]
        gate = jnp.dot(xb, wg_scr[...], preferred_element_type=F32)
        up = jnp.dot(xb, wu_scr[...], preferred_element_type=F32)
        act = (gate * jax.nn.sigmoid(gate) * up).astype(BF16)
        y = jnp.dot(act, wd_scr[...], preferred_element_type=F32) * wrow
        obuf[slot] = _pack_bf16_pairs(y.astype(BF16).astype(F32))
        out_copy(g, slot).start()
        return carry

    lax.fori_loop(0, nb, block, 0)

    @pl.when(e == pl.num_programs(0) - 1)
    def _():
        @pl.when(n_used >= 2)
        def _():
            out_copy(n_used - 2, n_used % 2).wait()

        out_copy(n_used - 1, (n_used - 1) % 2).wait()
        obuf[0] = jnp.zeros((ROW_BLOCK, PACK_W), jnp.int32)

        def fill(g, carry):
            out_copy(g, 0).start()
            return carry

        def drain(g, carry):
            out_copy(g, 0).wait()
            return carry

        lax.fori_loop(n_used, n_blocks, fill, 0)
        lax.fori_loop(n_used, n_blocks, drain, 0)


def _ffn_call(layer, group_src, blk_start, seg_blocks, n_used, n_blocks, xs, w_gate, w_up, w_down):
    d, f = w_gate.shape[-2:]
    w_map = lambda e, src, b0, nb, tot: (layer, e, 0, 0)
    return pl.pallas_call(
        functools.partial(_ffn_kernel, n_blocks=n_blocks),
        grid_spec=pltpu.PrefetchScalarGridSpec(
            num_scalar_prefetch=4,
            grid=(N_EXPERTS,),
            in_specs=[
                pl.BlockSpec(memory_space=pl.ANY),
                pl.BlockSpec((1, 1, d, f), w_map),
                pl.BlockSpec((1, 1, d, f), w_map),
                pl.BlockSpec((1, 1, f, d), w_map),
            ],
            out_specs=pl.BlockSpec(memory_space=pl.ANY),
            scratch_shapes=[
                pltpu.VMEM((2, ROW_BLOCK, XS_W), jnp.int32),
                pltpu.VMEM((2, ROW_BLOCK, PACK_W), jnp.int32),
                pltpu.VMEM((d, f), BF16),
                pltpu.VMEM((d, f), BF16),
                pltpu.VMEM((f, d), BF16),
                pltpu.SemaphoreType.DMA((2,)),
                pltpu.SemaphoreType.DMA((2,)),
            ],
        ),
        out_shape=jax.ShapeDtypeStruct((n_blocks * ROW_BLOCK, PACK_W), jnp.int32),
        compiler_params=pltpu.CompilerParams(
            dimension_semantics=("arbitrary",), vmem_limit_bytes=VMEM_LIMIT),
        name="expert_ffn",
    )(group_src, blk_start, seg_blocks, n_used, xs, w_gate, w_up, w_down)


def _combine_kernel(src_ref, ng_ref, ys_ref, pos_ref, x1_ref, ada_ref, g_ref, b_ref,
                    o_ref, buf, sem, *, ts, alpha):
    i = pl.program_id(0)
    n = pl.num_programs(0)
    cur = i % 2

    def gather(tile, slot):
        def issue(s, carry):
            pltpu.make_async_copy(_rows(ys_ref, src_ref[tile * TILE_GROUPS + s], SUBLANES),
                                  buf.at[slot, pl.ds(pl.multiple_of(s * SUBLANES, SUBLANES), SUBLANES), :],
                                  sem.at[slot]).start()
            return carry

        lax.fori_loop(0, ng_ref[tile], issue, 0)

    @pl.when(i == 0)
    def _():
        buf[...] = jnp.zeros_like(buf)
        gather(0, 0)

    @pl.when(i + 1 < n)
    def _():
        gather(i + 1, 1 - cur)

    _wait_groups(ys_ref, buf.at[cur], ng_ref[i], sem.at[cur])

    pos = pos_ref[...]
    slot_iota = lax.broadcasted_iota(jnp.int32, (ts, SLOTS), 1)
    pick = (slot_iota == pos[:, 0:1]) | (slot_iota == pos[:, 1:2])
    unsort = jnp.where(pick, 1.0, 0.0).astype(BF16)
    first, second = _unpack_bf16_pairs(buf[cur])
    y = jnp.concatenate([jnp.dot(unsort, first, preferred_element_type=F32),
                         jnp.dot(unsort, second, preferred_element_type=F32)], axis=1)
    g2 = ada_ref[0][5:6]
    o_ref[...] = _layer_norm(alpha * x1_ref[...] + g2 * y, g_ref[...], b_ref[...])


def _combine_call(group_src, n_groups, ys, pos_tok, x1_flat, ada_l, ln_g, ln_b, alpha, seq):
    n_tok, d = x1_flat.shape
    ts = SEQ_TILE
    n_tiles = n_tok // ts
    per_seq = seq // ts
    return pl.pallas_call(
        functools.partial(_combine_kernel, ts=ts, alpha=alpha),
        grid_spec=pltpu.PrefetchScalarGridSpec(
            num_scalar_prefetch=2,
            grid=(n_tiles,),
            in_specs=[
                pl.BlockSpec(memory_space=pl.ANY),
                pl.BlockSpec((ts, TOP_K), lambda i, src, ng: (i, 0)),
                pl.BlockSpec((ts, d), lambda i, src, ng: (i, 0)),
                pl.BlockSpec((1, ADA_CHUNKS, d), lambda i, src, ng: (i // per_seq, 0, 0)),
                pl.BlockSpec((1, d), lambda i, src, ng: (0, 0)),
                pl.BlockSpec((1, d), lambda i, src, ng: (0, 0)),
            ],
            out_specs=pl.BlockSpec((ts, d), lambda i, src, ng: (i, 0)),
            scratch_shapes=[
                pltpu.VMEM((2, SLOTS, PACK_W), jnp.int32),
                pltpu.SemaphoreType.DMA((2,)),
            ],
        ),
        out_shape=jax.ShapeDtypeStruct((n_tok, d), F32),
        compiler_params=pltpu.CompilerParams(
            dimension_semantics=("arbitrary",), vmem_limit_bytes=VMEM_LIMIT),
        name="combine_ln",
    )(group_src, n_groups, ys, pos_tok, x1_flat, ada_l, ln_g, ln_b)


def kernel(x, c, w_ada, b_ada, w_in, conv_w, conv_b, pool_w, pool_scale, sgu_ln_g, sgu_ln_b,
           sgu_w, sgu_b, w_out, ln1_g, ln1_b, w_router, b_router, w_gate, w_up, w_down,
           ln2_g, ln2_b):
    bsz, seq, d = x.shape
    depth = w_ada.shape[0]
    n_tok = bsz * seq
    n_tiles = n_tok // SEQ_TILE
    alpha = (2 * depth) ** 0.25
    assert d == D_MODEL and seq % SEQ_TILE == 0 and SEQ_TILE % CHUNK == 0

    ada = _ada_call(c, w_ada, b_ada).reshape(depth, bsz, ADA_CHUNKS, d)

    w_in_b = w_in.astype(BF16)
    w_out_b = w_out.astype(BF16)
    eye_g = jnp.eye(POOL_W // HEAD_DIM, dtype=F32)
    poolw_bd = jnp.einsum('lgcd,gh->lgchd', pool_w, eye_g).reshape(depth, POOL_W, POOL_W).astype(BF16)
    n_pairs = SGU_W // LANES
    sguw_pair = sgu_w.reshape(depth, n_pairs, 2, CHUNK, CHUNK).transpose(0, 1, 3, 2, 4).reshape(
        depth, n_pairs, CHUNK, 2 * CHUNK)
    sgub_pair = jnp.repeat(sgu_b.transpose(0, 2, 1), HEAD_DIM, axis=-1).reshape(
        depth, CHUNK, n_pairs, LANES).transpose(0, 2, 1, 3)
    wr_pad = jnp.pad(w_router, ((0, 0), (0, LANES - N_EXPERTS))).astype(BF16)
    br_pad = jnp.pad(b_router, (0, LANES - N_EXPERTS)).reshape(1, LANES)

    max_rows = n_tok * TOP_K + (SUBLANES - 1) * N_EXPERTS * n_tiles
    n_steps = -(-max_rows // ROW_BLOCK) + N_EXPERTS
    step = jnp.arange(n_steps, dtype=jnp.int32)
    zero_group = SLOTS - SUBLANES

    for l in range(depth):
        x1, pos, meta, xs = _mixer_call(
            x, ada[l], w_in_b[l], w_out_b[l], conv_w[l], conv_b[l].reshape(1, -1), poolw_bd[l],
            pool_scale[l].reshape(1, -1), sgu_ln_g[l].reshape(1, -1), sgu_ln_b[l].reshape(1, -1),
            sguw_pair[l], sgub_pair[l], ln1_g[l].reshape(1, -1), ln1_b[l].reshape(1, -1),
            wr_pad, br_pad, alpha)
        off = meta[:, :, META_OFF]
        n8 = meta[:, :, META_N8]
        base = meta[:, :, META_BASE]
        end = meta[:, :, META_END]
        seg_rows = end[n_tiles - 1]
        seg_blocks = (seg_rows + ROW_BLOCK - 1) // ROW_BLOCK
        blk_end = jnp.cumsum(seg_blocks)
        blk_start = blk_end - seg_blocks
        n_used = blk_end[-1:].astype(jnp.int32)
        active = step < n_used[0]
        block_e = jnp.minimum(jnp.sum(step[:, None] >= blk_end[None, :], axis=1), N_EXPERTS - 1)
        unit = ((step - blk_start[block_e]) * BLOCK_GROUPS)[:, None] + jnp.arange(BLOCK_GROUPS)[None, :]
        end_u = (end.T // SUBLANES)[block_e]
        tile_of = jnp.sum(end_u[:, None, :] <= unit[:, :, None], axis=2)
        valid = (tile_of < n_tiles) & active[:, None]
        tile_c = jnp.minimum(tile_of, n_tiles - 1)
        chunk_off = jnp.take_along_axis(off.T[block_e], tile_c, axis=1)
        chunk_base = jnp.take_along_axis(base.T[block_e], tile_c, axis=1)
        src = tile_c * SLOTS + chunk_off + unit * SUBLANES - chunk_base
        ffn_src = jnp.where(valid, src, zero_group).astype(jnp.int32).reshape(-1)
        grp = jnp.arange(TILE_GROUPS, dtype=jnp.int32)
        chunk_end_u = (off + n8) // SUBLANES
        exp_of = jnp.minimum(jnp.sum(chunk_end_u[:, None, :] <= grp[None, :, None], axis=2), N_EXPERTS - 1)
        seg_row0 = (blk_start * ROW_BLOCK)[None, :] + base - off
        comb_src = (jnp.take_along_axis(seg_row0, exp_of, axis=1) + grp[None, :] * SUBLANES)
        n_groups = chunk_end_u[:, N_EXPERTS - 1].astype(jnp.int32)
        comb_src = jnp.where(grp[None, :] < n_groups[:, None], comb_src, 0).astype(jnp.int32).reshape(-1)

        ys = _ffn_call(l, ffn_src, blk_start.astype(jnp.int32), seg_blocks.astype(jnp.int32), n_used, n_steps,
                       xs, w_gate, w_up, w_down)
        pos_tok = pos.transpose(0, 2, 1).reshape(n_tok, TOP_K)
        x = _combine_call(comb_src, n_groups, ys, pos_tok, x1.reshape(n_tok, d), ada[l],
                          ln2_g[l].reshape(1, -1), ln2_b[l].reshape(1, -1), alpha, seq).reshape(bsz, seq, d)
    return x
```

```python
import functools

import jax
import jax.numpy as jnp
from jax import lax
from jax.experimental import pallas as pl
from jax.experimental.pallas import tpu as pltpu

D_MODEL = 1024
HEAD_DIM = D_MODEL // 16
CONV_W = 6 * HEAD_DIM
POOL_W = 4 * HEAD_DIM
SGU_W = 6 * HEAD_DIM
D_IN = 3 * CONV_W + POOL_W + 2 * SGU_W
CONV_K = 3
POOL_WINDOWS = (2, 4, 8, 16)
CHUNK = 128
N_EXPERTS = 32
N_GROUPS = 4
EXPERTS_PER_GROUP = N_EXPERTS // N_GROUPS
TOP_K = 2
D_FF = D_MODEL // 2
ADA_CHUNKS = 6
LN_EPS = 1e-5

OFF_GB = 0
OFF_GC = CONV_W
OFF_P = 3 * CONV_W
OFF_U = OFF_P + POOL_W
OFF_V = OFF_U + SGU_W
YOFF_POOL = CONV_W
YOFF_SGU = CONV_W + POOL_W

LANES = 128
SUBLANES = 8
CONV_HALO = 8
POOL_HALO = 16
SEQ_TILE = 512
ROW_BLOCK = 256
ADA_TILE = 1536
GATHER_DEPTH = 3
WEIGHT_DMA_PRIORITY = 1
VMEM_LIMIT = 60 * 1024 * 1024

SLOTS = TOP_K * SEQ_TILE + N_EXPERTS * SUBLANES
TILE_GROUPS = SLOTS // SUBLANES
BLOCK_GROUPS = ROW_BLOCK // SUBLANES
WAIT_BITS = TILE_GROUPS.bit_length()
PACK_W = D_MODEL // 2
XS_W = PACK_W + LANES
META_OFF, META_N8, META_BASE, META_END = 0, 1, 2, 3

F32 = jnp.float32
BF16 = jnp.bfloat16


def _layer_norm(r, g, b):
    mu = jnp.mean(r, axis=-1, keepdims=True)
    d = r - mu
    var = jnp.mean(d * d, axis=-1, keepdims=True)
    return d * lax.rsqrt(var + LN_EPS) * g + b


def _rows(ref, start, size):
    return ref.at[pl.ds(pl.multiple_of(start, SUBLANES), size), :]


def _pack_bf16_pairs(v):
    bits = pltpu.bitcast(v, jnp.int32)
    return bits[:, 0:PACK_W] | lax.shift_right_logical(bits[:, PACK_W:2 * PACK_W], 16)


def _unpack_bf16_pairs(words):
    first = pltpu.bitcast(words & jnp.int32(-65536), F32).astype(BF16)
    second = pltpu.bitcast(lax.shift_left(words, 16), F32).astype(BF16)
    return first, second


def _wait_groups(src_ref, dst_ref, groups, sem):
    for k in range(WAIT_BITS):
        size = SUBLANES << k

        @pl.when(((groups >> k) & 1) == 1)
        def _():
            pltpu.make_async_copy(_rows(src_ref, 0, size), _rows(dst_ref, 0, size), sem).wait()


def _ada_kernel(c_ref, w_ref, b_ref, o_ref):
    c = c_ref[...]
    c_act = (c * jax.nn.sigmoid(c)).astype(BF16)
    o_ref[0] = jnp.dot(c_act, w_ref[0].astype(BF16), preferred_element_type=F32) + b_ref[0]


def _ada_call(c, w_ada, b_ada):
    depth, d, n = w_ada.shape
    bsz = c.shape[0]
    return pl.pallas_call(
        _ada_kernel,
        grid=(depth, n // ADA_TILE),
        in_specs=[
            pl.BlockSpec((bsz, d), lambda l, j: (0, 0)),
            pl.BlockSpec((1, d, ADA_TILE), lambda l, j: (l, 0, j)),
            pl.BlockSpec((1, 1, ADA_TILE), lambda l, j: (l, 0, j)),
        ],
        out_specs=pl.BlockSpec((1, bsz, ADA_TILE), lambda l, j: (l, 0, j)),
        out_shape=jax.ShapeDtypeStruct((depth, bsz, n), F32),
        compiler_params=pltpu.CompilerParams(
            dimension_semantics=("arbitrary", "arbitrary"), vmem_limit_bytes=VMEM_LIMIT),
        name="ada",
    )(c, w_ada, b_ada.reshape(depth, 1, n))


def _top2_of_group(pg, sub_iota):
    big = float(EXPERTS_PER_GROUP)
    m1 = jnp.max(pg, axis=0, keepdims=True)
    i1 = jnp.min(jnp.where(pg == m1, sub_iota, big), axis=0, keepdims=True)
    rest = jnp.where(sub_iota == i1, -1.0, pg)
    m2 = jnp.max(rest, axis=0, keepdims=True)
    i2 = jnp.min(jnp.where(rest == m2, sub_iota, big), axis=0, keepdims=True)
    return m1, i1, m2, i2


def _mixer_kernel(x_ref, ada_ref, win_ref, wout_ref, convw_ref, convb_ref, poolw_ref, pscale_ref,
                  lng_ref, lnb_ref, sguw_ref, sgub_ref, ln1g_ref, ln1b_ref, wr_ref, br_ref, tri_ref,
                  x1_ref, pos_ref, meta_ref, xs_ref,
                  gx_scr, p_scr, y_scr, base_scr, *, ts, alpha):
    b = pl.program_id(0)
    j = pl.program_id(1)

    @pl.when(j == 0)
    def _():
        gx_scr[0:CONV_HALO, :] = jnp.zeros((CONV_HALO, CONV_W), F32)
        p_scr[0:POOL_HALO, :] = jnp.zeros((POOL_HALO, POOL_W), F32)

    @pl.when((b == 0) & (j == 0))
    def _():
        base_scr[...] = jnp.zeros_like(base_scr)

    x = x_ref[0]
    ada = ada_ref[0]
    sh1, sc1, g1 = ada[0:1], ada[1:2], ada[2:3]
    sh2, sc2 = ada[3:4], ada[4:5]
    hb = (x * (1.0 + sc1) + sh1).astype(BF16)

    def proj(lo, hi):
        return jnp.dot(hb, win_ref[:, lo:hi], preferred_element_type=F32)

    gcxc = proj(OFF_GC, OFF_P)
    g = gcxc[:, :CONV_W] * gcxc[:, CONV_W:]
    gx_scr[CONV_HALO:CONV_HALO + ts, :] = g
    cw = convw_ref[...]
    conv = (cw[0:1] * gx_scr[CONV_HALO - 2:CONV_HALO - 2 + ts, :]
            + cw[1:2] * gx_scr[CONV_HALO - 1:CONV_HALO - 1 + ts, :]
            + cw[2:3] * g + convb_ref[...])
    gx_scr[0:CONV_HALO, :] = g[ts - CONV_HALO:ts, :]
    y_scr[:, 0:CONV_W] = (proj(OFF_GB, OFF_GC) * conv).astype(BF16)

    p = proj(OFF_P, OFF_U)
    p_scr[POOL_HALO:POOL_HALO + ts, :] = p

    def shifted(k, lo):
        return p_scr[POOL_HALO - k:POOL_HALO - k + ts, lo:lo + LANES]

    s2 = p[:, 0:LANES] + shifted(1, 0)
    s4 = s2 + shifted(2, 0) + shifted(3, 0)
    acc = p[:, LANES:2 * LANES]
    for k in range(1, 8):
        acc = acc + shifted(k, LANES)
    s8 = acc
    for k in range(8, 16):
        acc = acc + shifted(k, LANES)
    s16 = acc
    p_scr[0:POOL_HALO, :] = p[ts - POOL_HALO:ts, :]
    lane = lax.broadcasted_iota(jnp.int32, (ts, LANES), 1)
    tpos = (lax.broadcasted_iota(jnp.int32, (ts, LANES), 0) + (j * ts + 1)).astype(F32)
    lo_half = lane < HEAD_DIM
    cnt_a = jnp.minimum(tpos, jnp.where(lo_half, float(POOL_WINDOWS[0]), float(POOL_WINDOWS[1])))
    cnt_b = jnp.minimum(tpos, jnp.where(lo_half, float(POOL_WINDOWS[2]), float(POOL_WINDOWS[3])))
    pooled = jnp.concatenate([jnp.where(lo_half, s2, s4) / cnt_a,
                              jnp.where(lo_half, s8, s16) / cnt_b], axis=1) - p
    mixed = jnp.dot(pooled.astype(BF16), poolw_ref[...], preferred_element_type=F32)
    y_scr[:, YOFF_POOL:YOFF_POOL + POOL_W] = (mixed * pscale_ref[...]).astype(BF16)

    v = proj(OFF_V, D_IN)
    vnb = _layer_norm(v, lng_ref[...], lnb_ref[...]).astype(BF16)
    u = proj(OFF_U, OFF_V)
    row_c = lax.broadcasted_iota(jnp.int32, (CHUNK, 2 * CHUNK), 0)
    col_c = lax.broadcasted_iota(jnp.int32, (CHUNK, 2 * CHUNK), 1)
    causal = (col_c & (CHUNK - 1)) <= row_c
    lo_lanes = lax.broadcasted_iota(jnp.int32, (CHUNK, LANES), 1) < HEAD_DIM
    zero_b = jnp.zeros((CHUNK, LANES), BF16)
    for hp in range(SGU_W // LANES):
        wl = jnp.where(causal, sguw_ref[hp], 0.0).astype(BF16)
        bias = sgub_ref[hp]
        for ci in range(ts // CHUNK):
            rs = slice(ci * CHUNK, (ci + 1) * CHUNK)
            vc = vnb[rs, hp * LANES:(hp + 1) * LANES]
            rhs = jnp.concatenate([jnp.where(lo_lanes, vc, zero_b),
                                   jnp.where(lo_lanes, zero_b, vc)], axis=0)
            mixed_c = jnp.dot(wl, rhs, preferred_element_type=F32) + bias
            y_scr[rs, YOFF_SGU + hp * LANES:YOFF_SGU + (hp + 1) * LANES] = (
                u[rs, hp * LANES:(hp + 1) * LANES] * mixed_c).astype(BF16)

    yo = jnp.dot(y_scr[...], wout_ref[...], preferred_element_type=F32)
    x1 = _layer_norm(alpha * x + g1 * yo, ln1g_ref[...], ln1b_ref[...])
    x1_ref[0] = x1
    h2b = (x1 * (1.0 + sc2) + sh2).astype(BF16)

    logits = jnp.dot(h2b, wr_ref[...], preferred_element_type=F32) + br_ref[...]
    lt = logits.T[0:N_EXPERTS, :]
    ex = jnp.exp(lt - jnp.max(lt, axis=0, keepdims=True))
    probs = ex / jnp.sum(ex, axis=0, keepdims=True)
    sub_iota = lax.broadcasted_iota(jnp.int32, (EXPERTS_PER_GROUP, ts), 0).astype(F32)
    best = None
    for gi in range(N_GROUPS):
        m1, i1, m2, i2 = _top2_of_group(
            probs[gi * EXPERTS_PER_GROUP:(gi + 1) * EXPERTS_PER_GROUP, :], sub_iota)
        score = m1 + m2
        cand = (score, m1, m2, i1 + float(gi * EXPERTS_PER_GROUP), i2 + float(gi * EXPERTS_PER_GROUP))
        if best is None:
            best = cand
        else:
            better = score > best[0]
            best = tuple(jnp.where(better, c, o) for c, o in zip(cand, best))
    _, p1, p2, e1, e2 = best
    den = p1 + p2
    w1 = p1 / den
    w2 = p2 / den

    ex_iota = lax.broadcasted_iota(jnp.int32, (N_EXPERTS, ts), 0).astype(F32)
    oh1 = ex_iota == e1
    oh2 = ex_iota == e2
    either = jnp.where(oh1 | oh2, 1.0, 0.0)
    seen = jnp.dot(either.astype(BF16), tri_ref[...], preferred_element_type=F32)
    n_col = jnp.sum(either, axis=1, keepdims=True)
    n8_col = jnp.ceil(n_col * (1.0 / SUBLANES)) * float(SUBLANES)
    lower = (lax.broadcasted_iota(jnp.int32, (N_EXPERTS, N_EXPERTS), 1)
             < lax.broadcasted_iota(jnp.int32, (N_EXPERTS, N_EXPERTS), 0))
    off_col = jnp.dot(jnp.where(lower, 1.0, 0.0).astype(BF16),
                      jnp.broadcast_to(n8_col, (N_EXPERTS, LANES)).astype(BF16),
                      preferred_element_type=F32)[:, 0:1]
    slot = off_col + seen
    pos1 = jnp.sum(jnp.where(oh1, slot, 0.0), axis=0, keepdims=True)
    pos2 = jnp.sum(jnp.where(oh2, slot, 0.0), axis=0, keepdims=True)
    pos_ref[0] = jnp.concatenate([pos1, pos2], axis=0).astype(jnp.int32)

    base_col = base_scr[:, 0:1]
    end_col = base_col + n8_col
    base_scr[...] = jnp.broadcast_to(end_col, (N_EXPERTS, LANES))
    mlane = lax.broadcasted_iota(jnp.int32, (N_EXPERTS, LANES), 1)
    meta = jnp.where(mlane == META_OFF, off_col,
                     jnp.where(mlane == META_N8, n8_col,
                               jnp.where(mlane == META_BASE, base_col,
                                         jnp.where(mlane == META_END, end_col, 0.0)))).astype(jnp.int32)
    meta_ref[0] = meta

    slot_iota = lax.broadcasted_iota(jnp.int32, (SLOTS, ts), 0).astype(F32)
    sel1 = slot_iota == pos1
    sel2 = slot_iota == pos2
    p1f = jnp.where(sel1, 1.0, 0.0)
    p2f = jnp.where(sel2, 1.0, 0.0)
    perm = (p1f + p2f).astype(BF16)
    xs_ref[:, 0:PACK_W] = _pack_bf16_pairs(jnp.dot(perm, h2b, preferred_element_type=F32))
    w_slot = jnp.sum(jnp.where(sel1, w1, jnp.where(sel2, w2, 0.0)), axis=1, keepdims=True)
    xs_ref[:, PACK_W:XS_W] = pltpu.bitcast(jnp.broadcast_to(w_slot, (SLOTS, LANES)), jnp.int32)


def _mixer_call(x, ada_l, w_in, w_out, conv_w, conv_b, poolw_bd, pool_scale, ln_g, ln_b,
                sguw_pair, sgub_pair, ln1_g, ln1_b, wr_pad, br_pad, tri, alpha):
    bsz, s, d = x.shape
    ts = SEQ_TILE
    nt = s // ts
    n_tiles = bsz * nt
    const2 = lambda b, j: (0, 0)
    const3 = lambda b, j: (0, 0, 0)
    tile3 = lambda b, j: (b, j, 0)
    meta3 = lambda b, j: (b * nt + j, 0, 0)
    once = pl.Buffered(1)
    kern = functools.partial(_mixer_kernel, ts=ts, alpha=alpha)
    return pl.pallas_call(
        kern,
        grid=(bsz, nt),
        in_specs=[
            pl.BlockSpec((1, ts, d), tile3),
            pl.BlockSpec((1, ADA_CHUNKS, d), lambda b, j: (b, 0, 0)),
            pl.BlockSpec((d, D_IN), const2, pipeline_mode=once),
            pl.BlockSpec((d, d), const2, pipeline_mode=once),
            pl.BlockSpec((CONV_K, CONV_W), const2),
            pl.BlockSpec((1, CONV_W), const2),
            pl.BlockSpec((POOL_W, POOL_W), const2),
            pl.BlockSpec((1, POOL_W), const2),
            pl.BlockSpec((1, SGU_W), const2),
            pl.BlockSpec((1, SGU_W), const2),
            pl.BlockSpec((SGU_W // LANES, CHUNK, 2 * CHUNK), const3),
            pl.BlockSpec((SGU_W // LANES, CHUNK, LANES), const3),
            pl.BlockSpec((1, d), const2),
            pl.BlockSpec((1, d), const2),
            pl.BlockSpec((d, LANES), const2),
            pl.BlockSpec((1, LANES), const2),
            pl.BlockSpec((ts, ts), const2, pipeline_mode=once),
        ],
        out_specs=[
            pl.BlockSpec((1, ts, d), tile3),
            pl.BlockSpec((1, TOP_K, ts), meta3),
            pl.BlockSpec((1, N_EXPERTS, LANES), meta3),
            pl.BlockSpec((SLOTS, XS_W), lambda b, j: (b * nt + j, 0)),
        ],
        out_shape=[
            jax.ShapeDtypeStruct((bsz, s, d), F32),
            jax.ShapeDtypeStruct((n_tiles, TOP_K, ts), jnp.int32),
            jax.ShapeDtypeStruct((n_tiles, N_EXPERTS, LANES), jnp.int32),
            jax.ShapeDtypeStruct((n_tiles * SLOTS, XS_W), jnp.int32),
        ],
        scratch_shapes=[
            pltpu.VMEM((CONV_HALO + ts, CONV_W), F32),
            pltpu.VMEM((POOL_HALO + ts, POOL_W), F32),
            pltpu.VMEM((ts, d), BF16),
            pltpu.VMEM((N_EXPERTS, LANES), F32),
        ],
        compiler_params=pltpu.CompilerParams(
            dimension_semantics=("arbitrary", "arbitrary"), vmem_limit_bytes=VMEM_LIMIT),
        name="mixer_router",
    )(x, ada_l, w_in, w_out, conv_w, conv_b, poolw_bd, pool_scale, ln_g, ln_b,
      sguw_pair, sgub_pair, ln1_g, ln1_b, wr_pad, br_pad, tri)


def _ffn_kernel(src_ref, b0_ref, nb_ref, tot_ref, xs_ref, wg_ref, wu_ref, wd_ref, ys_ref,
                xbuf, obuf, wbuf_g, wbuf_u, wbuf_d, wg_scr, wu_scr, wd_scr, gsem, osem, wsem,
                *, layer, n_blocks):
    e = pl.program_id(0)
    n_exp = pl.num_programs(0)
    b0 = b0_ref[e]
    nb = nb_ref[e]
    n_used = tot_ref[0]

    def gather(g):
        slot = g % GATHER_DEPTH
        for r in range(BLOCK_GROUPS):
            pltpu.make_async_copy(_rows(xs_ref, src_ref[g * BLOCK_GROUPS + r], SUBLANES),
                                  xbuf.at[slot, pl.ds(r * SUBLANES, SUBLANES), :], gsem.at[slot]).start()

    def out_copy(g, slot):
        return pltpu.make_async_copy(obuf.at[slot], _rows(ys_ref, g * ROW_BLOCK, ROW_BLOCK), osem.at[slot])

    def weight_copies(expert, slot):
        return [pltpu.make_async_copy(src.at[layer, expert], dst.at[slot], wsem.at[slot])
                for src, dst in ((wg_ref, wbuf_g), (wu_ref, wbuf_u), (wd_ref, wbuf_d))]

    @pl.when(e == 0)
    def _():
        for cp in weight_copies(0, 0):
            cp.start(priority=WEIGHT_DMA_PRIORITY)
        for ahead in range(GATHER_DEPTH - 1):
            @pl.when(ahead < n_used)
            def _():
                gather(ahead)

    @pl.when(e + 1 < n_exp)
    def _():
        for cp in weight_copies(e + 1, (e + 1) % 2):
            cp.start(priority=WEIGHT_DMA_PRIORITY)

    wslot = e % 2
    for cp in weight_copies(e, wslot):
        cp.wait()

    @pl.when(nb > 0)
    def _():
        wg_scr[...] = wbuf_g[wslot].astype(BF16)
        wu_scr[...] = wbuf_u[wslot].astype(BF16)
        wd_scr[...] = wbuf_d[wslot].astype(BF16)

    def block(k, carry):
        g = b0 + k
        islot = g % GATHER_DEPTH
        oslot = g % 2

        @pl.when(g + (GATHER_DEPTH - 1) < n_used)
        def _():
            gather(g + (GATHER_DEPTH - 1))

        pltpu.make_async_copy(_rows(xs_ref, 0, ROW_BLOCK), xbuf.at[islot], gsem.at[islot]).wait()

        @pl.when(g >= 2)
        def _():
            out_copy(g - 2, oslot).wait()

        xb = jnp.concatenate(_unpack_bf16_pairs(xbuf[islot, :, 0:PACK_W]), axis=1)
        wrow = pltpu.bitcast(xbuf[islot, :, PACK_W:XS_W], F32)[:, 0:1]
        gate = jnp.dot(xb, wg_scr[...], preferred_element_type=F32)
        up = jnp.dot(xb, wu_scr[...], preferred_element_type=F32)
        act = (gate * jax.nn.sigmoid(gate) * up).astype(BF16)
        y = jnp.dot(act, wd_scr[...], preferred_element_type=F32) * wrow
        obuf[oslot] = _pack_bf16_pairs(y.astype(BF16).astype(F32))
        out_copy(g, oslot).start()
        return carry

    lax.fori_loop(0, nb, block, 0)

    @pl.when(e == n_exp - 1)
    def _():
        @pl.when(n_used >= 2)
        def _():
            out_copy(n_used - 2, n_used % 2).wait()

        out_copy(n_used - 1, (n_used - 1) % 2).wait()
        obuf[0] = jnp.zeros((ROW_BLOCK, PACK_W), jnp.int32)

        def fill(g, carry):
            out_copy(g, 0).start()
            return carry

        def drain(g, carry):
            out_copy(g, 0).wait()
            return carry

        lax.fori_loop(n_used, n_blocks, fill, 0)
        lax.fori_loop(n_used, n_blocks, drain, 0)


def _ffn_call(layer, group_src, blk_start, seg_blocks, n_used, n_blocks, xs, w_gate, w_up, w_down):
    d, f = w_gate.shape[-2:]
    any_spec = pl.BlockSpec(memory_space=pl.ANY)
    return pl.pallas_call(
        functools.partial(_ffn_kernel, layer=layer, n_blocks=n_blocks),
        grid_spec=pltpu.PrefetchScalarGridSpec(
            num_scalar_prefetch=4,
            grid=(N_EXPERTS,),
            in_specs=[any_spec, any_spec, any_spec, any_spec],
            out_specs=any_spec,
            scratch_shapes=[
                pltpu.VMEM((GATHER_DEPTH, ROW_BLOCK, XS_W), jnp.int32),
                pltpu.VMEM((2, ROW_BLOCK, PACK_W), jnp.int32),
                pltpu.VMEM((2, d, f), F32),
                pltpu.VMEM((2, d, f), F32),
                pltpu.VMEM((2, f, d), F32),
                pltpu.VMEM((d, f), BF16),
                pltpu.VMEM((d, f), BF16),
                pltpu.VMEM((f, d), BF16),
                pltpu.SemaphoreType.DMA((GATHER_DEPTH,)),
                pltpu.SemaphoreType.DMA((2,)),
                pltpu.SemaphoreType.DMA((2,)),
            ],
        ),
        out_shape=jax.ShapeDtypeStruct((n_blocks * ROW_BLOCK, PACK_W), jnp.int32),
        compiler_params=pltpu.CompilerParams(
            dimension_semantics=("arbitrary",), vmem_limit_bytes=VMEM_LIMIT),
        name="expert_ffn",
    )(group_src, blk_start, seg_blocks, n_used, xs, w_gate, w_up, w_down)


def _combine_kernel(src_ref, ng_ref, ys_ref, pos_ref, x1_ref, ada_ref, g_ref, b_ref,
                    o_ref, buf, sem, *, ts, alpha):
    i = pl.program_id(0)
    n = pl.num_programs(0)
    cur = i % 2

    def gather(tile, slot):
        def issue(s, carry):
            pltpu.make_async_copy(_rows(ys_ref, src_ref[tile * TILE_GROUPS + s], SUBLANES),
                                  buf.at[slot, pl.ds(pl.multiple_of(s * SUBLANES, SUBLANES), SUBLANES), :],
                                  sem.at[slot]).start()
            return carry

        lax.fori_loop(0, ng_ref[tile], issue, 0)

    @pl.when(i == 0)
    def _():
        buf[...] = jnp.zeros_like(buf)
        gather(0, 0)

    @pl.when(i + 1 < n)
    def _():
        gather(i + 1, 1 - cur)

    _wait_groups(ys_ref, buf.at[cur], ng_ref[i], sem.at[cur])

    pos = pos_ref[...]
    slot_iota = lax.broadcasted_iota(jnp.int32, (ts, SLOTS), 1)
    pick = (slot_iota == pos[:, 0:1]) | (slot_iota == pos[:, 1:2])
    unsort = jnp.where(pick, 1.0, 0.0).astype(BF16)
    first, second = _unpack_bf16_pairs(buf[cur])
    y = jnp.concatenate([jnp.dot(unsort, first, preferred_element_type=F32),
                         jnp.dot(unsort, second, preferred_element_type=F32)], axis=1)
    g2 = ada_ref[0][5:6]
    o_ref[...] = _layer_norm(alpha * x1_ref[...] + g2 * y, g_ref[...], b_ref[...])


def _combine_call(group_src, n_groups, ys, pos_tok, x1_flat, ada_l, ln_g, ln_b, alpha, seq):
    n_tok, d = x1_flat.shape
    ts = SEQ_TILE
    n_tiles = n_tok // ts
    per_seq = seq // ts
    return pl.pallas_call(
        functools.partial(_combine_kernel, ts=ts, alpha=alpha),
        grid_spec=pltpu.PrefetchScalarGridSpec(
            num_scalar_prefetch=2,
            grid=(n_tiles,),
            in_specs=[
                pl.BlockSpec(memory_space=pl.ANY),
                pl.BlockSpec((ts, TOP_K), lambda i, src, ng: (i, 0)),
                pl.BlockSpec((ts, d), lambda i, src, ng: (i, 0)),
                pl.BlockSpec((1, ADA_CHUNKS, d), lambda i, src, ng: (i // per_seq, 0, 0)),
                pl.BlockSpec((1, d), lambda i, src, ng: (0, 0)),
                pl.BlockSpec((1, d), lambda i, src, ng: (0, 0)),
            ],
            out_specs=pl.BlockSpec((ts, d), lambda i, src, ng: (i, 0)),
            scratch_shapes=[
                pltpu.VMEM((2, SLOTS, PACK_W), jnp.int32),
                pltpu.SemaphoreType.DMA((2,)),
            ],
        ),
        out_shape=jax.ShapeDtypeStruct((n_tok, d), F32),
        compiler_params=pltpu.CompilerParams(
            dimension_semantics=("arbitrary",), vmem_limit_bytes=VMEM_LIMIT),
        name="combine_ln",
    )(group_src, n_groups, ys, pos_tok, x1_flat, ada_l, ln_g, ln_b)


def kernel(x, c, w_ada, b_ada, w_in, conv_w, conv_b, pool_w, pool_scale, sgu_ln_g, sgu_ln_b,
           sgu_w, sgu_b, w_out, ln1_g, ln1_b, w_router, b_router, w_gate, w_up, w_down,
           ln2_g, ln2_b):
    bsz, seq, d = x.shape
    depth = w_ada.shape[0]
    n_tok = bsz * seq
    n_tiles = n_tok // SEQ_TILE
    alpha = (2 * depth) ** 0.25
    assert d == D_MODEL and seq % SEQ_TILE == 0 and SEQ_TILE % CHUNK == 0

    ada = _ada_call(c, w_ada, b_ada).reshape(depth, bsz, ADA_CHUNKS, d)

    w_in_b = w_in.astype(BF16)
    w_out_b = w_out.astype(BF16)
    eye_g = jnp.eye(POOL_W // HEAD_DIM, dtype=F32)
    poolw_bd = jnp.einsum('lgcd,gh->lgchd', pool_w, eye_g).reshape(depth, POOL_W, POOL_W).astype(BF16)
    n_pairs = SGU_W // LANES
    sguw_pair = sgu_w.reshape(depth, n_pairs, 2, CHUNK, CHUNK).transpose(0, 1, 3, 2, 4).reshape(
        depth, n_pairs, CHUNK, 2 * CHUNK)
    sgub_pair = jnp.repeat(sgu_b.transpose(0, 2, 1), HEAD_DIM, axis=-1).reshape(
        depth, CHUNK, n_pairs, LANES).transpose(0, 2, 1, 3)
    wr_pad = jnp.pad(w_router, ((0, 0), (0, LANES - N_EXPERTS))).astype(BF16)
    br_pad = jnp.pad(b_router, (0, LANES - N_EXPERTS)).reshape(1, LANES)

    max_rows = n_tok * TOP_K + (SUBLANES - 1) * N_EXPERTS * n_tiles
    n_steps = -(-max_rows // ROW_BLOCK) + N_EXPERTS
    step = jnp.arange(n_steps, dtype=jnp.int32)
    experts = jnp.arange(N_EXPERTS, dtype=jnp.int32)
    tiles = jnp.arange(n_tiles, dtype=jnp.int32)
    tri = jnp.triu(jnp.ones((SEQ_TILE, SEQ_TILE), BF16), k=1)
    zero_group = SLOTS - SUBLANES

    for l in range(depth):
        x1, pos, meta, xs = _mixer_call(
            x, ada[l], w_in_b[l], w_out_b[l], conv_w[l], conv_b[l].reshape(1, -1), poolw_bd[l],
            pool_scale[l].reshape(1, -1), sgu_ln_g[l].reshape(1, -1), sgu_ln_b[l].reshape(1, -1),
            sguw_pair[l], sgub_pair[l], ln1_g[l].reshape(1, -1), ln1_b[l].reshape(1, -1),
            wr_pad, br_pad, tri, alpha)
        off = meta[:, :, META_OFF]
        n8 = meta[:, :, META_N8]
        base = meta[:, :, META_BASE]
        end = meta[:, :, META_END]
        seg_rows = end[n_tiles - 1]
        seg_blocks = (seg_rows + ROW_BLOCK - 1) // ROW_BLOCK
        blk_end = jnp.cumsum(seg_blocks)
        blk_start = blk_end - seg_blocks
        n_used = blk_end[-1:].astype(jnp.int32)
        active = step < n_used[0]
        block_e = jnp.minimum(jnp.sum(step[:, None] >= blk_end[None, :], axis=1), N_EXPERTS - 1)
        is_e = block_e[:, None] == experts[None, :]

        def of_block_expert(tab):
            return jnp.sum(jnp.where(is_e[:, :, None], tab.T[None, :, :], 0), axis=1)

        first_unit = (step - jnp.sum(jnp.where(is_e, blk_start[None, :], 0), axis=1)) * BLOCK_GROUPS
        unit = first_unit[:, None] + jnp.arange(BLOCK_GROUPS)[None, :]
        tile_of = jnp.sum(of_block_expert(end // SUBLANES)[:, None, :] <= unit[:, :, None], axis=2)
        valid = (tile_of < n_tiles) & active[:, None]
        is_t = tile_of[:, :, None] == tiles[None, None, :]
        row0 = tiles[None, :] * SLOTS + of_block_expert(off) - of_block_expert(base)
        src = jnp.sum(jnp.where(is_t, row0[:, None, :], 0), axis=2) + unit * SUBLANES
        ffn_src = jnp.where(valid, src, zero_group).astype(jnp.int32).reshape(-1)
        grp = jnp.arange(TILE_GROUPS, dtype=jnp.int32)
        chunk_end_u = (off + n8) // SUBLANES
        exp_of = jnp.sum(chunk_end_u[:, None, :] <= grp[None, :, None], axis=2)
        seg_row0 = (blk_start * ROW_BLOCK)[None, :] + base - off
        comb_src = jnp.sum(jnp.where(exp_of[:, :, None] == experts[None, None, :], seg_row0[:, None, :], 0),
                           axis=2) + grp[None, :] * SUBLANES
        n_groups = chunk_end_u[:, N_EXPERTS - 1].astype(jnp.int32)
        comb_src = jnp.where(grp[None, :] < n_groups[:, None], comb_src, 0).astype(jnp.int32).reshape(-1)

        ys = _ffn_call(l, ffn_src, blk_start.astype(jnp.int32), seg_blocks.astype(jnp.int32), n_used, n_steps,
                       xs, w_gate, w_up, w_down)
        pos_tok = pos.transpose(0, 2, 1).reshape(n_tok, TOP_K)
        x = _combine_call(comb_src, n_groups, ys, pos_tok, x1.reshape(n_tok, d), ada[l],
                          ln2_g[l].reshape(1, -1), ln2_b[l].reshape(1, -1), alpha, seq).reshape(bsz, seq, d)
    return x
```

```python
import functools

import jax
import jax.numpy as jnp
from jax import lax
from jax.experimental import pallas as pl
from jax.experimental.pallas import tpu as pltpu

D_MODEL = 1024
HEAD_DIM = D_MODEL // 16
CONV_W = 6 * HEAD_DIM
POOL_W = 4 * HEAD_DIM
SGU_W = 6 * HEAD_DIM
D_IN = 3 * CONV_W + POOL_W + 2 * SGU_W
CONV_K = 3
POOL_WINDOWS = (2, 4, 8, 16)
CHUNK = 128
N_EXPERTS = 32
N_GROUPS = 4
EXPERTS_PER_GROUP = N_EXPERTS // N_GROUPS
TOP_K = 2
D_FF = D_MODEL // 2
ADA_CHUNKS = 6
LN_EPS = 1e-5

OFF_GB = 0
OFF_GC = CONV_W
OFF_P = 3 * CONV_W
OFF_U = OFF_P + POOL_W
OFF_V = OFF_U + SGU_W
YOFF_POOL = CONV_W
YOFF_SGU = CONV_W + POOL_W

LANES = 128
SUBLANES = 8
CONV_HALO = 8
POOL_HALO = 16
SEQ_TILE = 512
SUB_BLOCK = 256
ROW_BLOCK = 2 * SUB_BLOCK
ADA_TILE = 1536
COMBINE_ROWS = 128
GATHER_DEPTH = 3
WEIGHT_DMA_PRIORITY = 1
VMEM_LIMIT = 60 * 1024 * 1024

SLOTS = TOP_K * SEQ_TILE + N_EXPERTS * SUBLANES
TILE_GROUPS = SLOTS // SUBLANES
BLOCK_GROUPS = ROW_BLOCK // SUBLANES
PACK_W = D_MODEL // 2
XS_W = PACK_W + LANES
META_OFF, META_N8, META_BASE, META_END = 0, 1, 2, 3

F32 = jnp.float32
BF16 = jnp.bfloat16


def _layer_norm(r, g, b):
    mu = jnp.mean(r, axis=-1, keepdims=True)
    d = r - mu
    var = jnp.mean(d * d, axis=-1, keepdims=True)
    return d * lax.rsqrt(var + LN_EPS) * g + b


def _rows(ref, start, size):
    return ref.at[pl.ds(pl.multiple_of(start, SUBLANES), size), :]


def _pack_bf16_pairs(v):
    bits = pltpu.bitcast(v, jnp.int32)
    return bits[:, 0:PACK_W] | lax.shift_right_logical(bits[:, PACK_W:2 * PACK_W], 16)


def _unpack_bf16_pairs(words):
    first = pltpu.bitcast(words & jnp.int32(-65536), F32).astype(BF16)
    second = pltpu.bitcast(lax.shift_left(words, 16), F32).astype(BF16)
    return first, second


def _ada_kernel(c_ref, w_ref, b_ref, o_ref):
    c = c_ref[...]
    c_act = (c * jax.nn.sigmoid(c)).astype(BF16)
    o_ref[0] = jnp.dot(c_act, w_ref[0].astype(BF16), preferred_element_type=F32) + b_ref[0]


def _ada_call(c, w_ada, b_ada):
    depth, d, n = w_ada.shape
    bsz = c.shape[0]
    return pl.pallas_call(
        _ada_kernel,
        grid=(depth, n // ADA_TILE),
        in_specs=[
            pl.BlockSpec((bsz, d), lambda l, j: (0, 0)),
            pl.BlockSpec((1, d, ADA_TILE), lambda l, j: (l, 0, j)),
            pl.BlockSpec((1, 1, ADA_TILE), lambda l, j: (l, 0, j)),
        ],
        out_specs=pl.BlockSpec((1, bsz, ADA_TILE), lambda l, j: (l, 0, j)),
        out_shape=jax.ShapeDtypeStruct((depth, bsz, n), F32),
        compiler_params=pltpu.CompilerParams(
            dimension_semantics=("arbitrary", "arbitrary"), vmem_limit_bytes=VMEM_LIMIT),
        name="ada",
    )(c, w_ada, b_ada.reshape(depth, 1, n))


def _top2_of_group(pg, sub_iota):
    big = float(EXPERTS_PER_GROUP)
    m1 = jnp.max(pg, axis=0, keepdims=True)
    i1 = jnp.min(jnp.where(pg == m1, sub_iota, big), axis=0, keepdims=True)
    rest = jnp.where(sub_iota == i1, -1.0, pg)
    m2 = jnp.max(rest, axis=0, keepdims=True)
    i2 = jnp.min(jnp.where(rest == m2, sub_iota, big), axis=0, keepdims=True)
    return m1, i1, m2, i2


def _mixer_kernel(x_ref, ada_ref, win_ref, wout_ref, convw_ref, convb_ref, poolw_ref, pscale_ref,
                  lng_ref, lnb_ref, sguw_ref, sgub_ref, ln1g_ref, ln1b_ref, wr_ref, br_ref, tri_ref,
                  x1_ref, pos_ref, meta_ref, xs_ref,
                  gx_scr, p_scr, y_scr, base_scr, *, ts, alpha):
    b = pl.program_id(0)
    j = pl.program_id(1)

    @pl.when(j == 0)
    def _():
        gx_scr[0:CONV_HALO, :] = jnp.zeros((CONV_HALO, CONV_W), F32)
        p_scr[0:POOL_HALO, :] = jnp.zeros((POOL_HALO, POOL_W), F32)

    @pl.when((b == 0) & (j == 0))
    def _():
        base_scr[...] = jnp.zeros_like(base_scr)

    x = x_ref[0]
    ada = ada_ref[0]
    sh1, sc1, g1 = ada[0:1], ada[1:2], ada[2:3]
    sh2, sc2 = ada[3:4], ada[4:5]
    hb = (x * (1.0 + sc1) + sh1).astype(BF16)

    def proj(lo, hi):
        return jnp.dot(hb, win_ref[:, lo:hi], preferred_element_type=F32)

    gcxc = proj(OFF_GC, OFF_P)
    g = gcxc[:, :CONV_W] * gcxc[:, CONV_W:]
    gx_scr[CONV_HALO:CONV_HALO + ts, :] = g
    cw = convw_ref[...]
    conv = (cw[0:1] * gx_scr[CONV_HALO - 2:CONV_HALO - 2 + ts, :]
            + cw[1:2] * gx_scr[CONV_HALO - 1:CONV_HALO - 1 + ts, :]
            + cw[2:3] * g + convb_ref[...])
    gx_scr[0:CONV_HALO, :] = g[ts - CONV_HALO:ts, :]
    y_scr[:, 0:CONV_W] = (proj(OFF_GB, OFF_GC) * conv).astype(BF16)

    p = proj(OFF_P, OFF_U)
    p_scr[POOL_HALO:POOL_HALO + ts, :] = p

    def shifted(k, lo):
        return p_scr[POOL_HALO - k:POOL_HALO - k + ts, lo:lo + LANES]

    s2 = p[:, 0:LANES] + shifted(1, 0)
    s4 = s2 + shifted(2, 0) + shifted(3, 0)
    acc = p[:, LANES:2 * LANES]
    for k in range(1, 8):
        acc = acc + shifted(k, LANES)
    s8 = acc
    for k in range(8, 16):
        acc = acc + shifted(k, LANES)
    s16 = acc
    p_scr[0:POOL_HALO, :] = p[ts - POOL_HALO:ts, :]
    lane = lax.broadcasted_iota(jnp.int32, (ts, LANES), 1)
    tpos = (lax.broadcasted_iota(jnp.int32, (ts, LANES), 0) + (j * ts + 1)).astype(F32)
    lo_half = lane < HEAD_DIM
    cnt_a = jnp.minimum(tpos, jnp.where(lo_half, float(POOL_WINDOWS[0]), float(POOL_WINDOWS[1])))
    cnt_b = jnp.minimum(tpos, jnp.where(lo_half, float(POOL_WINDOWS[2]), float(POOL_WINDOWS[3])))
    pooled = jnp.concatenate([jnp.where(lo_half, s2, s4) / cnt_a,
                              jnp.where(lo_half, s8, s16) / cnt_b], axis=1) - p
    mixed = jnp.dot(pooled.astype(BF16), poolw_ref[...], preferred_element_type=F32)
    y_scr[:, YOFF_POOL:YOFF_POOL + POOL_W] = (mixed * pscale_ref[...]).astype(BF16)

    v = proj(OFF_V, D_IN)
    vnb = _layer_norm(v, lng_ref[...], lnb_ref[...]).astype(BF16)
    u = proj(OFF_U, OFF_V)
    row_c = lax.broadcasted_iota(jnp.int32, (CHUNK, 2 * CHUNK), 0)
    col_c = lax.broadcasted_iota(jnp.int32, (CHUNK, 2 * CHUNK), 1)
    causal = (col_c & (CHUNK - 1)) <= row_c
    lo_lanes = lax.broadcasted_iota(jnp.int32, (CHUNK, LANES), 1) < HEAD_DIM
    zero_b = jnp.zeros((CHUNK, LANES), BF16)
    for hp in range(SGU_W // LANES):
        wl = jnp.where(causal, sguw_ref[hp], 0.0).astype(BF16)
        bias = sgub_ref[hp]
        for ci in range(ts // CHUNK):
            rs = slice(ci * CHUNK, (ci + 1) * CHUNK)
            vc = vnb[rs, hp * LANES:(hp + 1) * LANES]
            rhs = jnp.concatenate([jnp.where(lo_lanes, vc, zero_b),
                                   jnp.where(lo_lanes, zero_b, vc)], axis=0)
            mixed_c = jnp.dot(wl, rhs, preferred_element_type=F32) + bias
            y_scr[rs, YOFF_SGU + hp * LANES:YOFF_SGU + (hp + 1) * LANES] = (
                u[rs, hp * LANES:(hp + 1) * LANES] * mixed_c).astype(BF16)

    yo = jnp.dot(y_scr[...], wout_ref[...], preferred_element_type=F32)
    x1 = _layer_norm(alpha * x + g1 * yo, ln1g_ref[...], ln1b_ref[...])
    x1_ref[0] = x1
    h2b = (x1 * (1.0 + sc2) + sh2).astype(BF16)

    logits = jnp.dot(h2b, wr_ref[...], preferred_element_type=F32) + br_ref[...]
    lt = logits.T[0:N_EXPERTS, :]
    ex = jnp.exp(lt - jnp.max(lt, axis=0, keepdims=True))
    probs = ex / jnp.sum(ex, axis=0, keepdims=True)
    sub_iota = lax.broadcasted_iota(jnp.int32, (EXPERTS_PER_GROUP, ts), 0).astype(F32)
    best = None
    for gi in range(N_GROUPS):
        m1, i1, m2, i2 = _top2_of_group(
            probs[gi * EXPERTS_PER_GROUP:(gi + 1) * EXPERTS_PER_GROUP, :], sub_iota)
        score = m1 + m2
        cand = (score, m1, m2, i1 + float(gi * EXPERTS_PER_GROUP), i2 + float(gi * EXPERTS_PER_GROUP))
        if best is None:
            best = cand
        else:
            better = score > best[0]
            best = tuple(jnp.where(better, c, o) for c, o in zip(cand, best))
    _, p1, p2, e1, e2 = best
    den = p1 + p2
    w1 = p1 / den
    w2 = p2 / den

    ex_iota = lax.broadcasted_iota(jnp.int32, (N_EXPERTS, ts), 0).astype(F32)
    oh1 = ex_iota == e1
    oh2 = ex_iota == e2
    either = jnp.where(oh1 | oh2, 1.0, 0.0)
    seen = jnp.dot(either.astype(BF16), tri_ref[...], preferred_element_type=F32)
    n_col = jnp.sum(either, axis=1, keepdims=True)
    n8_col = jnp.ceil(n_col * (1.0 / SUBLANES)) * float(SUBLANES)
    lower = (lax.broadcasted_iota(jnp.int32, (N_EXPERTS, N_EXPERTS), 1)
             < lax.broadcasted_iota(jnp.int32, (N_EXPERTS, N_EXPERTS), 0))
    off_col = jnp.dot(jnp.where(lower, 1.0, 0.0).astype(BF16),
                      jnp.broadcast_to(n8_col, (N_EXPERTS, LANES)).astype(BF16),
                      preferred_element_type=F32)[:, 0:1]
    slot = off_col + seen
    pos1 = jnp.sum(jnp.where(oh1, slot, 0.0), axis=0, keepdims=True)
    pos2 = jnp.sum(jnp.where(oh2, slot, 0.0), axis=0, keepdims=True)
    pos_ref[0] = jnp.concatenate([pos1, pos2], axis=0).astype(jnp.int32)

    base_col = base_scr[:, 0:1]
    end_col = base_col + n8_col
    base_scr[...] = jnp.broadcast_to(end_col, (N_EXPERTS, LANES))
    mlane = lax.broadcasted_iota(jnp.int32, (N_EXPERTS, LANES), 1)
    meta = jnp.where(mlane == META_OFF, off_col,
                     jnp.where(mlane == META_N8, n8_col,
                               jnp.where(mlane == META_BASE, base_col,
                                         jnp.where(mlane == META_END, end_col, 0.0)))).astype(jnp.int32)
    meta_ref[0] = meta

    slot_iota = lax.broadcasted_iota(jnp.int32, (SLOTS, ts), 0).astype(F32)
    sel1 = slot_iota == pos1
    sel2 = slot_iota == pos2
    p1f = jnp.where(sel1, 1.0, 0.0)
    p2f = jnp.where(sel2, 1.0, 0.0)
    perm = (p1f + p2f).astype(BF16)
    xs_ref[:, 0:PACK_W] = _pack_bf16_pairs(jnp.dot(perm, h2b, preferred_element_type=F32))
    w_slot = jnp.sum(jnp.where(sel1, w1, jnp.where(sel2, w2, 0.0)), axis=1, keepdims=True)
    xs_ref[:, PACK_W:XS_W] = pltpu.bitcast(jnp.broadcast_to(w_slot, (SLOTS, LANES)), jnp.int32)


def _mixer_call(x, ada_l, w_in, w_out, conv_w, conv_b, poolw_bd, pool_scale, ln_g, ln_b,
                sguw_pair, sgub_pair, ln1_g, ln1_b, wr_pad, br_pad, tri, alpha):
    bsz, s, d = x.shape
    ts = SEQ_TILE
    nt = s // ts
    n_tiles = bsz * nt
    const2 = lambda b, j: (0, 0)
    const3 = lambda b, j: (0, 0, 0)
    tile3 = lambda b, j: (b, j, 0)
    meta3 = lambda b, j: (b * nt + j, 0, 0)
    once = pl.Buffered(1)
    kern = functools.partial(_mixer_kernel, ts=ts, alpha=alpha)
    return pl.pallas_call(
        kern,
        grid=(bsz, nt),
        in_specs=[
            pl.BlockSpec((1, ts, d), tile3),
            pl.BlockSpec((1, ADA_CHUNKS, d), lambda b, j: (b, 0, 0)),
            pl.BlockSpec((d, D_IN), const2, pipeline_mode=once),
            pl.BlockSpec((d, d), const2, pipeline_mode=once),
            pl.BlockSpec((CONV_K, CONV_W), const2),
            pl.BlockSpec((1, CONV_W), const2),
            pl.BlockSpec((POOL_W, POOL_W), const2),
            pl.BlockSpec((1, POOL_W), const2),
            pl.BlockSpec((1, SGU_W), const2),
            pl.BlockSpec((1, SGU_W), const2),
            pl.BlockSpec((SGU_W // LANES, CHUNK, 2 * CHUNK), const3),
            pl.BlockSpec((SGU_W // LANES, CHUNK, LANES), const3),
            pl.BlockSpec((1, d), const2),
            pl.BlockSpec((1, d), const2),
            pl.BlockSpec((d, LANES), const2),
            pl.BlockSpec((1, LANES), const2),
            pl.BlockSpec((ts, ts), const2, pipeline_mode=once),
        ],
        out_specs=[
            pl.BlockSpec((1, ts, d), tile3),
            pl.BlockSpec((1, TOP_K, ts), meta3),
            pl.BlockSpec((1, N_EXPERTS, LANES), meta3),
            pl.BlockSpec((SLOTS, XS_W), lambda b, j: (b * nt + j, 0)),
        ],
        out_shape=[
            jax.ShapeDtypeStruct((bsz, s, d), F32),
            jax.ShapeDtypeStruct((n_tiles, TOP_K, ts), jnp.int32),
            jax.ShapeDtypeStruct((n_tiles, N_EXPERTS, LANES), jnp.int32),
            jax.ShapeDtypeStruct((n_tiles * SLOTS, XS_W), jnp.int32),
        ],
        scratch_shapes=[
            pltpu.VMEM((CONV_HALO + ts, CONV_W), F32),
            pltpu.VMEM((POOL_HALO + ts, POOL_W), F32),
            pltpu.VMEM((ts, d), BF16),
            pltpu.VMEM((N_EXPERTS, LANES), F32),
        ],
        compiler_params=pltpu.CompilerParams(
            dimension_semantics=("arbitrary", "arbitrary"), vmem_limit_bytes=VMEM_LIMIT),
        name="mixer_router",
    )(x, ada_l, w_in, w_out, conv_w, conv_b, poolw_bd, pool_scale, ln_g, ln_b,
      sguw_pair, sgub_pair, ln1_g, ln1_b, wr_pad, br_pad, tri)


def _ffn_kernel(src_ref, b0_ref, nb_ref, rows_ref, tot_ref, xs_ref, wg_ref, wu_ref, wd_ref, ys_ref,
                xbuf, obuf, wbuf_g, wbuf_u, wbuf_d, wg_scr, wu_scr, wd_scr, gsem, osem, wsem,
                *, layer, n_blocks):
    e = pl.program_id(0)
    n_exp = pl.num_programs(0)
    b0 = b0_ref[e]
    nb = nb_ref[e]
    n_used = tot_ref[0]

    def gather(g):
        slot = g % GATHER_DEPTH
        for r in range(BLOCK_GROUPS):
            pltpu.make_async_copy(_rows(xs_ref, src_ref[g * BLOCK_GROUPS + r], SUBLANES),
                                  xbuf.at[slot, pl.ds(r * SUBLANES, SUBLANES), :], gsem.at[slot]).start()

    def out_copy(g, slot):
        return pltpu.make_async_copy(obuf.at[slot], _rows(ys_ref, g * ROW_BLOCK, ROW_BLOCK), osem.at[slot])

    def weight_copies(expert, slot):
        return [pltpu.make_async_copy(src.at[layer, expert], dst.at[slot], wsem.at[slot])
                for src, dst in ((wg_ref, wbuf_g), (wu_ref, wbuf_u), (wd_ref, wbuf_d))]

    @pl.when(e == 0)
    def _():
        for cp in weight_copies(0, 0):
            cp.start(priority=WEIGHT_DMA_PRIORITY)
        for ahead in range(GATHER_DEPTH - 1):
            @pl.when(ahead < n_used)
            def _():
                gather(ahead)

    @pl.when(e + 1 < n_exp)
    def _():
        for cp in weight_copies(e + 1, (e + 1) % 2):
            cp.start(priority=WEIGHT_DMA_PRIORITY)

    wslot = e % 2
    for cp in weight_copies(e, wslot):
        cp.wait()

    @pl.when(nb > 0)
    def _():
        wg_scr[...] = wbuf_g[wslot].astype(BF16)
        wu_scr[...] = wbuf_u[wslot].astype(BF16)
        wd_scr[...] = wbuf_d[wslot].astype(BF16)

    def block(k, carry):
        g = b0 + k
        islot = g % GATHER_DEPTH
        oslot = g % 2

        @pl.when(g + (GATHER_DEPTH - 1) < n_used)
        def _():
            gather(g + (GATHER_DEPTH - 1))

        pltpu.make_async_copy(_rows(xs_ref, 0, ROW_BLOCK), xbuf.at[islot], gsem.at[islot]).wait()

        @pl.when(g >= 2)
        def _():
            out_copy(g - 2, oslot).wait()

        def chain(half):
            rows = pl.ds(half * SUB_BLOCK, SUB_BLOCK)
            xb = jnp.concatenate(_unpack_bf16_pairs(xbuf[islot, rows, 0:PACK_W]), axis=1)
            wrow = pltpu.bitcast(xbuf[islot, rows, PACK_W:XS_W], F32)[:, 0:1]
            gate = jnp.dot(xb, wg_scr[...], preferred_element_type=F32)
            up = jnp.dot(xb, wu_scr[...], preferred_element_type=F32)
            act = (gate * jax.nn.sigmoid(gate) * up).astype(BF16)
            y = jnp.dot(act, wd_scr[...], preferred_element_type=F32) * wrow
            obuf[oslot, rows, :] = _pack_bf16_pairs(y.astype(BF16).astype(F32))

        both = rows_ref[e] - k * ROW_BLOCK > SUB_BLOCK

        @pl.when(both)
        def _():
            chain(0)
            chain(1)

        @pl.when(jnp.logical_not(both))
        def _():
            chain(0)
            obuf[oslot, pl.ds(SUB_BLOCK, SUB_BLOCK), :] = jnp.zeros((SUB_BLOCK, PACK_W), jnp.int32)

        out_copy(g, oslot).start()
        return carry

    lax.fori_loop(0, nb, block, 0)

    @pl.when(e == n_exp - 1)
    def _():
        @pl.when(n_used >= 2)
        def _():
            out_copy(n_used - 2, n_used % 2).wait()

        out_copy(n_used - 1, (n_used - 1) % 2).wait()
        obuf[0] = jnp.zeros((ROW_BLOCK, PACK_W), jnp.int32)

        def fill(g, carry):
            out_copy(g, 0).start()
            return carry

        def drain(g, carry):
            out_copy(g, 0).wait()
            return carry

        lax.fori_loop(n_used, n_blocks, fill, 0)
        lax.fori_loop(n_used, n_blocks, drain, 0)


def _ffn_call(layer, group_src, blk_start, seg_blocks, seg_rows, n_used, n_blocks, xs, w_gate, w_up, w_down):
    d, f = w_gate.shape[-2:]
    any_spec = pl.BlockSpec(memory_space=pl.ANY)
    return pl.pallas_call(
        functools.partial(_ffn_kernel, layer=layer, n_blocks=n_blocks),
        grid_spec=pltpu.PrefetchScalarGridSpec(
            num_scalar_prefetch=5,
            grid=(N_EXPERTS,),
            in_specs=[any_spec, any_spec, any_spec, any_spec],
            out_specs=any_spec,
            scratch_shapes=[
                pltpu.VMEM((GATHER_DEPTH, ROW_BLOCK, XS_W), jnp.int32),
                pltpu.VMEM((2, ROW_BLOCK, PACK_W), jnp.int32),
                pltpu.VMEM((2, d, f), F32),
                pltpu.VMEM((2, d, f), F32),
                pltpu.VMEM((2, f, d), F32),
                pltpu.VMEM((d, f), BF16),
                pltpu.VMEM((d, f), BF16),
                pltpu.VMEM((f, d), BF16),
                pltpu.SemaphoreType.DMA((GATHER_DEPTH,)),
                pltpu.SemaphoreType.DMA((2,)),
                pltpu.SemaphoreType.DMA((2,)),
            ],
        ),
        out_shape=jax.ShapeDtypeStruct((n_blocks * ROW_BLOCK, PACK_W), jnp.int32),
        compiler_params=pltpu.CompilerParams(
            dimension_semantics=("arbitrary",), vmem_limit_bytes=VMEM_LIMIT),
        name="expert_ffn",
    )(group_src, blk_start, seg_blocks, seg_rows, n_used, xs, w_gate, w_up, w_down)


def _combine_kernel(src_ref, ys_ref, pos_ref, x1_ref, ada_ref, g_ref, b_ref,
                    o_ref, buf_even, buf_odd, sem, *, ts, alpha):
    i = pl.program_id(0)
    n = pl.num_programs(0)

    def gather(tile, buf, s):
        for q in range(TILE_GROUPS):
            pltpu.make_async_copy(_rows(ys_ref, src_ref[tile * TILE_GROUPS + q], SUBLANES),
                                  buf.at[pl.ds(q * SUBLANES, SUBLANES), :], s).start()

    def wait(buf, s):
        pltpu.make_async_copy(_rows(ys_ref, 0, SLOTS), buf, s).wait()

    @pl.when(i == 0)
    def _():
        gather(0, buf_even, sem.at[0])

    def step(mine, other, s_mine, s_other):
        gather(jnp.minimum(i + 1, n - 1), other, s_other)
        wait(mine, s_mine)
        first, second = _unpack_bf16_pairs(mine[...])
        g2 = ada_ref[0][5:6]
        slot_iota = lax.broadcasted_iota(jnp.int32, (COMBINE_ROWS, SLOTS), 1)
        for c in range(ts // COMBINE_ROWS):
            rows = pl.ds(c * COMBINE_ROWS, COMBINE_ROWS)
            pos = pos_ref[rows, :]
            pick = (slot_iota == pos[:, 0:1]) | (slot_iota == pos[:, 1:2])
            unsort = jnp.where(pick, 1.0, 0.0).astype(BF16)
            y = jnp.concatenate([jnp.dot(unsort, first, preferred_element_type=F32),
                                 jnp.dot(unsort, second, preferred_element_type=F32)], axis=1)
            o_ref[rows, :] = _layer_norm(alpha * x1_ref[rows, :] + g2 * y, g_ref[...], b_ref[...])

        @pl.when(i == n - 1)
        def _():
            wait(other, s_other)

    @pl.when(i % 2 == 0)
    def _():
        step(buf_even, buf_odd, sem.at[0], sem.at[1])

    @pl.when(i % 2 == 1)
    def _():
        step(buf_odd, buf_even, sem.at[1], sem.at[0])


def _combine_call(group_src, ys, pos_tok, x1_flat, ada_l, ln_g, ln_b, alpha, seq):
    n_tok, d = x1_flat.shape
    ts = SEQ_TILE
    n_tiles = n_tok // ts
    per_seq = seq // ts
    return pl.pallas_call(
        functools.partial(_combine_kernel, ts=ts, alpha=alpha),
        grid_spec=pltpu.PrefetchScalarGridSpec(
            num_scalar_prefetch=1,
            grid=(n_tiles,),
            in_specs=[
                pl.BlockSpec(memory_space=pl.ANY),
                pl.BlockSpec((ts, TOP_K), lambda i, src: (i, 0)),
                pl.BlockSpec((ts, d), lambda i, src: (i, 0)),
                pl.BlockSpec((1, ADA_CHUNKS, d), lambda i, src: (i // per_seq, 0, 0)),
                pl.BlockSpec((1, d), lambda i, src: (0, 0)),
                pl.BlockSpec((1, d), lambda i, src: (0, 0)),
            ],
            out_specs=pl.BlockSpec((ts, d), lambda i, src: (i, 0)),
            scratch_shapes=[
                pltpu.VMEM((SLOTS, PACK_W), jnp.int32),
                pltpu.VMEM((SLOTS, PACK_W), jnp.int32),
                pltpu.SemaphoreType.DMA((2,)),
            ],
        ),
        out_shape=jax.ShapeDtypeStruct((n_tok, d), F32),
        compiler_params=pltpu.CompilerParams(
            dimension_semantics=("arbitrary",), vmem_limit_bytes=VMEM_LIMIT),
        name="combine_ln",
    )(group_src, ys, pos_tok, x1_flat, ada_l, ln_g, ln_b)


def kernel(x, c, w_ada, b_ada, w_in, conv_w, conv_b, pool_w, pool_scale, sgu_ln_g, sgu_ln_b,
           sgu_w, sgu_b, w_out, ln1_g, ln1_b, w_router, b_router, w_gate, w_up, w_down,
           ln2_g, ln2_b):
    bsz, seq, d = x.shape
    depth = w_ada.shape[0]
    n_tok = bsz * seq
    n_tiles = n_tok // SEQ_TILE
    alpha = (2 * depth) ** 0.25
    assert d == D_MODEL and seq % SEQ_TILE == 0 and SEQ_TILE % CHUNK == 0

    ada = _ada_call(c, w_ada, b_ada).reshape(depth, bsz, ADA_CHUNKS, d)

    w_in_b = w_in.astype(BF16)
    w_out_b = w_out.astype(BF16)
    eye_g = jnp.eye(POOL_W // HEAD_DIM, dtype=F32)
    poolw_bd = jnp.einsum('lgcd,gh->lgchd', pool_w, eye_g).reshape(depth, POOL_W, POOL_W).astype(BF16)
    n_pairs = SGU_W // LANES
    sguw_pair = sgu_w.reshape(depth, n_pairs, 2, CHUNK, CHUNK).transpose(0, 1, 3, 2, 4).reshape(
        depth, n_pairs, CHUNK, 2 * CHUNK)
    sgub_pair = jnp.repeat(sgu_b.transpose(0, 2, 1), HEAD_DIM, axis=-1).reshape(
        depth, CHUNK, n_pairs, LANES).transpose(0, 2, 1, 3)
    wr_pad = jnp.pad(w_router, ((0, 0), (0, LANES - N_EXPERTS))).astype(BF16)
    br_pad = jnp.pad(b_router, (0, LANES - N_EXPERTS)).reshape(1, LANES)

    max_rows = n_tok * TOP_K + (SUBLANES - 1) * N_EXPERTS * n_tiles
    n_steps = -(-max_rows // ROW_BLOCK) + N_EXPERTS
    step = jnp.arange(n_steps, dtype=jnp.int32)
    experts = jnp.arange(N_EXPERTS, dtype=jnp.int32)
    tiles = jnp.arange(n_tiles, dtype=jnp.int32)
    tri = jnp.triu(jnp.ones((SEQ_TILE, SEQ_TILE), BF16), k=1)
    zero_group = SLOTS - SUBLANES

    for l in range(depth):
        x1, pos, meta, xs = _mixer_call(
            x, ada[l], w_in_b[l], w_out_b[l], conv_w[l], conv_b[l].reshape(1, -1), poolw_bd[l],
            pool_scale[l].reshape(1, -1), sgu_ln_g[l].reshape(1, -1), sgu_ln_b[l].reshape(1, -1),
            sguw_pair[l], sgub_pair[l], ln1_g[l].reshape(1, -1), ln1_b[l].reshape(1, -1),
            wr_pad, br_pad, tri, alpha)
        off = meta[:, :, META_OFF]
        n8 = meta[:, :, META_N8]
        base = meta[:, :, META_BASE]
        end = meta[:, :, META_END]
        seg_rows = end[n_tiles - 1]
        seg_blocks = (seg_rows + ROW_BLOCK - 1) // ROW_BLOCK
        blk_end = jnp.cumsum(seg_blocks)
        blk_start = blk_end - seg_blocks
        n_used = blk_end[-1:].astype(jnp.int32)
        active = step < n_used[0]
        block_e = jnp.minimum(jnp.sum(step[:, None] >= blk_end[None, :], axis=1), N_EXPERTS - 1)
        is_e = block_e[:, None] == experts[None, :]

        def of_block_expert(tab):
            return jnp.sum(jnp.where(is_e[:, :, None], tab.T[None, :, :], 0), axis=1)

        first_unit = (step - jnp.sum(jnp.where(is_e, blk_start[None, :], 0), axis=1)) * BLOCK_GROUPS
        unit = first_unit[:, None] + jnp.arange(BLOCK_GROUPS)[None, :]
        tile_of = jnp.sum(of_block_expert(end // SUBLANES)[:, None, :] <= unit[:, :, None], axis=2)
        valid = (tile_of < n_tiles) & active[:, None]
        is_t = tile_of[:, :, None] == tiles[None, None, :]
        row0 = tiles[None, :] * SLOTS + of_block_expert(off) - of_block_expert(base)
        src = jnp.sum(jnp.where(is_t, row0[:, None, :], 0), axis=2) + unit * SUBLANES
        ffn_src = jnp.where(valid, src, zero_group).astype(jnp.int32).reshape(-1)
        grp = jnp.arange(TILE_GROUPS, dtype=jnp.int32)
        chunk_end_u = (off + n8) // SUBLANES
        exp_of = jnp.sum(chunk_end_u[:, None, :] <= grp[None, :, None], axis=2)
        seg_row0 = (blk_start * ROW_BLOCK)[None, :] + base - off
        comb_src = jnp.sum(jnp.where(exp_of[:, :, None] == experts[None, None, :], seg_row0[:, None, :], 0),
                           axis=2) + grp[None, :] * SUBLANES
        n_groups = chunk_end_u[:, N_EXPERTS - 1].astype(jnp.int32)
        comb_src = jnp.where(grp[None, :] < n_groups[:, None], comb_src, 0).astype(jnp.int32).reshape(-1)

        ys = _ffn_call(l, ffn_src, blk_start.astype(jnp.int32), seg_blocks.astype(jnp.int32),
                       seg_rows.astype(jnp.int32), n_used, n_steps, xs, w_gate, w_up, w_down)
        pos_tok = pos.transpose(0, 2, 1).reshape(n_tok, TOP_K)
        x = _combine_call(comb_src, ys, pos_tok, x1.reshape(n_tok, d), ada[l],
                          ln2_g[l].reshape(1, -1), ln2_b[l].reshape(1, -1), alpha, seq).reshape(bsz, seq, d)
    return x
```

```python
import functools

import jax
import jax.numpy as jnp
from jax import lax
from jax.experimental import pallas as pl
from jax.experimental.pallas import tpu as pltpu

D_MODEL = 1024
HEAD_DIM = D_MODEL // 16
CONV_W = 6 * HEAD_DIM
POOL_W = 4 * HEAD_DIM
SGU_W = 6 * HEAD_DIM
D_IN = 3 * CONV_W + POOL_W + 2 * SGU_W
CONV_K = 3
POOL_WINDOWS = (2, 4, 8, 16)
CHUNK = 128
N_EXPERTS = 32
N_GROUPS = 4
EXPERTS_PER_GROUP = N_EXPERTS // N_GROUPS
TOP_K = 2
D_FF = D_MODEL // 2
ADA_CHUNKS = 6
LN_EPS = 1e-5

OFF_GB = 0
OFF_GC = CONV_W
OFF_P = 3 * CONV_W
OFF_U = OFF_P + POOL_W
OFF_V = OFF_U + SGU_W
YOFF_POOL = CONV_W
YOFF_SGU = CONV_W + POOL_W

LANES = 128
SUBLANES = 8
CONV_HALO = 8
POOL_HALO = 16
SEQ_TILE = 512
SUB_BLOCK = 256
ROW_BLOCK = 2 * SUB_BLOCK
ADA_TILE = 1536
COMBINE_ROWS = 128
GATHER_DEPTH = 3
WEIGHT_DMA_PRIORITY = 1
VMEM_LIMIT = 60 * 1024 * 1024

SLOTS = TOP_K * SEQ_TILE + N_EXPERTS * SUBLANES
TILE_GROUPS = SLOTS // SUBLANES
BLOCK_GROUPS = ROW_BLOCK // SUBLANES
PACK_W = D_MODEL // 2
XS_W = PACK_W + LANES
META_OFF, META_N8, META_BASE, META_END = 0, 1, 2, 3

F32 = jnp.float32
BF16 = jnp.bfloat16


def _layer_norm(r, g, b):
    mu = jnp.mean(r, axis=-1, keepdims=True)
    d = r - mu
    var = jnp.mean(d * d, axis=-1, keepdims=True)
    return d * lax.rsqrt(var + LN_EPS) * g + b


def _rows(ref, start, size):
    return ref.at[pl.ds(pl.multiple_of(start, SUBLANES), size), :]


def _pack_bf16_pairs(v):
    bits = pltpu.bitcast(v, jnp.int32)
    return bits[:, 0:PACK_W] | lax.shift_right_logical(bits[:, PACK_W:2 * PACK_W], 16)


def _unpack_bf16_pairs(words):
    first = pltpu.bitcast(words & jnp.int32(-65536), F32).astype(BF16)
    second = pltpu.bitcast(lax.shift_left(words, 16), F32).astype(BF16)
    return first, second


def _ada_kernel(c_ref, w_ref, b_ref, o_ref):
    c = c_ref[...]
    c_act = (c * jax.nn.sigmoid(c)).astype(BF16)
    o_ref[0] = jnp.dot(c_act, w_ref[0].astype(BF16), preferred_element_type=F32) + b_ref[0]


def _ada_call(c, w_ada, b_ada):
    depth, d, n = w_ada.shape
    bsz = c.shape[0]
    return pl.pallas_call(
        _ada_kernel,
        grid=(depth, n // ADA_TILE),
        in_specs=[
            pl.BlockSpec((bsz, d), lambda l, j: (0, 0)),
            pl.BlockSpec((1, d, ADA_TILE), lambda l, j: (l, 0, j)),
            pl.BlockSpec((1, 1, ADA_TILE), lambda l, j: (l, 0, j)),
        ],
        out_specs=pl.BlockSpec((1, bsz, ADA_TILE), lambda l, j: (l, 0, j)),
        out_shape=jax.ShapeDtypeStruct((depth, bsz, n), F32),
        compiler_params=pltpu.CompilerParams(
            dimension_semantics=("arbitrary", "arbitrary"), vmem_limit_bytes=VMEM_LIMIT),
        name="ada",
    )(c, w_ada, b_ada.reshape(depth, 1, n))


def _top2_of_group(pg, sub_iota):
    big = float(EXPERTS_PER_GROUP)
    m1 = jnp.max(pg, axis=0, keepdims=True)
    i1 = jnp.min(jnp.where(pg == m1, sub_iota, big), axis=0, keepdims=True)
    rest = jnp.where(sub_iota == i1, -1.0, pg)
    m2 = jnp.max(rest, axis=0, keepdims=True)
    i2 = jnp.min(jnp.where(rest == m2, sub_iota, big), axis=0, keepdims=True)
    return m1, i1, m2, i2


def _mixer_kernel(x_ref, ada_ref, win_ref, wout_ref, convw_ref, convb_ref, poolw_ref, pscale_ref,
                  lng_ref, lnb_ref, sguw_ref, sgub_ref, ln1g_ref, ln1b_ref, wr_ref, br_ref, tri_ref,
                  x1_ref, pos_ref, meta_ref, xs_ref,
                  gx_scr, p_scr, y_scr, base_scr, *, ts, alpha):
    b = pl.program_id(0)
    j = pl.program_id(1)

    @pl.when(j == 0)
    def _():
        gx_scr[0:CONV_HALO, :] = jnp.zeros((CONV_HALO, CONV_W), F32)
        p_scr[0:POOL_HALO, :] = jnp.zeros((POOL_HALO, POOL_W), F32)

    @pl.when((b == 0) & (j == 0))
    def _():
        base_scr[...] = jnp.zeros_like(base_scr)

    x = x_ref[0]
    ada = ada_ref[0]
    sh1, sc1, g1 = ada[0:1], ada[1:2], ada[2:3]
    sh2, sc2 = ada[3:4], ada[4:5]
    hb = (x * (1.0 + sc1) + sh1).astype(BF16)

    def proj(lo, hi):
        return jnp.dot(hb, win_ref[:, lo:hi], preferred_element_type=F32)

    gcxc = proj(OFF_GC, OFF_P)
    g = gcxc[:, :CONV_W] * gcxc[:, CONV_W:]
    gx_scr[CONV_HALO:CONV_HALO + ts, :] = g
    cw = convw_ref[...]
    conv = (cw[0:1] * gx_scr[CONV_HALO - 2:CONV_HALO - 2 + ts, :]
            + cw[1:2] * gx_scr[CONV_HALO - 1:CONV_HALO - 1 + ts, :]
            + cw[2:3] * g + convb_ref[...])
    gx_scr[0:CONV_HALO, :] = g[ts - CONV_HALO:ts, :]
    y_scr[:, 0:CONV_W] = (proj(OFF_GB, OFF_GC) * conv).astype(BF16)

    p = proj(OFF_P, OFF_U)
    p_scr[POOL_HALO:POOL_HALO + ts, :] = p

    def shifted(k, lo):
        return p_scr[POOL_HALO - k:POOL_HALO - k + ts, lo:lo + LANES]

    s2 = p[:, 0:LANES] + shifted(1, 0)
    s4 = s2 + shifted(2, 0) + shifted(3, 0)
    acc = p[:, LANES:2 * LANES]
    for k in range(1, 8):
        acc = acc + shifted(k, LANES)
    s8 = acc
    for k in range(8, 16):
        acc = acc + shifted(k, LANES)
    s16 = acc
    p_scr[0:POOL_HALO, :] = p[ts - POOL_HALO:ts, :]
    lane = lax.broadcasted_iota(jnp.int32, (ts, LANES), 1)
    tpos = (lax.broadcasted_iota(jnp.int32, (ts, LANES), 0) + (j * ts + 1)).astype(F32)
    lo_half = lane < HEAD_DIM
    cnt_a = jnp.minimum(tpos, jnp.where(lo_half, float(POOL_WINDOWS[0]), float(POOL_WINDOWS[1])))
    cnt_b = jnp.minimum(tpos, jnp.where(lo_half, float(POOL_WINDOWS[2]), float(POOL_WINDOWS[3])))
    pooled = jnp.concatenate([jnp.where(lo_half, s2, s4) / cnt_a,
                              jnp.where(lo_half, s8, s16) / cnt_b], axis=1) - p
    mixed = jnp.dot(pooled.astype(BF16), poolw_ref[...], preferred_element_type=F32)
    y_scr[:, YOFF_POOL:YOFF_POOL + POOL_W] = (mixed * pscale_ref[...]).astype(BF16)

    v = proj(OFF_V, D_IN)
    vnb = _layer_norm(v, lng_ref[...], lnb_ref[...]).astype(BF16)
    u = proj(OFF_U, OFF_V)
    row_c = lax.broadcasted_iota(jnp.int32, (CHUNK, 2 * CHUNK), 0)
    col_c = lax.broadcasted_iota(jnp.int32, (CHUNK, 2 * CHUNK), 1)
    causal = (col_c & (CHUNK - 1)) <= row_c
    lo_lanes = lax.broadcasted_iota(jnp.int32, (CHUNK, LANES), 1) < HEAD_DIM
    zero_b = jnp.zeros((CHUNK, LANES), BF16)
    for hp in range(SGU_W // LANES):
        wl = jnp.where(causal, sguw_ref[hp], 0.0).astype(BF16)
        bias = sgub_ref[hp]
        for ci in range(ts // CHUNK):
            rs = slice(ci * CHUNK, (ci + 1) * CHUNK)
            vc = vnb[rs, hp * LANES:(hp + 1) * LANES]
            rhs = jnp.concatenate([jnp.where(lo_lanes, vc, zero_b),
                                   jnp.where(lo_lanes, zero_b, vc)], axis=0)
            mixed_c = jnp.dot(wl, rhs, preferred_element_type=F32) + bias
            y_scr[rs, YOFF_SGU + hp * LANES:YOFF_SGU + (hp + 1) * LANES] = (
                u[rs, hp * LANES:(hp + 1) * LANES] * mixed_c).astype(BF16)

    yo = jnp.dot(y_scr[...], wout_ref[...], preferred_element_type=F32)
    x1 = _layer_norm(alpha * x + g1 * yo, ln1g_ref[...], ln1b_ref[...])
    x1_ref[0] = x1
    h2b = (x1 * (1.0 + sc2) + sh2).astype(BF16)

    logits = jnp.dot(h2b, wr_ref[...], preferred_element_type=F32) + br_ref[...]
    lt = logits.T[0:N_EXPERTS, :]
    ex = jnp.exp(lt - jnp.max(lt, axis=0, keepdims=True))
    probs = ex / jnp.sum(ex, axis=0, keepdims=True)
    sub_iota = lax.broadcasted_iota(jnp.int32, (EXPERTS_PER_GROUP, ts), 0).astype(F32)
    best = None
    for gi in range(N_GROUPS):
        m1, i1, m2, i2 = _top2_of_group(
            probs[gi * EXPERTS_PER_GROUP:(gi + 1) * EXPERTS_PER_GROUP, :], sub_iota)
        score = m1 + m2
        cand = (score, m1, m2, i1 + float(gi * EXPERTS_PER_GROUP), i2 + float(gi * EXPERTS_PER_GROUP))
        if best is None:
            best = cand
        else:
            better = score > best[0]
            best = tuple(jnp.where(better, c, o) for c, o in zip(cand, best))
    _, p1, p2, e1, e2 = best
    den = p1 + p2
    w1 = p1 / den
    w2 = p2 / den

    ex_iota = lax.broadcasted_iota(jnp.int32, (N_EXPERTS, ts), 0).astype(F32)
    oh1 = ex_iota == e1
    oh2 = ex_iota == e2
    either = jnp.where(oh1 | oh2, 1.0, 0.0)
    seen = jnp.dot(either.astype(BF16), tri_ref[...], preferred_element_type=F32)
    n_col = jnp.sum(either, axis=1, keepdims=True)
    n8_col = jnp.ceil(n_col * (1.0 / SUBLANES)) * float(SUBLANES)
    lower = (lax.broadcasted_iota(jnp.int32, (N_EXPERTS, N_EXPERTS), 1)
             < lax.broadcasted_iota(jnp.int32, (N_EXPERTS, N_EXPERTS), 0))
    off_col = jnp.dot(jnp.where(lower, 1.0, 0.0).astype(BF16),
                      jnp.broadcast_to(n8_col, (N_EXPERTS, LANES)).astype(BF16),
                      preferred_element_type=F32)[:, 0:1]
    slot = off_col + seen
    pos1 = jnp.sum(jnp.where(oh1, slot, 0.0), axis=0, keepdims=True)
    pos2 = jnp.sum(jnp.where(oh2, slot, 0.0), axis=0, keepdims=True)
    pos_ref[0] = jnp.concatenate([pos1, pos2], axis=0).astype(jnp.int32)

    base_col = base_scr[:, 0:1]
    end_col = base_col + n8_col
    base_scr[...] = jnp.broadcast_to(end_col, (N_EXPERTS, LANES))
    mlane = lax.broadcasted_iota(jnp.int32, (N_EXPERTS, LANES), 1)
    meta = jnp.where(mlane == META_OFF, off_col,
                     jnp.where(mlane == META_N8, n8_col,
                               jnp.where(mlane == META_BASE, base_col,
                                         jnp.where(mlane == META_END, end_col, 0.0)))).astype(jnp.int32)
    meta_ref[0] = meta

    slot_iota = lax.broadcasted_iota(jnp.int32, (SLOTS, ts), 0).astype(F32)
    sel1 = slot_iota == pos1
    sel2 = slot_iota == pos2
    p1f = jnp.where(sel1, 1.0, 0.0)
    p2f = jnp.where(sel2, 1.0, 0.0)
    perm = (p1f + p2f).astype(BF16)
    xs_ref[:, 0:PACK_W] = _pack_bf16_pairs(jnp.dot(perm, h2b, preferred_element_type=F32))
    w_slot = jnp.sum(jnp.where(sel1, w1, jnp.where(sel2, w2, 0.0)), axis=1, keepdims=True)
    xs_ref[:, PACK_W:XS_W] = pltpu.bitcast(jnp.broadcast_to(w_slot, (SLOTS, LANES)), jnp.int32)


def _mixer_call(x, ada_l, w_in, w_out, conv_w, conv_b, poolw_bd, pool_scale, ln_g, ln_b,
                sguw_pair, sgub_pair, ln1_g, ln1_b, wr_pad, br_pad, tri, alpha):
    bsz, s, d = x.shape
    ts = SEQ_TILE
    nt = s // ts
    n_tiles = bsz * nt
    const2 = lambda b, j: (0, 0)
    const3 = lambda b, j: (0, 0, 0)
    tile3 = lambda b, j: (b, j, 0)
    meta3 = lambda b, j: (b * nt + j, 0, 0)
    once = pl.Buffered(1)
    kern = functools.partial(_mixer_kernel, ts=ts, alpha=alpha)
    return pl.pallas_call(
        kern,
        grid=(bsz, nt),
        in_specs=[
            pl.BlockSpec((1, ts, d), tile3),
            pl.BlockSpec((1, ADA_CHUNKS, d), lambda b, j: (b, 0, 0)),
            pl.BlockSpec((d, D_IN), const2, pipeline_mode=once),
            pl.BlockSpec((d, d), const2, pipeline_mode=once),
            pl.BlockSpec((CONV_K, CONV_W), const2),
            pl.BlockSpec((1, CONV_W), const2),
            pl.BlockSpec((POOL_W, POOL_W), const2),
            pl.BlockSpec((1, POOL_W), const2),
            pl.BlockSpec((1, SGU_W), const2),
            pl.BlockSpec((1, SGU_W), const2),
            pl.BlockSpec((SGU_W // LANES, CHUNK, 2 * CHUNK), const3),
            pl.BlockSpec((SGU_W // LANES, CHUNK, LANES), const3),
            pl.BlockSpec((1, d), const2),
            pl.BlockSpec((1, d), const2),
            pl.BlockSpec((d, LANES), const2),
            pl.BlockSpec((1, LANES), const2),
            pl.BlockSpec((ts, ts), const2, pipeline_mode=once),
        ],
        out_specs=[
            pl.BlockSpec((1, ts, d), tile3),
            pl.BlockSpec((1, TOP_K, ts), meta3),
            pl.BlockSpec((1, N_EXPERTS, LANES), meta3),
            pl.BlockSpec((SLOTS, XS_W), lambda b, j: (b * nt + j, 0)),
        ],
        out_shape=[
            jax.ShapeDtypeStruct((bsz, s, d), F32),
            jax.ShapeDtypeStruct((n_tiles, TOP_K, ts), jnp.int32),
            jax.ShapeDtypeStruct((n_tiles, N_EXPERTS, LANES), jnp.int32),
            jax.ShapeDtypeStruct((n_tiles * SLOTS, XS_W), jnp.int32),
        ],
        scratch_shapes=[
            pltpu.VMEM((CONV_HALO + ts, CONV_W), F32),
            pltpu.VMEM((POOL_HALO + ts, POOL_W), F32),
            pltpu.VMEM((ts, d), BF16),
            pltpu.VMEM((N_EXPERTS, LANES), F32),
        ],
        compiler_params=pltpu.CompilerParams(
            dimension_semantics=("arbitrary", "arbitrary"), vmem_limit_bytes=VMEM_LIMIT),
        name="mixer_router",
    )(x, ada_l, w_in, w_out, conv_w, conv_b, poolw_bd, pool_scale, ln_g, ln_b,
      sguw_pair, sgub_pair, ln1_g, ln1_b, wr_pad, br_pad, tri)


def _ffn_kernel(src_ref, b0_ref, nb_ref, rows_ref, tot_ref, xs_ref, wg_ref, wu_ref, wd_ref, ys_ref,
                xbuf, obuf, wbuf_g, wbuf_u, wbuf_d, wg_scr, wu_scr, wd_scr, gsem, osem, wsem,
                *, layer, n_blocks):
    e = pl.program_id(0)
    n_exp = pl.num_programs(0)
    b0 = b0_ref[e]
    nb = nb_ref[e]
    n_used = tot_ref[0]

    def gather(g, slot, lo=0, hi=BLOCK_GROUPS):
        for r in range(lo, hi):
            pltpu.make_async_copy(_rows(xs_ref, src_ref[g * BLOCK_GROUPS + r], SUBLANES),
                                  xbuf.at[slot, pl.ds(r * SUBLANES, SUBLANES), :], gsem.at[slot]).start()

    def gather_wait(slot):
        pltpu.make_async_copy(_rows(xs_ref, 0, ROW_BLOCK), xbuf.at[slot], gsem.at[slot]).wait()

    def out_copy(g, slot):
        return pltpu.make_async_copy(obuf.at[slot], _rows(ys_ref, g * ROW_BLOCK, ROW_BLOCK), osem.at[slot])

    def weight_copies(expert, slot):
        return [pltpu.make_async_copy(src.at[layer, expert], dst.at[slot], wsem.at[slot])
                for src, dst in ((wg_ref, wbuf_g), (wu_ref, wbuf_u), (wd_ref, wbuf_d))]

    @pl.when(e == 0)
    def _():
        for cp in weight_copies(0, 0):
            cp.start(priority=WEIGHT_DMA_PRIORITY)
        for ahead in range(GATHER_DEPTH - 1):
            @pl.when(ahead < n_used)
            def _():
                gather(ahead, ahead)

    @pl.when(e + 1 < n_exp)
    def _():
        for cp in weight_copies(e + 1, (e + 1) % 2):
            cp.start(priority=WEIGHT_DMA_PRIORITY)

    wslot = e % 2
    for cp in weight_copies(e, wslot):
        cp.wait()

    @pl.when(nb > 0)
    def _():
        wg_scr[...] = wbuf_g[wslot].astype(BF16)
        wu_scr[...] = wbuf_u[wslot].astype(BF16)
        wd_scr[...] = wbuf_d[wslot].astype(BF16)

    def block(k, carry):
        g = b0 + k
        islot = g % GATHER_DEPTH
        oslot = g % 2
        ahead = jnp.minimum(g + (GATHER_DEPTH - 1), n_used - 1)
        aslot = (g + (GATHER_DEPTH - 1)) % GATHER_DEPTH
        half_groups = BLOCK_GROUPS // 2

        gather_wait(islot)

        @pl.when(g >= 2)
        def _():
            out_copy(g - 2, oslot).wait()

        def chain(half):
            gather(ahead, aslot, half * half_groups, (half + 1) * half_groups)
            rows = pl.ds(half * SUB_BLOCK, SUB_BLOCK)
            xb = jnp.concatenate(_unpack_bf16_pairs(xbuf[islot, rows, 0:PACK_W]), axis=1)
            wrow = pltpu.bitcast(xbuf[islot, rows, PACK_W:XS_W], F32)[:, 0:1]
            gate = jnp.dot(xb, wg_scr[...], preferred_element_type=F32)
            up = jnp.dot(xb, wu_scr[...], preferred_element_type=F32)
            act = (gate * jax.nn.sigmoid(gate) * up).astype(BF16)
            y = jnp.dot(act, wd_scr[...], preferred_element_type=F32) * wrow
            obuf[oslot, rows, :] = _pack_bf16_pairs(y.astype(BF16).astype(F32))

        both = rows_ref[e] - k * ROW_BLOCK > SUB_BLOCK

        @pl.when(both)
        def _():
            chain(0)
            chain(1)

        @pl.when(jnp.logical_not(both))
        def _():
            chain(0)
            gather(ahead, aslot, half_groups, BLOCK_GROUPS)
            obuf[oslot, pl.ds(SUB_BLOCK, SUB_BLOCK), :] = jnp.zeros((SUB_BLOCK, PACK_W), jnp.int32)

        out_copy(g, oslot).start()
        return carry

    lax.fori_loop(0, nb, block, 0)

    @pl.when(e == n_exp - 1)
    def _():
        @pl.when(n_used >= 2)
        def _():
            gather_wait(n_used % GATHER_DEPTH)
            out_copy(n_used - 2, n_used % 2).wait()

        gather_wait((n_used + 1) % GATHER_DEPTH)
        out_copy(n_used - 1, (n_used - 1) % 2).wait()
        obuf[0] = jnp.zeros((ROW_BLOCK, PACK_W), jnp.int32)

        def fill(g, carry):
            out_copy(g, 0).start()
            return carry

        def drain(g, carry):
            out_copy(g, 0).wait()
            return carry

        lax.fori_loop(n_used, n_blocks, fill, 0)
        lax.fori_loop(n_used, n_blocks, drain, 0)


def _ffn_call(layer, group_src, blk_start, seg_blocks, seg_rows, n_used, n_blocks, xs, w_gate, w_up, w_down):
    d, f = w_gate.shape[-2:]
    any_spec = pl.BlockSpec(memory_space=pl.ANY)
    return pl.pallas_call(
        functools.partial(_ffn_kernel, layer=layer, n_blocks=n_blocks),
        grid_spec=pltpu.PrefetchScalarGridSpec(
            num_scalar_prefetch=5,
            grid=(N_EXPERTS,),
            in_specs=[any_spec, any_spec, any_spec, any_spec],
            out_specs=any_spec,
            scratch_shapes=[
                pltpu.VMEM((GATHER_DEPTH, ROW_BLOCK, XS_W), jnp.int32),
                pltpu.VMEM((2, ROW_BLOCK, PACK_W), jnp.int32),
                pltpu.VMEM((2, d, f), F32),
                pltpu.VMEM((2, d, f), F32),
                pltpu.VMEM((2, f, d), F32),
                pltpu.VMEM((d, f), BF16),
                pltpu.VMEM((d, f), BF16),
                pltpu.VMEM((f, d), BF16),
                pltpu.SemaphoreType.DMA((GATHER_DEPTH,)),
                pltpu.SemaphoreType.DMA((2,)),
                pltpu.SemaphoreType.DMA((2,)),
            ],
        ),
        out_shape=jax.ShapeDtypeStruct((n_blocks * ROW_BLOCK, PACK_W), jnp.int32),
        compiler_params=pltpu.CompilerParams(
            dimension_semantics=("arbitrary",), vmem_limit_bytes=VMEM_LIMIT),
        name="expert_ffn",
    )(group_src, blk_start, seg_blocks, seg_rows, n_used, xs, w_gate, w_up, w_down)


def _combine_kernel(src_ref, ys_ref, pos_ref, x1_ref, ada_ref, g_ref, b_ref,
                    o_ref, buf_even, buf_odd, sem, *, ts, alpha):
    i = pl.program_id(0)
    n = pl.num_programs(0)

    def gather(tile, buf, s, lo=0, hi=TILE_GROUPS):
        for q in range(lo, hi):
            pltpu.make_async_copy(_rows(ys_ref, src_ref[tile * TILE_GROUPS + q], SUBLANES),
                                  buf.at[pl.ds(q * SUBLANES, SUBLANES), :], s).start(priority=q % 2)

    def wait(buf, s):
        pltpu.make_async_copy(_rows(ys_ref, 0, SLOTS), buf, s).wait()

    @pl.when(i == 0)
    def _():
        gather(0, buf_even, sem.at[0])

    def step(mine, other, s_mine, s_other):
        nxt = jnp.minimum(i + 1, n - 1)
        n_chains = ts // COMBINE_ROWS
        per_chain = TILE_GROUPS // n_chains
        wait(mine, s_mine)
        first, second = _unpack_bf16_pairs(mine[...])
        g2 = ada_ref[0][5:6]
        slot_iota = lax.broadcasted_iota(jnp.int32, (COMBINE_ROWS, SLOTS), 1)
        for c in range(n_chains):
            gather(nxt, other, s_other, c * per_chain, (c + 1) * per_chain)
            rows = pl.ds(c * COMBINE_ROWS, COMBINE_ROWS)
            pos = pos_ref[rows, :]
            pick = (slot_iota == pos[:, 0:1]) | (slot_iota == pos[:, 1:2])
            unsort = jnp.where(pick, 1.0, 0.0).astype(BF16)
            y = jnp.concatenate([jnp.dot(unsort, first, preferred_element_type=F32),
                                 jnp.dot(unsort, second, preferred_element_type=F32)], axis=1)
            o_ref[rows, :] = _layer_norm(alpha * x1_ref[rows, :] + g2 * y, g_ref[...], b_ref[...])

        @pl.when(i == n - 1)
        def _():
            wait(other, s_other)

    @pl.when(i % 2 == 0)
    def _():
        step(buf_even, buf_odd, sem.at[0], sem.at[1])

    @pl.when(i % 2 == 1)
    def _():
        step(buf_odd, buf_even, sem.at[1], sem.at[0])


def _combine_call(group_src, ys, pos_tok, x1_flat, ada_l, ln_g, ln_b, alpha, seq):
    n_tok, d = x1_flat.shape
    ts = SEQ_TILE
    n_tiles = n_tok // ts
    per_seq = seq // ts
    return pl.pallas_call(
        functools.partial(_combine_kernel, ts=ts, alpha=alpha),
        grid_spec=pltpu.PrefetchScalarGridSpec(
            num_scalar_prefetch=1,
            grid=(n_tiles,),
            in_specs=[
                pl.BlockSpec(memory_space=pl.ANY),
                pl.BlockSpec((ts, TOP_K), lambda i, src: (i, 0)),
                pl.BlockSpec((ts, d), lambda i, src: (i, 0)),
                pl.BlockSpec((1, ADA_CHUNKS, d), lambda i, src: (i // per_seq, 0, 0)),
                pl.BlockSpec((1, d), lambda i, src: (0, 0)),
                pl.BlockSpec((1, d), lambda i, src: (0, 0)),
            ],
            out_specs=pl.BlockSpec((ts, d), lambda i, src: (i, 0)),
            scratch_shapes=[
                pltpu.VMEM((SLOTS, PACK_W), jnp.int32),
                pltpu.VMEM((SLOTS, PACK_W), jnp.int32),
                pltpu.SemaphoreType.DMA((2,)),
            ],
        ),
        out_shape=jax.ShapeDtypeStruct((n_tok, d), F32),
        compiler_params=pltpu.CompilerParams(
            dimension_semantics=("arbitrary",), vmem_limit_bytes=VMEM_LIMIT),
        name="combine_ln",
    )(group_src, ys, pos_tok, x1_flat, ada_l, ln_g, ln_b)


def kernel(x, c, w_ada, b_ada, w_in, conv_w, conv_b, pool_w, pool_scale, sgu_ln_g, sgu_ln_b,
           sgu_w, sgu_b, w_out, ln1_g, ln1_b, w_router, b_router, w_gate, w_up, w_down,
           ln2_g, ln2_b):
    bsz, seq, d = x.shape
    depth = w_ada.shape[0]
    n_tok = bsz * seq
    n_tiles = n_tok // SEQ_TILE
    alpha = (2 * depth) ** 0.25
    assert d == D_MODEL and seq % SEQ_TILE == 0 and SEQ_TILE % CHUNK == 0

    ada = _ada_call(c, w_ada, b_ada).reshape(depth, bsz, ADA_CHUNKS, d)

    w_in_b = w_in.astype(BF16)
    w_out_b = w_out.astype(BF16)
    eye_g = jnp.eye(POOL_W // HEAD_DIM, dtype=F32)
    poolw_bd = jnp.einsum('lgcd,gh->lgchd', pool_w, eye_g).reshape(depth, POOL_W, POOL_W).astype(BF16)
    n_pairs = SGU_W // LANES
    sguw_pair = sgu_w.reshape(depth, n_pairs, 2, CHUNK, CHUNK).transpose(0, 1, 3, 2, 4).reshape(
        depth, n_pairs, CHUNK, 2 * CHUNK)
    sgub_pair = jnp.repeat(sgu_b.transpose(0, 2, 1), HEAD_DIM, axis=-1).reshape(
        depth, CHUNK, n_pairs, LANES).transpose(0, 2, 1, 3)
    wr_pad = jnp.pad(w_router, ((0, 0), (0, LANES - N_EXPERTS))).astype(BF16)
    br_pad = jnp.pad(b_router, (0, LANES - N_EXPERTS)).reshape(1, LANES)

    max_rows = n_tok * TOP_K + (SUBLANES - 1) * N_EXPERTS * n_tiles
    n_steps = -(-max_rows // ROW_BLOCK) + N_EXPERTS
    step = jnp.arange(n_steps, dtype=jnp.int32)
    experts = jnp.arange(N_EXPERTS, dtype=jnp.int32)
    tiles = jnp.arange(n_tiles, dtype=jnp.int32)
    tri = jnp.triu(jnp.ones((SEQ_TILE, SEQ_TILE), BF16), k=1)
    zero_group = SLOTS - SUBLANES

    for l in range(depth):
        x1, pos, meta, xs = _mixer_call(
            x, ada[l], w_in_b[l], w_out_b[l], conv_w[l], conv_b[l].reshape(1, -1), poolw_bd[l],
            pool_scale[l].reshape(1, -1), sgu_ln_g[l].reshape(1, -1), sgu_ln_b[l].reshape(1, -1),
            sguw_pair[l], sgub_pair[l], ln1_g[l].reshape(1, -1), ln1_b[l].reshape(1, -1),
            wr_pad, br_pad, tri, alpha)
        off = meta[:, :, META_OFF]
        n8 = meta[:, :, META_N8]
        base = meta[:, :, META_BASE]
        end = meta[:, :, META_END]
        seg_rows = end[n_tiles - 1]
        seg_blocks = (seg_rows + ROW_BLOCK - 1) // ROW_BLOCK
        blk_end = jnp.cumsum(seg_blocks)
        blk_start = blk_end - seg_blocks
        n_used = blk_end[-1:].astype(jnp.int32)
        active = step < n_used[0]
        block_e = jnp.minimum(jnp.sum(step[:, None] >= blk_end[None, :], axis=1), N_EXPERTS - 1)
        is_e = block_e[:, None] == experts[None, :]

        def of_block_expert(tab):
            return jnp.sum(jnp.where(is_e[:, :, None], tab.T[None, :, :], 0), axis=1)

        first_unit = (step - jnp.sum(jnp.where(is_e, blk_start[None, :], 0), axis=1)) * BLOCK_GROUPS
        unit = first_unit[:, None] + jnp.arange(BLOCK_GROUPS)[None, :]
        tile_of = jnp.sum(of_block_expert(end // SUBLANES)[:, None, :] <= unit[:, :, None], axis=2)
        valid = (tile_of < n_tiles) & active[:, None]
        is_t = tile_of[:, :, None] == tiles[None, None, :]
        row0 = tiles[None, :] * SLOTS + of_block_expert(off) - of_block_expert(base)
        src = jnp.sum(jnp.where(is_t, row0[:, None, :], 0), axis=2) + unit * SUBLANES
        ffn_src = jnp.where(valid, src, zero_group).astype(jnp.int32).reshape(-1)
        grp = jnp.arange(TILE_GROUPS, dtype=jnp.int32)
        chunk_end_u = (off + n8) // SUBLANES
        exp_of = jnp.sum(chunk_end_u[:, None, :] <= grp[None, :, None], axis=2)
        seg_row0 = (blk_start * ROW_BLOCK)[None, :] + base - off
        comb_src = jnp.sum(jnp.where(exp_of[:, :, None] == experts[None, None, :], seg_row0[:, None, :], 0),
                           axis=2) + grp[None, :] * SUBLANES
        n_groups = chunk_end_u[:, N_EXPERTS - 1].astype(jnp.int32)
        comb_src = jnp.where(grp[None, :] < n_groups[:, None], comb_src, 0).astype(jnp.int32).reshape(-1)

        ys = _ffn_call(l, ffn_src, blk_start.astype(jnp.int32), seg_blocks.astype(jnp.int32),
                       seg_rows.astype(jnp.int32), n_used, n_steps, xs, w_gate, w_up, w_down)
        pos_tok = pos.transpose(0, 2, 1).reshape(n_tok, TOP_K)
        x = _combine_call(comb_src, ys, pos_tok, x1.reshape(n_tok, d), ada[l],
                          ln2_g[l].reshape(1, -1), ln2_b[l].reshape(1, -1), alpha, seq).reshape(bsz, seq, d)
    return x
```

```python
import functools

import jax
import jax.numpy as jnp
from jax import lax
from jax.experimental import pallas as pl
from jax.experimental.pallas import tpu as pltpu

D_MODEL = 1024
HEAD_DIM = D_MODEL // 16
CONV_W = 6 * HEAD_DIM
POOL_W = 4 * HEAD_DIM
SGU_W = 6 * HEAD_DIM
D_IN = 3 * CONV_W + POOL_W + 2 * SGU_W
CONV_K = 3
POOL_WINDOWS = (2, 4, 8, 16)
CHUNK = 128
N_EXPERTS = 32
N_GROUPS = 4
EXPERTS_PER_GROUP = N_EXPERTS // N_GROUPS
TOP_K = 2
D_FF = D_MODEL // 2
ADA_CHUNKS = 6
LN_EPS = 1e-5

OFF_GB = 0
OFF_GC = CONV_W
OFF_P = 3 * CONV_W
OFF_U = OFF_P + POOL_W
OFF_V = OFF_U + SGU_W
YOFF_POOL = CONV_W
YOFF_SGU = CONV_W + POOL_W

LANES = 128
SUBLANES = 8
CONV_HALO = 8
POOL_HALO = 16
SEQ_TILE = 512
SUB_BLOCK = 256
ROW_BLOCK = 2 * SUB_BLOCK
ADA_TILE = 1536
COMBINE_ROWS = 128
GATHER_DEPTH = 3
WEIGHT_DEPTH = 3
WEIGHT_DMA_PRIORITY = 1
VMEM_LIMIT = 60 * 1024 * 1024

SLOTS = TOP_K * SEQ_TILE + N_EXPERTS * SUBLANES
TILE_GROUPS = SLOTS // SUBLANES
BLOCK_GROUPS = ROW_BLOCK // SUBLANES
PACK_W = D_MODEL // 2
XS_W = PACK_W + LANES
META_OFF, META_N8, META_BASE, META_END = 0, 1, 2, 3

F32 = jnp.float32
BF16 = jnp.bfloat16


def _layer_norm(r, g, b):
    mu = jnp.mean(r, axis=-1, keepdims=True)
    d = r - mu
    var = jnp.mean(d * d, axis=-1, keepdims=True)
    return d * lax.rsqrt(var + LN_EPS) * g + b


def _rows(ref, start, size):
    return ref.at[pl.ds(pl.multiple_of(start, SUBLANES), size), :]


def _pack_bf16_pairs(v):
    bits = pltpu.bitcast(v, jnp.int32)
    return bits[:, 0:PACK_W] | lax.shift_right_logical(bits[:, PACK_W:2 * PACK_W], 16)


def _unpack_bf16_pairs(words):
    first = pltpu.bitcast(words & jnp.int32(-65536), F32).astype(BF16)
    second = pltpu.bitcast(lax.shift_left(words, 16), F32).astype(BF16)
    return first, second


def _ada_kernel(c_ref, w_ref, b_ref, o_ref):
    c = c_ref[...]
    c_act = (c * jax.nn.sigmoid(c)).astype(BF16)
    o_ref[0] = jnp.dot(c_act, w_ref[0].astype(BF16), preferred_element_type=F32) + b_ref[0]


def _ada_call(c, w_ada, b_ada):
    depth, d, n = w_ada.shape
    bsz = c.shape[0]
    return pl.pallas_call(
        _ada_kernel,
        grid=(depth, n // ADA_TILE),
        in_specs=[
            pl.BlockSpec((bsz, d), lambda l, j: (0, 0)),
            pl.BlockSpec((1, d, ADA_TILE), lambda l, j: (l, 0, j)),
            pl.BlockSpec((1, 1, ADA_TILE), lambda l, j: (l, 0, j)),
        ],
        out_specs=pl.BlockSpec((1, bsz, ADA_TILE), lambda l, j: (l, 0, j)),
        out_shape=jax.ShapeDtypeStruct((depth, bsz, n), F32),
        compiler_params=pltpu.CompilerParams(
            dimension_semantics=("arbitrary", "arbitrary"), vmem_limit_bytes=VMEM_LIMIT),
        name="ada",
    )(c, w_ada, b_ada.reshape(depth, 1, n))


def _top2_of_group(pg, sub_iota):
    big = float(EXPERTS_PER_GROUP)
    m1 = jnp.max(pg, axis=0, keepdims=True)
    i1 = jnp.min(jnp.where(pg == m1, sub_iota, big), axis=0, keepdims=True)
    rest = jnp.where(sub_iota == i1, -1.0, pg)
    m2 = jnp.max(rest, axis=0, keepdims=True)
    i2 = jnp.min(jnp.where(rest == m2, sub_iota, big), axis=0, keepdims=True)
    return m1, i1, m2, i2


def _mixer_kernel(x_ref, ada_ref, win_ref, wout_ref, convw_ref, convb_ref, poolw_ref, pscale_ref,
                  lng_ref, lnb_ref, sguw_ref, sgub_ref, ln1g_ref, ln1b_ref, wr_ref, br_ref, tri_ref,
                  x1_ref, pos_ref, meta_ref, xs_ref,
                  gx_scr, p_scr, y_scr, base_scr, h2_scr, route_scr, *, ts, alpha, nt, n_tiles):
    i = pl.program_id(0)
    j = jnp.minimum(i, n_tiles - 1) % nt

    @pl.when(i == 0)
    def _():
        gx_scr[0:CONV_HALO, :] = jnp.zeros((CONV_HALO, CONV_W), F32)
        p_scr[0:POOL_HALO, :] = jnp.zeros((POOL_HALO, POOL_W), F32)
        base_scr[...] = jnp.zeros_like(base_scr)
        h2_scr[...] = jnp.zeros_like(h2_scr)
        route_scr[...] = jnp.zeros_like(route_scr)

    def sort_previous_tile():
        pos1, pos2 = route_scr[0:1, :], route_scr[1:2, :]
        w1, w2 = route_scr[2:3, :], route_scr[3:4, :]
        slot_iota = lax.broadcasted_iota(jnp.int32, (SLOTS, ts), 0).astype(F32)
        sel1 = slot_iota == pos1
        sel2 = slot_iota == pos2
        p1f = jnp.where(sel1, 1.0, 0.0)
        p2f = jnp.where(sel2, 1.0, 0.0)
        perm = (p1f + p2f).astype(BF16)
        xs_ref[:, 0:PACK_W] = _pack_bf16_pairs(jnp.dot(perm, h2_scr[...], preferred_element_type=F32))
        w_slot = jnp.sum(jnp.where(sel1, w1, jnp.where(sel2, w2, 0.0)), axis=1, keepdims=True)
        xs_ref[:, PACK_W:XS_W] = pltpu.bitcast(jnp.broadcast_to(w_slot, (SLOTS, LANES)), jnp.int32)

    @pl.when(i < n_tiles)
    def _():
        sort_previous_tile()
        _mixer_tile(j, x_ref, ada_ref, win_ref, wout_ref, convw_ref, convb_ref, poolw_ref, pscale_ref,
                    lng_ref, lnb_ref, sguw_ref, sgub_ref, ln1g_ref, ln1b_ref, wr_ref, br_ref, tri_ref,
                    x1_ref, pos_ref, meta_ref, gx_scr, p_scr, y_scr, base_scr, h2_scr, route_scr,
                    ts=ts, alpha=alpha)

    @pl.when(i == n_tiles)
    def _():
        sort_previous_tile()


def _mixer_tile(j, x_ref, ada_ref, win_ref, wout_ref, convw_ref, convb_ref, poolw_ref, pscale_ref,
                lng_ref, lnb_ref, sguw_ref, sgub_ref, ln1g_ref, ln1b_ref, wr_ref, br_ref, tri_ref,
                x1_ref, pos_ref, meta_ref, gx_scr, p_scr, y_scr, base_scr, h2_scr, route_scr, *, ts, alpha):
    first_of_seq = j == 0
    x = x_ref[0]
    ada = ada_ref[0]
    sh1, sc1, g1 = ada[0:1], ada[1:2], ada[2:3]
    sh2, sc2 = ada[3:4], ada[4:5]
    hb = (x * (1.0 + sc1) + sh1).astype(BF16)

    def proj(lo, hi):
        return jnp.dot(hb, win_ref[:, lo:hi], preferred_element_type=F32)

    gcxc = proj(OFF_GC, OFF_P)
    g = gcxc[:, :CONV_W] * gcxc[:, CONV_W:]
    gx_scr[CONV_HALO:CONV_HALO + ts, :] = g
    cw = convw_ref[...]
    gx_scr[0:CONV_HALO, :] = jnp.where(first_of_seq, 0.0, gx_scr[0:CONV_HALO, :])
    conv = (cw[0:1] * gx_scr[CONV_HALO - 2:CONV_HALO - 2 + ts, :]
            + cw[1:2] * gx_scr[CONV_HALO - 1:CONV_HALO - 1 + ts, :]
            + cw[2:3] * g + convb_ref[...])
    gx_scr[0:CONV_HALO, :] = g[ts - CONV_HALO:ts, :]
    y_scr[:, 0:CONV_W] = (proj(OFF_GB, OFF_GC) * conv).astype(BF16)

    p = proj(OFF_P, OFF_U)
    p_scr[POOL_HALO:POOL_HALO + ts, :] = p
    p_scr[0:POOL_HALO, :] = jnp.where(first_of_seq, 0.0, p_scr[0:POOL_HALO, :])

    def shifted(k, lo):
        return p_scr[POOL_HALO - k:POOL_HALO - k + ts, lo:lo + LANES]

    s2 = p[:, 0:LANES] + shifted(1, 0)
    s4 = s2 + shifted(2, 0) + shifted(3, 0)
    acc = p[:, LANES:2 * LANES]
    for k in range(1, 8):
        acc = acc + shifted(k, LANES)
    s8 = acc
    for k in range(8, 16):
        acc = acc + shifted(k, LANES)
    s16 = acc
    p_scr[0:POOL_HALO, :] = p[ts - POOL_HALO:ts, :]
    lane = lax.broadcasted_iota(jnp.int32, (ts, LANES), 1)
    tpos = (lax.broadcasted_iota(jnp.int32, (ts, LANES), 0) + (j * ts + 1)).astype(F32)
    lo_half = lane < HEAD_DIM
    cnt_a = jnp.minimum(tpos, jnp.where(lo_half, float(POOL_WINDOWS[0]), float(POOL_WINDOWS[1])))
    cnt_b = jnp.minimum(tpos, jnp.where(lo_half, float(POOL_WINDOWS[2]), float(POOL_WINDOWS[3])))
    pooled = jnp.concatenate([jnp.where(lo_half, s2, s4) / cnt_a,
                              jnp.where(lo_half, s8, s16) / cnt_b], axis=1) - p
    mixed = jnp.dot(pooled.astype(BF16), poolw_ref[...], preferred_element_type=F32)
    y_scr[:, YOFF_POOL:YOFF_POOL + POOL_W] = (mixed * pscale_ref[...]).astype(BF16)

    v = proj(OFF_V, D_IN)
    vnb = _layer_norm(v, lng_ref[...], lnb_ref[...]).astype(BF16)
    u = proj(OFF_U, OFF_V)
    row_c = lax.broadcasted_iota(jnp.int32, (CHUNK, 2 * CHUNK), 0)
    col_c = lax.broadcasted_iota(jnp.int32, (CHUNK, 2 * CHUNK), 1)
    causal = (col_c & (CHUNK - 1)) <= row_c
    lo_lanes = lax.broadcasted_iota(jnp.int32, (CHUNK, LANES), 1) < HEAD_DIM
    zero_b = jnp.zeros((CHUNK, LANES), BF16)
    for hp in range(SGU_W // LANES):
        wl = jnp.where(causal, sguw_ref[hp], 0.0).astype(BF16)
        bias = sgub_ref[hp]
        for ci in range(ts // CHUNK):
            rs = slice(ci * CHUNK, (ci + 1) * CHUNK)
            vc = vnb[rs, hp * LANES:(hp + 1) * LANES]
            rhs = jnp.concatenate([jnp.where(lo_lanes, vc, zero_b),
                                   jnp.where(lo_lanes, zero_b, vc)], axis=0)
            mixed_c = jnp.dot(wl, rhs, preferred_element_type=F32) + bias
            y_scr[rs, YOFF_SGU + hp * LANES:YOFF_SGU + (hp + 1) * LANES] = (
                u[rs, hp * LANES:(hp + 1) * LANES] * mixed_c).astype(BF16)

    yo = jnp.dot(y_scr[...], wout_ref[...], preferred_element_type=F32)
    x1 = _layer_norm(alpha * x + g1 * yo, ln1g_ref[...], ln1b_ref[...])
    x1_ref[0] = x1
    h2b = (x1 * (1.0 + sc2) + sh2).astype(BF16)

    logits = jnp.dot(h2b, wr_ref[...], preferred_element_type=F32) + br_ref[...]
    lt = logits.T[0:N_EXPERTS, :]
    ex = jnp.exp(lt - jnp.max(lt, axis=0, keepdims=True))
    probs = ex / jnp.sum(ex, axis=0, keepdims=True)
    sub_iota = lax.broadcasted_iota(jnp.int32, (EXPERTS_PER_GROUP, ts), 0).astype(F32)
    best = None
    for gi in range(N_GROUPS):
        m1, i1, m2, i2 = _top2_of_group(
            probs[gi * EXPERTS_PER_GROUP:(gi + 1) * EXPERTS_PER_GROUP, :], sub_iota)
        score = m1 + m2
        cand = (score, m1, m2, i1 + float(gi * EXPERTS_PER_GROUP), i2 + float(gi * EXPERTS_PER_GROUP))
        if best is None:
            best = cand
        else:
            better = score > best[0]
            best = tuple(jnp.where(better, c, o) for c, o in zip(cand, best))
    _, p1, p2, e1, e2 = best
    den = p1 + p2
    w1 = p1 / den
    w2 = p2 / den

    ex_iota = lax.broadcasted_iota(jnp.int32, (N_EXPERTS, ts), 0).astype(F32)
    oh1 = ex_iota == e1
    oh2 = ex_iota == e2
    either = jnp.where(oh1 | oh2, 1.0, 0.0)
    seen = jnp.dot(either.astype(BF16), tri_ref[...], preferred_element_type=F32)
    n_col = jnp.sum(either, axis=1, keepdims=True)
    n8_col = jnp.ceil(n_col * (1.0 / SUBLANES)) * float(SUBLANES)
    lower = (lax.broadcasted_iota(jnp.int32, (N_EXPERTS, N_EXPERTS), 1)
             < lax.broadcasted_iota(jnp.int32, (N_EXPERTS, N_EXPERTS), 0))
    off_col = jnp.dot(jnp.where(lower, 1.0, 0.0).astype(BF16),
                      jnp.broadcast_to(n8_col, (N_EXPERTS, LANES)).astype(BF16),
                      preferred_element_type=F32)[:, 0:1]
    slot = off_col + seen
    pos1 = jnp.sum(jnp.where(oh1, slot, 0.0), axis=0, keepdims=True)
    pos2 = jnp.sum(jnp.where(oh2, slot, 0.0), axis=0, keepdims=True)
    pos_ref[0] = jnp.concatenate([pos1, pos2], axis=0).astype(jnp.int32)

    base_col = base_scr[:, 0:1]
    end_col = base_col + n8_col
    base_scr[...] = jnp.broadcast_to(end_col, (N_EXPERTS, LANES))
    mlane = lax.broadcasted_iota(jnp.int32, (N_EXPERTS, LANES), 1)
    meta = jnp.where(mlane == META_OFF, off_col,
                     jnp.where(mlane == META_N8, n8_col,
                               jnp.where(mlane == META_BASE, base_col,
                                         jnp.where(mlane == META_END, end_col, 0.0)))).astype(jnp.int32)
    meta_ref[0] = meta

    h2_scr[...] = h2b
    route_scr[...] = jnp.concatenate([pos1, pos2, w1, w2, jnp.zeros((SUBLANES - 4, ts), F32)], axis=0)


def _mixer_call(x, ada_l, w_in, w_out, conv_w, conv_b, poolw_bd, pool_scale, ln_g, ln_b,
                sguw_pair, sgub_pair, ln1_g, ln1_b, wr_pad, br_pad, tri, alpha):
    bsz, s, d = x.shape
    ts = SEQ_TILE
    nt = s // ts
    n_tiles = bsz * nt
    last = n_tiles - 1
    const2 = lambda i: (0, 0)
    const3 = lambda i: (0, 0, 0)
    tile3 = lambda i: (jnp.minimum(i, last) // nt, jnp.minimum(i, last) % nt, 0)
    meta3 = lambda i: (jnp.minimum(i, last), 0, 0)
    once = pl.Buffered(1)
    kern = functools.partial(_mixer_kernel, ts=ts, alpha=alpha, nt=nt, n_tiles=n_tiles)
    return pl.pallas_call(
        kern,
        grid=(n_tiles + 1,),
        in_specs=[
            pl.BlockSpec((1, ts, d), tile3),
            pl.BlockSpec((1, ADA_CHUNKS, d), lambda i: (jnp.minimum(i, last) // nt, 0, 0)),
            pl.BlockSpec((d, D_IN), const2, pipeline_mode=once),
            pl.BlockSpec((d, d), const2, pipeline_mode=once),
            pl.BlockSpec((CONV_K, CONV_W), const2),
            pl.BlockSpec((1, CONV_W), const2),
            pl.BlockSpec((POOL_W, POOL_W), const2),
            pl.BlockSpec((1, POOL_W), const2),
            pl.BlockSpec((1, SGU_W), const2),
            pl.BlockSpec((1, SGU_W), const2),
            pl.BlockSpec((SGU_W // LANES, CHUNK, 2 * CHUNK), const3),
            pl.BlockSpec((SGU_W // LANES, CHUNK, LANES), const3),
            pl.BlockSpec((1, d), const2),
            pl.BlockSpec((1, d), const2),
            pl.BlockSpec((d, LANES), const2),
            pl.BlockSpec((1, LANES), const2),
            pl.BlockSpec((ts, ts), const2, pipeline_mode=once),
        ],
        out_specs=[
            pl.BlockSpec((1, ts, d), tile3),
            pl.BlockSpec((1, TOP_K, ts), meta3),
            pl.BlockSpec((1, N_EXPERTS, LANES), meta3),
            pl.BlockSpec((SLOTS, XS_W), lambda i: (jnp.maximum(i - 1, 0), 0)),
        ],
        out_shape=[
            jax.ShapeDtypeStruct((bsz, s, d), F32),
            jax.ShapeDtypeStruct((n_tiles, TOP_K, ts), jnp.int32),
            jax.ShapeDtypeStruct((n_tiles, N_EXPERTS, LANES), jnp.int32),
            jax.ShapeDtypeStruct((n_tiles * SLOTS, XS_W), jnp.int32),
        ],
        scratch_shapes=[
            pltpu.VMEM((CONV_HALO + ts, CONV_W), F32),
            pltpu.VMEM((POOL_HALO + ts, POOL_W), F32),
            pltpu.VMEM((ts, d), BF16),
            pltpu.VMEM((N_EXPERTS, LANES), F32),
            pltpu.VMEM((ts, d), BF16),
            pltpu.VMEM((SUBLANES, ts), F32),
        ],
        compiler_params=pltpu.CompilerParams(
            dimension_semantics=("arbitrary",), vmem_limit_bytes=VMEM_LIMIT),
        name="mixer_router",
    )(x, ada_l, w_in, w_out, conv_w, conv_b, poolw_bd, pool_scale, ln_g, ln_b,
      sguw_pair, sgub_pair, ln1_g, ln1_b, wr_pad, br_pad, tri)


def _ffn_kernel(src_ref, b0_ref, nb_ref, rows_ref, tot_ref, xs_ref, wg_ref, wu_ref, wd_ref, ys_ref,
                xbuf, obuf, wbuf_g, wbuf_u, wbuf_d, wg_scr, wu_scr, wd_scr, gsem, osem, wsem,
                *, layer, n_blocks):
    e = pl.program_id(0)
    n_exp = pl.num_programs(0)
    b0 = b0_ref[e]
    nb = nb_ref[e]
    n_used = tot_ref[0]

    def gather(g):
        slot = g % GATHER_DEPTH
        for r in range(BLOCK_GROUPS):
            pltpu.make_async_copy(_rows(xs_ref, src_ref[g * BLOCK_GROUPS + r], SUBLANES),
                                  xbuf.at[slot, pl.ds(r * SUBLANES, SUBLANES), :], gsem.at[slot]).start()

    def out_copy(g, slot):
        return pltpu.make_async_copy(obuf.at[slot], _rows(ys_ref, g * ROW_BLOCK, ROW_BLOCK), osem.at[slot])

    def weight_copies(expert, slot):
        return [pltpu.make_async_copy(src.at[layer, expert], dst.at[slot], wsem.at[slot])
                for src, dst in ((wg_ref, wbuf_g), (wu_ref, wbuf_u), (wd_ref, wbuf_d))]

    @pl.when(e == 0)
    def _():
        for first in range(WEIGHT_DEPTH - 1):
            for cp in weight_copies(first, first):
                cp.start(priority=WEIGHT_DMA_PRIORITY)
        for ahead in range(GATHER_DEPTH - 1):
            @pl.when(ahead < n_used)
            def _():
                gather(ahead)

    @pl.when(e + (WEIGHT_DEPTH - 1) < n_exp)
    def _():
        for cp in weight_copies(e + (WEIGHT_DEPTH - 1), (e + (WEIGHT_DEPTH - 1)) % WEIGHT_DEPTH):
            cp.start(priority=WEIGHT_DMA_PRIORITY)

    wslot = e % WEIGHT_DEPTH
    for cp in weight_copies(e, wslot):
        cp.wait()

    @pl.when(nb > 0)
    def _():
        wg_scr[...] = wbuf_g[wslot].astype(BF16)
        wu_scr[...] = wbuf_u[wslot].astype(BF16)
        wd_scr[...] = wbuf_d[wslot].astype(BF16)

    def block(k, carry):
        g = b0 + k
        islot = g % GATHER_DEPTH
        oslot = g % 2

        @pl.when(g + (GATHER_DEPTH - 1) < n_used)
        def _():
            gather(g + (GATHER_DEPTH - 1))

        pltpu.make_async_copy(_rows(xs_ref, 0, ROW_BLOCK), xbuf.at[islot], gsem.at[islot]).wait()

        @pl.when(g >= 2)
        def _():
            out_copy(g - 2, oslot).wait()

        def chain(half):
            rows = pl.ds(half * SUB_BLOCK, SUB_BLOCK)
            xb = jnp.concatenate(_unpack_bf16_pairs(xbuf[islot, rows, 0:PACK_W]), axis=1)
            wrow = pltpu.bitcast(xbuf[islot, rows, PACK_W:XS_W], F32)[:, 0:1]
            gate = jnp.dot(xb, wg_scr[...], preferred_element_type=F32)
            up = jnp.dot(xb, wu_scr[...], preferred_element_type=F32)
            act = (gate * jax.nn.sigmoid(gate) * up).astype(BF16)
            y = jnp.dot(act, wd_scr[...], preferred_element_type=F32) * wrow
            obuf[oslot, rows, :] = _pack_bf16_pairs(y.astype(BF16).astype(F32))

        both = rows_ref[e] - k * ROW_BLOCK > SUB_BLOCK

        @pl.when(both)
        def _():
            chain(0)
            chain(1)

        @pl.when(jnp.logical_not(both))
        def _():
            chain(0)
            obuf[oslot, pl.ds(SUB_BLOCK, SUB_BLOCK), :] = jnp.zeros((SUB_BLOCK, PACK_W), jnp.int32)

        out_copy(g, oslot).start()
        return carry

    lax.fori_loop(0, nb, block, 0)

    @pl.when(e == n_exp - 1)
    def _():
        @pl.when(n_used >= 2)
        def _():
            out_copy(n_used - 2, n_used % 2).wait()

        out_copy(n_used - 1, (n_used - 1) % 2).wait()
        obuf[0] = jnp.zeros((ROW_BLOCK, PACK_W), jnp.int32)

        def fill(g, carry):
            out_copy(g, 0).start()
            return carry

        def drain(g, carry):
            out_copy(g, 0).wait()
            return carry

        lax.fori_loop(n_used, n_blocks, fill, 0)
        lax.fori_loop(n_used, n_blocks, drain, 0)


def _ffn_call(layer, group_src, blk_start, seg_blocks, seg_rows, n_used, n_blocks, xs, w_gate, w_up, w_down):
    d, f = w_gate.shape[-2:]
    any_spec = pl.BlockSpec(memory_space=pl.ANY)
    return pl.pallas_call(
        functools.partial(_ffn_kernel, layer=layer, n_blocks=n_blocks),
        grid_spec=pltpu.PrefetchScalarGridSpec(
            num_scalar_prefetch=5,
            grid=(N_EXPERTS,),
            in_specs=[any_spec, any_spec, any_spec, any_spec],
            out_specs=any_spec,
            scratch_shapes=[
                pltpu.VMEM((GATHER_DEPTH, ROW_BLOCK, XS_W), jnp.int32),
                pltpu.VMEM((2, ROW_BLOCK, PACK_W), jnp.int32),
                pltpu.VMEM((WEIGHT_DEPTH, d, f), F32),
                pltpu.VMEM((WEIGHT_DEPTH, d, f), F32),
                pltpu.VMEM((WEIGHT_DEPTH, f, d), F32),
                pltpu.VMEM((d, f), BF16),
                pltpu.VMEM((d, f), BF16),
                pltpu.VMEM((f, d), BF16),
                pltpu.SemaphoreType.DMA((GATHER_DEPTH,)),
                pltpu.SemaphoreType.DMA((2,)),
                pltpu.SemaphoreType.DMA((WEIGHT_DEPTH,)),
            ],
        ),
        out_shape=jax.ShapeDtypeStruct((n_blocks * ROW_BLOCK, PACK_W), jnp.int32),
        compiler_params=pltpu.CompilerParams(
            dimension_semantics=("arbitrary",), vmem_limit_bytes=VMEM_LIMIT),
        name="expert_ffn",
    )(group_src, blk_start, seg_blocks, seg_rows, n_used, xs, w_gate, w_up, w_down)


def _combine_kernel(src_ref, ys_ref, pos_ref, x1_ref, ada_ref, g_ref, b_ref,
                    o_ref, buf_even, buf_odd, sem, *, ts, alpha):
    i = pl.program_id(0)
    n = pl.num_programs(0)

    def gather(tile, buf, s):
        for q in range(TILE_GROUPS):
            pltpu.make_async_copy(_rows(ys_ref, src_ref[tile * TILE_GROUPS + q], SUBLANES),
                                  buf.at[pl.ds(q * SUBLANES, SUBLANES), :], s).start()

    def wait(buf, s):
        pltpu.make_async_copy(_rows(ys_ref, 0, SLOTS), buf, s).wait()

    @pl.when(i == 0)
    def _():
        gather(0, buf_even, sem.at[0])

    def step(mine, other, s_mine, s_other):
        gather(jnp.minimum(i + 1, n - 1), other, s_other)
        wait(mine, s_mine)
        first, second = _unpack_bf16_pairs(mine[...])
        g2 = ada_ref[0][5:6]
        slot_iota = lax.broadcasted_iota(jnp.int32, (COMBINE_ROWS, SLOTS), 1)
        for c in range(ts // COMBINE_ROWS):
            rows = pl.ds(c * COMBINE_ROWS, COMBINE_ROWS)
            pos = pos_ref[rows, :]
            pick = (slot_iota == pos[:, 0:1]) | (slot_iota == pos[:, 1:2])
            unsort = jnp.where(pick, 1.0, 0.0).astype(BF16)
            y = jnp.concatenate([jnp.dot(unsort, first, preferred_element_type=F32),
                                 jnp.dot(unsort, second, preferred_element_type=F32)], axis=1)
            o_ref[rows, :] = _layer_norm(alpha * x1_ref[rows, :] + g2 * y, g_ref[...], b_ref[...])

        @pl.when(i == n - 1)
        def _():
            wait(other, s_other)

    @pl.when(i % 2 == 0)
    def _():
        step(buf_even, buf_odd, sem.at[0], sem.at[1])

    @pl.when(i % 2 == 1)
    def _():
        step(buf_odd, buf_even, sem.at[1], sem.at[0])


def _combine_call(group_src, ys, pos_tok, x1_flat, ada_l, ln_g, ln_b, alpha, seq):
    n_tok, d = x1_flat.shape
    ts = SEQ_TILE
    n_tiles = n_tok // ts
    per_seq = seq // ts
    return pl.pallas_call(
        functools.partial(_combine_kernel, ts=ts, alpha=alpha),
        grid_spec=pltpu.PrefetchScalarGridSpec(
            num_scalar_prefetch=1,
            grid=(n_tiles,),
            in_specs=[
                pl.BlockSpec(memory_space=pl.ANY),
                pl.BlockSpec((ts, TOP_K), lambda i, src: (i, 0)),
                pl.BlockSpec((ts, d), lambda i, src: (i, 0)),
                pl.BlockSpec((1, ADA_CHUNKS, d), lambda i, src: (i // per_seq, 0, 0)),
                pl.BlockSpec((1, d), lambda i, src: (0, 0)),
                pl.BlockSpec((1, d), lambda i, src: (0, 0)),
            ],
            out_specs=pl.BlockSpec((ts, d), lambda i, src: (i, 0)),
            scratch_shapes=[
                pltpu.VMEM((SLOTS, PACK_W), jnp.int32),
                pltpu.VMEM((SLOTS, PACK_W), jnp.int32),
                pltpu.SemaphoreType.DMA((2,)),
            ],
        ),
        out_shape=jax.ShapeDtypeStruct((n_tok, d), F32),
        compiler_params=pltpu.CompilerParams(
            dimension_semantics=("arbitrary",), vmem_limit_bytes=VMEM_LIMIT),
        name="combine_ln",
    )(group_src, ys, pos_tok, x1_flat, ada_l, ln_g, ln_b)


def kernel(x, c, w_ada, b_ada, w_in, conv_w, conv_b, pool_w, pool_scale, sgu_ln_g, sgu_ln_b,
           sgu_w, sgu_b, w_out, ln1_g, ln1_b, w_router, b_router, w_gate, w_up, w_down,
           ln2_g, ln2_b):
    bsz, seq, d = x.shape
    depth = w_ada.shape[0]
    n_tok = bsz * seq
    n_tiles = n_tok // SEQ_TILE
    alpha = (2 * depth) ** 0.25
    assert d == D_MODEL and seq % SEQ_TILE == 0 and SEQ_TILE % CHUNK == 0

    ada = _ada_call(c, w_ada, b_ada).reshape(depth, bsz, ADA_CHUNKS, d)

    w_in_b = w_in.astype(BF16)
    w_out_b = w_out.astype(BF16)
    eye_g = jnp.eye(POOL_W // HEAD_DIM, dtype=F32)
    poolw_bd = jnp.einsum('lgcd,gh->lgchd', pool_w, eye_g).reshape(depth, POOL_W, POOL_W).astype(BF16)
    n_pairs = SGU_W // LANES
    sguw_pair = sgu_w.reshape(depth, n_pairs, 2, CHUNK, CHUNK).transpose(0, 1, 3, 2, 4).reshape(
        depth, n_pairs, CHUNK, 2 * CHUNK)
    sgub_pair = jnp.repeat(sgu_b.transpose(0, 2, 1), HEAD_DIM, axis=-1).reshape(
        depth, CHUNK, n_pairs, LANES).transpose(0, 2, 1, 3)
    wr_pad = jnp.pad(w_router, ((0, 0), (0, LANES - N_EXPERTS))).astype(BF16)
    br_pad = jnp.pad(b_router, (0, LANES - N_EXPERTS)).reshape(1, LANES)

    max_rows = n_tok * TOP_K + (SUBLANES - 1) * N_EXPERTS * n_tiles
    n_steps = -(-max_rows // ROW_BLOCK) + N_EXPERTS
    step = jnp.arange(n_steps, dtype=jnp.int32)
    experts = jnp.arange(N_EXPERTS, dtype=jnp.int32)
    tiles = jnp.arange(n_tiles, dtype=jnp.int32)
    tri = jnp.triu(jnp.ones((SEQ_TILE, SEQ_TILE), BF16), k=1)
    zero_group = SLOTS - SUBLANES

    for l in range(depth):
        x1, pos, meta, xs = _mixer_call(
            x, ada[l], w_in_b[l], w_out_b[l], conv_w[l], conv_b[l].reshape(1, -1), poolw_bd[l],
            pool_scale[l].reshape(1, -1), sgu_ln_g[l].reshape(1, -1), sgu_ln_b[l].reshape(1, -1),
            sguw_pair[l], sgub_pair[l], ln1_g[l].reshape(1, -1), ln1_b[l].reshape(1, -1),
            wr_pad, br_pad, tri, alpha)
        off = meta[:, :, META_OFF]
        n8 = meta[:, :, META_N8]
        base = meta[:, :, META_BASE]
        end = meta[:, :, META_END]
        seg_rows = end[n_tiles - 1]
        seg_blocks = (seg_rows + ROW_BLOCK - 1) // ROW_BLOCK
        blk_end = jnp.cumsum(seg_blocks)
        blk_start = blk_end - seg_blocks
        n_used = blk_end[-1:].astype(jnp.int32)
        active = step < n_used[0]
        block_e = jnp.minimum(jnp.sum(step[:, None] >= blk_end[None, :], axis=1), N_EXPERTS - 1)
        is_e = block_e[:, None] == experts[None, :]

        def of_block_expert(tab):
            return jnp.sum(jnp.where(is_e[:, :, None], tab.T[None, :, :], 0), axis=1)

        first_unit = (step - jnp.sum(jnp.where(is_e, blk_start[None, :], 0), axis=1)) * BLOCK_GROUPS
        unit = first_unit[:, None] + jnp.arange(BLOCK_GROUPS)[None, :]
        tile_of = jnp.sum(of_block_expert(end // SUBLANES)[:, None, :] <= unit[:, :, None], axis=2)
        valid = (tile_of < n_tiles) & active[:, None]
        is_t = tile_of[:, :, None] == tiles[None, None, :]
        row0 = tiles[None, :] * SLOTS + of_block_expert(off) - of_block_expert(base)
        src = jnp.sum(jnp.where(is_t, row0[:, None, :], 0), axis=2) + unit * SUBLANES
        ffn_src = jnp.where(valid, src, zero_group).astype(jnp.int32).reshape(-1)
        grp = jnp.arange(TILE_GROUPS, dtype=jnp.int32)
        chunk_end_u = (off + n8) // SUBLANES
        exp_of = jnp.sum(chunk_end_u[:, None, :] <= grp[None, :, None], axis=2)
        seg_row0 = (blk_start * ROW_BLOCK)[None, :] + base - off
        comb_src = jnp.sum(jnp.where(exp_of[:, :, None] == experts[None, None, :], seg_row0[:, None, :], 0),
                           axis=2) + grp[None, :] * SUBLANES
        n_groups = chunk_end_u[:, N_EXPERTS - 1].astype(jnp.int32)
        comb_src = jnp.where(grp[None, :] < n_groups[:, None], comb_src, 0).astype(jnp.int32).reshape(-1)

        ys = _ffn_call(l, ffn_src, blk_start.astype(jnp.int32), seg_blocks.astype(jnp.int32),
                       seg_rows.astype(jnp.int32), n_used, n_steps, xs, w_gate, w_up, w_down)
        pos_tok = pos.transpose(0, 2, 1).reshape(n_tok, TOP_K)
        x = _combine_call(comb_src, ys, pos_tok, x1.reshape(n_tok, d), ada[l],
                          ln2_g[l].reshape(1, -1), ln2_b[l].reshape(1, -1), alpha, seq).reshape(bsz, seq, d)
    return x
```

```python
import functools

import jax
import jax.numpy as jnp
from jax import lax
from jax.experimental import pallas as pl
from jax.experimental.pallas import tpu as pltpu

D_MODEL = 1024
HEAD_DIM = D_MODEL // 16
CONV_W = 6 * HEAD_DIM
POOL_W = 4 * HEAD_DIM
SGU_W = 6 * HEAD_DIM
D_IN = 3 * CONV_W + POOL_W + 2 * SGU_W
CONV_K = 3
POOL_WINDOWS = (2, 4, 8, 16)
CHUNK = 128
N_EXPERTS = 32
N_GROUPS = 4
EXPERTS_PER_GROUP = N_EXPERTS // N_GROUPS
TOP_K = 2
D_FF = D_MODEL // 2
ADA_CHUNKS = 6
LN_EPS = 1e-5

OFF_GB = 0
OFF_GC = CONV_W
OFF_P = 3 * CONV_W
OFF_U = OFF_P + POOL_W
OFF_V = OFF_U + SGU_W
YOFF_POOL = CONV_W
YOFF_SGU = CONV_W + POOL_W

LANES = 128
SUBLANES = 8
CONV_HALO = 8
POOL_HALO = 16
SEQ_TILE = 512
SUB_BLOCK = 256
ROW_BLOCK = 2 * SUB_BLOCK
ADA_TILE = 1536
COMBINE_ROWS = 128
GATHER_DEPTH = 3
WEIGHT_DEPTH = 3
WEIGHT_DMA_PRIORITY = 1
VMEM_LIMIT = 60 * 1024 * 1024

SLOTS = TOP_K * SEQ_TILE + N_EXPERTS * SUBLANES
TILE_GROUPS = SLOTS // SUBLANES
BLOCK_GROUPS = ROW_BLOCK // SUBLANES
PACK_W = D_MODEL // 2
XS_W = PACK_W + LANES
META_OFF, META_N8, META_BASE, META_END = 0, 1, 2, 3

F32 = jnp.float32
BF16 = jnp.bfloat16


def _layer_norm(r, g, b):
    mu = jnp.mean(r, axis=-1, keepdims=True)
    d = r - mu
    var = jnp.mean(d * d, axis=-1, keepdims=True)
    return d * lax.rsqrt(var + LN_EPS) * g + b


def _rows(ref, start, size):
    return ref.at[pl.ds(pl.multiple_of(start, SUBLANES), size), :]


def _pack_bf16_pairs(v):
    bits = pltpu.bitcast(v, jnp.int32)
    return bits[:, 0:PACK_W] | lax.shift_right_logical(bits[:, PACK_W:2 * PACK_W], 16)


def _unpack_bf16_pairs(words):
    first = pltpu.bitcast(words & jnp.int32(-65536), F32).astype(BF16)
    second = pltpu.bitcast(lax.shift_left(words, 16), F32).astype(BF16)
    return first, second


def _ada_kernel(c_ref, w_ref, b_ref, o_ref):
    c = c_ref[...]
    c_act = (c * jax.nn.sigmoid(c)).astype(BF16)
    o_ref[0] = jnp.dot(c_act, w_ref[0].astype(BF16), preferred_element_type=F32) + b_ref[0]


def _ada_call(c, w_ada, b_ada):
    depth, d, n = w_ada.shape
    bsz = c.shape[0]
    return pl.pallas_call(
        _ada_kernel,
        grid=(depth, n // ADA_TILE),
        in_specs=[
            pl.BlockSpec((bsz, d), lambda l, j: (0, 0)),
            pl.BlockSpec((1, d, ADA_TILE), lambda l, j: (l, 0, j)),
            pl.BlockSpec((1, 1, ADA_TILE), lambda l, j: (l, 0, j)),
        ],
        out_specs=pl.BlockSpec((1, bsz, ADA_TILE), lambda l, j: (l, 0, j)),
        out_shape=jax.ShapeDtypeStruct((depth, bsz, n), F32),
        compiler_params=pltpu.CompilerParams(
            dimension_semantics=("arbitrary", "arbitrary"), vmem_limit_bytes=VMEM_LIMIT),
        name="ada",
    )(c, w_ada, b_ada.reshape(depth, 1, n))


def _top2_of_group(pg, sub_iota):
    big = float(EXPERTS_PER_GROUP)
    m1 = jnp.max(pg, axis=0, keepdims=True)
    i1 = jnp.min(jnp.where(pg == m1, sub_iota, big), axis=0, keepdims=True)
    rest = jnp.where(sub_iota == i1, -1.0, pg)
    m2 = jnp.max(rest, axis=0, keepdims=True)
    i2 = jnp.min(jnp.where(rest == m2, sub_iota, big), axis=0, keepdims=True)
    return m1, i1, m2, i2


def _unsort_tile(buf, pos_ref, x1_ref, gate, g_ref, b_ref, out_ref, *, ts, alpha):
    first, second = _unpack_bf16_pairs(buf)
    slot_iota = lax.broadcasted_iota(jnp.int32, (COMBINE_ROWS, SLOTS), 1)
    for c in range(ts // COMBINE_ROWS):
        rows = pl.ds(c * COMBINE_ROWS, COMBINE_ROWS)
        pos = pos_ref[rows, :]
        pick = (slot_iota == pos[:, 0:1]) | (slot_iota == pos[:, 1:2])
        unsort = jnp.where(pick, 1.0, 0.0).astype(BF16)
        y = jnp.concatenate([jnp.dot(unsort, first, preferred_element_type=F32),
                             jnp.dot(unsort, second, preferred_element_type=F32)], axis=1)
        out_ref[rows, :] = _layer_norm(alpha * x1_ref[rows, :] + gate * y, g_ref[...], b_ref[...])


def _mixer_kernel(*refs, ts, alpha, nt, n_tiles, fused):
    if fused:
        src_ref, x_ref = refs[0], None
        xprev_ref, ysp_ref, posp_ref, adap_ref, ln2g_ref, ln2b_ref = refs[1:7]
        refs = refs[7:]
    else:
        x_ref, refs = refs[0], refs[1:]
    (ada_ref, win_ref, wout_ref, convw_ref, convb_ref, poolw_ref, pscale_ref, lng_ref, lnb_ref,
     sguw_ref, sgub_ref, ln1g_ref, ln1b_ref, wr_ref, br_ref, tri_ref,
     x1_ref, pos_ref, meta_ref, xs_ref,
     gx_scr, p_scr, y_scr, base_scr, h2_scr, route_scr) = refs[:26]
    if fused:
        x_scr, cbuf, csem = refs[26:]

    i = pl.program_id(0)
    j = jnp.minimum(i, n_tiles - 1) % nt

    def gather_prev_layer(tile):
        slot = tile % 2

        def issue(q, carry):
            pltpu.make_async_copy(_rows(ysp_ref, src_ref[tile * TILE_GROUPS + q], SUBLANES),
                                  cbuf.at[slot, pl.ds(pl.multiple_of(q * SUBLANES, SUBLANES), SUBLANES), :],
                                  csem.at[slot]).start()
            return carry

        lax.fori_loop(0, TILE_GROUPS, issue, 0, unroll=8)

    @pl.when(i == 0)
    def _():
        gx_scr[0:CONV_HALO, :] = jnp.zeros((CONV_HALO, CONV_W), F32)
        p_scr[0:POOL_HALO, :] = jnp.zeros((POOL_HALO, POOL_W), F32)
        base_scr[...] = jnp.zeros_like(base_scr)
        h2_scr[...] = jnp.zeros_like(h2_scr)
        route_scr[...] = jnp.zeros_like(route_scr)
        if fused:
            gather_prev_layer(0)

    if fused:
        @pl.when(i + 1 < n_tiles)
        def _():
            gather_prev_layer(i + 1)

    def sort_previous_tile():
        pos1, pos2 = route_scr[0:1, :], route_scr[1:2, :]
        w1, w2 = route_scr[2:3, :], route_scr[3:4, :]
        slot_iota = lax.broadcasted_iota(jnp.int32, (SLOTS, ts), 0).astype(F32)
        sel1 = slot_iota == pos1
        sel2 = slot_iota == pos2
        p1f = jnp.where(sel1, 1.0, 0.0)
        p2f = jnp.where(sel2, 1.0, 0.0)
        perm = (p1f + p2f).astype(BF16)
        xs_ref[:, 0:PACK_W] = _pack_bf16_pairs(jnp.dot(perm, h2_scr[...], preferred_element_type=F32))
        w_slot = jnp.sum(jnp.where(sel1, w1, jnp.where(sel2, w2, 0.0)), axis=1, keepdims=True)
        xs_ref[:, PACK_W:XS_W] = pltpu.bitcast(jnp.broadcast_to(w_slot, (SLOTS, LANES)), jnp.int32)

    @pl.when(i < n_tiles)
    def _():
        if fused:
            slot = i % 2
            pltpu.make_async_copy(_rows(ysp_ref, 0, SLOTS), cbuf.at[slot], csem.at[slot]).wait()
        sort_previous_tile()
        if fused:
            _unsort_tile(cbuf[slot], posp_ref, xprev_ref, adap_ref[0][5:6], ln2g_ref, ln2b_ref, x_scr,
                         ts=ts, alpha=alpha)
            x = x_scr[...]
        else:
            x = x_ref[0]
        _mixer_tile(j, x, ada_ref, win_ref, wout_ref, convw_ref, convb_ref, poolw_ref, pscale_ref,
                    lng_ref, lnb_ref, sguw_ref, sgub_ref, ln1g_ref, ln1b_ref, wr_ref, br_ref, tri_ref,
                    x1_ref, pos_ref, meta_ref, gx_scr, p_scr, y_scr, base_scr, h2_scr, route_scr,
                    ts=ts, alpha=alpha)

    @pl.when(i == n_tiles)
    def _():
        sort_previous_tile()


def _mixer_tile(j, x, ada_ref, win_ref, wout_ref, convw_ref, convb_ref, poolw_ref, pscale_ref,
                lng_ref, lnb_ref, sguw_ref, sgub_ref, ln1g_ref, ln1b_ref, wr_ref, br_ref, tri_ref,
                x1_ref, pos_ref, meta_ref, gx_scr, p_scr, y_scr, base_scr, h2_scr, route_scr, *, ts, alpha):
    first_of_seq = j == 0
    ada = ada_ref[0]
    sh1, sc1, g1 = ada[0:1], ada[1:2], ada[2:3]
    sh2, sc2 = ada[3:4], ada[4:5]
    hb = (x * (1.0 + sc1) + sh1).astype(BF16)

    def proj(lo, hi):
        return jnp.dot(hb, win_ref[:, lo:hi], preferred_element_type=F32)

    gcxc = proj(OFF_GC, OFF_P)
    g = gcxc[:, :CONV_W] * gcxc[:, CONV_W:]
    gx_scr[CONV_HALO:CONV_HALO + ts, :] = g
    cw = convw_ref[...]
    gx_scr[0:CONV_HALO, :] = jnp.where(first_of_seq, 0.0, gx_scr[0:CONV_HALO, :])
    conv = (cw[0:1] * gx_scr[CONV_HALO - 2:CONV_HALO - 2 + ts, :]
            + cw[1:2] * gx_scr[CONV_HALO - 1:CONV_HALO - 1 + ts, :]
            + cw[2:3] * g + convb_ref[...])
    gx_scr[0:CONV_HALO, :] = g[ts - CONV_HALO:ts, :]
    y_scr[:, 0:CONV_W] = (proj(OFF_GB, OFF_GC) * conv).astype(BF16)

    p = proj(OFF_P, OFF_U)
    p_scr[POOL_HALO:POOL_HALO + ts, :] = p
    p_scr[0:POOL_HALO, :] = jnp.where(first_of_seq, 0.0, p_scr[0:POOL_HALO, :])

    def shifted(k, lo):
        return p_scr[POOL_HALO - k:POOL_HALO - k + ts, lo:lo + LANES]

    s2 = p[:, 0:LANES] + shifted(1, 0)
    s4 = s2 + shifted(2, 0) + shifted(3, 0)
    acc = p[:, LANES:2 * LANES]
    for k in range(1, 8):
        acc = acc + shifted(k, LANES)
    s8 = acc
    for k in range(8, 16):
        acc = acc + shifted(k, LANES)
    s16 = acc
    p_scr[0:POOL_HALO, :] = p[ts - POOL_HALO:ts, :]
    lane = lax.broadcasted_iota(jnp.int32, (ts, LANES), 1)
    tpos = (lax.broadcasted_iota(jnp.int32, (ts, LANES), 0) + (j * ts + 1)).astype(F32)
    lo_half = lane < HEAD_DIM
    cnt_a = jnp.minimum(tpos, jnp.where(lo_half, float(POOL_WINDOWS[0]), float(POOL_WINDOWS[1])))
    cnt_b = jnp.minimum(tpos, jnp.where(lo_half, float(POOL_WINDOWS[2]), float(POOL_WINDOWS[3])))
    pooled = jnp.concatenate([jnp.where(lo_half, s2, s4) / cnt_a,
                              jnp.where(lo_half, s8, s16) / cnt_b], axis=1) - p
    mixed = jnp.dot(pooled.astype(BF16), poolw_ref[...], preferred_element_type=F32)
    y_scr[:, YOFF_POOL:YOFF_POOL + POOL_W] = (mixed * pscale_ref[...]).astype(BF16)

    v = proj(OFF_V, D_IN)
    vnb = _layer_norm(v, lng_ref[...], lnb_ref[...]).astype(BF16)
    u = proj(OFF_U, OFF_V)
    row_c = lax.broadcasted_iota(jnp.int32, (CHUNK, 2 * CHUNK), 0)
    col_c = lax.broadcasted_iota(jnp.int32, (CHUNK, 2 * CHUNK), 1)
    causal = (col_c & (CHUNK - 1)) <= row_c
    lo_lanes = lax.broadcasted_iota(jnp.int32, (CHUNK, LANES), 1) < HEAD_DIM
    zero_b = jnp.zeros((CHUNK, LANES), BF16)
    for hp in range(SGU_W // LANES):
        wl = jnp.where(causal, sguw_ref[hp], 0.0).astype(BF16)
        bias = sgub_ref[hp]
        for ci in range(ts // CHUNK):
            rs = slice(ci * CHUNK, (ci + 1) * CHUNK)
            vc = vnb[rs, hp * LANES:(hp + 1) * LANES]
            rhs = jnp.concatenate([jnp.where(lo_lanes, vc, zero_b),
                                   jnp.where(lo_lanes, zero_b, vc)], axis=0)
            mixed_c = jnp.dot(wl, rhs, preferred_element_type=F32) + bias
            y_scr[rs, YOFF_SGU + hp * LANES:YOFF_SGU + (hp + 1) * LANES] = (
                u[rs, hp * LANES:(hp + 1) * LANES] * mixed_c).astype(BF16)

    yo = jnp.dot(y_scr[...], wout_ref[...], preferred_element_type=F32)
    x1 = _layer_norm(alpha * x + g1 * yo, ln1g_ref[...], ln1b_ref[...])
    x1_ref[0] = x1
    h2b = (x1 * (1.0 + sc2) + sh2).astype(BF16)

    logits = jnp.dot(h2b, wr_ref[...], preferred_element_type=F32) + br_ref[...]
    lt = logits.T[0:N_EXPERTS, :]
    ex = jnp.exp(lt - jnp.max(lt, axis=0, keepdims=True))
    probs = ex / jnp.sum(ex, axis=0, keepdims=True)
    sub_iota = lax.broadcasted_iota(jnp.int32, (EXPERTS_PER_GROUP, ts), 0).astype(F32)
    best = None
    for gi in range(N_GROUPS):
        m1, i1, m2, i2 = _top2_of_group(
            probs[gi * EXPERTS_PER_GROUP:(gi + 1) * EXPERTS_PER_GROUP, :], sub_iota)
        score = m1 + m2
        cand = (score, m1, m2, i1 + float(gi * EXPERTS_PER_GROUP), i2 + float(gi * EXPERTS_PER_GROUP))
        if best is None:
            best = cand
        else:
            better = score > best[0]
            best = tuple(jnp.where(better, c, o) for c, o in zip(cand, best))
    _, p1, p2, e1, e2 = best
    den = p1 + p2
    w1 = p1 / den
    w2 = p2 / den

    ex_iota = lax.broadcasted_iota(jnp.int32, (N_EXPERTS, ts), 0).astype(F32)
    oh1 = ex_iota == e1
    oh2 = ex_iota == e2
    either = jnp.where(oh1 | oh2, 1.0, 0.0)
    seen = jnp.dot(either.astype(BF16), tri_ref[...], preferred_element_type=F32)
    n_col = jnp.sum(either, axis=1, keepdims=True)
    n8_col = jnp.ceil(n_col * (1.0 / SUBLANES)) * float(SUBLANES)
    lower = (lax.broadcasted_iota(jnp.int32, (N_EXPERTS, N_EXPERTS), 1)
             < lax.broadcasted_iota(jnp.int32, (N_EXPERTS, N_EXPERTS), 0))
    off_col = jnp.dot(jnp.where(lower, 1.0, 0.0).astype(BF16),
                      jnp.broadcast_to(n8_col, (N_EXPERTS, LANES)).astype(BF16),
                      preferred_element_type=F32)[:, 0:1]
    slot = off_col + seen
    pos1 = jnp.sum(jnp.where(oh1, slot, 0.0), axis=0, keepdims=True)
    pos2 = jnp.sum(jnp.where(oh2, slot, 0.0), axis=0, keepdims=True)
    pos_ref[0] = jnp.concatenate([pos1, pos2], axis=0).astype(jnp.int32)

    base_col = base_scr[:, 0:1]
    end_col = base_col + n8_col
    base_scr[...] = jnp.broadcast_to(end_col, (N_EXPERTS, LANES))
    mlane = lax.broadcasted_iota(jnp.int32, (N_EXPERTS, LANES), 1)
    meta = jnp.where(mlane == META_OFF, off_col,
                     jnp.where(mlane == META_N8, n8_col,
                               jnp.where(mlane == META_BASE, base_col,
                                         jnp.where(mlane == META_END, end_col, 0.0)))).astype(jnp.int32)
    meta_ref[0] = meta

    h2_scr[...] = h2b
    route_scr[...] = jnp.concatenate([pos1, pos2, w1, w2, jnp.zeros((SUBLANES - 4, ts), F32)], axis=0)


def _mixer_call(x, ada_l, w_in, w_out, conv_w, conv_b, poolw_bd, pool_scale, ln_g, ln_b,
                sguw_pair, sgub_pair, ln1_g, ln1_b, wr_pad, br_pad, tri, alpha, prev=None):
    bsz, s, d = x.shape
    ts = SEQ_TILE
    nt = s // ts
    n_tiles = bsz * nt
    fused = prev is not None
    last = n_tiles - 1
    const2 = lambda i, *_: (0, 0)
    const3 = lambda i, *_: (0, 0, 0)
    seq_of = lambda i, *_: (jnp.minimum(i, last) // nt, 0, 0)
    tile3 = lambda i, *_: (jnp.minimum(i, last) // nt, jnp.minimum(i, last) % nt, 0)
    tile2 = lambda i, *_: (jnp.minimum(i, last), 0)
    meta3 = lambda i, *_: (jnp.minimum(i, last), 0, 0)
    once = pl.Buffered(1)
    in_specs = [
        pl.BlockSpec((1, ADA_CHUNKS, d), seq_of),
        pl.BlockSpec((d, D_IN), const2, pipeline_mode=once),
        pl.BlockSpec((d, d), const2, pipeline_mode=once),
        pl.BlockSpec((CONV_K, CONV_W), const2),
        pl.BlockSpec((1, CONV_W), const2),
        pl.BlockSpec((POOL_W, POOL_W), const2),
        pl.BlockSpec((1, POOL_W), const2),
        pl.BlockSpec((1, SGU_W), const2),
        pl.BlockSpec((1, SGU_W), const2),
        pl.BlockSpec((SGU_W // LANES, CHUNK, 2 * CHUNK), const3),
        pl.BlockSpec((SGU_W // LANES, CHUNK, LANES), const3),
        pl.BlockSpec((1, d), const2),
        pl.BlockSpec((1, d), const2),
        pl.BlockSpec((d, LANES), const2),
        pl.BlockSpec((1, LANES), const2),
        pl.BlockSpec((ts, ts), const2, pipeline_mode=once),
    ]
    operands = [ada_l, w_in, w_out, conv_w, conv_b, poolw_bd, pool_scale, ln_g, ln_b,
                sguw_pair, sgub_pair, ln1_g, ln1_b, wr_pad, br_pad, tri]
    scratch = [
        pltpu.VMEM((CONV_HALO + ts, CONV_W), F32),
        pltpu.VMEM((POOL_HALO + ts, POOL_W), F32),
        pltpu.VMEM((ts, d), BF16),
        pltpu.VMEM((N_EXPERTS, LANES), F32),
        pltpu.VMEM((ts, d), BF16),
        pltpu.VMEM((SUBLANES, ts), F32),
    ]
    if fused:
        comb_src, ys, pos_tok, ada_prev, ln2_g, ln2_b = prev
        in_specs = [
            pl.BlockSpec((ts, d), tile2),
            pl.BlockSpec(memory_space=pl.ANY),
            pl.BlockSpec((ts, TOP_K), tile2),
            pl.BlockSpec((1, ADA_CHUNKS, d), seq_of),
            pl.BlockSpec((1, d), const2),
            pl.BlockSpec((1, d), const2),
        ] + in_specs
        operands = [comb_src, x.reshape(bsz * s, d), ys, pos_tok, ada_prev, ln2_g, ln2_b] + operands
        scratch = scratch + [
            pltpu.VMEM((ts, d), F32),
            pltpu.VMEM((2, SLOTS, PACK_W), jnp.int32),
            pltpu.SemaphoreType.DMA((2,)),
        ]
    else:
        in_specs = [pl.BlockSpec((1, ts, d), tile3)] + in_specs
        operands = [x] + operands
    kern = functools.partial(_mixer_kernel, ts=ts, alpha=alpha, nt=nt, n_tiles=n_tiles, fused=fused)
    return pl.pallas_call(
        kern,
        grid_spec=pltpu.PrefetchScalarGridSpec(
            num_scalar_prefetch=1 if fused else 0,
            grid=(n_tiles + 1,),
            in_specs=in_specs,
            out_specs=[
                pl.BlockSpec((1, ts, d), tile3),
                pl.BlockSpec((1, TOP_K, ts), meta3),
                pl.BlockSpec((1, N_EXPERTS, LANES), meta3),
                pl.BlockSpec((SLOTS, XS_W), lambda i, *_: (jnp.maximum(i - 1, 0), 0)),
            ],
            scratch_shapes=scratch,
        ),
        out_shape=[
            jax.ShapeDtypeStruct((bsz, s, d), F32),
            jax.ShapeDtypeStruct((n_tiles, TOP_K, ts), jnp.int32),
            jax.ShapeDtypeStruct((n_tiles, N_EXPERTS, LANES), jnp.int32),
            jax.ShapeDtypeStruct((n_tiles * SLOTS, XS_W), jnp.int32),
        ],
        compiler_params=pltpu.CompilerParams(
            dimension_semantics=("arbitrary",), vmem_limit_bytes=VMEM_LIMIT),
        name="mixer_router",
    )(*operands)


def _ffn_kernel(src_ref, b0_ref, nb_ref, rows_ref, tot_ref, xs_ref, wg_ref, wu_ref, wd_ref, ys_ref,
                xbuf, obuf, wbuf_g, wbuf_u, wbuf_d, wg_scr, wu_scr, wd_scr, gsem, osem, wsem,
                *, layer, n_blocks):
    e = pl.program_id(0)
    n_exp = pl.num_programs(0)
    b0 = b0_ref[e]
    nb = nb_ref[e]
    n_used = tot_ref[0]

    def gather(g):
        slot = g % GATHER_DEPTH
        for r in range(BLOCK_GROUPS):
            pltpu.make_async_copy(_rows(xs_ref, src_ref[g * BLOCK_GROUPS + r], SUBLANES),
                                  xbuf.at[slot, pl.ds(r * SUBLANES, SUBLANES), :], gsem.at[slot]).start()

    def out_copy(g, slot):
        return pltpu.make_async_copy(obuf.at[slot], _rows(ys_ref, g * ROW_BLOCK, ROW_BLOCK), osem.at[slot])

    def weight_copies(expert, slot):
        return [pltpu.make_async_copy(src.at[layer, expert], dst.at[slot], wsem.at[slot])
                for src, dst in ((wg_ref, wbuf_g), (wu_ref, wbuf_u), (wd_ref, wbuf_d))]

    @pl.when(e == 0)
    def _():
        for first in range(WEIGHT_DEPTH - 1):
            for cp in weight_copies(first, first):
                cp.start(priority=WEIGHT_DMA_PRIORITY)
        for ahead in range(GATHER_DEPTH - 1):
            @pl.when(ahead < n_used)
            def _():
                gather(ahead)

    @pl.when(e + (WEIGHT_DEPTH - 1) < n_exp)
    def _():
        for cp in weight_copies(e + (WEIGHT_DEPTH - 1), (e + (WEIGHT_DEPTH - 1)) % WEIGHT_DEPTH):
            cp.start(priority=WEIGHT_DMA_PRIORITY)

    wslot = e % WEIGHT_DEPTH
    for cp in weight_copies(e, wslot):
        cp.wait()

    @pl.when(nb > 0)
    def _():
        wg_scr[...] = wbuf_g[wslot].astype(BF16)
        wu_scr[...] = wbuf_u[wslot].astype(BF16)
        wd_scr[...] = wbuf_d[wslot].astype(BF16)

    def block(k, carry):
        g = b0 + k
        islot = g % GATHER_DEPTH
        oslot = g % 2

        @pl.when(g + (GATHER_DEPTH - 1) < n_used)
        def _():
            gather(g + (GATHER_DEPTH - 1))

        pltpu.make_async_copy(_rows(xs_ref, 0, ROW_BLOCK), xbuf.at[islot], gsem.at[islot]).wait()

        @pl.when(g >= 2)
        def _():
            out_copy(g - 2, oslot).wait()

        def chain(half):
            rows = pl.ds(half * SUB_BLOCK, SUB_BLOCK)
            xb = jnp.concatenate(_unpack_bf16_pairs(xbuf[islot, rows, 0:PACK_W]), axis=1)
            wrow = pltpu.bitcast(xbuf[islot, rows, PACK_W:XS_W], F32)[:, 0:1]
            gate = jnp.dot(xb, wg_scr[...], preferred_element_type=F32)
            up = jnp.dot(xb, wu_scr[...], preferred_element_type=F32)
            act = (gate * jax.nn.sigmoid(gate) * up).astype(BF16)
            y = jnp.dot(act, wd_scr[...], preferred_element_type=F32) * wrow
            obuf[oslot, rows, :] = _pack_bf16_pairs(y.astype(BF16).astype(F32))

        both = rows_ref[e] - k * ROW_BLOCK > SUB_BLOCK

        @pl.when(both)
        def _():
            chain(0)
            chain(1)

        @pl.when(jnp.logical_not(both))
        def _():
            chain(0)
            obuf[oslot, pl.ds(SUB_BLOCK, SUB_BLOCK), :] = jnp.zeros((SUB_BLOCK, PACK_W), jnp.int32)

        out_copy(g, oslot).start()
        return carry

    lax.fori_loop(0, nb, block, 0)

    @pl.when(e == n_exp - 1)
    def _():
        @pl.when(n_used >= 2)
        def _():
            out_copy(n_used - 2, n_used % 2).wait()

        out_copy(n_used - 1, (n_used - 1) % 2).wait()
        obuf[0] = jnp.zeros((ROW_BLOCK, PACK_W), jnp.int32)

        def fill(g, carry):
            out_copy(g, 0).start()
            return carry

        def drain(g, carry):
            out_copy(g, 0).wait()
            return carry

        lax.fori_loop(n_used, n_blocks, fill, 0)
        lax.fori_loop(n_used, n_blocks, drain, 0)


def _ffn_call(layer, group_src, blk_start, seg_blocks, seg_rows, n_used, n_blocks, xs, w_gate, w_up, w_down):
    d, f = w_gate.shape[-2:]
    any_spec = pl.BlockSpec(memory_space=pl.ANY)
    return pl.pallas_call(
        functools.partial(_ffn_kernel, layer=layer, n_blocks=n_blocks),
        grid_spec=pltpu.PrefetchScalarGridSpec(
            num_scalar_prefetch=5,
            grid=(N_EXPERTS,),
            in_specs=[any_spec, any_spec, any_spec, any_spec],
            out_specs=any_spec,
            scratch_shapes=[
                pltpu.VMEM((GATHER_DEPTH, ROW_BLOCK, XS_W), jnp.int32),
                pltpu.VMEM((2, ROW_BLOCK, PACK_W), jnp.int32),
                pltpu.VMEM((WEIGHT_DEPTH, d, f), F32),
                pltpu.VMEM((WEIGHT_DEPTH, d, f), F32),
                pltpu.VMEM((WEIGHT_DEPTH, f, d), F32),
                pltpu.VMEM((d, f), BF16),
                pltpu.VMEM((d, f), BF16),
                pltpu.VMEM((f, d), BF16),
                pltpu.SemaphoreType.DMA((GATHER_DEPTH,)),
                pltpu.SemaphoreType.DMA((2,)),
                pltpu.SemaphoreType.DMA((WEIGHT_DEPTH,)),
            ],
        ),
        out_shape=jax.ShapeDtypeStruct((n_blocks * ROW_BLOCK, PACK_W), jnp.int32),
        compiler_params=pltpu.CompilerParams(
            dimension_semantics=("arbitrary",), vmem_limit_bytes=VMEM_LIMIT),
        name="expert_ffn",
    )(group_src, blk_start, seg_blocks, seg_rows, n_used, xs, w_gate, w_up, w_down)


def _combine_kernel(src_ref, ys_ref, pos_ref, x1_ref, ada_ref, g_ref, b_ref,
                    o_ref, buf_even, buf_odd, sem, *, ts, alpha):
    i = pl.program_id(0)
    n = pl.num_programs(0)

    def gather(tile, buf, s):
        for q in range(TILE_GROUPS):
            pltpu.make_async_copy(_rows(ys_ref, src_ref[tile * TILE_GROUPS + q], SUBLANES),
                                  buf.at[pl.ds(q * SUBLANES, SUBLANES), :], s).start()

    def wait(buf, s):
        pltpu.make_async_copy(_rows(ys_ref, 0, SLOTS), buf, s).wait()

    @pl.when(i == 0)
    def _():
        gather(0, buf_even, sem.at[0])

    def step(mine, other, s_mine, s_other):
        gather(jnp.minimum(i + 1, n - 1), other, s_other)
        wait(mine, s_mine)
        _unsort_tile(mine[...], pos_ref, x1_ref, ada_ref[0][5:6], g_ref, b_ref, o_ref, ts=ts, alpha=alpha)

        @pl.when(i == n - 1)
        def _():
            wait(other, s_other)

    @pl.when(i % 2 == 0)
    def _():
        step(buf_even, buf_odd, sem.at[0], sem.at[1])

    @pl.when(i % 2 == 1)
    def _():
        step(buf_odd, buf_even, sem.at[1], sem.at[0])


def _combine_call(group_src, ys, pos_tok, x1_flat, ada_l, ln_g, ln_b, alpha, seq):
    n_tok, d = x1_flat.shape
    ts = SEQ_TILE
    n_tiles = n_tok // ts
    per_seq = seq // ts
    return pl.pallas_call(
        functools.partial(_combine_kernel, ts=ts, alpha=alpha),
        grid_spec=pltpu.PrefetchScalarGridSpec(
            num_scalar_prefetch=1,
            grid=(n_tiles,),
            in_specs=[
                pl.BlockSpec(memory_space=pl.ANY),
                pl.BlockSpec((ts, TOP_K), lambda i, src: (i, 0)),
                pl.BlockSpec((ts, d), lambda i, src: (i, 0)),
                pl.BlockSpec((1, ADA_CHUNKS, d), lambda i, src: (i // per_seq, 0, 0)),
                pl.BlockSpec((1, d), lambda i, src: (0, 0)),
                pl.BlockSpec((1, d), lambda i, src: (0, 0)),
            ],
            out_specs=pl.BlockSpec((ts, d), lambda i, src: (i, 0)),
            scratch_shapes=[
                pltpu.VMEM((SLOTS, PACK_W), jnp.int32),
                pltpu.VMEM((SLOTS, PACK_W), jnp.int32),
                pltpu.SemaphoreType.DMA((2,)),
            ],
        ),
        out_shape=jax.ShapeDtypeStruct((n_tok, d), F32),
        compiler_params=pltpu.CompilerParams(
            dimension_semantics=("arbitrary",), vmem_limit_bytes=VMEM_LIMIT),
        name="combine_ln",
    )(group_src, ys, pos_tok, x1_flat, ada_l, ln_g, ln_b)


def kernel(x, c, w_ada, b_ada, w_in, conv_w, conv_b, pool_w, pool_scale, sgu_ln_g, sgu_ln_b,
           sgu_w, sgu_b, w_out, ln1_g, ln1_b, w_router, b_router, w_gate, w_up, w_down,
           ln2_g, ln2_b):
    bsz, seq, d = x.shape
    depth = w_ada.shape[0]
    n_tok = bsz * seq
    n_tiles = n_tok // SEQ_TILE
    alpha = (2 * depth) ** 0.25
    assert d == D_MODEL and seq % SEQ_TILE == 0 and SEQ_TILE % CHUNK == 0

    ada = _ada_call(c, w_ada, b_ada).reshape(depth, bsz, ADA_CHUNKS, d)

    w_in_b = w_in.astype(BF16)
    w_out_b = w_out.astype(BF16)
    eye_g = jnp.eye(POOL_W // HEAD_DIM, dtype=F32)
    poolw_bd = jnp.einsum('lgcd,gh->lgchd', pool_w, eye_g).reshape(depth, POOL_W, POOL_W).astype(BF16)
    n_pairs = SGU_W // LANES
    sguw_pair = sgu_w.reshape(depth, n_pairs, 2, CHUNK, CHUNK).transpose(0, 1, 3, 2, 4).reshape(
        depth, n_pairs, CHUNK, 2 * CHUNK)
    sgub_pair = jnp.repeat(sgu_b.transpose(0, 2, 1), HEAD_DIM, axis=-1).reshape(
        depth, CHUNK, n_pairs, LANES).transpose(0, 2, 1, 3)
    wr_pad = jnp.pad(w_router, ((0, 0), (0, LANES - N_EXPERTS))).astype(BF16)
    br_pad = jnp.pad(b_router, (0, LANES - N_EXPERTS)).reshape(1, LANES)

    max_rows = n_tok * TOP_K + (SUBLANES - 1) * N_EXPERTS * n_tiles
    n_steps = -(-max_rows // ROW_BLOCK) + N_EXPERTS
    step = jnp.arange(n_steps, dtype=jnp.int32)
    experts = jnp.arange(N_EXPERTS, dtype=jnp.int32)
    tiles = jnp.arange(n_tiles, dtype=jnp.int32)
    tri = jnp.triu(jnp.ones((SEQ_TILE, SEQ_TILE), BF16), k=1)
    zero_group = SLOTS - SUBLANES

    prev = None
    for l in range(depth):
        x1, pos, meta, xs = _mixer_call(
            x, ada[l], w_in_b[l], w_out_b[l], conv_w[l], conv_b[l].reshape(1, -1), poolw_bd[l],
            pool_scale[l].reshape(1, -1), sgu_ln_g[l].reshape(1, -1), sgu_ln_b[l].reshape(1, -1),
            sguw_pair[l], sgub_pair[l], ln1_g[l].reshape(1, -1), ln1_b[l].reshape(1, -1),
            wr_pad, br_pad, tri, alpha, prev)
        off = meta[:, :, META_OFF]
        n8 = meta[:, :, META_N8]
        base = meta[:, :, META_BASE]
        end = meta[:, :, META_END]
        seg_rows = end[n_tiles - 1]
        seg_blocks = (seg_rows + ROW_BLOCK - 1) // ROW_BLOCK
        blk_end = jnp.cumsum(seg_blocks)
        blk_start = blk_end - seg_blocks
        n_used = blk_end[-1:].astype(jnp.int32)
        active = step < n_used[0]
        block_e = jnp.minimum(jnp.sum(step[:, None] >= blk_end[None, :], axis=1), N_EXPERTS - 1)
        is_e = block_e[:, None] == experts[None, :]

        def of_block_expert(tab):
            return jnp.sum(jnp.where(is_e[:, :, None], tab.T[None, :, :], 0), axis=1)

        first_unit = (step - jnp.sum(jnp.where(is_e, blk_start[None, :], 0), axis=1)) * BLOCK_GROUPS
        unit = first_unit[:, None] + jnp.arange(BLOCK_GROUPS)[None, :]
        tile_of = jnp.sum(of_block_expert(end // SUBLANES)[:, None, :] <= unit[:, :, None], axis=2)
        valid = (tile_of < n_tiles) & active[:, None]
        is_t = tile_of[:, :, None] == tiles[None, None, :]
        row0 = tiles[None, :] * SLOTS + of_block_expert(off) - of_block_expert(base)
        src = jnp.sum(jnp.where(is_t, row0[:, None, :], 0), axis=2) + unit * SUBLANES
        ffn_src = jnp.where(valid, src, zero_group).astype(jnp.int32).reshape(-1)
        grp = jnp.arange(TILE_GROUPS, dtype=jnp.int32)
        chunk_end_u = (off + n8) // SUBLANES
        exp_of = jnp.sum(chunk_end_u[:, None, :] <= grp[None, :, None], axis=2)
        seg_row0 = (blk_start * ROW_BLOCK)[None, :] + base - off
        comb_src = jnp.sum(jnp.where(exp_of[:, :, None] == experts[None, None, :], seg_row0[:, None, :], 0),
                           axis=2) + grp[None, :] * SUBLANES
        n_groups = chunk_end_u[:, N_EXPERTS - 1].astype(jnp.int32)
        comb_src = jnp.where(grp[None, :] < n_groups[:, None], comb_src, 0).astype(jnp.int32).reshape(-1)

        ys = _ffn_call(l, ffn_src, blk_start.astype(jnp.int32), seg_blocks.astype(jnp.int32),
                       seg_rows.astype(jnp.int32), n_used, n_steps, xs, w_gate, w_up, w_down)
        pos_tok = pos.transpose(0, 2, 1).reshape(n_tok, TOP_K)
        prev = (comb_src, ys, pos_tok, ada[l], ln2_g[l].reshape(1, -1), ln2_b[l].reshape(1, -1))
        x = x1
    comb_src, ys, pos_tok, ada_last, ln2_g_last, ln2_b_last = prev
    return _combine_call(comb_src, ys, pos_tok, x.reshape(n_tok, d), ada_last, ln2_g_last, ln2_b_last,
                         alpha, seq).reshape(bsz, seq, d)
```

```python
import functools

import jax
import jax.numpy as jnp
from jax import lax
from jax.experimental import pallas as pl
from jax.experimental.pallas import tpu as pltpu
from jax.experimental.pallas import tpu_sc as plsc

D_MODEL = 1024
HEAD_DIM = D_MODEL // 16
CONV_W = 6 * HEAD_DIM
POOL_W = 4 * HEAD_DIM
SGU_W = 6 * HEAD_DIM
D_IN = 3 * CONV_W + POOL_W + 2 * SGU_W
CONV_K = 3
POOL_WINDOWS = (2, 4, 8, 16)
CHUNK = 128
N_EXPERTS = 32
N_GROUPS = 4
EXPERTS_PER_GROUP = N_EXPERTS // N_GROUPS
TOP_K = 2
D_FF = D_MODEL // 2
ADA_CHUNKS = 6
LN_EPS = 1e-5

OFF_GB = 0
OFF_GC = CONV_W
OFF_P = 3 * CONV_W
OFF_U = OFF_P + POOL_W
OFF_V = OFF_U + SGU_W
YOFF_POOL = CONV_W
YOFF_SGU = CONV_W + POOL_W

LANES = 128
SUBLANES = 8
CONV_HALO = 8
POOL_HALO = 16
SEQ_TILE = 512
SUB_BLOCK = 256
ROW_BLOCK = 2 * SUB_BLOCK
ADA_TILE = 1536
GATHER_DEPTH = 3
WEIGHT_DEPTH = 3
WEIGHT_DMA_PRIORITY = 1
VMEM_LIMIT = 60 * 1024 * 1024

PACK_W = D_MODEL // 2
SC_ROWS = 128
SC_WORKERS = 32

F32 = jnp.float32
BF16 = jnp.bfloat16


def _layer_norm(r, g, b):
    mu = jnp.mean(r, axis=-1, keepdims=True)
    d = r - mu
    var = jnp.mean(d * d, axis=-1, keepdims=True)
    return d * lax.rsqrt(var + LN_EPS) * g + b


def _rows(ref, start, size):
    return ref.at[pl.ds(pl.multiple_of(start, SUBLANES), size), :]


def _pack_bf16_pairs(v):
    bits = pltpu.bitcast(v, jnp.int32)
    return bits[:, 0:PACK_W] | lax.shift_right_logical(bits[:, PACK_W:2 * PACK_W], 16)


def _unpack_pairs_f32(words):
    return jnp.concatenate([pltpu.bitcast(words & jnp.int32(-65536), F32),
                            pltpu.bitcast(lax.shift_left(words, 16), F32)], axis=1)


def _unpack_bf16_pairs(words):
    first = pltpu.bitcast(words & jnp.int32(-65536), F32).astype(BF16)
    second = pltpu.bitcast(lax.shift_left(words, 16), F32).astype(BF16)
    return first, second


def _ada_kernel(c_ref, w_ref, b_ref, o_ref):
    c = c_ref[...]
    c_act = (c * jax.nn.sigmoid(c)).astype(BF16)
    o_ref[0] = jnp.dot(c_act, w_ref[0].astype(BF16), preferred_element_type=F32) + b_ref[0]


def _ada_call(c, w_ada, b_ada):
    depth, d, n = w_ada.shape
    bsz = c.shape[0]
    return pl.pallas_call(
        _ada_kernel,
        grid=(depth, n // ADA_TILE),
        in_specs=[
            pl.BlockSpec((bsz, d), lambda l, j: (0, 0)),
            pl.BlockSpec((1, d, ADA_TILE), lambda l, j: (l, 0, j)),
            pl.BlockSpec((1, 1, ADA_TILE), lambda l, j: (l, 0, j)),
        ],
        out_specs=pl.BlockSpec((1, bsz, ADA_TILE), lambda l, j: (l, 0, j)),
        out_shape=jax.ShapeDtypeStruct((depth, bsz, n), F32),
        compiler_params=pltpu.CompilerParams(
            dimension_semantics=("arbitrary", "arbitrary"), vmem_limit_bytes=VMEM_LIMIT),
        name="ada",
    )(c, w_ada, b_ada.reshape(depth, 1, n))


def _top2_of_group(pg, sub_iota):
    big = float(EXPERTS_PER_GROUP)
    m1 = jnp.max(pg, axis=0, keepdims=True)
    i1 = jnp.min(jnp.where(pg == m1, sub_iota, big), axis=0, keepdims=True)
    rest = jnp.where(sub_iota == i1, -1.0, pg)
    m2 = jnp.max(rest, axis=0, keepdims=True)
    i2 = jnp.min(jnp.where(rest == m2, sub_iota, big), axis=0, keepdims=True)
    return m1, i1, m2, i2


def _combine_tile(ya0_ref, ya1_ref, wt_ref, x1, gate, g_ref, b_ref, *, alpha):
    wt = wt_ref[...]
    y = wt[:, 0:1] * _unpack_pairs_f32(ya0_ref[...]) + wt[:, 1:2] * _unpack_pairs_f32(ya1_ref[...])
    return _layer_norm(alpha * x1 + gate * y, g_ref[...], b_ref[...])


def _mixer_kernel(*refs, ts, alpha, nt, fused):
    if fused:
        xprev_ref, ya0_ref, ya1_ref, wtp_ref, adap_ref, ln2g_ref, ln2b_ref = refs[:7]
        refs = refs[7:]
    else:
        x_ref, refs = refs[0], refs[1:]
    (ada_ref, win_ref, wout_ref, convw_ref, convb_ref, poolw_ref, pscale_ref, lng_ref, lnb_ref,
     sguw_ref, sgub_ref, ln1g_ref, ln1b_ref, wr_ref, br_ref, tri_ref,
     x1_ref, h2p_ref, e_ref, wt_ref, rank_ref, cnt_ref,
     gx_scr, p_scr, y_scr, cnt_scr) = refs

    i = pl.program_id(0)
    j = i % nt

    @pl.when(i == 0)
    def _():
        gx_scr[0:CONV_HALO, :] = jnp.zeros((CONV_HALO, CONV_W), F32)
        p_scr[0:POOL_HALO, :] = jnp.zeros((POOL_HALO, POOL_W), F32)
        cnt_scr[...] = jnp.zeros_like(cnt_scr)

    if fused:
        x = _combine_tile(ya0_ref, ya1_ref, wtp_ref, xprev_ref[...], adap_ref[0][5:6], ln2g_ref, ln2b_ref,
                          alpha=alpha)
    else:
        x = x_ref[0]
    _mixer_tile(j, x, ada_ref, win_ref, wout_ref, convw_ref, convb_ref, poolw_ref, pscale_ref,
                lng_ref, lnb_ref, sguw_ref, sgub_ref, ln1g_ref, ln1b_ref, wr_ref, br_ref, tri_ref,
                x1_ref, h2p_ref, e_ref, wt_ref, rank_ref, cnt_ref, gx_scr, p_scr, y_scr, cnt_scr,
                ts=ts, alpha=alpha)


def _mixer_tile(j, x, ada_ref, win_ref, wout_ref, convw_ref, convb_ref, poolw_ref, pscale_ref,
                lng_ref, lnb_ref, sguw_ref, sgub_ref, ln1g_ref, ln1b_ref, wr_ref, br_ref, tri_ref,
                x1_ref, h2p_ref, e_ref, wt_ref, rank_ref, cnt_ref, gx_scr, p_scr, y_scr, cnt_scr,
                *, ts, alpha):
    first_of_seq = j == 0
    ada = ada_ref[0]
    sh1, sc1, g1 = ada[0:1], ada[1:2], ada[2:3]
    sh2, sc2 = ada[3:4], ada[4:5]
    hb = (x * (1.0 + sc1) + sh1).astype(BF16)

    def proj(lo, hi):
        return jnp.dot(hb, win_ref[:, lo:hi], preferred_element_type=F32)

    gcxc = proj(OFF_GC, OFF_P)
    g = gcxc[:, :CONV_W] * gcxc[:, CONV_W:]
    gx_scr[CONV_HALO:CONV_HALO + ts, :] = g
    cw = convw_ref[...]
    gx_scr[0:CONV_HALO, :] = jnp.where(first_of_seq, 0.0, gx_scr[0:CONV_HALO, :])
    conv = (cw[0:1] * gx_scr[CONV_HALO - 2:CONV_HALO - 2 + ts, :]
            + cw[1:2] * gx_scr[CONV_HALO - 1:CONV_HALO - 1 + ts, :]
            + cw[2:3] * g + convb_ref[...])
    gx_scr[0:CONV_HALO, :] = g[ts - CONV_HALO:ts, :]
    y_scr[:, 0:CONV_W] = (proj(OFF_GB, OFF_GC) * conv).astype(BF16)

    p = proj(OFF_P, OFF_U)
    p_scr[POOL_HALO:POOL_HALO + ts, :] = p
    p_scr[0:POOL_HALO, :] = jnp.where(first_of_seq, 0.0, p_scr[0:POOL_HALO, :])

    def shifted(k, lo):
        return p_scr[POOL_HALO - k:POOL_HALO - k + ts, lo:lo + LANES]

    s2 = p[:, 0:LANES] + shifted(1, 0)
    s4 = s2 + shifted(2, 0) + shifted(3, 0)
    acc = p[:, LANES:2 * LANES]
    for k in range(1, 8):
        acc = acc + shifted(k, LANES)
    s8 = acc
    for k in range(8, 16):
        acc = acc + shifted(k, LANES)
    s16 = acc
    p_scr[0:POOL_HALO, :] = p[ts - POOL_HALO:ts, :]
    lane = lax.broadcasted_iota(jnp.int32, (ts, LANES), 1)
    tpos = (lax.broadcasted_iota(jnp.int32, (ts, LANES), 0) + (j * ts + 1)).astype(F32)
    lo_half = lane < HEAD_DIM
    cnt_a = jnp.minimum(tpos, jnp.where(lo_half, float(POOL_WINDOWS[0]), float(POOL_WINDOWS[1])))
    cnt_b = jnp.minimum(tpos, jnp.where(lo_half, float(POOL_WINDOWS[2]), float(POOL_WINDOWS[3])))
    pooled = jnp.concatenate([jnp.where(lo_half, s2, s4) / cnt_a,
                              jnp.where(lo_half, s8, s16) / cnt_b], axis=1) - p
    mixed = jnp.dot(pooled.astype(BF16), poolw_ref[...], preferred_element_type=F32)
    y_scr[:, YOFF_POOL:YOFF_POOL + POOL_W] = (mixed * pscale_ref[...]).astype(BF16)

    v = proj(OFF_V, D_IN)
    vnb = _layer_norm(v, lng_ref[...], lnb_ref[...]).astype(BF16)
    u = proj(OFF_U, OFF_V)
    row_c = lax.broadcasted_iota(jnp.int32, (CHUNK, 2 * CHUNK), 0)
    col_c = lax.broadcasted_iota(jnp.int32, (CHUNK, 2 * CHUNK), 1)
    causal = (col_c & (CHUNK - 1)) <= row_c
    lo_lanes = lax.broadcasted_iota(jnp.int32, (CHUNK, LANES), 1) < HEAD_DIM
    zero_b = jnp.zeros((CHUNK, LANES), BF16)
    for hp in range(SGU_W // LANES):
        wl = jnp.where(causal, sguw_ref[hp], 0.0).astype(BF16)
        bias = sgub_ref[hp]
        for ci in range(ts // CHUNK):
            rs = slice(ci * CHUNK, (ci + 1) * CHUNK)
            vc = vnb[rs, hp * LANES:(hp + 1) * LANES]
            rhs = jnp.concatenate([jnp.where(lo_lanes, vc, zero_b),
                                   jnp.where(lo_lanes, zero_b, vc)], axis=0)
            mixed_c = jnp.dot(wl, rhs, preferred_element_type=F32) + bias
            y_scr[rs, YOFF_SGU + hp * LANES:YOFF_SGU + (hp + 1) * LANES] = (
                u[rs, hp * LANES:(hp + 1) * LANES] * mixed_c).astype(BF16)

    yo = jnp.dot(y_scr[...], wout_ref[...], preferred_element_type=F32)
    x1 = _layer_norm(alpha * x + g1 * yo, ln1g_ref[...], ln1b_ref[...])
    x1_ref[0] = x1
    h2b = (x1 * (1.0 + sc2) + sh2).astype(BF16)
    h2p_ref[...] = _pack_bf16_pairs(h2b.astype(F32))

    logits = jnp.dot(h2b, wr_ref[...], preferred_element_type=F32) + br_ref[...]
    lt = logits.T[0:N_EXPERTS, :]
    ex = jnp.exp(lt - jnp.max(lt, axis=0, keepdims=True))
    probs = ex / jnp.sum(ex, axis=0, keepdims=True)
    sub_iota = lax.broadcasted_iota(jnp.int32, (EXPERTS_PER_GROUP, ts), 0).astype(F32)
    best = None
    for gi in range(N_GROUPS):
        m1, i1, m2, i2 = _top2_of_group(
            probs[gi * EXPERTS_PER_GROUP:(gi + 1) * EXPERTS_PER_GROUP, :], sub_iota)
        score = m1 + m2
        cand = (score, m1, m2, i1 + float(gi * EXPERTS_PER_GROUP), i2 + float(gi * EXPERTS_PER_GROUP))
        if best is None:
            best = cand
        else:
            better = score > best[0]
            best = tuple(jnp.where(better, c, o) for c, o in zip(cand, best))
    _, p1, p2, e1, e2 = best
    den = p1 + p2
    w1 = p1 / den
    w2 = p2 / den

    wt_ref[0] = jnp.concatenate([w1, w2], axis=0)
    e_ref[0] = jnp.concatenate([e1, e2], axis=0).astype(jnp.int32)

    ex_iota = lax.broadcasted_iota(jnp.int32, (N_EXPERTS, ts), 0).astype(F32)
    oh1 = ex_iota == e1
    oh2 = ex_iota == e2
    either = jnp.where(oh1 | oh2, 1.0, 0.0)
    seen = jnp.dot(either.astype(BF16), tri_ref[...], preferred_element_type=F32) + cnt_scr[:, 0:1]
    r1 = jnp.sum(jnp.where(oh1, seen, 0.0), axis=0, keepdims=True)
    r2 = jnp.sum(jnp.where(oh2, seen, 0.0), axis=0, keepdims=True)
    rank_ref[0] = jnp.concatenate([r1, r2], axis=0).astype(jnp.int32)
    cnt_new = cnt_scr[...] + jnp.sum(either, axis=1, keepdims=True)
    cnt_scr[...] = cnt_new
    cnt_ref[...] = cnt_new


def _mixer_call(x, ada_l, w_in, w_out, conv_w, conv_b, poolw_bd, pool_scale, ln_g, ln_b,
                sguw_pair, sgub_pair, ln1_g, ln1_b, wr_pad, br_pad, tri, alpha, prev=None):
    bsz, s, d = x.shape
    ts = SEQ_TILE
    nt = s // ts
    n_tiles = bsz * nt
    n_tok = bsz * s
    fused = prev is not None
    const2 = lambda i: (0, 0)
    const3 = lambda i: (0, 0, 0)
    seq_of = lambda i: (i // nt, 0, 0)
    tile3 = lambda i: (i // nt, i % nt, 0)
    tile2 = lambda i: (i, 0)
    meta3 = lambda i: (i, 0, 0)
    once = pl.Buffered(1)
    in_specs = [
        pl.BlockSpec((1, ADA_CHUNKS, d), seq_of),
        pl.BlockSpec((d, D_IN), const2, pipeline_mode=once),
        pl.BlockSpec((d, d), const2, pipeline_mode=once),
        pl.BlockSpec((CONV_K, CONV_W), const2),
        pl.BlockSpec((1, CONV_W), const2),
        pl.BlockSpec((POOL_W, POOL_W), const2),
        pl.BlockSpec((1, POOL_W), const2),
        pl.BlockSpec((1, SGU_W), const2),
        pl.BlockSpec((1, SGU_W), const2),
        pl.BlockSpec((SGU_W // LANES, CHUNK, 2 * CHUNK), const3),
        pl.BlockSpec((SGU_W // LANES, CHUNK, LANES), const3),
        pl.BlockSpec((1, d), const2),
        pl.BlockSpec((1, d), const2),
        pl.BlockSpec((d, LANES), const2),
        pl.BlockSpec((1, LANES), const2),
        pl.BlockSpec((ts, ts), const2, pipeline_mode=once),
    ]
    operands = [ada_l, w_in, w_out, conv_w, conv_b, poolw_bd, pool_scale, ln_g, ln_b,
                sguw_pair, sgub_pair, ln1_g, ln1_b, wr_pad, br_pad, tri]
    if fused:
        ya, wt_tok, ada_prev, ln2_g, ln2_b = prev
        in_specs = [
            pl.BlockSpec((ts, d), tile2),
            pl.BlockSpec((ts, PACK_W), tile2),
            pl.BlockSpec((ts, PACK_W), lambda i: (n_tiles + i, 0)),
            pl.BlockSpec((ts, TOP_K), tile2),
            pl.BlockSpec((1, ADA_CHUNKS, d), seq_of),
            pl.BlockSpec((1, d), const2),
            pl.BlockSpec((1, d), const2),
        ] + in_specs
        operands = [x.reshape(n_tok, d), ya, ya, wt_tok, ada_prev, ln2_g, ln2_b] + operands
    else:
        in_specs = [pl.BlockSpec((1, ts, d), tile3)] + in_specs
        operands = [x] + operands
    kern = functools.partial(_mixer_kernel, ts=ts, alpha=alpha, nt=nt, fused=fused)
    return pl.pallas_call(
        kern,
        grid=(n_tiles,),
        in_specs=in_specs,
        out_specs=[
            pl.BlockSpec((1, ts, d), tile3),
            pl.BlockSpec((ts, PACK_W), tile2),
            pl.BlockSpec((1, TOP_K, ts), meta3),
            pl.BlockSpec((1, TOP_K, ts), meta3),
            pl.BlockSpec((1, TOP_K, ts), meta3),
            pl.BlockSpec((N_EXPERTS, LANES), const2),
        ],
        out_shape=[
            jax.ShapeDtypeStruct((bsz, s, d), F32),
            jax.ShapeDtypeStruct((n_tok, PACK_W), jnp.int32),
            jax.ShapeDtypeStruct((n_tiles, TOP_K, ts), jnp.int32),
            jax.ShapeDtypeStruct((n_tiles, TOP_K, ts), F32),
            jax.ShapeDtypeStruct((n_tiles, TOP_K, ts), jnp.int32),
            jax.ShapeDtypeStruct((N_EXPERTS, LANES), F32),
        ],
        scratch_shapes=[
            pltpu.VMEM((CONV_HALO + ts, CONV_W), F32),
            pltpu.VMEM((POOL_HALO + ts, POOL_W), F32),
            pltpu.VMEM((ts, d), BF16),
            pltpu.VMEM((N_EXPERTS, LANES), F32),
        ],
        compiler_params=pltpu.CompilerParams(
            dimension_semantics=("arbitrary",), vmem_limit_bytes=VMEM_LIMIT),
        name="mixer_router",
    )(*operands)


def _sc_worker_chunks(n_chunks):
    per_worker = n_chunks // SC_WORKERS
    worker = lax.axis_index("s") * 2 + lax.axis_index("c")
    return worker * per_worker, per_worker


def _dispatch_call(h2p, dest, n_rows):
    n_tok, width = h2p.shape
    n_chunks = n_tok // SC_ROWS
    assert n_chunks % SC_WORKERS == 0

    def body(h_hbm, dest_hbm, xs_hbm, rows_v, idx_v):
        first, per_worker = _sc_worker_chunks(n_chunks)

        @pl.loop(0, per_worker)
        def _(c):
            chunk = first + c
            pltpu.sync_copy(h_hbm.at[pl.ds(chunk * SC_ROWS, SC_ROWS)], rows_v)
            for k in range(TOP_K):
                pltpu.sync_copy(dest_hbm.at[k, chunk], idx_v.at[k])
                pltpu.sync_copy(rows_v, xs_hbm.at[idx_v.at[k]])

    return pl.kernel(
        body,
        out_type=jax.ShapeDtypeStruct((n_rows, width), jnp.int32),
        mesh=plsc.VectorSubcoreMesh(core_axis_name="c", subcore_axis_name="s"),
        scratch_types=[pltpu.VMEM((SC_ROWS, width), jnp.int32), pltpu.VMEM((TOP_K, SC_ROWS), jnp.int32)],
        name="dispatch",
    )(h2p, dest)


def _collect_call(ys, dest, n_tok):
    width = ys.shape[1]
    n_chunks = n_tok // SC_ROWS
    assert n_chunks % SC_WORKERS == 0

    def body(ys_hbm, dest_hbm, ya_hbm, rows_v, idx_v):
        first, per_worker = _sc_worker_chunks(n_chunks)

        @pl.loop(0, per_worker)
        def _(c):
            chunk = first + c
            for k in range(TOP_K):
                pltpu.sync_copy(dest_hbm.at[k, chunk], idx_v.at[k])
                pltpu.sync_copy(ys_hbm.at[idx_v.at[k]], rows_v)
                pltpu.sync_copy(rows_v, ya_hbm.at[pl.ds(k * n_tok + chunk * SC_ROWS, SC_ROWS)])

    return pl.kernel(
        body,
        out_type=jax.ShapeDtypeStruct((TOP_K * n_tok, width), jnp.int32),
        mesh=plsc.VectorSubcoreMesh(core_axis_name="c", subcore_axis_name="s"),
        scratch_types=[pltpu.VMEM((SC_ROWS, width), jnp.int32), pltpu.VMEM((TOP_K, SC_ROWS), jnp.int32)],
        name="collect",
    )(ys, dest)


def _ffn_kernel(b0_ref, nb_ref, rows_ref, tot_ref, xs_ref, wg_ref, wu_ref, wd_ref, ys_ref,
                xbuf, obuf, wbuf_g, wbuf_u, wbuf_d, wg_scr, wu_scr, wd_scr, gsem, osem, wsem,
                *, layer, n_blocks):
    e = pl.program_id(0)
    n_exp = pl.num_programs(0)
    b0 = b0_ref[e]
    nb = nb_ref[e]
    n_used = tot_ref[0]

    def in_copy(g):
        slot = g % GATHER_DEPTH
        return pltpu.make_async_copy(_rows(xs_ref, g * ROW_BLOCK, ROW_BLOCK), xbuf.at[slot], gsem.at[slot])

    def out_copy(g, slot):
        return pltpu.make_async_copy(obuf.at[slot], _rows(ys_ref, g * ROW_BLOCK, ROW_BLOCK), osem.at[slot])

    def weight_copies(expert, slot):
        return [pltpu.make_async_copy(src.at[layer, expert], dst.at[slot], wsem.at[slot])
                for src, dst in ((wg_ref, wbuf_g), (wu_ref, wbuf_u), (wd_ref, wbuf_d))]

    @pl.when(e == 0)
    def _():
        for first in range(WEIGHT_DEPTH - 1):
            for cp in weight_copies(first, first):
                cp.start(priority=WEIGHT_DMA_PRIORITY)
        for ahead in range(GATHER_DEPTH - 1):
            @pl.when(ahead < n_used)
            def _():
                in_copy(ahead).start()

    @pl.when(e + (WEIGHT_DEPTH - 1) < n_exp)
    def _():
        for cp in weight_copies(e + (WEIGHT_DEPTH - 1), (e + (WEIGHT_DEPTH - 1)) % WEIGHT_DEPTH):
            cp.start(priority=WEIGHT_DMA_PRIORITY)

    wslot = e % WEIGHT_DEPTH
    for cp in weight_copies(e, wslot):
        cp.wait()

    @pl.when(nb > 0)
    def _():
        wg_scr[...] = wbuf_g[wslot].astype(BF16)
        wu_scr[...] = wbuf_u[wslot].astype(BF16)
        wd_scr[...] = wbuf_d[wslot].astype(BF16)

    def block(k, carry):
        g = b0 + k
        islot = g % GATHER_DEPTH
        oslot = g % 2

        @pl.when(g + (GATHER_DEPTH - 1) < n_used)
        def _():
            in_copy(g + (GATHER_DEPTH - 1)).start()

        in_copy(g).wait()

        @pl.when(g >= 2)
        def _():
            out_copy(g - 2, oslot).wait()

        def chain(half):
            rows = pl.ds(half * SUB_BLOCK, SUB_BLOCK)
            xb = jnp.concatenate(_unpack_bf16_pairs(xbuf[islot, rows, :]), axis=1)
            gate = jnp.dot(xb, wg_scr[...], preferred_element_type=F32)
            up = jnp.dot(xb, wu_scr[...], preferred_element_type=F32)
            act = (gate * jax.nn.sigmoid(gate) * up).astype(BF16)
            y = jnp.dot(act, wd_scr[...], preferred_element_type=F32)
            obuf[oslot, rows, :] = _pack_bf16_pairs(y.astype(BF16).astype(F32))

        both = rows_ref[e] - k * ROW_BLOCK > SUB_BLOCK

        @pl.when(both)
        def _():
            chain(0)
            chain(1)

        @pl.when(jnp.logical_not(both))
        def _():
            chain(0)
            obuf[oslot, pl.ds(SUB_BLOCK, SUB_BLOCK), :] = jnp.zeros((SUB_BLOCK, PACK_W), jnp.int32)

        out_copy(g, oslot).start()
        return carry

    lax.fori_loop(0, nb, block, 0)

    @pl.when(e == n_exp - 1)
    def _():
        @pl.when(n_used >= 2)
        def _():
            out_copy(n_used - 2, n_used % 2).wait()

        out_copy(n_used - 1, (n_used - 1) % 2).wait()
        obuf[0] = jnp.zeros((ROW_BLOCK, PACK_W), jnp.int32)

        def fill(g, carry):
            out_copy(g, 0).start()
            return carry

        def drain(g, carry):
            out_copy(g, 0).wait()
            return carry

        lax.fori_loop(n_used, n_blocks, fill, 0)
        lax.fori_loop(n_used, n_blocks, drain, 0)


def _ffn_call(layer, blk_start, seg_blocks, seg_rows, n_used, n_blocks, xs, w_gate, w_up, w_down):
    d, f = w_gate.shape[-2:]
    any_spec = pl.BlockSpec(memory_space=pl.ANY)
    return pl.pallas_call(
        functools.partial(_ffn_kernel, layer=layer, n_blocks=n_blocks),
        grid_spec=pltpu.PrefetchScalarGridSpec(
            num_scalar_prefetch=4,
            grid=(N_EXPERTS,),
            in_specs=[any_spec, any_spec, any_spec, any_spec],
            out_specs=any_spec,
            scratch_shapes=[
                pltpu.VMEM((GATHER_DEPTH, ROW_BLOCK, PACK_W), jnp.int32),
                pltpu.VMEM((2, ROW_BLOCK, PACK_W), jnp.int32),
                pltpu.VMEM((WEIGHT_DEPTH, d, f), F32),
                pltpu.VMEM((WEIGHT_DEPTH, d, f), F32),
                pltpu.VMEM((WEIGHT_DEPTH, f, d), F32),
                pltpu.VMEM((d, f), BF16),
                pltpu.VMEM((d, f), BF16),
                pltpu.VMEM((f, d), BF16),
                pltpu.SemaphoreType.DMA((GATHER_DEPTH,)),
                pltpu.SemaphoreType.DMA((2,)),
                pltpu.SemaphoreType.DMA((WEIGHT_DEPTH,)),
            ],
        ),
        out_shape=jax.ShapeDtypeStruct((n_blocks * ROW_BLOCK, PACK_W), jnp.int32),
        compiler_params=pltpu.CompilerParams(
            dimension_semantics=("arbitrary",), vmem_limit_bytes=VMEM_LIMIT),
        name="expert_ffn",
    )(blk_start, seg_blocks, seg_rows, n_used, xs, w_gate, w_up, w_down)


def _combine_kernel(ya0_ref, ya1_ref, wt_ref, x1_ref, ada_ref, g_ref, b_ref, o_ref, *, alpha):
    o_ref[...] = _combine_tile(ya0_ref, ya1_ref, wt_ref, x1_ref[...], ada_ref[0][5:6], g_ref, b_ref, alpha=alpha)


def _combine_call(ya, wt_tok, x1_flat, ada_l, ln_g, ln_b, alpha, seq):
    n_tok, d = x1_flat.shape
    ts = SEQ_TILE
    n_tiles = n_tok // ts
    per_seq = seq // ts
    return pl.pallas_call(
        functools.partial(_combine_kernel, alpha=alpha),
        grid=(n_tiles,),
        in_specs=[
            pl.BlockSpec((ts, PACK_W), lambda i: (i, 0)),
            pl.BlockSpec((ts, PACK_W), lambda i: (n_tiles + i, 0)),
            pl.BlockSpec((ts, TOP_K), lambda i: (i, 0)),
            pl.BlockSpec((ts, d), lambda i: (i, 0)),
            pl.BlockSpec((1, ADA_CHUNKS, d), lambda i: (i // per_seq, 0, 0)),
            pl.BlockSpec((1, d), lambda i: (0, 0)),
            pl.BlockSpec((1, d), lambda i: (0, 0)),
        ],
        out_specs=pl.BlockSpec((ts, d), lambda i: (i, 0)),
        out_shape=jax.ShapeDtypeStruct((n_tok, d), F32),
        compiler_params=pltpu.CompilerParams(
            dimension_semantics=("arbitrary",), vmem_limit_bytes=VMEM_LIMIT),
        name="combine_ln",
    )(ya, ya, wt_tok, x1_flat, ada_l, ln_g, ln_b)


def kernel(x, c, w_ada, b_ada, w_in, conv_w, conv_b, pool_w, pool_scale, sgu_ln_g, sgu_ln_b,
           sgu_w, sgu_b, w_out, ln1_g, ln1_b, w_router, b_router, w_gate, w_up, w_down,
           ln2_g, ln2_b):
    bsz, seq, d = x.shape
    depth = w_ada.shape[0]
    n_tok = bsz * seq
    alpha = (2 * depth) ** 0.25
    assert d == D_MODEL and seq % SEQ_TILE == 0 and SEQ_TILE % CHUNK == 0

    ada = _ada_call(c, w_ada, b_ada).reshape(depth, bsz, ADA_CHUNKS, d)

    w_in_b = w_in.astype(BF16)
    w_out_b = w_out.astype(BF16)
    eye_g = jnp.eye(POOL_W // HEAD_DIM, dtype=F32)
    poolw_bd = jnp.einsum('lgcd,gh->lgchd', pool_w, eye_g).reshape(depth, POOL_W, POOL_W).astype(BF16)
    n_pairs = SGU_W // LANES
    sguw_pair = sgu_w.reshape(depth, n_pairs, 2, CHUNK, CHUNK).transpose(0, 1, 3, 2, 4).reshape(
        depth, n_pairs, CHUNK, 2 * CHUNK)
    sgub_pair = jnp.repeat(sgu_b.transpose(0, 2, 1), HEAD_DIM, axis=-1).reshape(
        depth, CHUNK, n_pairs, LANES).transpose(0, 2, 1, 3)
    wr_pad = jnp.pad(w_router, ((0, 0), (0, LANES - N_EXPERTS))).astype(BF16)
    br_pad = jnp.pad(b_router, (0, LANES - N_EXPERTS)).reshape(1, LANES)

    n_steps = (n_tok * TOP_K) // ROW_BLOCK + N_EXPERTS
    experts = jnp.arange(N_EXPERTS, dtype=jnp.int32)
    tri = jnp.triu(jnp.ones((SEQ_TILE, SEQ_TILE), BF16), k=1)

    prev = None
    for l in range(depth):
        x1, h2p, e_idx, wts, rank, counts = _mixer_call(
            x, ada[l], w_in_b[l], w_out_b[l], conv_w[l], conv_b[l].reshape(1, -1), poolw_bd[l],
            pool_scale[l].reshape(1, -1), sgu_ln_g[l].reshape(1, -1), sgu_ln_b[l].reshape(1, -1),
            sguw_pair[l], sgub_pair[l], ln1_g[l].reshape(1, -1), ln1_b[l].reshape(1, -1),
            wr_pad, br_pad, tri, alpha, prev)
        sizes = counts[:, 0].astype(jnp.int32)
        seg_blocks = (sizes + ROW_BLOCK - 1) // ROW_BLOCK
        blk_end = jnp.cumsum(seg_blocks)
        blk_start = (blk_end - seg_blocks).astype(jnp.int32)
        n_used = blk_end[-1:].astype(jnp.int32)
        e_tok = e_idx.transpose(1, 0, 2).reshape(TOP_K, n_tok)
        rank_tok = rank.transpose(1, 0, 2).reshape(TOP_K, n_tok)
        seg_row0 = jnp.sum(jnp.where(e_tok[:, :, None] == experts[None, None, :],
                                     (blk_start * ROW_BLOCK)[None, None, :], 0), axis=2)
        dest = (seg_row0 + rank_tok).astype(jnp.int32).reshape(TOP_K, n_tok // SC_ROWS, SC_ROWS)

        xs = _dispatch_call(h2p, dest, n_steps * ROW_BLOCK)
        ys = _ffn_call(l, blk_start, seg_blocks.astype(jnp.int32), sizes, n_used, n_steps,
                       xs, w_gate, w_up, w_down)
        ya = _collect_call(ys, dest, n_tok)
        wt_tok = wts.transpose(0, 2, 1).reshape(n_tok, TOP_K)
        prev = (ya, wt_tok, ada[l], ln2_g[l].reshape(1, -1), ln2_b[l].reshape(1, -1))
        x = x1
    ya, wt_tok, ada_last, ln2_g_last, ln2_b_last = prev
    return _combine_call(ya, wt_tok, x.reshape(n_tok, d), ada_last, ln2_g_last, ln2_b_last,
                         alpha, seq).reshape(bsz, seq, d)
```

```python
import functools

import jax
import jax.numpy as jnp
from jax import lax
from jax.experimental import pallas as pl
from jax.experimental.pallas import tpu as pltpu
from jax.experimental.pallas import tpu_sc as plsc

D_MODEL = 1024
HEAD_DIM = D_MODEL // 16
CONV_W = 6 * HEAD_DIM
POOL_W = 4 * HEAD_DIM
SGU_W = 6 * HEAD_DIM
D_IN = 3 * CONV_W + POOL_W + 2 * SGU_W
CONV_K = 3
POOL_WINDOWS = (2, 4, 8, 16)
CHUNK = 128
N_EXPERTS = 32
N_GROUPS = 4
EXPERTS_PER_GROUP = N_EXPERTS // N_GROUPS
TOP_K = 2
D_FF = D_MODEL // 2
ADA_CHUNKS = 6
LN_EPS = 1e-5

OFF_GB = 0
OFF_GC = CONV_W
OFF_P = 3 * CONV_W
OFF_U = OFF_P + POOL_W
OFF_V = OFF_U + SGU_W
YOFF_POOL = CONV_W
YOFF_SGU = CONV_W + POOL_W

LANES = 128
SUBLANES = 8
CONV_HALO = 8
POOL_HALO = 16
SEQ_TILE = 512
TILES_PER_STEP = 2
BATCH_STREAMS = 2
SUB_BLOCK = 256
ROW_BLOCK = 2 * SUB_BLOCK
ADA_TILE = 1536
GATHER_DEPTH = 3
WEIGHT_DEPTH = 3
WEIGHT_DMA_PRIORITY = 1
VMEM_LIMIT = 60 * 1024 * 1024

PACK_W = D_MODEL // 2
SC_ROWS = 128
SC_WORKERS = 32

F32 = jnp.float32
BF16 = jnp.bfloat16


def _layer_norm(r, g, b):
    mu = jnp.mean(r, axis=-1, keepdims=True)
    d = r - mu
    var = jnp.mean(d * d, axis=-1, keepdims=True)
    return d * lax.rsqrt(var + LN_EPS) * g + b


def _rows(ref, start, size):
    return ref.at[pl.ds(pl.multiple_of(start, SUBLANES), size), :]


def _pack_bf16_pairs(v):
    bits = pltpu.bitcast(v, jnp.int32)
    return bits[:, 0:PACK_W] | lax.shift_right_logical(bits[:, PACK_W:2 * PACK_W], 16)


def _unpack_pairs_f32(words):
    return jnp.concatenate([pltpu.bitcast(words & jnp.int32(-65536), F32),
                            pltpu.bitcast(lax.shift_left(words, 16), F32)], axis=1)


def _unpack_bf16_pairs(words):
    first = pltpu.bitcast(words & jnp.int32(-65536), F32).astype(BF16)
    second = pltpu.bitcast(lax.shift_left(words, 16), F32).astype(BF16)
    return first, second


def _ada_kernel(c_ref, w_ref, b_ref, o_ref):
    c = c_ref[...]
    c_act = (c * jax.nn.sigmoid(c)).astype(BF16)
    o_ref[0] = jnp.dot(c_act, w_ref[0].astype(BF16), preferred_element_type=F32) + b_ref[0]


def _ada_call(c, w_ada, b_ada):
    depth, d, n = w_ada.shape
    bsz = c.shape[0]
    return pl.pallas_call(
        _ada_kernel,
        grid=(depth, n // ADA_TILE),
        in_specs=[
            pl.BlockSpec((bsz, d), lambda l, j: (0, 0)),
            pl.BlockSpec((1, d, ADA_TILE), lambda l, j: (l, 0, j)),
            pl.BlockSpec((1, 1, ADA_TILE), lambda l, j: (l, 0, j)),
        ],
        out_specs=pl.BlockSpec((1, bsz, ADA_TILE), lambda l, j: (l, 0, j)),
        out_shape=jax.ShapeDtypeStruct((depth, bsz, n), F32),
        compiler_params=pltpu.CompilerParams(
            dimension_semantics=("arbitrary", "arbitrary"), vmem_limit_bytes=VMEM_LIMIT),
        name="ada",
    )(c, w_ada, b_ada.reshape(depth, 1, n))


def _top2_of_group(pg, sub_iota):
    big = float(EXPERTS_PER_GROUP)
    m1 = jnp.max(pg, axis=0, keepdims=True)
    i1 = jnp.min(jnp.where(pg == m1, sub_iota, big), axis=0, keepdims=True)
    rest = jnp.where(sub_iota == i1, -1.0, pg)
    m2 = jnp.max(rest, axis=0, keepdims=True)
    i2 = jnp.min(jnp.where(rest == m2, sub_iota, big), axis=0, keepdims=True)
    return m1, i1, m2, i2


def _combine_tile(ya0_ref, ya1_ref, wt_ref, x1, gate, g_ref, b_ref, *, alpha):
    wt = wt_ref[...]
    y = wt[:, 0:1] * _unpack_pairs_f32(ya0_ref[...]) + wt[:, 1:2] * _unpack_pairs_f32(ya1_ref[...])
    return _layer_norm(alpha * x1 + gate * y, g_ref[...], b_ref[...])


def _mixer_kernel(*refs, ts, alpha, nt, fused):
    if fused:
        xprev_ref, ya0_ref, ya1_ref, wtp_ref, adap_ref, ln2g_ref, ln2b_ref = refs[:7]
        refs = refs[7:]
    else:
        x_ref, refs = refs[0], refs[1:]
    (ada_ref, win_ref, wout_ref, convw_ref, convb_ref, poolw_ref, pscale_ref, lng_ref, lnb_ref,
     sguw_ref, sgub_ref, ln1g_ref, ln1b_ref, wr_ref, br_ref, tri_ref,
     x1_ref, h2p_ref, e_ref, wt_ref, rank_ref, cnt_ref,
     gx_scr, p_scr, y_scr, cnt_scr, gx_halo, p_halo) = refs

    i = pl.program_id(0)
    first_of_seq = (i % (nt // TILES_PER_STEP)) == 0

    @pl.when(i == 0)
    def _():
        gx_halo[...] = jnp.zeros_like(gx_halo)
        p_halo[...] = jnp.zeros_like(p_halo)
        cnt_scr[...] = jnp.zeros_like(cnt_scr)

    conv_halo = jnp.where(first_of_seq, 0.0, gx_halo[...])
    pool_halo = jnp.where(first_of_seq, 0.0, p_halo[...])
    counts = cnt_scr[...]
    for sub in range(TILES_PER_STEP):
        rows = pl.ds(sub * ts, ts)
        if fused:
            x = _combine_tile(ya0_ref.at[rows, :], ya1_ref.at[rows, :], wtp_ref.at[rows, :], xprev_ref[rows, :],
                              adap_ref[0][5:6], ln2g_ref, ln2b_ref, alpha=alpha)
        else:
            x = x_ref[0, rows, :]
        tile_in_seq = (i % (nt // TILES_PER_STEP)) * TILES_PER_STEP + sub
        conv_halo, pool_halo, counts = _mixer_tile(
            tile_in_seq, x, conv_halo, pool_halo, counts,
            ada_ref, win_ref, wout_ref, convw_ref, convb_ref, poolw_ref, pscale_ref,
            lng_ref, lnb_ref, sguw_ref, sgub_ref, ln1g_ref, ln1b_ref, wr_ref, br_ref, tri_ref,
            x1_ref.at[0, rows, :], h2p_ref.at[rows, :], e_ref.at[sub], wt_ref.at[sub], rank_ref.at[sub],
            gx_scr.at[sub], p_scr.at[sub], y_scr.at[sub], ts=ts, alpha=alpha)
    gx_halo[...] = conv_halo
    p_halo[...] = pool_halo
    cnt_scr[...] = counts
    cnt_ref[...] = counts


def _mixer_tile(j, x, conv_halo, pool_halo, counts,
                ada_ref, win_ref, wout_ref, convw_ref, convb_ref, poolw_ref, pscale_ref,
                lng_ref, lnb_ref, sguw_ref, sgub_ref, ln1g_ref, ln1b_ref, wr_ref, br_ref, tri_ref,
                x1_out, h2p_out, e_out, wt_out, rank_out, gx_scr, p_scr, y_scr, *, ts, alpha):
    ada = ada_ref[0]
    sh1, sc1, g1 = ada[0:1], ada[1:2], ada[2:3]
    sh2, sc2 = ada[3:4], ada[4:5]
    hb = (x * (1.0 + sc1) + sh1).astype(BF16)

    def proj(lo, hi):
        return jnp.dot(hb, win_ref[:, lo:hi], preferred_element_type=F32)

    gcxc = proj(OFF_GC, OFF_P)
    g = gcxc[:, :CONV_W] * gcxc[:, CONV_W:]
    gx_scr[CONV_HALO:CONV_HALO + ts, :] = g
    cw = convw_ref[...]
    gx_scr[0:CONV_HALO, :] = conv_halo
    conv = (cw[0:1] * gx_scr[CONV_HALO - 2:CONV_HALO - 2 + ts, :]
            + cw[1:2] * gx_scr[CONV_HALO - 1:CONV_HALO - 1 + ts, :]
            + cw[2:3] * g + convb_ref[...])
    y_scr[:, 0:CONV_W] = (proj(OFF_GB, OFF_GC) * conv).astype(BF16)

    p = proj(OFF_P, OFF_U)
    p_scr[POOL_HALO:POOL_HALO + ts, :] = p
    p_scr[0:POOL_HALO, :] = pool_halo

    def shifted(k, lo):
        return p_scr[POOL_HALO - k:POOL_HALO - k + ts, lo:lo + LANES]

    s2 = p[:, 0:LANES] + shifted(1, 0)
    s4 = s2 + shifted(2, 0) + shifted(3, 0)
    acc = p[:, LANES:2 * LANES]
    for k in range(1, 8):
        acc = acc + shifted(k, LANES)
    s8 = acc
    for k in range(8, 16):
        acc = acc + shifted(k, LANES)
    s16 = acc
    lane = lax.broadcasted_iota(jnp.int32, (ts, LANES), 1)
    tpos = (lax.broadcasted_iota(jnp.int32, (ts, LANES), 0) + (j * ts + 1)).astype(F32)
    lo_half = lane < HEAD_DIM
    cnt_a = jnp.minimum(tpos, jnp.where(lo_half, float(POOL_WINDOWS[0]), float(POOL_WINDOWS[1])))
    cnt_b = jnp.minimum(tpos, jnp.where(lo_half, float(POOL_WINDOWS[2]), float(POOL_WINDOWS[3])))
    pooled = jnp.concatenate([jnp.where(lo_half, s2, s4) / cnt_a,
                              jnp.where(lo_half, s8, s16) / cnt_b], axis=1) - p
    mixed = jnp.dot(pooled.astype(BF16), poolw_ref[...], preferred_element_type=F32)
    y_scr[:, YOFF_POOL:YOFF_POOL + POOL_W] = (mixed * pscale_ref[...]).astype(BF16)

    v = proj(OFF_V, D_IN)
    vnb = _layer_norm(v, lng_ref[...], lnb_ref[...]).astype(BF16)
    u = proj(OFF_U, OFF_V)
    row_c = lax.broadcasted_iota(jnp.int32, (CHUNK, 2 * CHUNK), 0)
    col_c = lax.broadcasted_iota(jnp.int32, (CHUNK, 2 * CHUNK), 1)
    causal = (col_c & (CHUNK - 1)) <= row_c
    lo_lanes = lax.broadcasted_iota(jnp.int32, (CHUNK, LANES), 1) < HEAD_DIM
    zero_b = jnp.zeros((CHUNK, LANES), BF16)
    for hp in range(SGU_W // LANES):
        wl = jnp.where(causal, sguw_ref[hp], 0.0).astype(BF16)
        bias = sgub_ref[hp]
        for ci in range(ts // CHUNK):
            rs = slice(ci * CHUNK, (ci + 1) * CHUNK)
            vc = vnb[rs, hp * LANES:(hp + 1) * LANES]
            rhs = jnp.concatenate([jnp.where(lo_lanes, vc, zero_b),
                                   jnp.where(lo_lanes, zero_b, vc)], axis=0)
            mixed_c = jnp.dot(wl, rhs, preferred_element_type=F32) + bias
            y_scr[rs, YOFF_SGU + hp * LANES:YOFF_SGU + (hp + 1) * LANES] = (
                u[rs, hp * LANES:(hp + 1) * LANES] * mixed_c).astype(BF16)

    yo = jnp.dot(y_scr[...], wout_ref[...], preferred_element_type=F32)
    x1 = _layer_norm(alpha * x + g1 * yo, ln1g_ref[...], ln1b_ref[...])
    x1_out[...] = x1
    h2b = (x1 * (1.0 + sc2) + sh2).astype(BF16)
    h2p_out[...] = _pack_bf16_pairs(h2b.astype(F32))

    logits = jnp.dot(h2b, wr_ref[...], preferred_element_type=F32) + br_ref[...]
    lt = logits.T[0:N_EXPERTS, :]
    ex = jnp.exp(lt - jnp.max(lt, axis=0, keepdims=True))
    probs = ex / jnp.sum(ex, axis=0, keepdims=True)
    sub_iota = lax.broadcasted_iota(jnp.int32, (EXPERTS_PER_GROUP, ts), 0).astype(F32)
    best = None
    for gi in range(N_GROUPS):
        m1, i1, m2, i2 = _top2_of_group(
            probs[gi * EXPERTS_PER_GROUP:(gi + 1) * EXPERTS_PER_GROUP, :], sub_iota)
        score = m1 + m2
        cand = (score, m1, m2, i1 + float(gi * EXPERTS_PER_GROUP), i2 + float(gi * EXPERTS_PER_GROUP))
        if best is None:
            best = cand
        else:
            better = score > best[0]
            best = tuple(jnp.where(better, c, o) for c, o in zip(cand, best))
    _, p1, p2, e1, e2 = best
    den = p1 + p2
    w1 = p1 / den
    w2 = p2 / den

    wt_out[...] = jnp.concatenate([w1, w2], axis=0)
    e_out[...] = jnp.concatenate([e1, e2], axis=0).astype(jnp.int32)

    ex_iota = lax.broadcasted_iota(jnp.int32, (N_EXPERTS, ts), 0).astype(F32)
    oh1 = ex_iota == e1
    oh2 = ex_iota == e2
    either = jnp.where(oh1 | oh2, 1.0, 0.0)
    seen = jnp.dot(either.astype(BF16), tri_ref[...], preferred_element_type=F32) + counts[:, 0:1]
    r1 = jnp.sum(jnp.where(oh1, seen, 0.0), axis=0, keepdims=True)
    r2 = jnp.sum(jnp.where(oh2, seen, 0.0), axis=0, keepdims=True)
    rank_out[...] = jnp.concatenate([r1, r2], axis=0).astype(jnp.int32)
    return g[ts - CONV_HALO:ts, :], p[ts - POOL_HALO:ts, :], counts + jnp.sum(either, axis=1, keepdims=True)


def _mixer_call(x, ada_l, w_in, w_out, conv_w, conv_b, poolw_bd, pool_scale, ln_g, ln_b,
                sguw_pair, sgub_pair, ln1_g, ln1_b, wr_pad, br_pad, tri, alpha, prev=None):
    bsz, s, d = x.shape
    ts = SEQ_TILE
    nt = s // ts
    n_tiles = bsz * nt
    n_tok = bsz * s
    fused = prev is not None
    assert nt % TILES_PER_STEP == 0
    steps_per_seq = nt // TILES_PER_STEP
    n_steps = n_tiles // TILES_PER_STEP
    rows = TILES_PER_STEP * ts
    const2 = lambda i: (0, 0)
    const3 = lambda i: (0, 0, 0)
    seq_of = lambda i: (i // steps_per_seq, 0, 0)
    tile3 = lambda i: (i // steps_per_seq, i % steps_per_seq, 0)
    tile2 = lambda i: (i, 0)
    meta3 = lambda i: (i, 0, 0)
    once = pl.Buffered(1)
    in_specs = [
        pl.BlockSpec((1, ADA_CHUNKS, d), seq_of),
        pl.BlockSpec((d, D_IN), const2, pipeline_mode=once),
        pl.BlockSpec((d, d), const2, pipeline_mode=once),
        pl.BlockSpec((CONV_K, CONV_W), const2),
        pl.BlockSpec((1, CONV_W), const2),
        pl.BlockSpec((POOL_W, POOL_W), const2),
        pl.BlockSpec((1, POOL_W), const2),
        pl.BlockSpec((1, SGU_W), const2),
        pl.BlockSpec((1, SGU_W), const2),
        pl.BlockSpec((SGU_W // LANES, CHUNK, 2 * CHUNK), const3),
        pl.BlockSpec((SGU_W // LANES, CHUNK, LANES), const3),
        pl.BlockSpec((1, d), const2),
        pl.BlockSpec((1, d), const2),
        pl.BlockSpec((d, LANES), const2),
        pl.BlockSpec((1, LANES), const2),
        pl.BlockSpec((ts, ts), const2, pipeline_mode=once),
    ]
    operands = [ada_l, w_in, w_out, conv_w, conv_b, poolw_bd, pool_scale, ln_g, ln_b,
                sguw_pair, sgub_pair, ln1_g, ln1_b, wr_pad, br_pad, tri]
    if fused:
        ya, wt_tok, ada_prev, ln2_g, ln2_b = prev
        in_specs = [
            pl.BlockSpec((rows, d), tile2),
            pl.BlockSpec((rows, PACK_W), tile2),
            pl.BlockSpec((rows, PACK_W), lambda i: (n_steps + i, 0)),
            pl.BlockSpec((rows, TOP_K), tile2),
            pl.BlockSpec((1, ADA_CHUNKS, d), seq_of),
            pl.BlockSpec((1, d), const2),
            pl.BlockSpec((1, d), const2),
        ] + in_specs
        operands = [x.reshape(n_tok, d), ya, ya, wt_tok, ada_prev, ln2_g, ln2_b] + operands
    else:
        in_specs = [pl.BlockSpec((1, rows, d), tile3)] + in_specs
        operands = [x] + operands
    kern = functools.partial(_mixer_kernel, ts=ts, alpha=alpha, nt=nt, fused=fused)
    return pl.pallas_call(
        kern,
        grid=(n_steps,),
        in_specs=in_specs,
        out_specs=[
            pl.BlockSpec((1, rows, d), tile3),
            pl.BlockSpec((rows, PACK_W), tile2),
            pl.BlockSpec((TILES_PER_STEP, TOP_K, ts), meta3),
            pl.BlockSpec((TILES_PER_STEP, TOP_K, ts), meta3),
            pl.BlockSpec((TILES_PER_STEP, TOP_K, ts), meta3),
            pl.BlockSpec((N_EXPERTS, LANES), const2),
        ],
        out_shape=[
            jax.ShapeDtypeStruct((bsz, s, d), F32),
            jax.ShapeDtypeStruct((n_tok, PACK_W), jnp.int32),
            jax.ShapeDtypeStruct((n_tiles, TOP_K, ts), jnp.int32),
            jax.ShapeDtypeStruct((n_tiles, TOP_K, ts), F32),
            jax.ShapeDtypeStruct((n_tiles, TOP_K, ts), jnp.int32),
            jax.ShapeDtypeStruct((N_EXPERTS, LANES), F32),
        ],
        scratch_shapes=[
            pltpu.VMEM((TILES_PER_STEP, CONV_HALO + ts, CONV_W), F32),
            pltpu.VMEM((TILES_PER_STEP, POOL_HALO + ts, POOL_W), F32),
            pltpu.VMEM((TILES_PER_STEP, ts, d), BF16),
            pltpu.VMEM((N_EXPERTS, LANES), F32),
            pltpu.VMEM((CONV_HALO, CONV_W), F32),
            pltpu.VMEM((POOL_HALO, POOL_W), F32),
        ],
        compiler_params=pltpu.CompilerParams(
            dimension_semantics=("arbitrary",), vmem_limit_bytes=VMEM_LIMIT),
        name="mixer_router",
    )(*operands)


def _sc_worker_chunks(n_chunks):
    per_worker = n_chunks // SC_WORKERS
    worker = lax.axis_index("s") * 2 + lax.axis_index("c")
    return worker * per_worker, per_worker


def _dispatch_call(h2p, dest, n_rows):
    n_tok, width = h2p.shape
    n_chunks = n_tok // SC_ROWS
    assert n_chunks % SC_WORKERS == 0

    def body(h_hbm, dest_hbm, xs_hbm, rows_v, idx_v):
        first, per_worker = _sc_worker_chunks(n_chunks)

        @pl.loop(0, per_worker)
        def _(c):
            chunk = first + c
            pltpu.sync_copy(h_hbm.at[pl.ds(chunk * SC_ROWS, SC_ROWS)], rows_v)
            for k in range(TOP_K):
                pltpu.sync_copy(dest_hbm.at[k, chunk], idx_v.at[k])
                pltpu.sync_copy(rows_v, xs_hbm.at[idx_v.at[k]])

    return pl.kernel(
        body,
        out_type=jax.ShapeDtypeStruct((n_rows, width), jnp.int32),
        mesh=plsc.VectorSubcoreMesh(core_axis_name="c", subcore_axis_name="s"),
        scratch_types=[pltpu.VMEM((SC_ROWS, width), jnp.int32), pltpu.VMEM((TOP_K, SC_ROWS), jnp.int32)],
        name="dispatch",
    )(h2p, dest)


def _collect_call(ys, dest, n_tok):
    width = ys.shape[1]
    n_chunks = n_tok // SC_ROWS
    assert n_chunks % SC_WORKERS == 0

    def body(ys_hbm, dest_hbm, ya_hbm, rows_v, idx_v):
        first, per_worker = _sc_worker_chunks(n_chunks)

        @pl.loop(0, per_worker)
        def _(c):
            chunk = first + c
            for k in range(TOP_K):
                pltpu.sync_copy(dest_hbm.at[k, chunk], idx_v.at[k])
                pltpu.sync_copy(ys_hbm.at[idx_v.at[k]], rows_v)
                pltpu.sync_copy(rows_v, ya_hbm.at[pl.ds(k * n_tok + chunk * SC_ROWS, SC_ROWS)])

    return pl.kernel(
        body,
        out_type=jax.ShapeDtypeStruct((TOP_K * n_tok, width), jnp.int32),
        mesh=plsc.VectorSubcoreMesh(core_axis_name="c", subcore_axis_name="s"),
        scratch_types=[pltpu.VMEM((SC_ROWS, width), jnp.int32), pltpu.VMEM((TOP_K, SC_ROWS), jnp.int32)],
        name="collect",
    )(ys, dest)


def _ffn_kernel(b0_ref, nb_ref, rows_ref, tot_ref, xs_ref, wg_ref, wu_ref, wd_ref, ys_ref,
                xbuf, obuf, wbuf_g, wbuf_u, wbuf_d, wg_scr, wu_scr, wd_scr, gsem, osem, wsem,
                *, layer, n_blocks):
    e = pl.program_id(0)
    n_exp = pl.num_programs(0)
    b0 = b0_ref[e]
    nb = nb_ref[e]
    n_used = tot_ref[0]

    def in_copy(g):
        slot = g % GATHER_DEPTH
        return pltpu.make_async_copy(_rows(xs_ref, g * ROW_BLOCK, ROW_BLOCK), xbuf.at[slot], gsem.at[slot])

    def out_copy(g, slot):
        return pltpu.make_async_copy(obuf.at[slot], _rows(ys_ref, g * ROW_BLOCK, ROW_BLOCK), osem.at[slot])

    def weight_copies(expert, slot):
        return [pltpu.make_async_copy(src.at[layer, expert], dst.at[slot], wsem.at[slot])
                for src, dst in ((wg_ref, wbuf_g), (wu_ref, wbuf_u), (wd_ref, wbuf_d))]

    @pl.when(e == 0)
    def _():
        for first in range(WEIGHT_DEPTH - 1):
            for cp in weight_copies(first, first):
                cp.start(priority=WEIGHT_DMA_PRIORITY)
        for ahead in range(GATHER_DEPTH - 1):
            @pl.when(ahead < n_used)
            def _():
                in_copy(ahead).start()

    @pl.when(e + (WEIGHT_DEPTH - 1) < n_exp)
    def _():
        for cp in weight_copies(e + (WEIGHT_DEPTH - 1), (e + (WEIGHT_DEPTH - 1)) % WEIGHT_DEPTH):
            cp.start(priority=WEIGHT_DMA_PRIORITY)

    wslot = e % WEIGHT_DEPTH
    for cp in weight_copies(e, wslot):
        cp.wait()

    @pl.when(nb > 0)
    def _():
        wg_scr[...] = wbuf_g[wslot].astype(BF16)
        wu_scr[...] = wbuf_u[wslot].astype(BF16)
        wd_scr[...] = wbuf_d[wslot].astype(BF16)

    def block(k, carry):
        g = b0 + k
        islot = g % GATHER_DEPTH
        oslot = g % 2

        @pl.when(g + (GATHER_DEPTH - 1) < n_used)
        def _():
            in_copy(g + (GATHER_DEPTH - 1)).start()

        in_copy(g).wait()

        @pl.when(g >= 2)
        def _():
            out_copy(g - 2, oslot).wait()

        def chain(half):
            rows = pl.ds(half * SUB_BLOCK, SUB_BLOCK)
            xb = jnp.concatenate(_unpack_bf16_pairs(xbuf[islot, rows, :]), axis=1)
            gate = jnp.dot(xb, wg_scr[...], preferred_element_type=F32)
            up = jnp.dot(xb, wu_scr[...], preferred_element_type=F32)
            act = (gate * jax.nn.sigmoid(gate) * up).astype(BF16)
            y = jnp.dot(act, wd_scr[...], preferred_element_type=F32)
            obuf[oslot, rows, :] = _pack_bf16_pairs(y.astype(BF16).astype(F32))

        both = rows_ref[e] - k * ROW_BLOCK > SUB_BLOCK

        @pl.when(both)
        def _():
            chain(0)
            chain(1)

        @pl.when(jnp.logical_not(both))
        def _():
            chain(0)
            obuf[oslot, pl.ds(SUB_BLOCK, SUB_BLOCK), :] = jnp.zeros((SUB_BLOCK, PACK_W), jnp.int32)

        out_copy(g, oslot).start()
        return carry

    lax.fori_loop(0, nb, block, 0)

    @pl.when(e == n_exp - 1)
    def _():
        @pl.when(n_used >= 2)
        def _():
            out_copy(n_used - 2, n_used % 2).wait()

        out_copy(n_used - 1, (n_used - 1) % 2).wait()
        obuf[0] = jnp.zeros((ROW_BLOCK, PACK_W), jnp.int32)

        def fill(g, carry):
            out_copy(g, 0).start()
            return carry

        def drain(g, carry):
            out_copy(g, 0).wait()
            return carry

        lax.fori_loop(n_used, n_blocks, fill, 0)
        lax.fori_loop(n_used, n_blocks, drain, 0)


def _ffn_call(layer, blk_start, seg_blocks, seg_rows, n_used, n_blocks, xs, w_gate, w_up, w_down):
    d, f = w_gate.shape[-2:]
    any_spec = pl.BlockSpec(memory_space=pl.ANY)
    return pl.pallas_call(
        functools.partial(_ffn_kernel, layer=layer, n_blocks=n_blocks),
        grid_spec=pltpu.PrefetchScalarGridSpec(
            num_scalar_prefetch=4,
            grid=(N_EXPERTS,),
            in_specs=[any_spec, any_spec, any_spec, any_spec],
            out_specs=any_spec,
            scratch_shapes=[
                pltpu.VMEM((GATHER_DEPTH, ROW_BLOCK, PACK_W), jnp.int32),
                pltpu.VMEM((2, ROW_BLOCK, PACK_W), jnp.int32),
                pltpu.VMEM((WEIGHT_DEPTH, d, f), F32),
                pltpu.VMEM((WEIGHT_DEPTH, d, f), F32),
                pltpu.VMEM((WEIGHT_DEPTH, f, d), F32),
                pltpu.VMEM((d, f), BF16),
                pltpu.VMEM((d, f), BF16),
                pltpu.VMEM((f, d), BF16),
                pltpu.SemaphoreType.DMA((GATHER_DEPTH,)),
                pltpu.SemaphoreType.DMA((2,)),
                pltpu.SemaphoreType.DMA((WEIGHT_DEPTH,)),
            ],
        ),
        out_shape=jax.ShapeDtypeStruct((n_blocks * ROW_BLOCK, PACK_W), jnp.int32),
        compiler_params=pltpu.CompilerParams(
            dimension_semantics=("arbitrary",), vmem_limit_bytes=VMEM_LIMIT),
        name="expert_ffn",
    )(blk_start, seg_blocks, seg_rows, n_used, xs, w_gate, w_up, w_down)


def _combine_kernel(ya0_ref, ya1_ref, wt_ref, x1_ref, ada_ref, g_ref, b_ref, o_ref, *, alpha):
    o_ref[...] = _combine_tile(ya0_ref, ya1_ref, wt_ref, x1_ref[...], ada_ref[0][5:6], g_ref, b_ref, alpha=alpha)


def _combine_call(ya, wt_tok, x1_flat, ada_l, ln_g, ln_b, alpha, seq):
    n_tok, d = x1_flat.shape
    ts = SEQ_TILE
    n_tiles = n_tok // ts
    per_seq = seq // ts
    return pl.pallas_call(
        functools.partial(_combine_kernel, alpha=alpha),
        grid=(n_tiles,),
        in_specs=[
            pl.BlockSpec((ts, PACK_W), lambda i: (i, 0)),
            pl.BlockSpec((ts, PACK_W), lambda i: (n_tiles + i, 0)),
            pl.BlockSpec((ts, TOP_K), lambda i: (i, 0)),
            pl.BlockSpec((ts, d), lambda i: (i, 0)),
            pl.BlockSpec((1, ADA_CHUNKS, d), lambda i: (i // per_seq, 0, 0)),
            pl.BlockSpec((1, d), lambda i: (0, 0)),
            pl.BlockSpec((1, d), lambda i: (0, 0)),
        ],
        out_specs=pl.BlockSpec((ts, d), lambda i: (i, 0)),
        out_shape=jax.ShapeDtypeStruct((n_tok, d), F32),
        compiler_params=pltpu.CompilerParams(
            dimension_semantics=("arbitrary",), vmem_limit_bytes=VMEM_LIMIT),
        name="combine_ln",
    )(ya, ya, wt_tok, x1_flat, ada_l, ln_g, ln_b)


def kernel(x, c, w_ada, b_ada, w_in, conv_w, conv_b, pool_w, pool_scale, sgu_ln_g, sgu_ln_b,
           sgu_w, sgu_b, w_out, ln1_g, ln1_b, w_router, b_router, w_gate, w_up, w_down,
           ln2_g, ln2_b):
    bsz, seq, d = x.shape
    depth = w_ada.shape[0]
    alpha = (2 * depth) ** 0.25
    assert d == D_MODEL and seq % SEQ_TILE == 0 and SEQ_TILE % CHUNK == 0

    ada = _ada_call(c, w_ada, b_ada).reshape(depth, bsz, ADA_CHUNKS, d)

    w_in_b = w_in.astype(BF16)
    w_out_b = w_out.astype(BF16)
    eye_g = jnp.eye(POOL_W // HEAD_DIM, dtype=F32)
    poolw_bd = jnp.einsum('lgcd,gh->lgchd', pool_w, eye_g).reshape(depth, POOL_W, POOL_W).astype(BF16)
    n_pairs = SGU_W // LANES
    sguw_pair = sgu_w.reshape(depth, n_pairs, 2, CHUNK, CHUNK).transpose(0, 1, 3, 2, 4).reshape(
        depth, n_pairs, CHUNK, 2 * CHUNK)
    sgub_pair = jnp.repeat(sgu_b.transpose(0, 2, 1), HEAD_DIM, axis=-1).reshape(
        depth, CHUNK, n_pairs, LANES).transpose(0, 2, 1, 3)
    wr_pad = jnp.pad(w_router, ((0, 0), (0, LANES - N_EXPERTS))).astype(BF16)
    br_pad = jnp.pad(b_router, (0, LANES - N_EXPERTS)).reshape(1, LANES)

    assert bsz % BATCH_STREAMS == 0
    part = bsz // BATCH_STREAMS
    n_part = part * seq
    n_steps = (n_part * TOP_K) // ROW_BLOCK + N_EXPERTS
    experts = jnp.arange(N_EXPERTS, dtype=jnp.int32)
    tri = jnp.triu(jnp.ones((SEQ_TILE, SEQ_TILE), BF16), k=1)

    def moe_layer(l, x_in, ada_part, prev):
        x1, h2p, e_idx, wts, rank, counts = _mixer_call(
            x_in, ada_part[l], w_in_b[l], w_out_b[l], conv_w[l], conv_b[l].reshape(1, -1), poolw_bd[l],
            pool_scale[l].reshape(1, -1), sgu_ln_g[l].reshape(1, -1), sgu_ln_b[l].reshape(1, -1),
            sguw_pair[l], sgub_pair[l], ln1_g[l].reshape(1, -1), ln1_b[l].reshape(1, -1),
            wr_pad, br_pad, tri, alpha, prev)
        sizes = counts[:, 0].astype(jnp.int32)
        seg_blocks = (sizes + ROW_BLOCK - 1) // ROW_BLOCK
        blk_end = jnp.cumsum(seg_blocks)
        blk_start = (blk_end - seg_blocks).astype(jnp.int32)
        n_used = blk_end[-1:].astype(jnp.int32)
        e_tok = e_idx.transpose(1, 0, 2).reshape(TOP_K, n_part)
        rank_tok = rank.transpose(1, 0, 2).reshape(TOP_K, n_part)
        seg_row0 = jnp.sum(jnp.where(e_tok[:, :, None] == experts[None, None, :],
                                     (blk_start * ROW_BLOCK)[None, None, :], 0), axis=2)
        dest = (seg_row0 + rank_tok).astype(jnp.int32).reshape(TOP_K, n_part // SC_ROWS, SC_ROWS)

        xs = _dispatch_call(h2p, dest, n_steps * ROW_BLOCK)
        ys = _ffn_call(l, blk_start, seg_blocks.astype(jnp.int32), sizes, n_used, n_steps,
                       xs, w_gate, w_up, w_down)
        ya = _collect_call(ys, dest, n_part)
        wt_tok = wts.transpose(0, 2, 1).reshape(n_part, TOP_K)
        return x1, (ya, wt_tok, ada_part[l], ln2_g[l].reshape(1, -1), ln2_b[l].reshape(1, -1))

    x_parts = [x[p * part:(p + 1) * part] for p in range(BATCH_STREAMS)]
    ada_parts = [ada[:, p * part:(p + 1) * part] for p in range(BATCH_STREAMS)]
    prevs = [None] * BATCH_STREAMS
    for l in range(depth):
        for p in range(BATCH_STREAMS):
            x_parts[p], prevs[p] = moe_layer(l, x_parts[p], ada_parts[p], prevs[p])
    outs = []
    for x_p, (ya, wt_tok, ada_last, ln2_g_last, ln2_b_last) in zip(x_parts, prevs):
        outs.append(_combine_call(ya, wt_tok, x_p.reshape(n_part, d), ada_last, ln2_g_last, ln2_b_last,
                                  alpha, seq).reshape(part, seq, d))
    return jnp.concatenate(outs, axis=0)
```

```python
import functools

import jax
import jax.numpy as jnp
from jax import lax
from jax.experimental import pallas as pl
from jax.experimental.pallas import tpu as pltpu
from jax.experimental.pallas import tpu_sc as plsc

D_MODEL = 1024
HEAD_DIM = D_MODEL // 16
CONV_W = 6 * HEAD_DIM
POOL_W = 4 * HEAD_DIM
SGU_W = 6 * HEAD_DIM
D_IN = 3 * CONV_W + POOL_W + 2 * SGU_W
CONV_K = 3
POOL_WINDOWS = (2, 4, 8, 16)
CHUNK = 128
N_EXPERTS = 32
N_GROUPS = 4
EXPERTS_PER_GROUP = N_EXPERTS // N_GROUPS
TOP_K = 2
D_FF = D_MODEL // 2
ADA_CHUNKS = 6
LN_EPS = 1e-5

OFF_GB = 0
OFF_GC = CONV_W
OFF_P = 3 * CONV_W
OFF_U = OFF_P + POOL_W
OFF_V = OFF_U + SGU_W
YOFF_POOL = CONV_W
YOFF_SGU = CONV_W + POOL_W

LANES = 128
SUBLANES = 8
CONV_HALO = 8
POOL_HALO = 16
POOL_SLACK = 8
SEQ_TILE = 512
SUB_BLOCK = 256
ROW_BLOCK = 2 * SUB_BLOCK
ADA_TILE = 1536
GATHER_DEPTH = 3
WEIGHT_DEPTH = 3
WEIGHT_DMA_PRIORITY = 1
VMEM_LIMIT = 60 * 1024 * 1024

PACK_W = D_MODEL // 2
SC_ROWS = 128
SC_WORKERS = 32

F32 = jnp.float32
BF16 = jnp.bfloat16


def _layer_norm(r, g, b):
    mu = jnp.mean(r, axis=-1, keepdims=True)
    d = r - mu
    var = jnp.mean(d * d, axis=-1, keepdims=True)
    return d * lax.rsqrt(var + LN_EPS) * g + b


def _rows(ref, start, size):
    return ref.at[pl.ds(pl.multiple_of(start, SUBLANES), size), :]


def _pack_bf16_pairs(v):
    bits = pltpu.bitcast(v, jnp.int32)
    return bits[:, 0:PACK_W] | lax.shift_right_logical(bits[:, PACK_W:2 * PACK_W], 16)


def _unpack_pairs_f32(words):
    return jnp.concatenate([pltpu.bitcast(words & jnp.int32(-65536), F32),
                            pltpu.bitcast(lax.shift_left(words, 16), F32)], axis=1)


def _unpack_bf16_pairs(words):
    first = pltpu.bitcast(words & jnp.int32(-65536), F32).astype(BF16)
    second = pltpu.bitcast(lax.shift_left(words, 16), F32).astype(BF16)
    return first, second


def _ada_kernel(c_ref, w_ref, b_ref, o_ref):
    c = c_ref[...]
    c_act = (c * jax.nn.sigmoid(c)).astype(BF16)
    col = pl.program_id(1) * ADA_TILE + lax.broadcasted_iota(jnp.int32, (1, ADA_TILE), 1)
    chunk = col // D_MODEL
    one = jnp.where((chunk == 1) | (chunk == 4), 1.0, 0.0)
    o_ref[0] = jnp.dot(c_act, w_ref[0].astype(BF16), preferred_element_type=F32) + (b_ref[0] + one)


def _ada_call(c, w_ada, b_ada):
    depth, d, n = w_ada.shape
    bsz = c.shape[0]
    return pl.pallas_call(
        _ada_kernel,
        grid=(depth, n // ADA_TILE),
        in_specs=[
            pl.BlockSpec((bsz, d), lambda l, j: (0, 0)),
            pl.BlockSpec((1, d, ADA_TILE), lambda l, j: (l, 0, j)),
            pl.BlockSpec((1, 1, ADA_TILE), lambda l, j: (l, 0, j)),
        ],
        out_specs=pl.BlockSpec((1, bsz, ADA_TILE), lambda l, j: (l, 0, j)),
        out_shape=jax.ShapeDtypeStruct((depth, bsz, n), F32),
        compiler_params=pltpu.CompilerParams(
            dimension_semantics=("arbitrary", "arbitrary"), vmem_limit_bytes=VMEM_LIMIT),
        name="ada",
    )(c, w_ada, b_ada.reshape(depth, 1, n))


def _top2_of_group(pg, sub_iota):
    big = float(EXPERTS_PER_GROUP)
    m1 = jnp.max(pg, axis=0, keepdims=True)
    i1 = jnp.min(jnp.where(pg == m1, sub_iota, big), axis=0, keepdims=True)
    rest = jnp.where(sub_iota == i1, -1.0, pg)
    m2 = jnp.max(rest, axis=0, keepdims=True)
    i2 = jnp.min(jnp.where(rest == m2, sub_iota, big), axis=0, keepdims=True)
    return m1, i1, m2, i2


def _combine_tile(ya0_ref, ya1_ref, wt_ref, x1, gate, g_ref, b_ref, *, alpha):
    wt = wt_ref[...]
    y = wt[:, 0:1] * _unpack_pairs_f32(ya0_ref[...]) + wt[:, 1:2] * _unpack_pairs_f32(ya1_ref[...])
    return _layer_norm(alpha * x1 + gate * y, g_ref[...], b_ref[...])


def _mixer_kernel(*refs, ts, alpha, nt, fused):
    if fused:
        xprev_ref, ya0_ref, ya1_ref, wtp_ref, adap_ref, ln2g_ref, ln2b_ref = refs[:7]
        refs = refs[7:]
    else:
        x_ref, refs = refs[0], refs[1:]
    (ada_ref, win_ref, wout_ref, convw_ref, convb_ref, poolw_ref, pscale_ref, lng_ref, lnb_ref,
     sguw_ref, sgub_ref, ln1g_ref, ln1b_ref, wr_ref, br_ref, tri_ref,
     x1_ref, h2p_ref, e_ref, wt_ref, rank_ref, cnt_ref,
     gx_scr, p_scr, q_scr, r_scr, y_scr, cnt_scr) = refs

    i = pl.program_id(0)
    j = i % nt

    @pl.when(i == 0)
    def _():
        gx_scr[0:CONV_HALO, :] = jnp.zeros((CONV_HALO, CONV_W), F32)
        for scr in (p_scr, q_scr, r_scr):
            scr[0:POOL_SLACK + POOL_HALO, :] = jnp.zeros((POOL_SLACK + POOL_HALO, POOL_W), F32)
        cnt_scr[...] = jnp.zeros_like(cnt_scr)

    if fused:
        x = _combine_tile(ya0_ref, ya1_ref, wtp_ref, xprev_ref[...], adap_ref[0][5:6], ln2g_ref, ln2b_ref,
                          alpha=alpha)
    else:
        x = x_ref[0]
    _mixer_tile(j, x, ada_ref, win_ref, wout_ref, convw_ref, convb_ref, poolw_ref, pscale_ref,
                lng_ref, lnb_ref, sguw_ref, sgub_ref, ln1g_ref, ln1b_ref, wr_ref, br_ref, tri_ref,
                x1_ref, h2p_ref, e_ref, wt_ref, rank_ref, cnt_ref, gx_scr, p_scr, q_scr, r_scr, y_scr, cnt_scr,
                ts=ts, alpha=alpha)


def _mixer_tile(j, x, ada_ref, win_ref, wout_ref, convw_ref, convb_ref, poolw_ref, pscale_ref,
                lng_ref, lnb_ref, sguw_ref, sgub_ref, ln1g_ref, ln1b_ref, wr_ref, br_ref, tri_ref,
                x1_ref, h2p_ref, e_ref, wt_ref, rank_ref, cnt_ref, gx_scr, p_scr, q_scr, r_scr, y_scr, cnt_scr,
                *, ts, alpha):
    first_of_seq = j == 0
    ada = ada_ref[0]
    sh1, mul1, g1 = ada[0:1], ada[1:2], ada[2:3]
    sh2, mul2 = ada[3:4], ada[4:5]
    hb = (x * mul1 + sh1).astype(BF16)

    def proj(lo, hi):
        return jnp.dot(hb, win_ref[:, lo:hi], preferred_element_type=F32)

    gcxc = proj(OFF_GC, OFF_P)
    g = gcxc[:, :CONV_W] * gcxc[:, CONV_W:]
    gx_scr[CONV_HALO:CONV_HALO + ts, :] = g
    cw = convw_ref[...]
    gx_scr[0:CONV_HALO, :] = jnp.where(first_of_seq, 0.0, gx_scr[0:CONV_HALO, :])
    conv = (cw[0:1] * gx_scr[CONV_HALO - 2:CONV_HALO - 2 + ts, :]
            + cw[1:2] * gx_scr[CONV_HALO - 1:CONV_HALO - 1 + ts, :]
            + cw[2:3] * g + convb_ref[...])
    gx_scr[0:CONV_HALO, :] = g[ts - CONV_HALO:ts, :]
    y_scr[:, 0:CONV_W] = (proj(OFF_GB, OFF_GC) * conv).astype(BF16)

    p = proj(OFF_P, OFF_U)
    top = POOL_SLACK
    n_ext = POOL_HALO + ts
    p_scr[top + POOL_HALO:top + n_ext, :] = p
    p_scr[top:top + POOL_HALO, :] = jnp.where(first_of_seq, 0.0, p_scr[top:top + POOL_HALO, :])
    w2 = p_scr[top:top + n_ext, :] + p_scr[top - 1:top - 1 + n_ext, :]
    q_scr[top:top + n_ext, :] = w2
    w4 = w2 + q_scr[top - 2:top - 2 + n_ext, :]
    r_scr[top:top + n_ext, :] = w4
    w8 = w4 + r_scr[top - 4:top - 4 + n_ext, :]
    q_scr[top:top + n_ext, :] = w8
    w16 = w8 + q_scr[top - 8:top - 8 + n_ext, :]
    s2, s4 = w2[POOL_HALO:, 0:LANES], w4[POOL_HALO:, 0:LANES]
    s8, s16 = w8[POOL_HALO:, LANES:2 * LANES], w16[POOL_HALO:, LANES:2 * LANES]
    p_scr[top:top + POOL_HALO, :] = p[ts - POOL_HALO:ts, :]
    lane = lax.broadcasted_iota(jnp.int32, (ts, LANES), 1)
    tpos = (lax.broadcasted_iota(jnp.int32, (ts, LANES), 0) + (j * ts + 1)).astype(F32)
    lo_half = lane < HEAD_DIM
    cnt_a = jnp.minimum(tpos, jnp.where(lo_half, float(POOL_WINDOWS[0]), float(POOL_WINDOWS[1])))
    cnt_b = jnp.minimum(tpos, jnp.where(lo_half, float(POOL_WINDOWS[2]), float(POOL_WINDOWS[3])))
    pooled = jnp.concatenate([jnp.where(lo_half, s2, s4) / cnt_a,
                              jnp.where(lo_half, s8, s16) / cnt_b], axis=1) - p
    mixed = jnp.dot(pooled.astype(BF16), poolw_ref[...], preferred_element_type=F32)
    y_scr[:, YOFF_POOL:YOFF_POOL + POOL_W] = (mixed * pscale_ref[...]).astype(BF16)

    v = proj(OFF_V, D_IN)
    vnb = _layer_norm(v, lng_ref[...], lnb_ref[...]).astype(BF16)
    u = proj(OFF_U, OFF_V)
    row_c = lax.broadcasted_iota(jnp.int32, (CHUNK, 2 * CHUNK), 0)
    col_c = lax.broadcasted_iota(jnp.int32, (CHUNK, 2 * CHUNK), 1)
    causal = (col_c & (CHUNK - 1)) <= row_c
    lo_lanes = lax.broadcasted_iota(jnp.int32, (CHUNK, LANES), 1) < HEAD_DIM
    zero_b = jnp.zeros((CHUNK, LANES), BF16)
    for hp in range(SGU_W // LANES):
        wl = jnp.where(causal, sguw_ref[hp], 0.0).astype(BF16)
        bias = sgub_ref[hp]
        for ci in range(ts // CHUNK):
            rs = slice(ci * CHUNK, (ci + 1) * CHUNK)
            vc = vnb[rs, hp * LANES:(hp + 1) * LANES]
            rhs = jnp.concatenate([jnp.where(lo_lanes, vc, zero_b),
                                   jnp.where(lo_lanes, zero_b, vc)], axis=0)
            mixed_c = jnp.dot(wl, rhs, preferred_element_type=F32) + bias
            y_scr[rs, YOFF_SGU + hp * LANES:YOFF_SGU + (hp + 1) * LANES] = (
                u[rs, hp * LANES:(hp + 1) * LANES] * mixed_c).astype(BF16)

    yo = jnp.dot(y_scr[...], wout_ref[...], preferred_element_type=F32)
    x1 = _layer_norm(alpha * x + g1 * yo, ln1g_ref[...], ln1b_ref[...])
    x1_ref[0] = x1
    h2b = (x1 * mul2 + sh2).astype(BF16)
    h2p_ref[...] = _pack_bf16_pairs(h2b.astype(F32))

    logits = jnp.dot(h2b, wr_ref[...], preferred_element_type=F32) + br_ref[...]
    lt = logits.T[0:N_EXPERTS, :]
    ex = jnp.exp(lt - jnp.max(lt, axis=0, keepdims=True))
    probs = ex / jnp.sum(ex, axis=0, keepdims=True)
    sub_iota = lax.broadcasted_iota(jnp.int32, (EXPERTS_PER_GROUP, ts), 0).astype(F32)
    best = None
    for gi in range(N_GROUPS):
        m1, i1, m2, i2 = _top2_of_group(
            probs[gi * EXPERTS_PER_GROUP:(gi + 1) * EXPERTS_PER_GROUP, :], sub_iota)
        score = m1 + m2
        cand = (score, m1, m2, i1 + float(gi * EXPERTS_PER_GROUP), i2 + float(gi * EXPERTS_PER_GROUP))
        if best is None:
            best = cand
        else:
            better = score > best[0]
            best = tuple(jnp.where(better, c, o) for c, o in zip(cand, best))
    _, p1, p2, e1, e2 = best
    den = p1 + p2
    w1 = p1 / den
    w2 = p2 / den

    wt_ref[...] = jnp.concatenate([w1, w2], axis=0)
    e_ref[...] = jnp.concatenate([e1, e2], axis=0).astype(jnp.int32)

    ex_iota = lax.broadcasted_iota(jnp.int32, (N_EXPERTS, ts), 0).astype(F32)
    oh1 = ex_iota == e1
    oh2 = ex_iota == e2
    either = jnp.where(oh1 | oh2, 1.0, 0.0)
    seen = jnp.dot(either.astype(BF16), tri_ref[...], preferred_element_type=F32) + cnt_scr[:, 0:1]
    r1 = jnp.sum(jnp.where(oh1, seen, 0.0), axis=0, keepdims=True)
    r2 = jnp.sum(jnp.where(oh2, seen, 0.0), axis=0, keepdims=True)
    rank_ref[...] = jnp.concatenate([r1, r2], axis=0).astype(jnp.int32)
    cnt_new = cnt_scr[...] + jnp.sum(either, axis=1, keepdims=True)
    cnt_scr[...] = cnt_new
    cnt_ref[...] = cnt_new


def _mixer_call(x, ada_l, w_in, w_out, conv_w, conv_b, poolw_bd, pool_scale, ln_g, ln_b,
                sguw_pair, sgub_pair, ln1_g, ln1_b, wr_pad, br_pad, tri, alpha, prev=None):
    bsz, s, d = x.shape
    ts = SEQ_TILE
    nt = s // ts
    n_tiles = bsz * nt
    n_tok = bsz * s
    fused = prev is not None
    const2 = lambda i: (0, 0)
    const3 = lambda i: (0, 0, 0)
    seq_of = lambda i: (i // nt, 0, 0)
    tile3 = lambda i: (i // nt, i % nt, 0)
    tile2 = lambda i: (i, 0)
    by_token = lambda i: (0, i)
    once = pl.Buffered(1)
    in_specs = [
        pl.BlockSpec((1, ADA_CHUNKS, d), seq_of),
        pl.BlockSpec((d, D_IN), const2, pipeline_mode=once),
        pl.BlockSpec((d, d), const2, pipeline_mode=once),
        pl.BlockSpec((CONV_K, CONV_W), const2),
        pl.BlockSpec((1, CONV_W), const2),
        pl.BlockSpec((POOL_W, POOL_W), const2),
        pl.BlockSpec((1, POOL_W), const2),
        pl.BlockSpec((1, SGU_W), const2),
        pl.BlockSpec((1, SGU_W), const2),
        pl.BlockSpec((SGU_W // LANES, CHUNK, 2 * CHUNK), const3),
        pl.BlockSpec((SGU_W // LANES, CHUNK, LANES), const3),
        pl.BlockSpec((1, d), const2),
        pl.BlockSpec((1, d), const2),
        pl.BlockSpec((d, LANES), const2),
        pl.BlockSpec((1, LANES), const2),
        pl.BlockSpec((ts, ts), const2, pipeline_mode=once),
    ]
    operands = [ada_l, w_in, w_out, conv_w, conv_b, poolw_bd, pool_scale, ln_g, ln_b,
                sguw_pair, sgub_pair, ln1_g, ln1_b, wr_pad, br_pad, tri]
    if fused:
        ya, wt_tok, ada_prev, ln2_g, ln2_b = prev
        in_specs = [
            pl.BlockSpec((ts, d), tile2),
            pl.BlockSpec((ts, PACK_W), tile2),
            pl.BlockSpec((ts, PACK_W), lambda i: (n_tiles + i, 0)),
            pl.BlockSpec((ts, TOP_K), tile2),
            pl.BlockSpec((1, ADA_CHUNKS, d), seq_of),
            pl.BlockSpec((1, d), const2),
            pl.BlockSpec((1, d), const2),
        ] + in_specs
        operands = [x.reshape(n_tok, d), ya, ya, wt_tok, ada_prev, ln2_g, ln2_b] + operands
    else:
        in_specs = [pl.BlockSpec((1, ts, d), tile3)] + in_specs
        operands = [x] + operands
    kern = functools.partial(_mixer_kernel, ts=ts, alpha=alpha, nt=nt, fused=fused)
    pool_rows = POOL_SLACK + POOL_HALO + ts
    return pl.pallas_call(
        kern,
        grid=(n_tiles,),
        in_specs=in_specs,
        out_specs=[
            pl.BlockSpec((1, ts, d), tile3),
            pl.BlockSpec((ts, PACK_W), tile2),
            pl.BlockSpec((TOP_K, ts), by_token),
            pl.BlockSpec((TOP_K, ts), by_token),
            pl.BlockSpec((TOP_K, ts), by_token),
            pl.BlockSpec((N_EXPERTS, LANES), const2),
        ],
        out_shape=[
            jax.ShapeDtypeStruct((bsz, s, d), F32),
            jax.ShapeDtypeStruct((n_tok, PACK_W), jnp.int32),
            jax.ShapeDtypeStruct((TOP_K, n_tok), jnp.int32),
            jax.ShapeDtypeStruct((TOP_K, n_tok), F32),
            jax.ShapeDtypeStruct((TOP_K, n_tok), jnp.int32),
            jax.ShapeDtypeStruct((N_EXPERTS, LANES), F32),
        ],
        scratch_shapes=[
            pltpu.VMEM((CONV_HALO + ts, CONV_W), F32),
            pltpu.VMEM((pool_rows, POOL_W), F32),
            pltpu.VMEM((pool_rows, POOL_W), F32),
            pltpu.VMEM((pool_rows, POOL_W), F32),
            pltpu.VMEM((ts, d), BF16),
            pltpu.VMEM((N_EXPERTS, LANES), F32),
        ],
        compiler_params=pltpu.CompilerParams(
            dimension_semantics=("arbitrary",), vmem_limit_bytes=VMEM_LIMIT),
        name="mixer_router",
    )(*operands)


def _sc_worker_chunks(n_chunks):
    per_worker = n_chunks // SC_WORKERS
    worker = lax.axis_index("s") * 2 + lax.axis_index("c")
    return worker * per_worker, per_worker


def _dispatch_call(h2p, dest, n_rows):
    n_tok, width = h2p.shape
    n_chunks = n_tok // SC_ROWS
    assert n_chunks % SC_WORKERS == 0

    def body(h_hbm, dest_hbm, xs_hbm, rows_v, idx_v):
        first, per_worker = _sc_worker_chunks(n_chunks)

        @pl.loop(0, per_worker)
        def _(c):
            chunk = first + c
            pltpu.sync_copy(h_hbm.at[pl.ds(chunk * SC_ROWS, SC_ROWS)], rows_v)
            for k in range(TOP_K):
                pltpu.sync_copy(dest_hbm.at[k, chunk], idx_v.at[k])
                pltpu.sync_copy(rows_v, xs_hbm.at[idx_v.at[k]])

    return pl.kernel(
        body,
        out_type=jax.ShapeDtypeStruct((n_rows, width), jnp.int32),
        mesh=plsc.VectorSubcoreMesh(core_axis_name="c", subcore_axis_name="s"),
        scratch_types=[pltpu.VMEM((SC_ROWS, width), jnp.int32), pltpu.VMEM((TOP_K, SC_ROWS), jnp.int32)],
        name="dispatch",
    )(h2p, dest)


def _collect_call(ys, dest, n_tok):
    width = ys.shape[1]
    n_chunks = n_tok // SC_ROWS
    assert n_chunks % SC_WORKERS == 0

    def body(ys_hbm, dest_hbm, ya_hbm, rows_v, idx_v):
        first, per_worker = _sc_worker_chunks(n_chunks)

        @pl.loop(0, per_worker)
        def _(c):
            chunk = first + c
            for k in range(TOP_K):
                pltpu.sync_copy(dest_hbm.at[k, chunk], idx_v.at[k])
                pltpu.sync_copy(ys_hbm.at[idx_v.at[k]], rows_v)
                pltpu.sync_copy(rows_v, ya_hbm.at[pl.ds(k * n_tok + chunk * SC_ROWS, SC_ROWS)])

    return pl.kernel(
        body,
        out_type=jax.ShapeDtypeStruct((TOP_K * n_tok, width), jnp.int32),
        mesh=plsc.VectorSubcoreMesh(core_axis_name="c", subcore_axis_name="s"),
        scratch_types=[pltpu.VMEM((SC_ROWS, width), jnp.int32), pltpu.VMEM((TOP_K, SC_ROWS), jnp.int32)],
        name="collect",
    )(ys, dest)


def _ffn_kernel(b0_ref, nb_ref, rows_ref, tot_ref, xs_ref, wg_ref, wu_ref, wd_ref, ys_ref,
                xbuf, obuf, wbuf_g, wbuf_u, wbuf_d, wg_scr, wu_scr, wd_scr, gsem, osem, wsem,
                *, layer, n_blocks):
    e = pl.program_id(0)
    n_exp = pl.num_programs(0)
    b0 = b0_ref[e]
    nb = nb_ref[e]
    n_used = tot_ref[0]

    def in_copy(g):
        slot = g % GATHER_DEPTH
        return pltpu.make_async_copy(_rows(xs_ref, g * ROW_BLOCK, ROW_BLOCK), xbuf.at[slot], gsem.at[slot])

    def out_copy(g, slot):
        return pltpu.make_async_copy(obuf.at[slot], _rows(ys_ref, g * ROW_BLOCK, ROW_BLOCK), osem.at[slot])

    def weight_copies(expert, slot):
        return [pltpu.make_async_copy(src.at[layer, expert], dst.at[slot], wsem.at[slot])
                for src, dst in ((wg_ref, wbuf_g), (wu_ref, wbuf_u), (wd_ref, wbuf_d))]

    @pl.when(e == 0)
    def _():
        for first in range(WEIGHT_DEPTH - 1):
            for cp in weight_copies(first, first):
                cp.start(priority=WEIGHT_DMA_PRIORITY)
        for ahead in range(GATHER_DEPTH - 1):
            @pl.when(ahead < n_used)
            def _():
                in_copy(ahead).start()

    @pl.when(e + (WEIGHT_DEPTH - 1) < n_exp)
    def _():
        for cp in weight_copies(e + (WEIGHT_DEPTH - 1), (e + (WEIGHT_DEPTH - 1)) % WEIGHT_DEPTH):
            cp.start(priority=WEIGHT_DMA_PRIORITY)

    wslot = e % WEIGHT_DEPTH
    for cp in weight_copies(e, wslot):
        cp.wait()

    @pl.when(nb > 0)
    def _():
        wg_scr[...] = wbuf_g[wslot].astype(BF16)
        wu_scr[...] = wbuf_u[wslot].astype(BF16)
        wd_scr[...] = wbuf_d[wslot].astype(BF16)

    def block(k, carry):
        g = b0 + k
        islot = g % GATHER_DEPTH
        oslot = g % 2

        @pl.when(g + (GATHER_DEPTH - 1) < n_used)
        def _():
            in_copy(g + (GATHER_DEPTH - 1)).start()

        in_copy(g).wait()

        @pl.when(g >= 2)
        def _():
            out_copy(g - 2, oslot).wait()

        def chain(half):
            rows = pl.ds(half * SUB_BLOCK, SUB_BLOCK)
            xb = jnp.concatenate(_unpack_bf16_pairs(xbuf[islot, rows, :]), axis=1)
            gate = jnp.dot(xb, wg_scr[...], preferred_element_type=F32)
            up = jnp.dot(xb, wu_scr[...], preferred_element_type=F32)
            act = (gate * jax.nn.sigmoid(gate) * up).astype(BF16)
            y = jnp.dot(act, wd_scr[...], preferred_element_type=F32)
            obuf[oslot, rows, :] = _pack_bf16_pairs(y.astype(BF16).astype(F32))

        both = rows_ref[e] - k * ROW_BLOCK > SUB_BLOCK

        @pl.when(both)
        def _():
            chain(0)
            chain(1)

        @pl.when(jnp.logical_not(both))
        def _():
            chain(0)
            obuf[oslot, pl.ds(SUB_BLOCK, SUB_BLOCK), :] = jnp.zeros((SUB_BLOCK, PACK_W), jnp.int32)

        out_copy(g, oslot).start()
        return carry

    lax.fori_loop(0, nb, block, 0)

    @pl.when(e == n_exp - 1)
    def _():
        @pl.when(n_used >= 2)
        def _():
            out_copy(n_used - 2, n_used % 2).wait()

        out_copy(n_used - 1, (n_used - 1) % 2).wait()
        obuf[0] = jnp.zeros((ROW_BLOCK, PACK_W), jnp.int32)

        def fill(g, carry):
            out_copy(g, 0).start()
            return carry

        def drain(g, carry):
            out_copy(g, 0).wait()
            return carry

        lax.fori_loop(n_used, n_blocks, fill, 0)
        lax.fori_loop(n_used, n_blocks, drain, 0)


def _ffn_call(layer, blk_start, seg_blocks, seg_rows, n_used, n_blocks, xs, w_gate, w_up, w_down):
    d, f = w_gate.shape[-2:]
    any_spec = pl.BlockSpec(memory_space=pl.ANY)
    return pl.pallas_call(
        functools.partial(_ffn_kernel, layer=layer, n_blocks=n_blocks),
        grid_spec=pltpu.PrefetchScalarGridSpec(
            num_scalar_prefetch=4,
            grid=(N_EXPERTS,),
            in_specs=[any_spec, any_spec, any_spec, any_spec],
            out_specs=any_spec,
            scratch_shapes=[
                pltpu.VMEM((GATHER_DEPTH, ROW_BLOCK, PACK_W), jnp.int32),
                pltpu.VMEM((2, ROW_BLOCK, PACK_W), jnp.int32),
                pltpu.VMEM((WEIGHT_DEPTH, d, f), F32),
                pltpu.VMEM((WEIGHT_DEPTH, d, f), F32),
                pltpu.VMEM((WEIGHT_DEPTH, f, d), F32),
                pltpu.VMEM((d, f), BF16),
                pltpu.VMEM((d, f), BF16),
                pltpu.VMEM((f, d), BF16),
                pltpu.SemaphoreType.DMA((GATHER_DEPTH,)),
                pltpu.SemaphoreType.DMA((2,)),
                pltpu.SemaphoreType.DMA((WEIGHT_DEPTH,)),
            ],
        ),
        out_shape=jax.ShapeDtypeStruct((n_blocks * ROW_BLOCK, PACK_W), jnp.int32),
        compiler_params=pltpu.CompilerParams(
            dimension_semantics=("arbitrary",), vmem_limit_bytes=VMEM_LIMIT),
        name="expert_ffn",
    )(blk_start, seg_blocks, seg_rows, n_used, xs, w_gate, w_up, w_down)


def _combine_kernel(ya0_ref, ya1_ref, wt_ref, x1_ref, ada_ref, g_ref, b_ref, o_ref, *, alpha):
    o_ref[...] = _combine_tile(ya0_ref, ya1_ref, wt_ref, x1_ref[...], ada_ref[0][5:6], g_ref, b_ref, alpha=alpha)


def _combine_call(ya, wt_tok, x1_flat, ada_l, ln_g, ln_b, alpha, seq):
    n_tok, d = x1_flat.shape
    ts = SEQ_TILE
    n_tiles = n_tok // ts
    per_seq = seq // ts
    return pl.pallas_call(
        functools.partial(_combine_kernel, alpha=alpha),
        grid=(n_tiles,),
        in_specs=[
            pl.BlockSpec((ts, PACK_W), lambda i: (i, 0)),
            pl.BlockSpec((ts, PACK_W), lambda i: (n_tiles + i, 0)),
            pl.BlockSpec((ts, TOP_K), lambda i: (i, 0)),
            pl.BlockSpec((ts, d), lambda i: (i, 0)),
            pl.BlockSpec((1, ADA_CHUNKS, d), lambda i: (i // per_seq, 0, 0)),
            pl.BlockSpec((1, d), lambda i: (0, 0)),
            pl.BlockSpec((1, d), lambda i: (0, 0)),
        ],
        out_specs=pl.BlockSpec((ts, d), lambda i: (i, 0)),
        out_shape=jax.ShapeDtypeStruct((n_tok, d), F32),
        compiler_params=pltpu.CompilerParams(
            dimension_semantics=("arbitrary",), vmem_limit_bytes=VMEM_LIMIT),
        name="combine_ln",
    )(ya, ya, wt_tok, x1_flat, ada_l, ln_g, ln_b)


def kernel(x, c, w_ada, b_ada, w_in, conv_w, conv_b, pool_w, pool_scale, sgu_ln_g, sgu_ln_b,
           sgu_w, sgu_b, w_out, ln1_g, ln1_b, w_router, b_router, w_gate, w_up, w_down,
           ln2_g, ln2_b):
    bsz, seq, d = x.shape
    depth = w_ada.shape[0]
    n_tok = bsz * seq
    alpha = (2 * depth) ** 0.25
    assert d == D_MODEL and seq % SEQ_TILE == 0 and SEQ_TILE % CHUNK == 0

    ada = _ada_call(c, w_ada, b_ada).reshape(depth, bsz, ADA_CHUNKS, d)

    w_in_b = w_in.astype(BF16)
    w_out_b = w_out.astype(BF16)
    eye_g = jnp.eye(POOL_W // HEAD_DIM, dtype=F32)
    poolw_bd = jnp.einsum('lgcd,gh->lgchd', pool_w, eye_g).reshape(depth, POOL_W, POOL_W).astype(BF16)
    n_pairs = SGU_W // LANES
    sguw_pair = sgu_w.reshape(depth, n_pairs, 2, CHUNK, CHUNK).transpose(0, 1, 3, 2, 4).reshape(
        depth, n_pairs, CHUNK, 2 * CHUNK)
    sgub_pair = jnp.repeat(sgu_b.transpose(0, 2, 1), HEAD_DIM, axis=-1).reshape(
        depth, CHUNK, n_pairs, LANES).transpose(0, 2, 1, 3)
    wr_pad = jnp.pad(w_router, ((0, 0), (0, LANES - N_EXPERTS))).astype(BF16)
    br_pad = jnp.pad(b_router, (0, LANES - N_EXPERTS)).reshape(1, LANES)

    n_steps = (n_tok * TOP_K) // ROW_BLOCK + N_EXPERTS
    experts = jnp.arange(N_EXPERTS, dtype=jnp.int32)
    tri = jnp.triu(jnp.ones((SEQ_TILE, SEQ_TILE), BF16), k=1)

    prev = None
    for l in range(depth):
        x1, h2p, e_idx, wts, rank, counts = _mixer_call(
            x, ada[l], w_in_b[l], w_out_b[l], conv_w[l], conv_b[l].reshape(1, -1), poolw_bd[l],
            pool_scale[l].reshape(1, -1), sgu_ln_g[l].reshape(1, -1), sgu_ln_b[l].reshape(1, -1),
            sguw_pair[l], sgub_pair[l], ln1_g[l].reshape(1, -1), ln1_b[l].reshape(1, -1),
            wr_pad, br_pad, tri, alpha, prev)
        sizes = counts[:, 0].astype(jnp.int32)
        seg_blocks = (sizes + ROW_BLOCK - 1) // ROW_BLOCK
        blk_end = jnp.cumsum(seg_blocks)
        blk_start = (blk_end - seg_blocks).astype(jnp.int32)
        n_used = blk_end[-1:].astype(jnp.int32)
        seg_row0 = jnp.sum(jnp.where(e_idx[:, :, None] == experts[None, None, :],
                                     (blk_start * ROW_BLOCK)[None, None, :], 0), axis=2)
        dest = (seg_row0 + rank).astype(jnp.int32).reshape(TOP_K, n_tok // SC_ROWS, SC_ROWS)

        xs = _dispatch_call(h2p, dest, n_steps * ROW_BLOCK)
        ys = _ffn_call(l, blk_start, seg_blocks.astype(jnp.int32), sizes, n_used, n_steps,
                       xs, w_gate, w_up, w_down)
        ya = _collect_call(ys, dest, n_tok)
        wt_tok = wts.T
        prev = (ya, wt_tok, ada[l], ln2_g[l].reshape(1, -1), ln2_b[l].reshape(1, -1))
        x = x1
    ya, wt_tok, ada_last, ln2_g_last, ln2_b_last = prev
    return _combine_call(ya, wt_tok, x.reshape(n_tok, d), ada_last, ln2_g_last, ln2_b_last,
                         alpha, seq).reshape(bsz, seq, d)
```

```python
import functools

import jax
import jax.numpy as jnp
from jax import lax
from jax.experimental import pallas as pl
from jax.experimental.pallas import tpu as pltpu
from jax.experimental.pallas import tpu_sc as plsc

D_MODEL = 1024
HEAD_DIM = D_MODEL // 16
CONV_W = 6 * HEAD_DIM
POOL_W = 4 * HEAD_DIM
SGU_W = 6 * HEAD_DIM
D_IN = 3 * CONV_W + POOL_W + 2 * SGU_W
CONV_K = 3
POOL_WINDOWS = (2, 4, 8, 16)
CHUNK = 128
N_EXPERTS = 32
N_GROUPS = 4
EXPERTS_PER_GROUP = N_EXPERTS // N_GROUPS
TOP_K = 2
D_FF = D_MODEL // 2
ADA_CHUNKS = 6
LN_EPS = 1e-5

OFF_GB = 0
OFF_GC = CONV_W
OFF_P = 3 * CONV_W
OFF_U = OFF_P + POOL_W
OFF_V = OFF_U + SGU_W
YOFF_POOL = CONV_W
YOFF_SGU = CONV_W + POOL_W

LANES = 128
SUBLANES = 8
CONV_HALO = 8
POOL_HALO = 16
POOL_SLACK = 8
SEQ_TILE = 512
SUB_BLOCK = 256
ROW_BLOCK = 2 * SUB_BLOCK
ADA_TILE = 1536
GATHER_DEPTH = 3
WEIGHT_DEPTH = 3
WEIGHT_DMA_PRIORITY = 1
VMEM_LIMIT = 60 * 1024 * 1024

PACK_W = D_MODEL // 2
SC_ROWS = 128
SC_WORKERS = 32

F32 = jnp.float32
BF16 = jnp.bfloat16


def _layer_norm(r, g, b):
    mu = jnp.mean(r, axis=-1, keepdims=True)
    d = r - mu
    var = jnp.mean(d * d, axis=-1, keepdims=True)
    return d * lax.rsqrt(var + LN_EPS) * g + b


def _rows(ref, start, size):
    return ref.at[pl.ds(pl.multiple_of(start, SUBLANES), size), :]


def _pack_bf16_pairs(v):
    bits = pltpu.bitcast(v, jnp.int32)
    return bits[:, 0:PACK_W] | lax.shift_right_logical(bits[:, PACK_W:2 * PACK_W], 16)


def _unpack_pairs_f32(words):
    return jnp.concatenate([pltpu.bitcast(words & jnp.int32(-65536), F32),
                            pltpu.bitcast(lax.shift_left(words, 16), F32)], axis=1)


def _unpack_bf16_pairs(words):
    first = pltpu.bitcast(words & jnp.int32(-65536), F32).astype(BF16)
    second = pltpu.bitcast(lax.shift_left(words, 16), F32).astype(BF16)
    return first, second


def _ada_kernel(c_ref, w_ref, b_ref, o_ref):
    c = c_ref[...]
    c_act = (c * jax.nn.sigmoid(c)).astype(BF16)
    col = pl.program_id(1) * ADA_TILE + lax.broadcasted_iota(jnp.int32, (1, ADA_TILE), 1)
    chunk = col // D_MODEL
    one = jnp.where((chunk == 1) | (chunk == 4), 1.0, 0.0)
    o_ref[0] = jnp.dot(c_act, w_ref[0].astype(BF16), preferred_element_type=F32) + (b_ref[0] + one)


def _ada_call(c, w_ada, b_ada):
    depth, d, n = w_ada.shape
    bsz = c.shape[0]
    return pl.pallas_call(
        _ada_kernel,
        grid=(depth, n // ADA_TILE),
        in_specs=[
            pl.BlockSpec((bsz, d), lambda l, j: (0, 0)),
            pl.BlockSpec((1, d, ADA_TILE), lambda l, j: (l, 0, j)),
            pl.BlockSpec((1, 1, ADA_TILE), lambda l, j: (l, 0, j)),
        ],
        out_specs=pl.BlockSpec((1, bsz, ADA_TILE), lambda l, j: (l, 0, j)),
        out_shape=jax.ShapeDtypeStruct((depth, bsz, n), F32),
        compiler_params=pltpu.CompilerParams(
            dimension_semantics=("arbitrary", "arbitrary"), vmem_limit_bytes=VMEM_LIMIT),
        name="ada",
    )(c, w_ada, b_ada.reshape(depth, 1, n))


def _top2_of_group(pg, sub_iota):
    big = float(EXPERTS_PER_GROUP)
    m1 = jnp.max(pg, axis=0, keepdims=True)
    i1 = jnp.min(jnp.where(pg == m1, sub_iota, big), axis=0, keepdims=True)
    rest = jnp.where(sub_iota == i1, -1.0, pg)
    m2 = jnp.max(rest, axis=0, keepdims=True)
    i2 = jnp.min(jnp.where(rest == m2, sub_iota, big), axis=0, keepdims=True)
    return m1, i1, m2, i2


def _combine_tile(ya0_ref, ya1_ref, wt_ref, x1, gate, g_ref, b_ref, *, alpha):
    wt = wt_ref[...]
    y = wt[:, 0:1] * _unpack_pairs_f32(ya0_ref[...]) + wt[:, 1:2] * _unpack_pairs_f32(ya1_ref[...])
    return _layer_norm(alpha * x1 + gate * y, g_ref[...], b_ref[...])


def _mixer_kernel(*refs, ts, alpha, nt, fused):
    if fused:
        xprev_ref, ya0_ref, ya1_ref, wtp_ref, adap_ref, ln2g_ref, ln2b_ref = refs[:7]
        refs = refs[7:]
    else:
        x_ref, refs = refs[0], refs[1:]
    (ada_ref, win_ref, wout_ref, convw_ref, convb_ref, poolw_ref, pscale_ref, lng_ref, lnb_ref,
     sguw_ref, sgub_ref, ln1g_ref, ln1b_ref, wr_ref, br_ref, tri_ref,
     x1_ref, h2p_ref, e_ref, wt_ref, rank_ref, cnt_ref,
     gx_scr, p_scr, q_scr, r_scr, y_scr, cnt_scr) = refs

    i = pl.program_id(0)
    j = i % nt

    @pl.when(i == 0)
    def _():
        gx_scr[0:CONV_HALO, :] = jnp.zeros((CONV_HALO, CONV_W), F32)
        for scr in (p_scr, q_scr, r_scr):
            scr[0:POOL_SLACK + POOL_HALO, :] = jnp.zeros((POOL_SLACK + POOL_HALO, POOL_W), F32)
        cnt_scr[...] = jnp.zeros_like(cnt_scr)

    if fused:
        x = _combine_tile(ya0_ref, ya1_ref, wtp_ref, xprev_ref[...], adap_ref[0][5:6], ln2g_ref, ln2b_ref,
                          alpha=alpha)
    else:
        x = x_ref[0]
    _mixer_tile(j, x, ada_ref, win_ref, wout_ref, convw_ref, convb_ref, poolw_ref, pscale_ref,
                lng_ref, lnb_ref, sguw_ref, sgub_ref, ln1g_ref, ln1b_ref, wr_ref, br_ref, tri_ref,
                x1_ref, h2p_ref, e_ref, wt_ref, rank_ref, cnt_ref, gx_scr, p_scr, q_scr, r_scr, y_scr, cnt_scr,
                ts=ts, alpha=alpha)


def _mixer_tile(j, x, ada_ref, win_ref, wout_ref, convw_ref, convb_ref, poolw_ref, pscale_ref,
                lng_ref, lnb_ref, sguw_ref, sgub_ref, ln1g_ref, ln1b_ref, wr_ref, br_ref, tri_ref,
                x1_ref, h2p_ref, e_ref, wt_ref, rank_ref, cnt_ref, gx_scr, p_scr, q_scr, r_scr, y_scr, cnt_scr,
                *, ts, alpha):
    first_of_seq = j == 0
    ada = ada_ref[0]
    sh1, mul1, g1 = ada[0:1], ada[1:2], ada[2:3]
    sh2, mul2 = ada[3:4], ada[4:5]
    hb = (x * mul1 + sh1).astype(BF16)

    z_conv = jnp.dot(hb, win_ref[:, OFF_GB:OFF_P], preferred_element_type=F32)
    z_rest = jnp.dot(hb, win_ref[:, OFF_P:D_IN], preferred_element_type=F32)

    def proj(lo, hi):
        if hi <= OFF_P:
            return z_conv[:, lo:hi]
        return z_rest[:, lo - OFF_P:hi - OFF_P]

    gcxc = proj(OFF_GC, OFF_P)
    g = gcxc[:, :CONV_W] * gcxc[:, CONV_W:]
    gx_scr[CONV_HALO:CONV_HALO + ts, :] = g
    cw = convw_ref[...]
    gx_scr[0:CONV_HALO, :] = jnp.where(first_of_seq, 0.0, gx_scr[0:CONV_HALO, :])
    conv = (cw[0:1] * gx_scr[CONV_HALO - 2:CONV_HALO - 2 + ts, :]
            + cw[1:2] * gx_scr[CONV_HALO - 1:CONV_HALO - 1 + ts, :]
            + cw[2:3] * g + convb_ref[...])
    gx_scr[0:CONV_HALO, :] = g[ts - CONV_HALO:ts, :]
    y_scr[:, 0:CONV_W] = (proj(OFF_GB, OFF_GC) * conv).astype(BF16)

    p = proj(OFF_P, OFF_U)
    top = POOL_SLACK
    n_ext = POOL_HALO + ts
    p_scr[top + POOL_HALO:top + n_ext, :] = p
    p_scr[top:top + POOL_HALO, :] = jnp.where(first_of_seq, 0.0, p_scr[top:top + POOL_HALO, :])
    w2 = p_scr[top:top + n_ext, :] + p_scr[top - 1:top - 1 + n_ext, :]
    q_scr[top:top + n_ext, :] = w2
    w4 = w2 + q_scr[top - 2:top - 2 + n_ext, :]
    r_scr[top:top + n_ext, :] = w4
    w8 = w4 + r_scr[top - 4:top - 4 + n_ext, :]
    q_scr[top:top + n_ext, :] = w8
    w16 = w8 + q_scr[top - 8:top - 8 + n_ext, :]
    s2, s4 = w2[POOL_HALO:, 0:LANES], w4[POOL_HALO:, 0:LANES]
    s8, s16 = w8[POOL_HALO:, LANES:2 * LANES], w16[POOL_HALO:, LANES:2 * LANES]
    p_scr[top:top + POOL_HALO, :] = p[ts - POOL_HALO:ts, :]
    lane = lax.broadcasted_iota(jnp.int32, (ts, LANES), 1)
    tpos = (lax.broadcasted_iota(jnp.int32, (ts, LANES), 0) + (j * ts + 1)).astype(F32)
    lo_half = lane < HEAD_DIM
    cnt_a = jnp.minimum(tpos, jnp.where(lo_half, float(POOL_WINDOWS[0]), float(POOL_WINDOWS[1])))
    cnt_b = jnp.minimum(tpos, jnp.where(lo_half, float(POOL_WINDOWS[2]), float(POOL_WINDOWS[3])))
    pooled = jnp.concatenate([jnp.where(lo_half, s2, s4) / cnt_a,
                              jnp.where(lo_half, s8, s16) / cnt_b], axis=1) - p
    mixed = jnp.dot(pooled.astype(BF16), poolw_ref[...], preferred_element_type=F32)
    y_scr[:, YOFF_POOL:YOFF_POOL + POOL_W] = (mixed * pscale_ref[...]).astype(BF16)

    v = proj(OFF_V, D_IN)
    vnb = _layer_norm(v, lng_ref[...], lnb_ref[...]).astype(BF16)
    u = proj(OFF_U, OFF_V)
    row_c = lax.broadcasted_iota(jnp.int32, (CHUNK, 2 * CHUNK), 0)
    col_c = lax.broadcasted_iota(jnp.int32, (CHUNK, 2 * CHUNK), 1)
    causal = (col_c & (CHUNK - 1)) <= row_c
    lo_lanes = lax.broadcasted_iota(jnp.int32, (CHUNK, LANES), 1) < HEAD_DIM
    zero_b = jnp.zeros((CHUNK, LANES), BF16)
    for hp in range(SGU_W // LANES):
        wl = jnp.where(causal, sguw_ref[hp], 0.0).astype(BF16)
        bias = sgub_ref[hp]
        cols = slice(hp * LANES, (hp + 1) * LANES)
        for ci in range(0, ts // CHUNK, 2):
            pair = []
            for cj in (ci, ci + 1):
                vc = vnb[cj * CHUNK:(cj + 1) * CHUNK, cols]
                pair.append(jnp.concatenate([jnp.where(lo_lanes, vc, zero_b),
                                             jnp.where(lo_lanes, zero_b, vc)], axis=0))
            mixed2 = jnp.dot(wl, jnp.concatenate(pair, axis=1), preferred_element_type=F32)
            for k, cj in enumerate((ci, ci + 1)):
                rs = slice(cj * CHUNK, (cj + 1) * CHUNK)
                y_scr[rs, YOFF_SGU + hp * LANES:YOFF_SGU + (hp + 1) * LANES] = (
                    u[rs, cols] * (mixed2[:, k * LANES:(k + 1) * LANES] + bias)).astype(BF16)

    yo = jnp.dot(y_scr[...], wout_ref[...], preferred_element_type=F32)
    x1 = _layer_norm(alpha * x + g1 * yo, ln1g_ref[...], ln1b_ref[...])
    x1_ref[0] = x1
    h2b = (x1 * mul2 + sh2).astype(BF16)
    h2p_ref[...] = _pack_bf16_pairs(h2b.astype(F32))

    logits = jnp.dot(h2b, wr_ref[...], preferred_element_type=F32) + br_ref[...]
    lt = logits.T[0:N_EXPERTS, :]
    ex = jnp.exp(lt - jnp.max(lt, axis=0, keepdims=True))
    probs = ex / jnp.sum(ex, axis=0, keepdims=True)
    sub_iota = lax.broadcasted_iota(jnp.int32, (EXPERTS_PER_GROUP, ts), 0).astype(F32)
    best = None
    for gi in range(N_GROUPS):
        m1, i1, m2, i2 = _top2_of_group(
            probs[gi * EXPERTS_PER_GROUP:(gi + 1) * EXPERTS_PER_GROUP, :], sub_iota)
        score = m1 + m2
        cand = (score, m1, m2, i1 + float(gi * EXPERTS_PER_GROUP), i2 + float(gi * EXPERTS_PER_GROUP))
        if best is None:
            best = cand
        else:
            better = score > best[0]
            best = tuple(jnp.where(better, c, o) for c, o in zip(cand, best))
    _, p1, p2, e1, e2 = best
    den = p1 + p2
    w1 = p1 / den
    w2 = p2 / den

    wt_ref[...] = jnp.concatenate([w1, w2], axis=0)
    e_ref[...] = jnp.concatenate([e1, e2], axis=0).astype(jnp.int32)

    ex_iota = lax.broadcasted_iota(jnp.int32, (N_EXPERTS, ts), 0).astype(F32)
    oh1 = ex_iota == e1
    oh2 = ex_iota == e2
    either = jnp.where(oh1 | oh2, 1.0, 0.0)
    seen = jnp.dot(either.astype(BF16), tri_ref[...], preferred_element_type=F32) + cnt_scr[:, 0:1]
    r1 = jnp.sum(jnp.where(oh1, seen, 0.0), axis=0, keepdims=True)
    r2 = jnp.sum(jnp.where(oh2, seen, 0.0), axis=0, keepdims=True)
    rank_ref[...] = jnp.concatenate([r1, r2], axis=0).astype(jnp.int32)
    cnt_new = cnt_scr[...] + jnp.sum(either, axis=1, keepdims=True)
    cnt_scr[...] = cnt_new
    cnt_ref[...] = cnt_new


def _mixer_call(x, ada_l, w_in, w_out, conv_w, conv_b, poolw_bd, pool_scale, ln_g, ln_b,
                sguw_pair, sgub_pair, ln1_g, ln1_b, wr_pad, br_pad, tri, alpha, prev=None):
    bsz, s, d = x.shape
    ts = SEQ_TILE
    nt = s // ts
    n_tiles = bsz * nt
    n_tok = bsz * s
    fused = prev is not None
    const2 = lambda i: (0, 0)
    const3 = lambda i: (0, 0, 0)
    seq_of = lambda i: (i // nt, 0, 0)
    tile3 = lambda i: (i // nt, i % nt, 0)
    tile2 = lambda i: (i, 0)
    by_token = lambda i: (0, i)
    once = pl.Buffered(1)
    in_specs = [
        pl.BlockSpec((1, ADA_CHUNKS, d), seq_of),
        pl.BlockSpec((d, D_IN), const2, pipeline_mode=once),
        pl.BlockSpec((d, d), const2, pipeline_mode=once),
        pl.BlockSpec((CONV_K, CONV_W), const2),
        pl.BlockSpec((1, CONV_W), const2),
        pl.BlockSpec((POOL_W, POOL_W), const2),
        pl.BlockSpec((1, POOL_W), const2),
        pl.BlockSpec((1, SGU_W), const2),
        pl.BlockSpec((1, SGU_W), const2),
        pl.BlockSpec((SGU_W // LANES, CHUNK, 2 * CHUNK), const3),
        pl.BlockSpec((SGU_W // LANES, CHUNK, LANES), const3),
        pl.BlockSpec((1, d), const2),
        pl.BlockSpec((1, d), const2),
        pl.BlockSpec((d, LANES), const2),
        pl.BlockSpec((1, LANES), const2),
        pl.BlockSpec((ts, ts), const2, pipeline_mode=once),
    ]
    operands = [ada_l, w_in, w_out, conv_w, conv_b, poolw_bd, pool_scale, ln_g, ln_b,
                sguw_pair, sgub_pair, ln1_g, ln1_b, wr_pad, br_pad, tri]
    if fused:
        ya, wt_tok, ada_prev, ln2_g, ln2_b = prev
        in_specs = [
            pl.BlockSpec((ts, d), tile2),
            pl.BlockSpec((ts, PACK_W), tile2),
            pl.BlockSpec((ts, PACK_W), lambda i: (n_tiles + i, 0)),
            pl.BlockSpec((ts, TOP_K), tile2),
            pl.BlockSpec((1, ADA_CHUNKS, d), seq_of),
            pl.BlockSpec((1, d), const2),
            pl.BlockSpec((1, d), const2),
        ] + in_specs
        operands = [x.reshape(n_tok, d), ya, ya, wt_tok, ada_prev, ln2_g, ln2_b] + operands
    else:
        in_specs = [pl.BlockSpec((1, ts, d), tile3)] + in_specs
        operands = [x] + operands
    kern = functools.partial(_mixer_kernel, ts=ts, alpha=alpha, nt=nt, fused=fused)
    pool_rows = POOL_SLACK + POOL_HALO + ts
    return pl.pallas_call(
        kern,
        grid=(n_tiles,),
        in_specs=in_specs,
        out_specs=[
            pl.BlockSpec((1, ts, d), tile3),
            pl.BlockSpec((ts, PACK_W), tile2),
            pl.BlockSpec((TOP_K, ts), by_token),
            pl.BlockSpec((TOP_K, ts), by_token),
            pl.BlockSpec((TOP_K, ts), by_token),
            pl.BlockSpec((N_EXPERTS, LANES), const2),
        ],
        out_shape=[
            jax.ShapeDtypeStruct((bsz, s, d), F32),
            jax.ShapeDtypeStruct((n_tok, PACK_W), jnp.int32),
            jax.ShapeDtypeStruct((TOP_K, n_tok), jnp.int32),
            jax.ShapeDtypeStruct((TOP_K, n_tok), F32),
            jax.ShapeDtypeStruct((TOP_K, n_tok), jnp.int32),
            jax.ShapeDtypeStruct((N_EXPERTS, LANES), F32),
        ],
        scratch_shapes=[
            pltpu.VMEM((CONV_HALO + ts, CONV_W), F32),
            pltpu.VMEM((pool_rows, POOL_W), F32),
            pltpu.VMEM((pool_rows, POOL_W), F32),
            pltpu.VMEM((pool_rows, POOL_W), F32),
            pltpu.VMEM((ts, d), BF16),
            pltpu.VMEM((N_EXPERTS, LANES), F32),
        ],
        compiler_params=pltpu.CompilerParams(
            dimension_semantics=("arbitrary",), vmem_limit_bytes=VMEM_LIMIT),
        name="mixer_router",
    )(*operands)


def _sc_worker_chunks(n_chunks):
    per_worker = n_chunks // SC_WORKERS
    worker = lax.axis_index("s") * 2 + lax.axis_index("c")
    return worker * per_worker, per_worker


def _dispatch_call(h2p, dest, n_rows):
    n_tok, width = h2p.shape
    n_chunks = n_tok // SC_ROWS
    assert n_chunks % SC_WORKERS == 0

    def body(h_hbm, dest_hbm, xs_hbm, rows_v, idx_v):
        first, per_worker = _sc_worker_chunks(n_chunks)

        @pl.loop(0, per_worker)
        def _(c):
            chunk = first + c
            pltpu.sync_copy(h_hbm.at[pl.ds(chunk * SC_ROWS, SC_ROWS)], rows_v)
            for k in range(TOP_K):
                pltpu.sync_copy(dest_hbm.at[k, chunk], idx_v.at[k])
                pltpu.sync_copy(rows_v, xs_hbm.at[idx_v.at[k]])

    return pl.kernel(
        body,
        out_type=jax.ShapeDtypeStruct((n_rows, width), jnp.int32),
        mesh=plsc.VectorSubcoreMesh(core_axis_name="c", subcore_axis_name="s"),
        scratch_types=[pltpu.VMEM((SC_ROWS, width), jnp.int32), pltpu.VMEM((TOP_K, SC_ROWS), jnp.int32)],
        name="dispatch",
    )(h2p, dest)


def _collect_call(ys, dest, n_tok):
    width = ys.shape[1]
    n_chunks = n_tok // SC_ROWS
    assert n_chunks % SC_WORKERS == 0

    def body(ys_hbm, dest_hbm, ya_hbm, rows_v, idx_v):
        first, per_worker = _sc_worker_chunks(n_chunks)

        @pl.loop(0, per_worker)
        def _(c):
            chunk = first + c
            for k in range(TOP_K):
                pltpu.sync_copy(dest_hbm.at[k, chunk], idx_v.at[k])
                pltpu.sync_copy(ys_hbm.at[idx_v.at[k]], rows_v)
                pltpu.sync_copy(rows_v, ya_hbm.at[pl.ds(k * n_tok + chunk * SC_ROWS, SC_ROWS)])

    return pl.kernel(
        body,
        out_type=jax.ShapeDtypeStruct((TOP_K * n_tok, width), jnp.int32),
        mesh=plsc.VectorSubcoreMesh(core_axis_name="c", subcore_axis_name="s"),
        scratch_types=[pltpu.VMEM((SC_ROWS, width), jnp.int32), pltpu.VMEM((TOP_K, SC_ROWS), jnp.int32)],
        name="collect",
    )(ys, dest)


def _ffn_kernel(b0_ref, nb_ref, rows_ref, tot_ref, xs_ref, wg_ref, wu_ref, wd_ref, ys_ref,
                xbuf, obuf, wbuf_g, wbuf_u, wbuf_d, wg_scr, wu_scr, wd_scr, gsem, osem, wsem,
                *, layer, n_blocks):
    e = pl.program_id(0)
    n_exp = pl.num_programs(0)
    b0 = b0_ref[e]
    nb = nb_ref[e]
    n_used = tot_ref[0]

    def in_copy(g):
        slot = g % GATHER_DEPTH
        return pltpu.make_async_copy(_rows(xs_ref, g * ROW_BLOCK, ROW_BLOCK), xbuf.at[slot], gsem.at[slot])

    def out_copy(g, slot):
        return pltpu.make_async_copy(obuf.at[slot], _rows(ys_ref, g * ROW_BLOCK, ROW_BLOCK), osem.at[slot])

    def weight_copies(expert, slot):
        return [pltpu.make_async_copy(src.at[layer, expert], dst.at[slot], wsem.at[slot])
                for src, dst in ((wg_ref, wbuf_g), (wu_ref, wbuf_u), (wd_ref, wbuf_d))]

    @pl.when(e == 0)
    def _():
        for first in range(WEIGHT_DEPTH - 1):
            for cp in weight_copies(first, first):
                cp.start(priority=WEIGHT_DMA_PRIORITY)
        for ahead in range(GATHER_DEPTH - 1):
            @pl.when(ahead < n_used)
            def _():
                in_copy(ahead).start()

    @pl.when(e + (WEIGHT_DEPTH - 1) < n_exp)
    def _():
        for cp in weight_copies(e + (WEIGHT_DEPTH - 1), (e + (WEIGHT_DEPTH - 1)) % WEIGHT_DEPTH):
            cp.start(priority=WEIGHT_DMA_PRIORITY)

    wslot = e % WEIGHT_DEPTH
    for cp in weight_copies(e, wslot):
        cp.wait()

    @pl.when(nb > 0)
    def _():
        wg_scr[...] = wbuf_g[wslot].astype(BF16)
        wu_scr[...] = wbuf_u[wslot].astype(BF16)
        wd_scr[...] = wbuf_d[wslot].astype(BF16)

    def block(k, carry):
        g = b0 + k
        islot = g % GATHER_DEPTH
        oslot = g % 2

        @pl.when(g + (GATHER_DEPTH - 1) < n_used)
        def _():
            in_copy(g + (GATHER_DEPTH - 1)).start()

        in_copy(g).wait()

        @pl.when(g >= 2)
        def _():
            out_copy(g - 2, oslot).wait()

        def chain(half):
            rows = pl.ds(half * SUB_BLOCK, SUB_BLOCK)
            xb = jnp.concatenate(_unpack_bf16_pairs(xbuf[islot, rows, :]), axis=1)
            gate = jnp.dot(xb, wg_scr[...], preferred_element_type=F32)
            up = jnp.dot(xb, wu_scr[...], preferred_element_type=F32)
            act = (gate * jax.nn.sigmoid(gate) * up).astype(BF16)
            y = jnp.dot(act, wd_scr[...], preferred_element_type=F32)
            obuf[oslot, rows, :] = _pack_bf16_pairs(y.astype(BF16).astype(F32))

        both = rows_ref[e] - k * ROW_BLOCK > SUB_BLOCK

        @pl.when(both)
        def _():
            chain(0)
            chain(1)

        @pl.when(jnp.logical_not(both))
        def _():
            chain(0)
            obuf[oslot, pl.ds(SUB_BLOCK, SUB_BLOCK), :] = jnp.zeros((SUB_BLOCK, PACK_W), jnp.int32)

        out_copy(g, oslot).start()
        return carry

    lax.fori_loop(0, nb, block, 0)

    @pl.when(e == n_exp - 1)
    def _():
        @pl.when(n_used >= 2)
        def _():
            out_copy(n_used - 2, n_used % 2).wait()

        out_copy(n_used - 1, (n_used - 1) % 2).wait()
        obuf[0] = jnp.zeros((ROW_BLOCK, PACK_W), jnp.int32)

        def fill(g, carry):
            out_copy(g, 0).start()
            return carry

        def drain(g, carry):
            out_copy(g, 0).wait()
            return carry

        lax.fori_loop(n_used, n_blocks, fill, 0)
        lax.fori_loop(n_used, n_blocks, drain, 0)


def _ffn_call(layer, blk_start, seg_blocks, seg_rows, n_used, n_blocks, xs, w_gate, w_up, w_down):
    d, f = w_gate.shape[-2:]
    any_spec = pl.BlockSpec(memory_space=pl.ANY)
    return pl.pallas_call(
        functools.partial(_ffn_kernel, layer=layer, n_blocks=n_blocks),
        grid_spec=pltpu.PrefetchScalarGridSpec(
            num_scalar_prefetch=4,
            grid=(N_EXPERTS,),
            in_specs=[any_spec, any_spec, any_spec, any_spec],
            out_specs=any_spec,
            scratch_shapes=[
                pltpu.VMEM((GATHER_DEPTH, ROW_BLOCK, PACK_W), jnp.int32),
                pltpu.VMEM((2, ROW_BLOCK, PACK_W), jnp.int32),
                pltpu.VMEM((WEIGHT_DEPTH, d, f), F32),
                pltpu.VMEM((WEIGHT_DEPTH, d, f), F32),
                pltpu.VMEM((WEIGHT_DEPTH, f, d), F32),
                pltpu.VMEM((d, f), BF16),
                pltpu.VMEM((d, f), BF16),
                pltpu.VMEM((f, d), BF16),
                pltpu.SemaphoreType.DMA((GATHER_DEPTH,)),
                pltpu.SemaphoreType.DMA((2,)),
                pltpu.SemaphoreType.DMA((WEIGHT_DEPTH,)),
            ],
        ),
        out_shape=jax.ShapeDtypeStruct((n_blocks * ROW_BLOCK, PACK_W), jnp.int32),
        compiler_params=pltpu.CompilerParams(
            dimension_semantics=("arbitrary",), vmem_limit_bytes=VMEM_LIMIT),
        name="expert_ffn",
    )(blk_start, seg_blocks, seg_rows, n_used, xs, w_gate, w_up, w_down)


def _combine_kernel(ya0_ref, ya1_ref, wt_ref, x1_ref, ada_ref, g_ref, b_ref, o_ref, *, alpha):
    o_ref[...] = _combine_tile(ya0_ref, ya1_ref, wt_ref, x1_ref[...], ada_ref[0][5:6], g_ref, b_ref, alpha=alpha)


def _combine_call(ya, wt_tok, x1_flat, ada_l, ln_g, ln_b, alpha, seq):
    n_tok, d = x1_flat.shape
    ts = SEQ_TILE
    n_tiles = n_tok // ts
    per_seq = seq // ts
    return pl.pallas_call(
        functools.partial(_combine_kernel, alpha=alpha),
        grid=(n_tiles,),
        in_specs=[
            pl.BlockSpec((ts, PACK_W), lambda i: (i, 0)),
            pl.BlockSpec((ts, PACK_W), lambda i: (n_tiles + i, 0)),
            pl.BlockSpec((ts, TOP_K), lambda i: (i, 0)),
            pl.BlockSpec((ts, d), lambda i: (i, 0)),
            pl.BlockSpec((1, ADA_CHUNKS, d), lambda i: (i // per_seq, 0, 0)),
            pl.BlockSpec((1, d), lambda i: (0, 0)),
            pl.BlockSpec((1, d), lambda i: (0, 0)),
        ],
        out_specs=pl.BlockSpec((ts, d), lambda i: (i, 0)),
        out_shape=jax.ShapeDtypeStruct((n_tok, d), F32),
        compiler_params=pltpu.CompilerParams(
            dimension_semantics=("arbitrary",), vmem_limit_bytes=VMEM_LIMIT),
        name="combine_ln",
    )(ya, ya, wt_tok, x1_flat, ada_l, ln_g, ln_b)


def kernel(x, c, w_ada, b_ada, w_in, conv_w, conv_b, pool_w, pool_scale, sgu_ln_g, sgu_ln_b,
           sgu_w, sgu_b, w_out, ln1_g, ln1_b, w_router, b_router, w_gate, w_up, w_down,
           ln2_g, ln2_b):
    bsz, seq, d = x.shape
    depth = w_ada.shape[0]
    n_tok = bsz * seq
    alpha = (2 * depth) ** 0.25
    assert d == D_MODEL and seq % SEQ_TILE == 0 and SEQ_TILE % CHUNK == 0

    ada = _ada_call(c, w_ada, b_ada).reshape(depth, bsz, ADA_CHUNKS, d)

    w_in_b = w_in.astype(BF16)
    w_out_b = w_out.astype(BF16)
    eye_g = jnp.eye(POOL_W // HEAD_DIM, dtype=F32)
    poolw_bd = jnp.einsum('lgcd,gh->lgchd', pool_w, eye_g).reshape(depth, POOL_W, POOL_W).astype(BF16)
    n_pairs = SGU_W // LANES
    sguw_pair = sgu_w.reshape(depth, n_pairs, 2, CHUNK, CHUNK).transpose(0, 1, 3, 2, 4).reshape(
        depth, n_pairs, CHUNK, 2 * CHUNK)
    sgub_pair = jnp.repeat(sgu_b.transpose(0, 2, 1), HEAD_DIM, axis=-1).reshape(
        depth, CHUNK, n_pairs, LANES).transpose(0, 2, 1, 3)
    wr_pad = jnp.pad(w_router, ((0, 0), (0, LANES - N_EXPERTS))).astype(BF16)
    br_pad = jnp.pad(b_router, (0, LANES - N_EXPERTS)).reshape(1, LANES)

    n_steps = (n_tok * TOP_K) // ROW_BLOCK + N_EXPERTS
    experts = jnp.arange(N_EXPERTS, dtype=jnp.int32)
    tri = jnp.triu(jnp.ones((SEQ_TILE, SEQ_TILE), BF16), k=1)

    prev = None
    for l in range(depth):
        x1, h2p, e_idx, wts, rank, counts = _mixer_call(
            x, ada[l], w_in_b[l], w_out_b[l], conv_w[l], conv_b[l].reshape(1, -1), poolw_bd[l],
            pool_scale[l].reshape(1, -1), sgu_ln_g[l].reshape(1, -1), sgu_ln_b[l].reshape(1, -1),
            sguw_pair[l], sgub_pair[l], ln1_g[l].reshape(1, -1), ln1_b[l].reshape(1, -1),
            wr_pad, br_pad, tri, alpha, prev)
        sizes = counts[:, 0].astype(jnp.int32)
        seg_blocks = (sizes + ROW_BLOCK - 1) // ROW_BLOCK
        blk_end = jnp.cumsum(seg_blocks)
        blk_start = (blk_end - seg_blocks).astype(jnp.int32)
        n_used = blk_end[-1:].astype(jnp.int32)
        seg_row0 = jnp.sum(jnp.where(e_idx[:, :, None] == experts[None, None, :],
                                     (blk_start * ROW_BLOCK)[None, None, :], 0), axis=2)
        dest = (seg_row0 + rank).astype(jnp.int32).reshape(TOP_K, n_tok // SC_ROWS, SC_ROWS)

        xs = _dispatch_call(h2p, dest, n_steps * ROW_BLOCK)
        ys = _ffn_call(l, blk_start, seg_blocks.astype(jnp.int32), sizes, n_used, n_steps,
                       xs, w_gate, w_up, w_down)
        ya = _collect_call(ys, dest, n_tok)
        wt_tok = wts.T
        prev = (ya, wt_tok, ada[l], ln2_g[l].reshape(1, -1), ln2_b[l].reshape(1, -1))
        x = x1
    ya, wt_tok, ada_last, ln2_g_last, ln2_b_last = prev
    return _combine_call(ya, wt_tok, x.reshape(n_tok, d), ada_last, ln2_g_last, ln2_b_last,
                         alpha, seq).reshape(bsz, seq, d)
```

```python
import functools

import jax
import jax.numpy as jnp
from jax import lax
from jax.experimental import pallas as pl
from jax.experimental.pallas import tpu as pltpu
from jax.experimental.pallas import tpu_sc as plsc

D_MODEL = 1024
HEAD_DIM = D_MODEL // 16
CONV_W = 6 * HEAD_DIM
POOL_W = 4 * HEAD_DIM
SGU_W = 6 * HEAD_DIM
D_IN = 3 * CONV_W + POOL_W + 2 * SGU_W
CONV_K = 3
POOL_WINDOWS = (2, 4, 8, 16)
CHUNK = 128
N_EXPERTS = 32
N_GROUPS = 4
EXPERTS_PER_GROUP = N_EXPERTS // N_GROUPS
TOP_K = 2
D_FF = D_MODEL // 2
ADA_CHUNKS = 6
LN_EPS = 1e-5

OFF_GB = 0
OFF_GC = CONV_W
OFF_P = 3 * CONV_W
OFF_U = OFF_P + POOL_W
OFF_V = OFF_U + SGU_W
YOFF_POOL = CONV_W
YOFF_SGU = CONV_W + POOL_W

LANES = 128
SUBLANES = 8
CONV_HALO = 8
POOL_HALO = 16
POOL_SLACK = 8
SEQ_TILE = 512
SUB_BLOCK = 256
ROW_BLOCK = 2 * SUB_BLOCK
ADA_TILE = 1536
GATHER_DEPTH = 3
WEIGHT_DEPTH = 3
WEIGHT_DMA_PRIORITY = 1
VMEM_LIMIT = 60 * 1024 * 1024

PACK_W = D_MODEL // 2
SC_ROWS = 64
SC_WORKERS = 32

F32 = jnp.float32
BF16 = jnp.bfloat16


def _layer_norm(r, g, b):
    mu = jnp.mean(r, axis=-1, keepdims=True)
    d = r - mu
    var = jnp.mean(d * d, axis=-1, keepdims=True)
    return d * lax.rsqrt(var + LN_EPS) * g + b


def _rows(ref, start, size):
    return ref.at[pl.ds(pl.multiple_of(start, SUBLANES), size), :]


def _pack_bf16_pairs(v):
    bits = pltpu.bitcast(v, jnp.int32)
    return bits[:, 0:PACK_W] | lax.shift_right_logical(bits[:, PACK_W:2 * PACK_W], 16)


def _unpack_pairs_f32(words):
    return jnp.concatenate([pltpu.bitcast(words & jnp.int32(-65536), F32),
                            pltpu.bitcast(lax.shift_left(words, 16), F32)], axis=1)


def _unpack_bf16_pairs(words):
    first = pltpu.bitcast(words & jnp.int32(-65536), F32).astype(BF16)
    second = pltpu.bitcast(lax.shift_left(words, 16), F32).astype(BF16)
    return first, second


def _ada_kernel(c_ref, w_ref, b_ref, o_ref):
    c = c_ref[...]
    c_act = (c * jax.nn.sigmoid(c)).astype(BF16)
    col = pl.program_id(0) * ADA_TILE + lax.broadcasted_iota(jnp.int32, (1, ADA_TILE), 1)
    chunk = col // D_MODEL
    one = jnp.where((chunk == 1) | (chunk == 4), 1.0, 0.0)
    o_ref[...] = jnp.dot(c_act, w_ref[0].astype(BF16), preferred_element_type=F32) + (b_ref[0] + one)


def _ada_call(layer, c, w_ada, b_ada):
    depth, d, n = w_ada.shape
    bsz = c.shape[0]
    return pl.pallas_call(
        _ada_kernel,
        grid=(n // ADA_TILE,),
        in_specs=[
            pl.BlockSpec((bsz, d), lambda j: (0, 0)),
            pl.BlockSpec((1, d, ADA_TILE), lambda j: (layer, 0, j)),
            pl.BlockSpec((1, 1, ADA_TILE), lambda j: (layer, 0, j)),
        ],
        out_specs=pl.BlockSpec((bsz, ADA_TILE), lambda j: (0, j)),
        out_shape=jax.ShapeDtypeStruct((bsz, n), F32),
        compiler_params=pltpu.CompilerParams(
            dimension_semantics=("arbitrary",), vmem_limit_bytes=VMEM_LIMIT),
        name="ada",
    )(c, w_ada, b_ada.reshape(depth, 1, n))


def _top2_of_group(pg, sub_iota):
    big = float(EXPERTS_PER_GROUP)
    m1 = jnp.max(pg, axis=0, keepdims=True)
    i1 = jnp.min(jnp.where(pg == m1, sub_iota, big), axis=0, keepdims=True)
    rest = jnp.where(sub_iota == i1, -1.0, pg)
    m2 = jnp.max(rest, axis=0, keepdims=True)
    i2 = jnp.min(jnp.where(rest == m2, sub_iota, big), axis=0, keepdims=True)
    return m1, i1, m2, i2


def _combine_tile(ya0_ref, ya1_ref, wt_ref, x1, gate, g_ref, b_ref, *, alpha):
    wt = wt_ref[...]
    y = wt[:, 0:1] * _unpack_pairs_f32(ya0_ref[...]) + wt[:, 1:2] * _unpack_pairs_f32(ya1_ref[...])
    return _layer_norm(alpha * x1 + gate * y, g_ref[...], b_ref[...])


def _mixer_kernel(*refs, ts, alpha, nt, fused):
    if fused:
        xprev_ref, ya0_ref, ya1_ref, wtp_ref, adap_ref, ln2g_ref, ln2b_ref = refs[:7]
        refs = refs[7:]
    else:
        x_ref, refs = refs[0], refs[1:]
    (ada_ref, win_ref, wout_ref, convw_ref, convb_ref, poolw_ref, pscale_ref, lng_ref, lnb_ref,
     sguw_ref, sgub_ref, ln1g_ref, ln1b_ref, wr_ref, br_ref, tri_ref,
     x1_ref, h2p_ref, e_ref, wt_ref, rank_ref, cnt_ref,
     gx_scr, p_scr, q_scr, r_scr, y_scr, cnt_scr) = refs

    i = pl.program_id(0)
    j = i % nt

    @pl.when(i == 0)
    def _():
        gx_scr[0:CONV_HALO, :] = jnp.zeros((CONV_HALO, CONV_W), F32)
        for scr in (p_scr, q_scr, r_scr):
            scr[0:POOL_SLACK + POOL_HALO, :] = jnp.zeros((POOL_SLACK + POOL_HALO, POOL_W), F32)
        cnt_scr[...] = jnp.zeros_like(cnt_scr)

    if fused:
        x = _combine_tile(ya0_ref, ya1_ref, wtp_ref, xprev_ref[...], adap_ref[0][5:6], ln2g_ref, ln2b_ref,
                          alpha=alpha)
    else:
        x = x_ref[0]
    _mixer_tile(j, x, ada_ref, win_ref, wout_ref, convw_ref, convb_ref, poolw_ref, pscale_ref,
                lng_ref, lnb_ref, sguw_ref, sgub_ref, ln1g_ref, ln1b_ref, wr_ref, br_ref, tri_ref,
                x1_ref, h2p_ref, e_ref, wt_ref, rank_ref, cnt_ref, gx_scr, p_scr, q_scr, r_scr, y_scr, cnt_scr,
                ts=ts, alpha=alpha)


def _mixer_tile(j, x, ada_ref, win_ref, wout_ref, convw_ref, convb_ref, poolw_ref, pscale_ref,
                lng_ref, lnb_ref, sguw_ref, sgub_ref, ln1g_ref, ln1b_ref, wr_ref, br_ref, tri_ref,
                x1_ref, h2p_ref, e_ref, wt_ref, rank_ref, cnt_ref, gx_scr, p_scr, q_scr, r_scr, y_scr, cnt_scr,
                *, ts, alpha):
    first_of_seq = j == 0
    ada = ada_ref[0]
    sh1, mul1, g1 = ada[0:1], ada[1:2], ada[2:3]
    sh2, mul2 = ada[3:4], ada[4:5]
    hb = (x * mul1 + sh1).astype(BF16)

    def proj(lo, hi):
        return jnp.dot(hb, win_ref[:, lo:hi], preferred_element_type=F32)

    gcxc = proj(OFF_GC, OFF_P)
    g = gcxc[:, :CONV_W] * gcxc[:, CONV_W:]
    gx_scr[CONV_HALO:CONV_HALO + ts, :] = g
    cw = convw_ref[...]
    gx_scr[0:CONV_HALO, :] = jnp.where(first_of_seq, 0.0, gx_scr[0:CONV_HALO, :])
    conv = (cw[0:1] * gx_scr[CONV_HALO - 2:CONV_HALO - 2 + ts, :]
            + cw[1:2] * gx_scr[CONV_HALO - 1:CONV_HALO - 1 + ts, :]
            + cw[2:3] * g + convb_ref[...])
    gx_scr[0:CONV_HALO, :] = g[ts - CONV_HALO:ts, :]
    y_scr[:, 0:CONV_W] = (proj(OFF_GB, OFF_GC) * conv).astype(BF16)

    p = proj(OFF_P, OFF_U)
    top = POOL_SLACK
    n_ext = POOL_HALO + ts
    p_scr[top + POOL_HALO:top + n_ext, :] = p
    p_scr[top:top + POOL_HALO, :] = jnp.where(first_of_seq, 0.0, p_scr[top:top + POOL_HALO, :])
    w2 = p_scr[top:top + n_ext, :] + p_scr[top - 1:top - 1 + n_ext, :]
    q_scr[top:top + n_ext, :] = w2
    w4 = w2 + q_scr[top - 2:top - 2 + n_ext, :]
    r_scr[top:top + n_ext, :] = w4
    w8 = w4 + r_scr[top - 4:top - 4 + n_ext, :]
    q_scr[top:top + n_ext, :] = w8
    w16 = w8 + q_scr[top - 8:top - 8 + n_ext, :]
    s2, s4 = w2[POOL_HALO:, 0:LANES], w4[POOL_HALO:, 0:LANES]
    s8, s16 = w8[POOL_HALO:, LANES:2 * LANES], w16[POOL_HALO:, LANES:2 * LANES]
    p_scr[top:top + POOL_HALO, :] = p[ts - POOL_HALO:ts, :]
    lane = lax.broadcasted_iota(jnp.int32, (ts, LANES), 1)
    tpos = (lax.broadcasted_iota(jnp.int32, (ts, LANES), 0) + (j * ts + 1)).astype(F32)
    lo_half = lane < HEAD_DIM
    cnt_a = jnp.minimum(tpos, jnp.where(lo_half, float(POOL_WINDOWS[0]), float(POOL_WINDOWS[1])))
    cnt_b = jnp.minimum(tpos, jnp.where(lo_half, float(POOL_WINDOWS[2]), float(POOL_WINDOWS[3])))
    pooled = jnp.concatenate([jnp.where(lo_half, s2, s4) / cnt_a,
                              jnp.where(lo_half, s8, s16) / cnt_b], axis=1) - p
    mixed = jnp.dot(pooled.astype(BF16), poolw_ref[...], preferred_element_type=F32)
    y_scr[:, YOFF_POOL:YOFF_POOL + POOL_W] = (mixed * pscale_ref[...]).astype(BF16)

    v = proj(OFF_V, D_IN)
    vnb = _layer_norm(v, lng_ref[...], lnb_ref[...]).astype(BF16)
    u = proj(OFF_U, OFF_V)
    row_c = lax.broadcasted_iota(jnp.int32, (CHUNK, 2 * CHUNK), 0)
    col_c = lax.broadcasted_iota(jnp.int32, (CHUNK, 2 * CHUNK), 1)
    causal = (col_c & (CHUNK - 1)) <= row_c
    lo_lanes = lax.broadcasted_iota(jnp.int32, (CHUNK, LANES), 1) < HEAD_DIM
    zero_b = jnp.zeros((CHUNK, LANES), BF16)
    for hp in range(SGU_W // LANES):
        wl = jnp.where(causal, sguw_ref[hp], 0.0).astype(BF16)
        bias = sgub_ref[hp]
        for ci in range(ts // CHUNK):
            rs = slice(ci * CHUNK, (ci + 1) * CHUNK)
            vc = vnb[rs, hp * LANES:(hp + 1) * LANES]
            rhs = jnp.concatenate([jnp.where(lo_lanes, vc, zero_b),
                                   jnp.where(lo_lanes, zero_b, vc)], axis=0)
            mixed_c = jnp.dot(wl, rhs, preferred_element_type=F32) + bias
            y_scr[rs, YOFF_SGU + hp * LANES:YOFF_SGU + (hp + 1) * LANES] = (
                u[rs, hp * LANES:(hp + 1) * LANES] * mixed_c).astype(BF16)

    yo = jnp.dot(y_scr[...], wout_ref[...], preferred_element_type=F32)
    x1 = _layer_norm(alpha * x + g1 * yo, ln1g_ref[...], ln1b_ref[...])
    x1_ref[0] = x1
    h2b = (x1 * mul2 + sh2).astype(BF16)
    h2p_ref[...] = _pack_bf16_pairs(h2b.astype(F32))

    logits = jnp.dot(h2b, wr_ref[...], preferred_element_type=F32) + br_ref[...]
    lt = logits.T[0:N_EXPERTS, :]
    ex = jnp.exp(lt - jnp.max(lt, axis=0, keepdims=True))
    probs = ex / jnp.sum(ex, axis=0, keepdims=True)
    sub_iota = lax.broadcasted_iota(jnp.int32, (EXPERTS_PER_GROUP, ts), 0).astype(F32)
    best = None
    for gi in range(N_GROUPS):
        m1, i1, m2, i2 = _top2_of_group(
            probs[gi * EXPERTS_PER_GROUP:(gi + 1) * EXPERTS_PER_GROUP, :], sub_iota)
        score = m1 + m2
        cand = (score, m1, m2, i1 + float(gi * EXPERTS_PER_GROUP), i2 + float(gi * EXPERTS_PER_GROUP))
        if best is None:
            best = cand
        else:
            better = score > best[0]
            best = tuple(jnp.where(better, c, o) for c, o in zip(cand, best))
    _, p1, p2, e1, e2 = best
    den = p1 + p2
    w1 = p1 / den
    w2 = p2 / den

    wt_ref[...] = jnp.concatenate([w1, w2], axis=0)
    e_ref[...] = jnp.concatenate([e1, e2], axis=0).astype(jnp.int32)

    ex_iota = lax.broadcasted_iota(jnp.int32, (N_EXPERTS, ts), 0).astype(F32)
    oh1 = ex_iota == e1
    oh2 = ex_iota == e2
    either = jnp.where(oh1 | oh2, 1.0, 0.0)
    seen = jnp.dot(either.astype(BF16), tri_ref[...], preferred_element_type=F32) + cnt_scr[:, 0:1]
    r1 = jnp.sum(jnp.where(oh1, seen, 0.0), axis=0, keepdims=True)
    r2 = jnp.sum(jnp.where(oh2, seen, 0.0), axis=0, keepdims=True)
    rank_ref[...] = jnp.concatenate([r1, r2], axis=0).astype(jnp.int32)
    cnt_new = cnt_scr[...] + jnp.sum(either, axis=1, keepdims=True)
    cnt_scr[...] = cnt_new
    cnt_ref[...] = cnt_new


def _mixer_call(x, ada_l, w_in, w_out, conv_w, conv_b, poolw_bd, pool_scale, ln_g, ln_b,
                sguw_pair, sgub_pair, ln1_g, ln1_b, wr_pad, br_pad, tri, alpha, prev=None):
    bsz, s, d = x.shape
    ts = SEQ_TILE
    nt = s // ts
    n_tiles = bsz * nt
    n_tok = bsz * s
    fused = prev is not None
    const2 = lambda i: (0, 0)
    const3 = lambda i: (0, 0, 0)
    seq_of = lambda i: (i // nt, 0, 0)
    tile3 = lambda i: (i // nt, i % nt, 0)
    tile2 = lambda i: (i, 0)
    by_token = lambda i: (0, i)
    once = pl.Buffered(1)
    in_specs = [
        pl.BlockSpec((1, ADA_CHUNKS, d), seq_of),
        pl.BlockSpec((d, D_IN), const2, pipeline_mode=once),
        pl.BlockSpec((d, d), const2, pipeline_mode=once),
        pl.BlockSpec((CONV_K, CONV_W), const2),
        pl.BlockSpec((1, CONV_W), const2),
        pl.BlockSpec((POOL_W, POOL_W), const2),
        pl.BlockSpec((1, POOL_W), const2),
        pl.BlockSpec((1, SGU_W), const2),
        pl.BlockSpec((1, SGU_W), const2),
        pl.BlockSpec((SGU_W // LANES, CHUNK, 2 * CHUNK), const3),
        pl.BlockSpec((SGU_W // LANES, CHUNK, LANES), const3),
        pl.BlockSpec((1, d), const2),
        pl.BlockSpec((1, d), const2),
        pl.BlockSpec((d, LANES), const2),
        pl.BlockSpec((1, LANES), const2),
        pl.BlockSpec((ts, ts), const2, pipeline_mode=once),
    ]
    operands = [ada_l, w_in, w_out, conv_w, conv_b, poolw_bd, pool_scale, ln_g, ln_b,
                sguw_pair, sgub_pair, ln1_g, ln1_b, wr_pad, br_pad, tri]
    if fused:
        ya, wt_tok, ada_prev, ln2_g, ln2_b = prev
        in_specs = [
            pl.BlockSpec((ts, d), tile2),
            pl.BlockSpec((ts, PACK_W), tile2),
            pl.BlockSpec((ts, PACK_W), lambda i: (n_tiles + i, 0)),
            pl.BlockSpec((ts, TOP_K), tile2),
            pl.BlockSpec((1, ADA_CHUNKS, d), seq_of),
            pl.BlockSpec((1, d), const2),
            pl.BlockSpec((1, d), const2),
        ] + in_specs
        operands = [x.reshape(n_tok, d), ya, ya, wt_tok, ada_prev, ln2_g, ln2_b] + operands
    else:
        in_specs = [pl.BlockSpec((1, ts, d), tile3)] + in_specs
        operands = [x] + operands
    kern = functools.partial(_mixer_kernel, ts=ts, alpha=alpha, nt=nt, fused=fused)
    pool_rows = POOL_SLACK + POOL_HALO + ts
    return pl.pallas_call(
        kern,
        grid=(n_tiles,),
        in_specs=in_specs,
        out_specs=[
            pl.BlockSpec((1, ts, d), tile3),
            pl.BlockSpec((ts, PACK_W), tile2),
            pl.BlockSpec((TOP_K, ts), by_token),
            pl.BlockSpec((TOP_K, ts), by_token),
            pl.BlockSpec((TOP_K, ts), by_token),
            pl.BlockSpec((N_EXPERTS, LANES), const2),
        ],
        out_shape=[
            jax.ShapeDtypeStruct((bsz, s, d), F32),
            jax.ShapeDtypeStruct((n_tok, PACK_W), jnp.int32),
            jax.ShapeDtypeStruct((TOP_K, n_tok), jnp.int32),
            jax.ShapeDtypeStruct((TOP_K, n_tok), F32),
            jax.ShapeDtypeStruct((TOP_K, n_tok), jnp.int32),
            jax.ShapeDtypeStruct((N_EXPERTS, LANES), F32),
        ],
        scratch_shapes=[
            pltpu.VMEM((CONV_HALO + ts, CONV_W), F32),
            pltpu.VMEM((pool_rows, POOL_W), F32),
            pltpu.VMEM((pool_rows, POOL_W), F32),
            pltpu.VMEM((pool_rows, POOL_W), F32),
            pltpu.VMEM((ts, d), BF16),
            pltpu.VMEM((N_EXPERTS, LANES), F32),
        ],
        compiler_params=pltpu.CompilerParams(
            dimension_semantics=("arbitrary",), vmem_limit_bytes=VMEM_LIMIT),
        name="mixer_router",
    )(*operands)


def _sc_worker_chunks(n_chunks):
    per_worker = n_chunks // SC_WORKERS
    worker = lax.axis_index("s") * 2 + lax.axis_index("c")
    return worker * per_worker, per_worker


def _dispatch_call(h2p, dest, n_rows):
    n_tok, width = h2p.shape
    n_chunks = n_tok // SC_ROWS
    assert n_chunks % (2 * SC_WORKERS) == 0

    def body(h_hbm, dest_hbm, xs_hbm, rows_v, idx_v, sem):
        first, per_worker = _sc_worker_chunks(n_chunks)

        def load(c, slot):
            return pltpu.make_async_copy(h_hbm.at[pl.ds((first + c) * SC_ROWS, SC_ROWS)], rows_v.at[slot],
                                         sem.at[slot])

        load(0, 0).start()

        @pl.loop(0, per_worker, step=2)
        def _(c0):
            for slot in range(2):
                c = c0 + slot

                @pl.when(c + 1 < per_worker)
                def _():
                    load(c + 1, 1 - slot).start()

                for k in range(TOP_K):
                    pltpu.sync_copy(dest_hbm.at[k, first + c], idx_v.at[slot, k])
                load(c, slot).wait()
                for k in range(TOP_K):
                    pltpu.sync_copy(rows_v.at[slot], xs_hbm.at[idx_v.at[slot, k]])

    return pl.kernel(
        body,
        out_type=jax.ShapeDtypeStruct((n_rows, width), jnp.int32),
        mesh=plsc.VectorSubcoreMesh(core_axis_name="c", subcore_axis_name="s"),
        scratch_types=[pltpu.VMEM((2, SC_ROWS, width), jnp.int32), pltpu.VMEM((2, TOP_K, SC_ROWS), jnp.int32),
                       pltpu.SemaphoreType.DMA((2,))],
        name="dispatch",
    )(h2p, dest)


def _collect_call(ys, dest, n_tok):
    width = ys.shape[1]
    n_chunks = n_tok // SC_ROWS
    assert n_chunks % SC_WORKERS == 0

    def body(ys_hbm, dest_hbm, ya_hbm, rows_v, idx_v, sem):
        first, per_worker = _sc_worker_chunks(n_chunks)

        def store(c, k):
            return pltpu.make_async_copy(rows_v.at[k], ya_hbm.at[pl.ds(k * n_tok + (first + c) * SC_ROWS, SC_ROWS)],
                                         sem.at[k])

        @pl.loop(0, per_worker)
        def _(c):
            for k in range(TOP_K):
                pltpu.sync_copy(dest_hbm.at[k, first + c], idx_v.at[k])

                @pl.when(c > 0)
                def _():
                    store(c - 1, k).wait()

                pltpu.sync_copy(ys_hbm.at[idx_v.at[k]], rows_v.at[k])
                store(c, k).start()

        for k in range(TOP_K):
            store(per_worker - 1, k).wait()

    return pl.kernel(
        body,
        out_type=jax.ShapeDtypeStruct((TOP_K * n_tok, width), jnp.int32),
        mesh=plsc.VectorSubcoreMesh(core_axis_name="c", subcore_axis_name="s"),
        scratch_types=[pltpu.VMEM((TOP_K, SC_ROWS, width), jnp.int32), pltpu.VMEM((TOP_K, SC_ROWS), jnp.int32),
                       pltpu.SemaphoreType.DMA((TOP_K,))],
        name="collect",
    )(ys, dest)


def _ffn_kernel(b0_ref, nb_ref, rows_ref, tot_ref, xs_ref, wg_ref, wu_ref, wd_ref, ys_ref,
                xbuf, obuf, wbuf_g, wbuf_u, wbuf_d, wg_scr, wu_scr, wd_scr, gsem, osem, wsem,
                *, layer, n_blocks):
    e = pl.program_id(0)
    n_exp = pl.num_programs(0)
    b0 = b0_ref[e]
    nb = nb_ref[e]
    n_used = tot_ref[0]

    def in_copy(g):
        slot = g % GATHER_DEPTH
        return pltpu.make_async_copy(_rows(xs_ref, g * ROW_BLOCK, ROW_BLOCK), xbuf.at[slot], gsem.at[slot])

    def out_copy(g, slot):
        return pltpu.make_async_copy(obuf.at[slot], _rows(ys_ref, g * ROW_BLOCK, ROW_BLOCK), osem.at[slot])

    def weight_copies(expert, slot):
        return [pltpu.make_async_copy(src.at[layer, expert], dst.at[slot], wsem.at[slot])
                for src, dst in ((wg_ref, wbuf_g), (wu_ref, wbuf_u), (wd_ref, wbuf_d))]

    @pl.when(e == 0)
    def _():
        for first in range(WEIGHT_DEPTH - 1):
            for cp in weight_copies(first, first):
                cp.start(priority=WEIGHT_DMA_PRIORITY)
        for ahead in range(GATHER_DEPTH - 1):
            @pl.when(ahead < n_used)
            def _():
                in_copy(ahead).start()

    @pl.when(e + (WEIGHT_DEPTH - 1) < n_exp)
    def _():
        for cp in weight_copies(e + (WEIGHT_DEPTH - 1), (e + (WEIGHT_DEPTH - 1)) % WEIGHT_DEPTH):
            cp.start(priority=WEIGHT_DMA_PRIORITY)

    wslot = e % WEIGHT_DEPTH
    for cp in weight_copies(e, wslot):
        cp.wait()

    @pl.when(nb > 0)
    def _():
        wg_scr[...] = wbuf_g[wslot].astype(BF16)
        wu_scr[...] = wbuf_u[wslot].astype(BF16)
        wd_scr[...] = wbuf_d[wslot].astype(BF16)

    def block(k, carry):
        g = b0 + k
        islot = g % GATHER_DEPTH
        oslot = g % 2

        @pl.when(g + (GATHER_DEPTH - 1) < n_used)
        def _():
            in_copy(g + (GATHER_DEPTH - 1)).start()

        in_copy(g).wait()

        @pl.when(g >= 2)
        def _():
            out_copy(g - 2, oslot).wait()

        def chain(half):
            rows = pl.ds(half * SUB_BLOCK, SUB_BLOCK)
            xb = jnp.concatenate(_unpack_bf16_pairs(xbuf[islot, rows, :]), axis=1)
            gate = jnp.dot(xb, wg_scr[...], preferred_element_type=F32)
            up = jnp.dot(xb, wu_scr[...], preferred_element_type=F32)
            act = (gate * jax.nn.sigmoid(gate) * up).astype(BF16)
            y = jnp.dot(act, wd_scr[...], preferred_element_type=F32)
            obuf[oslot, rows, :] = _pack_bf16_pairs(y.astype(BF16).astype(F32))

        both = rows_ref[e] - k * ROW_BLOCK > SUB_BLOCK

        @pl.when(both)
        def _():
            chain(0)
            chain(1)

        @pl.when(jnp.logical_not(both))
        def _():
            chain(0)
            obuf[oslot, pl.ds(SUB_BLOCK, SUB_BLOCK), :] = jnp.zeros((SUB_BLOCK, PACK_W), jnp.int32)

        out_copy(g, oslot).start()
        return carry

    lax.fori_loop(0, nb, block, 0)

    @pl.when(e == n_exp - 1)
    def _():
        @pl.when(n_used >= 2)
        def _():
            out_copy(n_used - 2, n_used % 2).wait()

        out_copy(n_used - 1, (n_used - 1) % 2).wait()
        obuf[0] = jnp.zeros((ROW_BLOCK, PACK_W), jnp.int32)

        def fill(g, carry):
            out_copy(g, 0).start()
            return carry

        def drain(g, carry):
            out_copy(g, 0).wait()
            return carry

        lax.fori_loop(n_used, n_blocks, fill, 0)
        lax.fori_loop(n_used, n_blocks, drain, 0)


def _ffn_call(layer, blk_start, seg_blocks, seg_rows, n_used, n_blocks, xs, w_gate, w_up, w_down):
    d, f = w_gate.shape[-2:]
    any_spec = pl.BlockSpec(memory_space=pl.ANY)
    return pl.pallas_call(
        functools.partial(_ffn_kernel, layer=layer, n_blocks=n_blocks),
        grid_spec=pltpu.PrefetchScalarGridSpec(
            num_scalar_prefetch=4,
            grid=(N_EXPERTS,),
            in_specs=[any_spec, any_spec, any_spec, any_spec],
            out_specs=any_spec,
            scratch_shapes=[
                pltpu.VMEM((GATHER_DEPTH, ROW_BLOCK, PACK_W), jnp.int32),
                pltpu.VMEM((2, ROW_BLOCK, PACK_W), jnp.int32),
                pltpu.VMEM((WEIGHT_DEPTH, d, f), F32),
                pltpu.VMEM((WEIGHT_DEPTH, d, f), F32),
                pltpu.VMEM((WEIGHT_DEPTH, f, d), F32),
                pltpu.VMEM((d, f), BF16),
                pltpu.VMEM((d, f), BF16),
                pltpu.VMEM((f, d), BF16),
                pltpu.SemaphoreType.DMA((GATHER_DEPTH,)),
                pltpu.SemaphoreType.DMA((2,)),
                pltpu.SemaphoreType.DMA((WEIGHT_DEPTH,)),
            ],
        ),
        out_shape=jax.ShapeDtypeStruct((n_blocks * ROW_BLOCK, PACK_W), jnp.int32),
        compiler_params=pltpu.CompilerParams(
            dimension_semantics=("arbitrary",), vmem_limit_bytes=VMEM_LIMIT),
        name="expert_ffn",
    )(blk_start, seg_blocks, seg_rows, n_used, xs, w_gate, w_up, w_down)


def _combine_kernel(ya0_ref, ya1_ref, wt_ref, x1_ref, ada_ref, g_ref, b_ref, o_ref, *, alpha):
    o_ref[...] = _combine_tile(ya0_ref, ya1_ref, wt_ref, x1_ref[...], ada_ref[0][5:6], g_ref, b_ref, alpha=alpha)


def _combine_call(ya, wt_tok, x1_flat, ada_l, ln_g, ln_b, alpha, seq):
    n_tok, d = x1_flat.shape
    ts = SEQ_TILE
    n_tiles = n_tok // ts
    per_seq = seq // ts
    return pl.pallas_call(
        functools.partial(_combine_kernel, alpha=alpha),
        grid=(n_tiles,),
        in_specs=[
            pl.BlockSpec((ts, PACK_W), lambda i: (i, 0)),
            pl.BlockSpec((ts, PACK_W), lambda i: (n_tiles + i, 0)),
            pl.BlockSpec((ts, TOP_K), lambda i: (i, 0)),
            pl.BlockSpec((ts, d), lambda i: (i, 0)),
            pl.BlockSpec((1, ADA_CHUNKS, d), lambda i: (i // per_seq, 0, 0)),
            pl.BlockSpec((1, d), lambda i: (0, 0)),
            pl.BlockSpec((1, d), lambda i: (0, 0)),
        ],
        out_specs=pl.BlockSpec((ts, d), lambda i: (i, 0)),
        out_shape=jax.ShapeDtypeStruct((n_tok, d), F32),
        compiler_params=pltpu.CompilerParams(
            dimension_semantics=("arbitrary",), vmem_limit_bytes=VMEM_LIMIT),
        name="combine_ln",
    )(ya, ya, wt_tok, x1_flat, ada_l, ln_g, ln_b)


def kernel(x, c, w_ada, b_ada, w_in, conv_w, conv_b, pool_w, pool_scale, sgu_ln_g, sgu_ln_b,
           sgu_w, sgu_b, w_out, ln1_g, ln1_b, w_router, b_router, w_gate, w_up, w_down,
           ln2_g, ln2_b):
    bsz, seq, d = x.shape
    depth = w_ada.shape[0]
    n_tok = bsz * seq
    alpha = (2 * depth) ** 0.25
    assert d == D_MODEL and seq % SEQ_TILE == 0 and SEQ_TILE % CHUNK == 0

    ada = [_ada_call(l, c, w_ada, b_ada).reshape(bsz, ADA_CHUNKS, d) for l in range(depth)]

    w_in_b = w_in.astype(BF16)
    w_out_b = w_out.astype(BF16)
    eye_g = jnp.eye(POOL_W // HEAD_DIM, dtype=F32)
    poolw_bd = jnp.einsum('lgcd,gh->lgchd', pool_w, eye_g).reshape(depth, POOL_W, POOL_W).astype(BF16)
    n_pairs = SGU_W // LANES
    sguw_pair = sgu_w.reshape(depth, n_pairs, 2, CHUNK, CHUNK).transpose(0, 1, 3, 2, 4).reshape(
        depth, n_pairs, CHUNK, 2 * CHUNK)
    sgub_pair = jnp.repeat(sgu_b.transpose(0, 2, 1), HEAD_DIM, axis=-1).reshape(
        depth, CHUNK, n_pairs, LANES).transpose(0, 2, 1, 3)
    wr_pad = jnp.pad(w_router, ((0, 0), (0, LANES - N_EXPERTS))).astype(BF16)
    br_pad = jnp.pad(b_router, (0, LANES - N_EXPERTS)).reshape(1, LANES)

    n_steps = (n_tok * TOP_K) // ROW_BLOCK + N_EXPERTS
    experts = jnp.arange(N_EXPERTS, dtype=jnp.int32)
    tri = jnp.triu(jnp.ones((SEQ_TILE, SEQ_TILE), BF16), k=1)

    prev = None
    for l in range(depth):
        x1, h2p, e_idx, wts, rank, counts = _mixer_call(
            x, ada[l], w_in_b[l], w_out_b[l], conv_w[l], conv_b[l].reshape(1, -1), poolw_bd[l],
            pool_scale[l].reshape(1, -1), sgu_ln_g[l].reshape(1, -1), sgu_ln_b[l].reshape(1, -1),
            sguw_pair[l], sgub_pair[l], ln1_g[l].reshape(1, -1), ln1_b[l].reshape(1, -1),
            wr_pad, br_pad, tri, alpha, prev)
        sizes = counts[:, 0].astype(jnp.int32)
        seg_blocks = (sizes + ROW_BLOCK - 1) // ROW_BLOCK
        blk_end = jnp.cumsum(seg_blocks)
        blk_start = (blk_end - seg_blocks).astype(jnp.int32)
        n_used = blk_end[-1:].astype(jnp.int32)
        seg_row0 = jnp.sum(jnp.where(e_idx[:, :, None] == experts[None, None, :],
                                     (blk_start * ROW_BLOCK)[None, None, :], 0), axis=2)
        dest = (seg_row0 + rank).astype(jnp.int32).reshape(TOP_K, n_tok // SC_ROWS, SC_ROWS)

        xs = _dispatch_call(h2p, dest, n_steps * ROW_BLOCK)
        ys = _ffn_call(l, blk_start, seg_blocks.astype(jnp.int32), sizes, n_used, n_steps,
                       xs, w_gate, w_up, w_down)
        ya = _collect_call(ys, dest, n_tok)
        wt_tok = wts.T
        prev = (ya, wt_tok, ada[l], ln2_g[l].reshape(1, -1), ln2_b[l].reshape(1, -1))
        x = x1
    ya, wt_tok, ada_last, ln2_g_last, ln2_b_last = prev
    return _combine_call(ya, wt_tok, x.reshape(n_tok, d), ada_last, ln2_g_last, ln2_b_last,
                         alpha, seq).reshape(bsz, seq, d)
```

```python
import functools

import jax
import jax.numpy as jnp
from jax import lax
from jax.experimental import pallas as pl
from jax.experimental.pallas import tpu as pltpu
from jax.experimental.pallas import tpu_sc as plsc

D_MODEL = 1024
HEAD_DIM = D_MODEL // 16
CONV_W = 6 * HEAD_DIM
POOL_W = 4 * HEAD_DIM
SGU_W = 6 * HEAD_DIM
D_IN = 3 * CONV_W + POOL_W + 2 * SGU_W
CONV_K = 3
POOL_WINDOWS = (2, 4, 8, 16)
CHUNK = 128
N_EXPERTS = 32
N_GROUPS = 4
EXPERTS_PER_GROUP = N_EXPERTS // N_GROUPS
TOP_K = 2
D_FF = D_MODEL // 2
ADA_CHUNKS = 6
LN_EPS = 1e-5

OFF_GB = 0
OFF_GC = CONV_W
OFF_P = 3 * CONV_W
OFF_U = OFF_P + POOL_W
OFF_V = OFF_U + SGU_W
YOFF_POOL = CONV_W
YOFF_SGU = CONV_W + POOL_W

LANES = 128
SUBLANES = 8
CONV_HALO = 8
POOL_HALO = 16
POOL_SLACK = 8
SEQ_TILE = 1024
SUB_BLOCK = 256
ROW_BLOCK = 2 * SUB_BLOCK
ADA_TILE = 1536
GATHER_DEPTH = 3
WEIGHT_DEPTH = 3
WEIGHT_DMA_PRIORITY = 1
VMEM_LIMIT = 60 * 1024 * 1024

PACK_W = D_MODEL // 2
SC_ROWS = 128
SC_WORKERS = 32

F32 = jnp.float32
BF16 = jnp.bfloat16


def _layer_norm(r, g, b):
    mu = jnp.mean(r, axis=-1, keepdims=True)
    d = r - mu
    var = jnp.mean(d * d, axis=-1, keepdims=True)
    return d * lax.rsqrt(var + LN_EPS) * g + b


def _rows(ref, start, size):
    return ref.at[pl.ds(pl.multiple_of(start, SUBLANES), size), :]


def _pack_bf16_pairs(v):
    bits = pltpu.bitcast(v, jnp.int32)
    return bits[:, 0:PACK_W] | lax.shift_right_logical(bits[:, PACK_W:2 * PACK_W], 16)


def _unpack_pairs_f32(words):
    return jnp.concatenate([pltpu.bitcast(words & jnp.int32(-65536), F32),
                            pltpu.bitcast(lax.shift_left(words, 16), F32)], axis=1)


def _unpack_bf16_pairs(words):
    first = pltpu.bitcast(words & jnp.int32(-65536), F32).astype(BF16)
    second = pltpu.bitcast(lax.shift_left(words, 16), F32).astype(BF16)
    return first, second


def _ada_kernel(c_ref, w_ref, b_ref, o_ref):
    c = c_ref[...]
    c_act = (c * jax.nn.sigmoid(c)).astype(BF16)
    col = pl.program_id(1) * ADA_TILE + lax.broadcasted_iota(jnp.int32, (1, ADA_TILE), 1)
    chunk = col // D_MODEL
    one = jnp.where((chunk == 1) | (chunk == 4), 1.0, 0.0)
    o_ref[0] = jnp.dot(c_act, w_ref[0].astype(BF16), preferred_element_type=F32) + (b_ref[0] + one)


def _ada_call(c, w_ada, b_ada):
    depth, d, n = w_ada.shape
    bsz = c.shape[0]
    return pl.pallas_call(
        _ada_kernel,
        grid=(depth, n // ADA_TILE),
        in_specs=[
            pl.BlockSpec((bsz, d), lambda l, j: (0, 0)),
            pl.BlockSpec((1, d, ADA_TILE), lambda l, j: (l, 0, j)),
            pl.BlockSpec((1, 1, ADA_TILE), lambda l, j: (l, 0, j)),
        ],
        out_specs=pl.BlockSpec((1, bsz, ADA_TILE), lambda l, j: (l, 0, j)),
        out_shape=jax.ShapeDtypeStruct((depth, bsz, n), F32),
        compiler_params=pltpu.CompilerParams(
            dimension_semantics=("arbitrary", "arbitrary"), vmem_limit_bytes=VMEM_LIMIT),
        name="ada",
    )(c, w_ada, b_ada.reshape(depth, 1, n))


def _top2_of_group(pg, sub_iota):
    big = float(EXPERTS_PER_GROUP)
    m1 = jnp.max(pg, axis=0, keepdims=True)
    i1 = jnp.min(jnp.where(pg == m1, sub_iota, big), axis=0, keepdims=True)
    rest = jnp.where(sub_iota == i1, -1.0, pg)
    m2 = jnp.max(rest, axis=0, keepdims=True)
    i2 = jnp.min(jnp.where(rest == m2, sub_iota, big), axis=0, keepdims=True)
    return m1, i1, m2, i2


def _combine_tile(ya0_ref, ya1_ref, wt_ref, x1, gate, g_ref, b_ref, *, alpha):
    wt = wt_ref[...]
    y = wt[:, 0:1] * _unpack_pairs_f32(ya0_ref[...]) + wt[:, 1:2] * _unpack_pairs_f32(ya1_ref[...])
    return _layer_norm(alpha * x1 + gate * y, g_ref[...], b_ref[...])


def _mixer_kernel(*refs, ts, alpha, nt, fused):
    if fused:
        xprev_ref, ya0_ref, ya1_ref, wtp_ref, adap_ref, ln2g_ref, ln2b_ref = refs[:7]
        refs = refs[7:]
    else:
        x_ref, refs = refs[0], refs[1:]
    (ada_ref, win_ref, wout_ref, convw_ref, convb_ref, poolw_ref, pscale_ref, lng_ref, lnb_ref,
     sguw_ref, sgub_ref, ln1g_ref, ln1b_ref, wr_ref, br_ref, tri_ref,
     x1_ref, h2p_ref, e_ref, wt_ref, rank_ref, cnt_ref,
     gx_scr, p_scr, q_scr, r_scr, y_scr, cnt_scr) = refs

    i = pl.program_id(0)
    j = i % nt

    @pl.when(i == 0)
    def _():
        gx_scr[0:CONV_HALO, :] = jnp.zeros((CONV_HALO, CONV_W), F32)
        for scr in (p_scr, q_scr, r_scr):
            scr[0:POOL_SLACK + POOL_HALO, :] = jnp.zeros((POOL_SLACK + POOL_HALO, POOL_W), F32)
        cnt_scr[...] = jnp.zeros_like(cnt_scr)

    if fused:
        x = _combine_tile(ya0_ref, ya1_ref, wtp_ref, xprev_ref[...], adap_ref[0][5:6], ln2g_ref, ln2b_ref,
                          alpha=alpha)
    else:
        x = x_ref[0]
    _mixer_tile(j, x, ada_ref, win_ref, wout_ref, convw_ref, convb_ref, poolw_ref, pscale_ref,
                lng_ref, lnb_ref, sguw_ref, sgub_ref, ln1g_ref, ln1b_ref, wr_ref, br_ref, tri_ref,
                x1_ref, h2p_ref, e_ref, wt_ref, rank_ref, cnt_ref, gx_scr, p_scr, q_scr, r_scr, y_scr, cnt_scr,
                ts=ts, alpha=alpha)


def _mixer_tile(j, x, ada_ref, win_ref, wout_ref, convw_ref, convb_ref, poolw_ref, pscale_ref,
                lng_ref, lnb_ref, sguw_ref, sgub_ref, ln1g_ref, ln1b_ref, wr_ref, br_ref, tri_ref,
                x1_ref, h2p_ref, e_ref, wt_ref, rank_ref, cnt_ref, gx_scr, p_scr, q_scr, r_scr, y_scr, cnt_scr,
                *, ts, alpha):
    first_of_seq = j == 0
    ada = ada_ref[0]
    sh1, mul1, g1 = ada[0:1], ada[1:2], ada[2:3]
    sh2, mul2 = ada[3:4], ada[4:5]
    hb = (x * mul1 + sh1).astype(BF16)

    def proj(lo, hi):
        return jnp.dot(hb, win_ref[:, lo:hi], preferred_element_type=F32)

    gcxc = proj(OFF_GC, OFF_P)
    g = gcxc[:, :CONV_W] * gcxc[:, CONV_W:]
    gx_scr[CONV_HALO:CONV_HALO + ts, :] = g
    cw = convw_ref[...]
    gx_scr[0:CONV_HALO, :] = jnp.where(first_of_seq, 0.0, gx_scr[0:CONV_HALO, :])
    conv = (cw[0:1] * gx_scr[CONV_HALO - 2:CONV_HALO - 2 + ts, :]
            + cw[1:2] * gx_scr[CONV_HALO - 1:CONV_HALO - 1 + ts, :]
            + cw[2:3] * g + convb_ref[...])
    gx_scr[0:CONV_HALO, :] = g[ts - CONV_HALO:ts, :]
    y_scr[:, 0:CONV_W] = (proj(OFF_GB, OFF_GC) * conv).astype(BF16)

    p = proj(OFF_P, OFF_U)
    top = POOL_SLACK
    n_ext = POOL_HALO + ts
    p_scr[top + POOL_HALO:top + n_ext, :] = p
    p_scr[top:top + POOL_HALO, :] = jnp.where(first_of_seq, 0.0, p_scr[top:top + POOL_HALO, :])
    w2 = p_scr[top:top + n_ext, :] + p_scr[top - 1:top - 1 + n_ext, :]
    q_scr[top:top + n_ext, :] = w2
    w4 = w2 + q_scr[top - 2:top - 2 + n_ext, :]
    r_scr[top:top + n_ext, :] = w4
    w8 = w4 + r_scr[top - 4:top - 4 + n_ext, :]
    q_scr[top:top + n_ext, :] = w8
    w16 = w8 + q_scr[top - 8:top - 8 + n_ext, :]
    s2, s4 = w2[POOL_HALO:, 0:LANES], w4[POOL_HALO:, 0:LANES]
    s8, s16 = w8[POOL_HALO:, LANES:2 * LANES], w16[POOL_HALO:, LANES:2 * LANES]
    p_scr[top:top + POOL_HALO, :] = p[ts - POOL_HALO:ts, :]
    lane = lax.broadcasted_iota(jnp.int32, (ts, LANES), 1)
    tpos = (lax.broadcasted_iota(jnp.int32, (ts, LANES), 0) + (j * ts + 1)).astype(F32)
    lo_half = lane < HEAD_DIM
    cnt_a = jnp.minimum(tpos, jnp.where(lo_half, float(POOL_WINDOWS[0]), float(POOL_WINDOWS[1])))
    cnt_b = jnp.minimum(tpos, jnp.where(lo_half, float(POOL_WINDOWS[2]), float(POOL_WINDOWS[3])))
    pooled = jnp.concatenate([jnp.where(lo_half, s2, s4) / cnt_a,
                              jnp.where(lo_half, s8, s16) / cnt_b], axis=1) - p
    mixed = jnp.dot(pooled.astype(BF16), poolw_ref[...], preferred_element_type=F32)
    y_scr[:, YOFF_POOL:YOFF_POOL + POOL_W] = (mixed * pscale_ref[...]).astype(BF16)

    v = proj(OFF_V, D_IN)
    vnb = _layer_norm(v, lng_ref[...], lnb_ref[...]).astype(BF16)
    u = proj(OFF_U, OFF_V)
    row_c = lax.broadcasted_iota(jnp.int32, (CHUNK, 2 * CHUNK), 0)
    col_c = lax.broadcasted_iota(jnp.int32, (CHUNK, 2 * CHUNK), 1)
    causal = (col_c & (CHUNK - 1)) <= row_c
    lo_lanes = lax.broadcasted_iota(jnp.int32, (CHUNK, LANES), 1) < HEAD_DIM
    zero_b = jnp.zeros((CHUNK, LANES), BF16)
    for hp in range(SGU_W // LANES):
        wl = jnp.where(causal, sguw_ref[hp], 0.0).astype(BF16)
        bias = sgub_ref[hp]
        for ci in range(ts // CHUNK):
            rs = slice(ci * CHUNK, (ci + 1) * CHUNK)
            vc = vnb[rs, hp * LANES:(hp + 1) * LANES]
            rhs = jnp.concatenate([jnp.where(lo_lanes, vc, zero_b),
                                   jnp.where(lo_lanes, zero_b, vc)], axis=0)
            mixed_c = jnp.dot(wl, rhs, preferred_element_type=F32) + bias
            y_scr[rs, YOFF_SGU + hp * LANES:YOFF_SGU + (hp + 1) * LANES] = (
                u[rs, hp * LANES:(hp + 1) * LANES] * mixed_c).astype(BF16)

    yo = jnp.dot(y_scr[...], wout_ref[...], preferred_element_type=F32)
    x1 = _layer_norm(alpha * x + g1 * yo, ln1g_ref[...], ln1b_ref[...])
    x1_ref[0] = x1
    h2b = (x1 * mul2 + sh2).astype(BF16)
    h2p_ref[...] = _pack_bf16_pairs(h2b.astype(F32))

    logits = jnp.dot(h2b, wr_ref[...], preferred_element_type=F32) + br_ref[...]
    lt = logits.T[0:N_EXPERTS, :]
    ex = jnp.exp(lt - jnp.max(lt, axis=0, keepdims=True))
    probs = ex / jnp.sum(ex, axis=0, keepdims=True)
    sub_iota = lax.broadcasted_iota(jnp.int32, (EXPERTS_PER_GROUP, ts), 0).astype(F32)
    best = None
    for gi in range(N_GROUPS):
        m1, i1, m2, i2 = _top2_of_group(
            probs[gi * EXPERTS_PER_GROUP:(gi + 1) * EXPERTS_PER_GROUP, :], sub_iota)
        score = m1 + m2
        cand = (score, m1, m2, i1 + float(gi * EXPERTS_PER_GROUP), i2 + float(gi * EXPERTS_PER_GROUP))
        if best is None:
            best = cand
        else:
            better = score > best[0]
            best = tuple(jnp.where(better, c, o) for c, o in zip(cand, best))
    _, p1, p2, e1, e2 = best
    den = p1 + p2
    w1 = p1 / den
    w2 = p2 / den

    wt_ref[...] = jnp.concatenate([w1, w2, jnp.zeros((LANES - TOP_K, ts), F32)], axis=0).T
    e_ref[...] = jnp.concatenate([e1, e2], axis=0).astype(jnp.int32)

    ex_iota = lax.broadcasted_iota(jnp.int32, (N_EXPERTS, ts), 0).astype(F32)
    oh1 = ex_iota == e1
    oh2 = ex_iota == e2
    either = jnp.where(oh1 | oh2, 1.0, 0.0)
    seen = jnp.dot(either.astype(BF16), tri_ref[...], preferred_element_type=F32) + cnt_scr[:, 0:1]
    r1 = jnp.sum(jnp.where(oh1, seen, 0.0), axis=0, keepdims=True)
    r2 = jnp.sum(jnp.where(oh2, seen, 0.0), axis=0, keepdims=True)
    rank_ref[...] = jnp.concatenate([r1, r2], axis=0).astype(jnp.int32)
    cnt_new = cnt_scr[...] + jnp.sum(either, axis=1, keepdims=True)
    cnt_scr[...] = cnt_new
    cnt_ref[...] = cnt_new


def _mixer_call(x, ada_l, w_in, w_out, conv_w, conv_b, poolw_bd, pool_scale, ln_g, ln_b,
                sguw_pair, sgub_pair, ln1_g, ln1_b, wr_pad, br_pad, tri, alpha, prev=None):
    bsz, s, d = x.shape
    ts = SEQ_TILE
    nt = s // ts
    n_tiles = bsz * nt
    n_tok = bsz * s
    fused = prev is not None
    const2 = lambda i: (0, 0)
    const3 = lambda i: (0, 0, 0)
    seq_of = lambda i: (i // nt, 0, 0)
    tile3 = lambda i: (i // nt, i % nt, 0)
    tile2 = lambda i: (i, 0)
    by_token = lambda i: (0, i)
    once = pl.Buffered(1)
    in_specs = [
        pl.BlockSpec((1, ADA_CHUNKS, d), seq_of),
        pl.BlockSpec((d, D_IN), const2, pipeline_mode=once),
        pl.BlockSpec((d, d), const2, pipeline_mode=once),
        pl.BlockSpec((CONV_K, CONV_W), const2),
        pl.BlockSpec((1, CONV_W), const2),
        pl.BlockSpec((POOL_W, POOL_W), const2),
        pl.BlockSpec((1, POOL_W), const2),
        pl.BlockSpec((1, SGU_W), const2),
        pl.BlockSpec((1, SGU_W), const2),
        pl.BlockSpec((SGU_W // LANES, CHUNK, 2 * CHUNK), const3),
        pl.BlockSpec((SGU_W // LANES, CHUNK, LANES), const3),
        pl.BlockSpec((1, d), const2),
        pl.BlockSpec((1, d), const2),
        pl.BlockSpec((d, LANES), const2),
        pl.BlockSpec((1, LANES), const2),
        pl.BlockSpec((ts, ts), const2, pipeline_mode=once),
    ]
    operands = [ada_l, w_in, w_out, conv_w, conv_b, poolw_bd, pool_scale, ln_g, ln_b,
                sguw_pair, sgub_pair, ln1_g, ln1_b, wr_pad, br_pad, tri]
    if fused:
        ya, wt_tok, ada_prev, ln2_g, ln2_b = prev
        in_specs = [
            pl.BlockSpec((ts, d), tile2),
            pl.BlockSpec((ts, PACK_W), tile2),
            pl.BlockSpec((ts, PACK_W), lambda i: (n_tiles + i, 0)),
            pl.BlockSpec((ts, LANES), tile2),
            pl.BlockSpec((1, ADA_CHUNKS, d), seq_of),
            pl.BlockSpec((1, d), const2),
            pl.BlockSpec((1, d), const2),
        ] + in_specs
        operands = [x.reshape(n_tok, d), ya, ya, wt_tok, ada_prev, ln2_g, ln2_b] + operands
    else:
        in_specs = [pl.BlockSpec((1, ts, d), tile3)] + in_specs
        operands = [x] + operands
    kern = functools.partial(_mixer_kernel, ts=ts, alpha=alpha, nt=nt, fused=fused)
    pool_rows = POOL_SLACK + POOL_HALO + ts
    return pl.pallas_call(
        kern,
        grid=(n_tiles,),
        in_specs=in_specs,
        out_specs=[
            pl.BlockSpec((1, ts, d), tile3),
            pl.BlockSpec((ts, PACK_W), tile2),
            pl.BlockSpec((TOP_K, ts), by_token),
            pl.BlockSpec((ts, LANES), tile2),
            pl.BlockSpec((TOP_K, ts), by_token),
            pl.BlockSpec((N_EXPERTS, LANES), const2),
        ],
        out_shape=[
            jax.ShapeDtypeStruct((bsz, s, d), F32),
            jax.ShapeDtypeStruct((n_tok, PACK_W), jnp.int32),
            jax.ShapeDtypeStruct((TOP_K, n_tok), jnp.int32),
            jax.ShapeDtypeStruct((n_tok, LANES), F32),
            jax.ShapeDtypeStruct((TOP_K, n_tok), jnp.int32),
            jax.ShapeDtypeStruct((N_EXPERTS, LANES), F32),
        ],
        scratch_shapes=[
            pltpu.VMEM((CONV_HALO + ts, CONV_W), F32),
            pltpu.VMEM((pool_rows, POOL_W), F32),
            pltpu.VMEM((pool_rows, POOL_W), F32),
            pltpu.VMEM((pool_rows, POOL_W), F32),
            pltpu.VMEM((ts, d), BF16),
            pltpu.VMEM((N_EXPERTS, LANES), F32),
        ],
        compiler_params=pltpu.CompilerParams(
            dimension_semantics=("arbitrary",), vmem_limit_bytes=VMEM_LIMIT),
        name="mixer_router",
    )(*operands)


def _sc_worker_chunks(n_chunks):
    per_worker = n_chunks // SC_WORKERS
    worker = lax.axis_index("s") * 2 + lax.axis_index("c")
    return worker * per_worker, per_worker


def _dispatch_call(h2p, dest, n_rows):
    n_tok, width = h2p.shape
    n_chunks = n_tok // SC_ROWS
    assert n_chunks % SC_WORKERS == 0

    def body(h_hbm, dest_hbm, xs_hbm, rows_v, idx_v):
        first, per_worker = _sc_worker_chunks(n_chunks)

        @pl.loop(0, per_worker)
        def _(c):
            chunk = first + c
            pltpu.sync_copy(h_hbm.at[pl.ds(chunk * SC_ROWS, SC_ROWS)], rows_v)
            for k in range(TOP_K):
                pltpu.sync_copy(dest_hbm.at[k, chunk], idx_v.at[k])
                pltpu.sync_copy(rows_v, xs_hbm.at[idx_v.at[k]])

    return pl.kernel(
        body,
        out_type=jax.ShapeDtypeStruct((n_rows, width), jnp.int32),
        mesh=plsc.VectorSubcoreMesh(core_axis_name="c", subcore_axis_name="s"),
        scratch_types=[pltpu.VMEM((SC_ROWS, width), jnp.int32), pltpu.VMEM((TOP_K, SC_ROWS), jnp.int32)],
        name="dispatch",
    )(h2p, dest)


def _collect_call(ys, dest, n_tok):
    width = ys.shape[1]
    n_chunks = n_tok // SC_ROWS
    assert n_chunks % SC_WORKERS == 0

    def body(ys_hbm, dest_hbm, ya_hbm, rows_v, idx_v):
        first, per_worker = _sc_worker_chunks(n_chunks)

        @pl.loop(0, per_worker)
        def _(c):
            chunk = first + c
            for k in range(TOP_K):
                pltpu.sync_copy(dest_hbm.at[k, chunk], idx_v.at[k])
                pltpu.sync_copy(ys_hbm.at[idx_v.at[k]], rows_v)
                pltpu.sync_copy(rows_v, ya_hbm.at[pl.ds(k * n_tok + chunk * SC_ROWS, SC_ROWS)])

    return pl.kernel(
        body,
        out_type=jax.ShapeDtypeStruct((TOP_K * n_tok, width), jnp.int32),
        mesh=plsc.VectorSubcoreMesh(core_axis_name="c", subcore_axis_name="s"),
        scratch_types=[pltpu.VMEM((SC_ROWS, width), jnp.int32), pltpu.VMEM((TOP_K, SC_ROWS), jnp.int32)],
        name="collect",
    )(ys, dest)


def _ffn_kernel(b0_ref, nb_ref, rows_ref, tot_ref, xs_ref, wg_ref, wu_ref, wd_ref, ys_ref,
                xbuf, obuf, wbuf_g, wbuf_u, wbuf_d, wg_scr, wu_scr, wd_scr, gsem, osem, wsem,
                *, layer, n_blocks):
    e = pl.program_id(0)
    n_exp = pl.num_programs(0)
    b0 = b0_ref[e]
    nb = nb_ref[e]
    n_used = tot_ref[0]

    def in_copy(g):
        slot = g % GATHER_DEPTH
        return pltpu.make_async_copy(_rows(xs_ref, g * ROW_BLOCK, ROW_BLOCK), xbuf.at[slot], gsem.at[slot])

    def out_copy(g, slot):
        return pltpu.make_async_copy(obuf.at[slot], _rows(ys_ref, g * ROW_BLOCK, ROW_BLOCK), osem.at[slot])

    def weight_copies(expert, slot):
        return [pltpu.make_async_copy(src.at[layer, expert], dst.at[slot], wsem.at[slot])
                for src, dst in ((wg_ref, wbuf_g), (wu_ref, wbuf_u), (wd_ref, wbuf_d))]

    @pl.when(e == 0)
    def _():
        for first in range(WEIGHT_DEPTH - 1):
            for cp in weight_copies(first, first):
                cp.start(priority=WEIGHT_DMA_PRIORITY)
        for ahead in range(GATHER_DEPTH - 1):
            @pl.when(ahead < n_used)
            def _():
                in_copy(ahead).start()

    @pl.when(e + (WEIGHT_DEPTH - 1) < n_exp)
    def _():
        for cp in weight_copies(e + (WEIGHT_DEPTH - 1), (e + (WEIGHT_DEPTH - 1)) % WEIGHT_DEPTH):
            cp.start(priority=WEIGHT_DMA_PRIORITY)

    wslot = e % WEIGHT_DEPTH
    for cp in weight_copies(e, wslot):
        cp.wait()

    @pl.when(nb > 0)
    def _():
        wg_scr[...] = wbuf_g[wslot].astype(BF16)
        wu_scr[...] = wbuf_u[wslot].astype(BF16)
        wd_scr[...] = wbuf_d[wslot].astype(BF16)

    def block(k, carry):
        g = b0 + k
        islot = g % GATHER_DEPTH
        oslot = g % 2

        @pl.when(g + (GATHER_DEPTH - 1) < n_used)
        def _():
            in_copy(g + (GATHER_DEPTH - 1)).start()

        in_copy(g).wait()

        @pl.when(g >= 2)
        def _():
            out_copy(g - 2, oslot).wait()

        def chain(half):
            rows = pl.ds(half * SUB_BLOCK, SUB_BLOCK)
            xb = jnp.concatenate(_unpack_bf16_pairs(xbuf[islot, rows, :]), axis=1)
            gate = jnp.dot(xb, wg_scr[...], preferred_element_type=F32)
            up = jnp.dot(xb, wu_scr[...], preferred_element_type=F32)
            act = (gate * jax.nn.sigmoid(gate) * up).astype(BF16)
            y = jnp.dot(act, wd_scr[...], preferred_element_type=F32)
            obuf[oslot, rows, :] = _pack_bf16_pairs(y.astype(BF16).astype(F32))

        both = rows_ref[e] - k * ROW_BLOCK > SUB_BLOCK

        @pl.when(both)
        def _():
            chain(0)
            chain(1)

        @pl.when(jnp.logical_not(both))
        def _():
            chain(0)
            obuf[oslot, pl.ds(SUB_BLOCK, SUB_BLOCK), :] = jnp.zeros((SUB_BLOCK, PACK_W), jnp.int32)

        out_copy(g, oslot).start()
        return carry

    lax.fori_loop(0, nb, block, 0)

    @pl.when(e == n_exp - 1)
    def _():
        @pl.when(n_used >= 2)
        def _():
            out_copy(n_used - 2, n_used % 2).wait()

        out_copy(n_used - 1, (n_used - 1) % 2).wait()
        obuf[0] = jnp.zeros((ROW_BLOCK, PACK_W), jnp.int32)

        def fill(g, carry):
            out_copy(g, 0).start()
            return carry

        def drain(g, carry):
            out_copy(g, 0).wait()
            return carry

        lax.fori_loop(n_used, n_blocks, fill, 0)
        lax.fori_loop(n_used, n_blocks, drain, 0)


def _ffn_call(layer, blk_start, seg_blocks, seg_rows, n_used, n_blocks, xs, w_gate, w_up, w_down):
    d, f = w_gate.shape[-2:]
    any_spec = pl.BlockSpec(memory_space=pl.ANY)
    return pl.pallas_call(
        functools.partial(_ffn_kernel, layer=layer, n_blocks=n_blocks),
        grid_spec=pltpu.PrefetchScalarGridSpec(
            num_scalar_prefetch=4,
            grid=(N_EXPERTS,),
            in_specs=[any_spec, any_spec, any_spec, any_spec],
            out_specs=any_spec,
            scratch_shapes=[
                pltpu.VMEM((GATHER_DEPTH, ROW_BLOCK, PACK_W), jnp.int32),
                pltpu.VMEM((2, ROW_BLOCK, PACK_W), jnp.int32),
                pltpu.VMEM((WEIGHT_DEPTH, d, f), F32),
                pltpu.VMEM((WEIGHT_DEPTH, d, f), F32),
                pltpu.VMEM((WEIGHT_DEPTH, f, d), F32),
                pltpu.VMEM((d, f), BF16),
                pltpu.VMEM((d, f), BF16),
                pltpu.VMEM((f, d), BF16),
                pltpu.SemaphoreType.DMA((GATHER_DEPTH,)),
                pltpu.SemaphoreType.DMA((2,)),
                pltpu.SemaphoreType.DMA((WEIGHT_DEPTH,)),
            ],
        ),
        out_shape=jax.ShapeDtypeStruct((n_blocks * ROW_BLOCK, PACK_W), jnp.int32),
        compiler_params=pltpu.CompilerParams(
            dimension_semantics=("arbitrary",), vmem_limit_bytes=VMEM_LIMIT),
        name="expert_ffn",
    )(blk_start, seg_blocks, seg_rows, n_used, xs, w_gate, w_up, w_down)


def _combine_kernel(ya0_ref, ya1_ref, wt_ref, x1_ref, ada_ref, g_ref, b_ref, o_ref, *, alpha):
    o_ref[...] = _combine_tile(ya0_ref, ya1_ref, wt_ref, x1_ref[...], ada_ref[0][5:6], g_ref, b_ref, alpha=alpha)


def _combine_call(ya, wt_tok, x1_flat, ada_l, ln_g, ln_b, alpha, seq):
    n_tok, d = x1_flat.shape
    ts = SEQ_TILE
    n_tiles = n_tok // ts
    per_seq = seq // ts
    return pl.pallas_call(
        functools.partial(_combine_kernel, alpha=alpha),
        grid=(n_tiles,),
        in_specs=[
            pl.BlockSpec((ts, PACK_W), lambda i: (i, 0)),
            pl.BlockSpec((ts, PACK_W), lambda i: (n_tiles + i, 0)),
            pl.BlockSpec((ts, LANES), lambda i: (i, 0)),
            pl.BlockSpec((ts, d), lambda i: (i, 0)),
            pl.BlockSpec((1, ADA_CHUNKS, d), lambda i: (i // per_seq, 0, 0)),
            pl.BlockSpec((1, d), lambda i: (0, 0)),
            pl.BlockSpec((1, d), lambda i: (0, 0)),
        ],
        out_specs=pl.BlockSpec((ts, d), lambda i: (i, 0)),
        out_shape=jax.ShapeDtypeStruct((n_tok, d), F32),
        compiler_params=pltpu.CompilerParams(
            dimension_semantics=("arbitrary",), vmem_limit_bytes=VMEM_LIMIT),
        name="combine_ln",
    )(ya, ya, wt_tok, x1_flat, ada_l, ln_g, ln_b)


def kernel(x, c, w_ada, b_ada, w_in, conv_w, conv_b, pool_w, pool_scale, sgu_ln_g, sgu_ln_b,
           sgu_w, sgu_b, w_out, ln1_g, ln1_b, w_router, b_router, w_gate, w_up, w_down,
           ln2_g, ln2_b):
    bsz, seq, d = x.shape
    depth = w_ada.shape[0]
    n_tok = bsz * seq
    alpha = (2 * depth) ** 0.25
    assert d == D_MODEL and seq % SEQ_TILE == 0 and SEQ_TILE % CHUNK == 0

    ada = _ada_call(c, w_ada, b_ada).reshape(depth, bsz, ADA_CHUNKS, d)

    w_in_b = w_in.astype(BF16)
    w_out_b = w_out.astype(BF16)
    eye_g = jnp.eye(POOL_W // HEAD_DIM, dtype=F32)
    poolw_bd = jnp.einsum('lgcd,gh->lgchd', pool_w, eye_g).reshape(depth, POOL_W, POOL_W).astype(BF16)
    n_pairs = SGU_W // LANES
    sguw_pair = sgu_w.reshape(depth, n_pairs, 2, CHUNK, CHUNK).transpose(0, 1, 3, 2, 4).reshape(
        depth, n_pairs, CHUNK, 2 * CHUNK)
    sgub_pair = jnp.repeat(sgu_b.transpose(0, 2, 1), HEAD_DIM, axis=-1).reshape(
        depth, CHUNK, n_pairs, LANES).transpose(0, 2, 1, 3)
    wr_pad = jnp.pad(w_router, ((0, 0), (0, LANES - N_EXPERTS))).astype(BF16)
    br_pad = jnp.pad(b_router, (0, LANES - N_EXPERTS)).reshape(1, LANES)

    n_steps = (n_tok * TOP_K) // ROW_BLOCK + N_EXPERTS
    experts = jnp.arange(N_EXPERTS, dtype=jnp.int32)
    tri = jnp.triu(jnp.ones((SEQ_TILE, SEQ_TILE), BF16), k=1)

    prev = None
    for l in range(depth):
        x1, h2p, e_idx, wts, rank, counts = _mixer_call(
            x, ada[l], w_in_b[l], w_out_b[l], conv_w[l], conv_b[l].reshape(1, -1), poolw_bd[l],
            pool_scale[l].reshape(1, -1), sgu_ln_g[l].reshape(1, -1), sgu_ln_b[l].reshape(1, -1),
            sguw_pair[l], sgub_pair[l], ln1_g[l].reshape(1, -1), ln1_b[l].reshape(1, -1),
            wr_pad, br_pad, tri, alpha, prev)
        sizes = counts[:, 0].astype(jnp.int32)
        seg_blocks = (sizes + ROW_BLOCK - 1) // ROW_BLOCK
        blk_end = jnp.cumsum(seg_blocks)
        blk_start = (blk_end - seg_blocks).astype(jnp.int32)
        n_used = blk_end[-1:].astype(jnp.int32)
        seg_row0 = jnp.sum(jnp.where(e_idx[:, :, None] == experts[None, None, :],
                                     (blk_start * ROW_BLOCK)[None, None, :], 0), axis=2)
        dest = (seg_row0 + rank).astype(jnp.int32).reshape(TOP_K, n_tok // SC_ROWS, SC_ROWS)

        xs = _dispatch_call(h2p, dest, n_steps * ROW_BLOCK)
        ys = _ffn_call(l, blk_start, seg_blocks.astype(jnp.int32), sizes, n_used, n_steps,
                       xs, w_gate, w_up, w_down)
        ya = _collect_call(ys, dest, n_tok)
        prev = (ya, wts, ada[l], ln2_g[l].reshape(1, -1), ln2_b[l].reshape(1, -1))
        x = x1
    ya, wt_tok, ada_last, ln2_g_last, ln2_b_last = prev
    return _combine_call(ya, wt_tok, x.reshape(n_tok, d), ada_last, ln2_g_last, ln2_b_last,
                         alpha, seq).reshape(bsz, seq, d)
```

```python
import functools

import jax
import jax.numpy as jnp
from jax import lax
from jax.experimental import pallas as pl
from jax.experimental.pallas import tpu as pltpu
from jax.experimental.pallas import tpu_sc as plsc

D_MODEL = 1024
HEAD_DIM = D_MODEL // 16
CONV_W = 6 * HEAD_DIM
POOL_W = 4 * HEAD_DIM
SGU_W = 6 * HEAD_DIM
D_IN = 3 * CONV_W + POOL_W + 2 * SGU_W
CONV_K = 3
POOL_WINDOWS = (2, 4, 8, 16)
CHUNK = 128
N_EXPERTS = 32
N_GROUPS = 4
EXPERTS_PER_GROUP = N_EXPERTS // N_GROUPS
TOP_K = 2
D_FF = D_MODEL // 2
ADA_CHUNKS = 6
LN_EPS = 1e-5

OFF_GB = 0
OFF_GC = CONV_W
OFF_P = 3 * CONV_W
OFF_U = OFF_P + POOL_W
OFF_V = OFF_U + SGU_W
YOFF_POOL = CONV_W
YOFF_SGU = CONV_W + POOL_W

LANES = 128
SUBLANES = 8
CONV_HALO = 8
POOL_HALO = 16
POOL_SLACK = 8
SEQ_TILE = 1024
SUB_BLOCK = 256
ROW_BLOCK = 2 * SUB_BLOCK
ADA_TILE = 1536
GATHER_DEPTH = 3
WEIGHT_DEPTH = 3
WEIGHT_DMA_PRIORITY = 1
VMEM_LIMIT = 60 * 1024 * 1024

PACK_W = D_MODEL // 2
SC_ROWS = 128
SC_WORKERS = 32
N_MIXER_STREAMS = 5
TOKEN_PARTS = 2

F32 = jnp.float32
BF16 = jnp.bfloat16


def _layer_norm(r, g, b):
    mu = jnp.mean(r, axis=-1, keepdims=True)
    d = r - mu
    var = jnp.mean(d * d, axis=-1, keepdims=True)
    return d * lax.rsqrt(var + LN_EPS) * g + b


def _rows(ref, start, size):
    return ref.at[pl.ds(pl.multiple_of(start, SUBLANES), size), :]


def _pack_bf16_pairs(v):
    bits = pltpu.bitcast(v, jnp.int32)
    return bits[:, 0:PACK_W] | lax.shift_right_logical(bits[:, PACK_W:2 * PACK_W], 16)


def _unpack_pairs_f32(words):
    return jnp.concatenate([pltpu.bitcast(words & jnp.int32(-65536), F32),
                            pltpu.bitcast(lax.shift_left(words, 16), F32)], axis=1)


def _unpack_bf16_pairs(words):
    first = pltpu.bitcast(words & jnp.int32(-65536), F32).astype(BF16)
    second = pltpu.bitcast(lax.shift_left(words, 16), F32).astype(BF16)
    return first, second


def _ada_kernel(c_ref, w_ref, b_ref, o_ref):
    c = c_ref[...]
    c_act = (c * jax.nn.sigmoid(c)).astype(BF16)
    col = pl.program_id(1) * ADA_TILE + lax.broadcasted_iota(jnp.int32, (1, ADA_TILE), 1)
    chunk = col // D_MODEL
    one = jnp.where((chunk == 1) | (chunk == 4), 1.0, 0.0)
    o_ref[0] = jnp.dot(c_act, w_ref[0].astype(BF16), preferred_element_type=F32) + (b_ref[0] + one)


def _ada_call(c, w_ada, b_ada):
    depth, d, n = w_ada.shape
    bsz = c.shape[0]
    return pl.pallas_call(
        _ada_kernel,
        grid=(depth, n // ADA_TILE),
        in_specs=[
            pl.BlockSpec((bsz, d), lambda l, j: (0, 0)),
            pl.BlockSpec((1, d, ADA_TILE), lambda l, j: (l, 0, j)),
            pl.BlockSpec((1, 1, ADA_TILE), lambda l, j: (l, 0, j)),
        ],
        out_specs=pl.BlockSpec((1, bsz, ADA_TILE), lambda l, j: (l, 0, j)),
        out_shape=jax.ShapeDtypeStruct((depth, bsz, n), F32),
        compiler_params=pltpu.CompilerParams(
            dimension_semantics=("arbitrary", "arbitrary"), vmem_limit_bytes=VMEM_LIMIT),
        name="ada",
    )(c, w_ada, b_ada.reshape(depth, 1, n))


def _top2_of_group(pg, sub_iota):
    big = float(EXPERTS_PER_GROUP)
    m1 = jnp.max(pg, axis=0, keepdims=True)
    i1 = jnp.min(jnp.where(pg == m1, sub_iota, big), axis=0, keepdims=True)
    rest = jnp.where(sub_iota == i1, -1.0, pg)
    m2 = jnp.max(rest, axis=0, keepdims=True)
    i2 = jnp.min(jnp.where(rest == m2, sub_iota, big), axis=0, keepdims=True)
    return m1, i1, m2, i2


def _combine_tile(ya0_ref, ya1_ref, wt_ref, x1, gate, g_ref, b_ref, *, alpha):
    wt = wt_ref[...]
    y = wt[:, 0:1] * _unpack_pairs_f32(ya0_ref[...]) + wt[:, 1:2] * _unpack_pairs_f32(ya1_ref[...])
    return _layer_norm(alpha * x1 + gate * y, g_ref[...], b_ref[...])


def _mixer_kernel(*refs, ts, alpha, nt, fused, continued):
    if fused:
        xprev_ref, ya0_ref, ya1_ref, wtp_ref, adap_ref, ln2g_ref, ln2b_ref = refs[:7]
        refs = refs[7:]
    else:
        x_ref, refs = refs[0], refs[1:]
    (ada_ref, win_ref, wout_ref, convw_ref, convb_ref, poolw_ref, pscale_ref, lng_ref, lnb_ref,
     sguw_ref, sgub_ref, ln1g_ref, ln1b_ref, wr_ref, br_ref, tri_ref) = refs[:16]
    refs = refs[16:]
    if continued:
        cnt0_ref, refs = refs[N_MIXER_STREAMS], refs[N_MIXER_STREAMS + 1:]
    (x1_ref, h2p_ref, e_ref, wt_ref, rank_ref, cnt_ref,
     gx_scr, p_scr, q_scr, r_scr, y_scr, cnt_scr) = refs

    i = pl.program_id(0)
    j = i % nt

    @pl.when(i == 0)
    def _():
        gx_scr[0:CONV_HALO, :] = jnp.zeros((CONV_HALO, CONV_W), F32)
        for scr in (p_scr, q_scr, r_scr):
            scr[0:POOL_SLACK + POOL_HALO, :] = jnp.zeros((POOL_SLACK + POOL_HALO, POOL_W), F32)
        cnt_scr[...] = cnt0_ref[...] if continued else jnp.zeros_like(cnt_scr)

    if fused:
        x = _combine_tile(ya0_ref, ya1_ref, wtp_ref, xprev_ref[...], adap_ref[0][5:6], ln2g_ref, ln2b_ref,
                          alpha=alpha)
    else:
        x = x_ref[0]
    _mixer_tile(j, x, ada_ref, win_ref, wout_ref, convw_ref, convb_ref, poolw_ref, pscale_ref,
                lng_ref, lnb_ref, sguw_ref, sgub_ref, ln1g_ref, ln1b_ref, wr_ref, br_ref, tri_ref,
                x1_ref, h2p_ref, e_ref, wt_ref, rank_ref, cnt_ref, gx_scr, p_scr, q_scr, r_scr, y_scr, cnt_scr,
                ts=ts, alpha=alpha)


def _mixer_tile(j, x, ada_ref, win_ref, wout_ref, convw_ref, convb_ref, poolw_ref, pscale_ref,
                lng_ref, lnb_ref, sguw_ref, sgub_ref, ln1g_ref, ln1b_ref, wr_ref, br_ref, tri_ref,
                x1_ref, h2p_ref, e_ref, wt_ref, rank_ref, cnt_ref, gx_scr, p_scr, q_scr, r_scr, y_scr, cnt_scr,
                *, ts, alpha):
    first_of_seq = j == 0
    ada = ada_ref[0]
    sh1, mul1, g1 = ada[0:1], ada[1:2], ada[2:3]
    sh2, mul2 = ada[3:4], ada[4:5]
    hb = (x * mul1 + sh1).astype(BF16)

    def proj(lo, hi):
        return jnp.dot(hb, win_ref[:, lo:hi], preferred_element_type=F32)

    gcxc = proj(OFF_GC, OFF_P)
    g = gcxc[:, :CONV_W] * gcxc[:, CONV_W:]
    gx_scr[CONV_HALO:CONV_HALO + ts, :] = g
    cw = convw_ref[...]
    gx_scr[0:CONV_HALO, :] = jnp.where(first_of_seq, 0.0, gx_scr[0:CONV_HALO, :])
    conv = (cw[0:1] * gx_scr[CONV_HALO - 2:CONV_HALO - 2 + ts, :]
            + cw[1:2] * gx_scr[CONV_HALO - 1:CONV_HALO - 1 + ts, :]
            + cw[2:3] * g + convb_ref[...])
    gx_scr[0:CONV_HALO, :] = g[ts - CONV_HALO:ts, :]
    y_scr[:, 0:CONV_W] = (proj(OFF_GB, OFF_GC) * conv).astype(BF16)

    p = proj(OFF_P, OFF_U)
    top = POOL_SLACK
    n_ext = POOL_HALO + ts
    p_scr[top + POOL_HALO:top + n_ext, :] = p
    p_scr[top:top + POOL_HALO, :] = jnp.where(first_of_seq, 0.0, p_scr[top:top + POOL_HALO, :])
    w2 = p_scr[top:top + n_ext, :] + p_scr[top - 1:top - 1 + n_ext, :]
    q_scr[top:top + n_ext, :] = w2
    w4 = w2 + q_scr[top - 2:top - 2 + n_ext, :]
    r_scr[top:top + n_ext, :] = w4
    w8 = w4 + r_scr[top - 4:top - 4 + n_ext, :]
    q_scr[top:top + n_ext, :] = w8
    w16 = w8 + q_scr[top - 8:top - 8 + n_ext, :]
    s2, s4 = w2[POOL_HALO:, 0:LANES], w4[POOL_HALO:, 0:LANES]
    s8, s16 = w8[POOL_HALO:, LANES:2 * LANES], w16[POOL_HALO:, LANES:2 * LANES]
    p_scr[top:top + POOL_HALO, :] = p[ts - POOL_HALO:ts, :]
    lane = lax.broadcasted_iota(jnp.int32, (ts, LANES), 1)
    tpos = (lax.broadcasted_iota(jnp.int32, (ts, LANES), 0) + (j * ts + 1)).astype(F32)
    lo_half = lane < HEAD_DIM
    cnt_a = jnp.minimum(tpos, jnp.where(lo_half, float(POOL_WINDOWS[0]), float(POOL_WINDOWS[1])))
    cnt_b = jnp.minimum(tpos, jnp.where(lo_half, float(POOL_WINDOWS[2]), float(POOL_WINDOWS[3])))
    pooled = jnp.concatenate([jnp.where(lo_half, s2, s4) / cnt_a,
                              jnp.where(lo_half, s8, s16) / cnt_b], axis=1) - p
    mixed = jnp.dot(pooled.astype(BF16), poolw_ref[...], preferred_element_type=F32)
    y_scr[:, YOFF_POOL:YOFF_POOL + POOL_W] = (mixed * pscale_ref[...]).astype(BF16)

    v = proj(OFF_V, D_IN)
    vnb = _layer_norm(v, lng_ref[...], lnb_ref[...]).astype(BF16)
    u = proj(OFF_U, OFF_V)
    row_c = lax.broadcasted_iota(jnp.int32, (CHUNK, 2 * CHUNK), 0)
    col_c = lax.broadcasted_iota(jnp.int32, (CHUNK, 2 * CHUNK), 1)
    causal = (col_c & (CHUNK - 1)) <= row_c
    lo_lanes = lax.broadcasted_iota(jnp.int32, (CHUNK, LANES), 1) < HEAD_DIM
    zero_b = jnp.zeros((CHUNK, LANES), BF16)
    for hp in range(SGU_W // LANES):
        wl = jnp.where(causal, sguw_ref[hp], 0.0).astype(BF16)
        bias = sgub_ref[hp]
        for ci in range(ts // CHUNK):
            rs = slice(ci * CHUNK, (ci + 1) * CHUNK)
            vc = vnb[rs, hp * LANES:(hp + 1) * LANES]
            rhs = jnp.concatenate([jnp.where(lo_lanes, vc, zero_b),
                                   jnp.where(lo_lanes, zero_b, vc)], axis=0)
            mixed_c = jnp.dot(wl, rhs, preferred_element_type=F32) + bias
            y_scr[rs, YOFF_SGU + hp * LANES:YOFF_SGU + (hp + 1) * LANES] = (
                u[rs, hp * LANES:(hp + 1) * LANES] * mixed_c).astype(BF16)

    yo = jnp.dot(y_scr[...], wout_ref[...], preferred_element_type=F32)
    x1 = _layer_norm(alpha * x + g1 * yo, ln1g_ref[...], ln1b_ref[...])
    x1_ref[0] = x1
    h2b = (x1 * mul2 + sh2).astype(BF16)
    h2p_ref[...] = _pack_bf16_pairs(h2b.astype(F32))

    logits = jnp.dot(h2b, wr_ref[...], preferred_element_type=F32) + br_ref[...]
    lt = logits.T[0:N_EXPERTS, :]
    ex = jnp.exp(lt - jnp.max(lt, axis=0, keepdims=True))
    probs = ex / jnp.sum(ex, axis=0, keepdims=True)
    sub_iota = lax.broadcasted_iota(jnp.int32, (EXPERTS_PER_GROUP, ts), 0).astype(F32)
    best = None
    for gi in range(N_GROUPS):
        m1, i1, m2, i2 = _top2_of_group(
            probs[gi * EXPERTS_PER_GROUP:(gi + 1) * EXPERTS_PER_GROUP, :], sub_iota)
        score = m1 + m2
        cand = (score, m1, m2, i1 + float(gi * EXPERTS_PER_GROUP), i2 + float(gi * EXPERTS_PER_GROUP))
        if best is None:
            best = cand
        else:
            better = score > best[0]
            best = tuple(jnp.where(better, c, o) for c, o in zip(cand, best))
    _, p1, p2, e1, e2 = best
    den = p1 + p2
    w1 = p1 / den
    w2 = p2 / den

    wt_ref[...] = jnp.concatenate([w1, w2], axis=0)
    e_ref[...] = jnp.concatenate([e1, e2], axis=0).astype(jnp.int32)

    ex_iota = lax.broadcasted_iota(jnp.int32, (N_EXPERTS, ts), 0).astype(F32)
    oh1 = ex_iota == e1
    oh2 = ex_iota == e2
    either = jnp.where(oh1 | oh2, 1.0, 0.0)
    seen = jnp.dot(either.astype(BF16), tri_ref[...], preferred_element_type=F32) + cnt_scr[:, 0:1]
    r1 = jnp.sum(jnp.where(oh1, seen, 0.0), axis=0, keepdims=True)
    r2 = jnp.sum(jnp.where(oh2, seen, 0.0), axis=0, keepdims=True)
    rank_ref[...] = jnp.concatenate([r1, r2], axis=0).astype(jnp.int32)
    cnt_new = cnt_scr[...] + jnp.sum(either, axis=1, keepdims=True)
    cnt_scr[...] = cnt_new
    cnt_ref[...] = cnt_new


def _mixer_call(x, ada_l, w_in, w_out, conv_w, conv_b, poolw_bd, pool_scale, ln_g, ln_b,
                sguw_pair, sgub_pair, ln1_g, ln1_b, wr_pad, br_pad, tri, alpha, prev=None,
                seq0=0, n_seq=None, earlier=None):
    bsz, s, d = x.shape
    ts = SEQ_TILE
    nt = s // ts
    n_seq = bsz if n_seq is None else n_seq
    n_tiles = n_seq * nt
    tile0 = seq0 * nt
    n_tok = bsz * s
    fused = prev is not None
    continued = earlier is not None
    const2 = lambda i: (0, 0)
    const3 = lambda i: (0, 0, 0)
    seq_of = lambda i: (seq0 + i // nt, 0, 0)
    tile3 = lambda i: (seq0 + i // nt, i % nt, 0)
    tile2 = lambda i: (tile0 + i, 0)
    by_token = lambda i: (0, tile0 + i)
    once = pl.Buffered(1)
    in_specs = [
        pl.BlockSpec((1, ADA_CHUNKS, d), seq_of),
        pl.BlockSpec((d, D_IN), const2, pipeline_mode=once),
        pl.BlockSpec((d, d), const2, pipeline_mode=once),
        pl.BlockSpec((CONV_K, CONV_W), const2),
        pl.BlockSpec((1, CONV_W), const2),
        pl.BlockSpec((POOL_W, POOL_W), const2),
        pl.BlockSpec((1, POOL_W), const2),
        pl.BlockSpec((1, SGU_W), const2),
        pl.BlockSpec((1, SGU_W), const2),
        pl.BlockSpec((SGU_W // LANES, CHUNK, 2 * CHUNK), const3),
        pl.BlockSpec((SGU_W // LANES, CHUNK, LANES), const3),
        pl.BlockSpec((1, d), const2),
        pl.BlockSpec((1, d), const2),
        pl.BlockSpec((d, LANES), const2),
        pl.BlockSpec((1, LANES), const2),
        pl.BlockSpec((ts, ts), const2, pipeline_mode=once),
    ]
    operands = [ada_l, w_in, w_out, conv_w, conv_b, poolw_bd, pool_scale, ln_g, ln_b,
                sguw_pair, sgub_pair, ln1_g, ln1_b, wr_pad, br_pad, tri]
    if fused:
        ya, wt_tok, ada_prev, ln2_g, ln2_b = prev
        in_specs = [
            pl.BlockSpec((ts, d), tile2),
            pl.BlockSpec((ts, PACK_W), lambda i: (i, 0)),
            pl.BlockSpec((ts, PACK_W), lambda i: (n_tiles + i, 0)),
            pl.BlockSpec((ts, TOP_K), tile2),
            pl.BlockSpec((1, ADA_CHUNKS, d), seq_of),
            pl.BlockSpec((1, d), const2),
            pl.BlockSpec((1, d), const2),
        ] + in_specs
        operands = [x.reshape(n_tok, d), ya, ya, wt_tok, ada_prev, ln2_g, ln2_b] + operands
    else:
        in_specs = [pl.BlockSpec((1, ts, d), tile3)] + in_specs
        operands = [x] + operands
    aliases = {}
    if continued:
        aliases = {len(operands) + k: k for k in range(N_MIXER_STREAMS)}
        in_specs = in_specs + [pl.BlockSpec(memory_space=pl.ANY)] * N_MIXER_STREAMS + [
            pl.BlockSpec((N_EXPERTS, LANES), const2)]
        operands = operands + list(earlier)
    kern = functools.partial(_mixer_kernel, ts=ts, alpha=alpha, nt=nt, fused=fused, continued=continued)
    pool_rows = POOL_SLACK + POOL_HALO + ts
    return pl.pallas_call(
        kern,
        grid=(n_tiles,),
        in_specs=in_specs,
        input_output_aliases=aliases,
        out_specs=[
            pl.BlockSpec((1, ts, d), tile3),
            pl.BlockSpec((ts, PACK_W), tile2),
            pl.BlockSpec((TOP_K, ts), by_token),
            pl.BlockSpec((TOP_K, ts), by_token),
            pl.BlockSpec((TOP_K, ts), by_token),
            pl.BlockSpec((N_EXPERTS, LANES), const2),
        ],
        out_shape=[
            jax.ShapeDtypeStruct((bsz, s, d), F32),
            jax.ShapeDtypeStruct((n_tok, PACK_W), jnp.int32),
            jax.ShapeDtypeStruct((TOP_K, n_tok), jnp.int32),
            jax.ShapeDtypeStruct((TOP_K, n_tok), F32),
            jax.ShapeDtypeStruct((TOP_K, n_tok), jnp.int32),
            jax.ShapeDtypeStruct((N_EXPERTS, LANES), F32),
        ],
        scratch_shapes=[
            pltpu.VMEM((CONV_HALO + ts, CONV_W), F32),
            pltpu.VMEM((pool_rows, POOL_W), F32),
            pltpu.VMEM((pool_rows, POOL_W), F32),
            pltpu.VMEM((pool_rows, POOL_W), F32),
            pltpu.VMEM((ts, d), BF16),
            pltpu.VMEM((N_EXPERTS, LANES), F32),
        ],
        compiler_params=pltpu.CompilerParams(
            dimension_semantics=("arbitrary",), vmem_limit_bytes=VMEM_LIMIT),
        name="mixer_router",
    )(*operands)


def _sc_worker_chunks(n_chunks):
    per_worker = n_chunks // SC_WORKERS
    worker = lax.axis_index("s") * 2 + lax.axis_index("c")
    return worker * per_worker, per_worker


def _dispatch_call(h2p, dest, n_rows):
    n_tok, width = h2p.shape
    n_chunks = n_tok // SC_ROWS
    assert n_chunks % SC_WORKERS == 0

    def body(h_hbm, dest_hbm, xs_hbm, rows_v, idx_v):
        first, per_worker = _sc_worker_chunks(n_chunks)

        @pl.loop(0, per_worker)
        def _(c):
            chunk = first + c
            pltpu.sync_copy(h_hbm.at[pl.ds(chunk * SC_ROWS, SC_ROWS)], rows_v)
            for k in range(TOP_K):
                pltpu.sync_copy(dest_hbm.at[k, chunk], idx_v.at[k])
                pltpu.sync_copy(rows_v, xs_hbm.at[idx_v.at[k]])

    return pl.kernel(
        body,
        out_type=jax.ShapeDtypeStruct((n_rows, width), jnp.int32),
        mesh=plsc.VectorSubcoreMesh(core_axis_name="c", subcore_axis_name="s"),
        scratch_types=[pltpu.VMEM((SC_ROWS, width), jnp.int32), pltpu.VMEM((TOP_K, SC_ROWS), jnp.int32)],
        name="dispatch",
    )(h2p, dest)


def _collect_call(ys, dest, tok0, n_tok):
    width = ys.shape[1]
    n_chunks = n_tok // SC_ROWS
    chunk0 = tok0 // SC_ROWS
    assert n_chunks % SC_WORKERS == 0 and tok0 % SC_ROWS == 0

    def body(ys_hbm, dest_hbm, ya_hbm, rows_v, idx_v):
        first, per_worker = _sc_worker_chunks(n_chunks)

        @pl.loop(0, per_worker)
        def _(c):
            chunk = first + c
            for k in range(TOP_K):
                pltpu.sync_copy(dest_hbm.at[k, chunk0 + chunk], idx_v.at[k])
                pltpu.sync_copy(ys_hbm.at[idx_v.at[k]], rows_v)
                pltpu.sync_copy(rows_v, ya_hbm.at[pl.ds(k * n_tok + chunk * SC_ROWS, SC_ROWS)])

    return pl.kernel(
        body,
        out_type=jax.ShapeDtypeStruct((TOP_K * n_tok, width), jnp.int32),
        mesh=plsc.VectorSubcoreMesh(core_axis_name="c", subcore_axis_name="s"),
        scratch_types=[pltpu.VMEM((SC_ROWS, width), jnp.int32), pltpu.VMEM((TOP_K, SC_ROWS), jnp.int32)],
        name="collect",
    )(ys, dest)


def _ffn_kernel(b0_ref, nb_ref, rows_ref, tot_ref, xs_ref, wg_ref, wu_ref, wd_ref, ys_ref,
                xbuf, obuf, wbuf_g, wbuf_u, wbuf_d, wg_scr, wu_scr, wd_scr, gsem, osem, wsem,
                *, layer, n_blocks):
    e = pl.program_id(0)
    n_exp = pl.num_programs(0)
    b0 = b0_ref[e]
    nb = nb_ref[e]
    n_used = tot_ref[0]

    def in_copy(g):
        slot = g % GATHER_DEPTH
        return pltpu.make_async_copy(_rows(xs_ref, g * ROW_BLOCK, ROW_BLOCK), xbuf.at[slot], gsem.at[slot])

    def out_copy(g, slot):
        return pltpu.make_async_copy(obuf.at[slot], _rows(ys_ref, g * ROW_BLOCK, ROW_BLOCK), osem.at[slot])

    def weight_copies(expert, slot):
        return [pltpu.make_async_copy(src.at[layer, expert], dst.at[slot], wsem.at[slot])
                for src, dst in ((wg_ref, wbuf_g), (wu_ref, wbuf_u), (wd_ref, wbuf_d))]

    @pl.when(e == 0)
    def _():
        for first in range(WEIGHT_DEPTH - 1):
            for cp in weight_copies(first, first):
                cp.start(priority=WEIGHT_DMA_PRIORITY)
        for ahead in range(GATHER_DEPTH - 1):
            @pl.when(ahead < n_used)
            def _():
                in_copy(ahead).start()

    @pl.when(e + (WEIGHT_DEPTH - 1) < n_exp)
    def _():
        for cp in weight_copies(e + (WEIGHT_DEPTH - 1), (e + (WEIGHT_DEPTH - 1)) % WEIGHT_DEPTH):
            cp.start(priority=WEIGHT_DMA_PRIORITY)

    wslot = e % WEIGHT_DEPTH
    for cp in weight_copies(e, wslot):
        cp.wait()

    @pl.when(nb > 0)
    def _():
        wg_scr[...] = wbuf_g[wslot].astype(BF16)
        wu_scr[...] = wbuf_u[wslot].astype(BF16)
        wd_scr[...] = wbuf_d[wslot].astype(BF16)

    def block(k, carry):
        g = b0 + k
        islot = g % GATHER_DEPTH
        oslot = g % 2

        @pl.when(g + (GATHER_DEPTH - 1) < n_used)
        def _():
            in_copy(g + (GATHER_DEPTH - 1)).start()

        in_copy(g).wait()

        @pl.when(g >= 2)
        def _():
            out_copy(g - 2, oslot).wait()

        def chain(half):
            rows = pl.ds(half * SUB_BLOCK, SUB_BLOCK)
            xb = jnp.concatenate(_unpack_bf16_pairs(xbuf[islot, rows, :]), axis=1)
            gate = jnp.dot(xb, wg_scr[...], preferred_element_type=F32)
            up = jnp.dot(xb, wu_scr[...], preferred_element_type=F32)
            act = (gate * jax.nn.sigmoid(gate) * up).astype(BF16)
            y = jnp.dot(act, wd_scr[...], preferred_element_type=F32)
            obuf[oslot, rows, :] = _pack_bf16_pairs(y.astype(BF16).astype(F32))

        both = rows_ref[e] - k * ROW_BLOCK > SUB_BLOCK

        @pl.when(both)
        def _():
            chain(0)
            chain(1)

        @pl.when(jnp.logical_not(both))
        def _():
            chain(0)
            obuf[oslot, pl.ds(SUB_BLOCK, SUB_BLOCK), :] = jnp.zeros((SUB_BLOCK, PACK_W), jnp.int32)

        out_copy(g, oslot).start()
        return carry

    lax.fori_loop(0, nb, block, 0)

    @pl.when(e == n_exp - 1)
    def _():
        @pl.when(n_used >= 2)
        def _():
            out_copy(n_used - 2, n_used % 2).wait()

        out_copy(n_used - 1, (n_used - 1) % 2).wait()
        obuf[0] = jnp.zeros((ROW_BLOCK, PACK_W), jnp.int32)

        def fill(g, carry):
            out_copy(g, 0).start()
            return carry

        def drain(g, carry):
            out_copy(g, 0).wait()
            return carry

        lax.fori_loop(n_used, n_blocks, fill, 0)
        lax.fori_loop(n_used, n_blocks, drain, 0)


def _ffn_call(layer, blk_start, seg_blocks, seg_rows, n_used, n_blocks, xs, w_gate, w_up, w_down):
    d, f = w_gate.shape[-2:]
    any_spec = pl.BlockSpec(memory_space=pl.ANY)
    return pl.pallas_call(
        functools.partial(_ffn_kernel, layer=layer, n_blocks=n_blocks),
        grid_spec=pltpu.PrefetchScalarGridSpec(
            num_scalar_prefetch=4,
            grid=(N_EXPERTS,),
            in_specs=[any_spec, any_spec, any_spec, any_spec],
            out_specs=any_spec,
            scratch_shapes=[
                pltpu.VMEM((GATHER_DEPTH, ROW_BLOCK, PACK_W), jnp.int32),
                pltpu.VMEM((2, ROW_BLOCK, PACK_W), jnp.int32),
                pltpu.VMEM((WEIGHT_DEPTH, d, f), F32),
                pltpu.VMEM((WEIGHT_DEPTH, d, f), F32),
                pltpu.VMEM((WEIGHT_DEPTH, f, d), F32),
                pltpu.VMEM((d, f), BF16),
                pltpu.VMEM((d, f), BF16),
                pltpu.VMEM((f, d), BF16),
                pltpu.SemaphoreType.DMA((GATHER_DEPTH,)),
                pltpu.SemaphoreType.DMA((2,)),
                pltpu.SemaphoreType.DMA((WEIGHT_DEPTH,)),
            ],
        ),
        out_shape=jax.ShapeDtypeStruct((n_blocks * ROW_BLOCK, PACK_W), jnp.int32),
        compiler_params=pltpu.CompilerParams(
            dimension_semantics=("arbitrary",), vmem_limit_bytes=VMEM_LIMIT),
        name="expert_ffn",
    )(blk_start, seg_blocks, seg_rows, n_used, xs, w_gate, w_up, w_down)


def _combine_kernel(ya0_ref, ya1_ref, wt_ref, x1_ref, ada_ref, g_ref, b_ref, *rest, alpha):
    o_ref = rest[-1]
    o_ref[...] = _combine_tile(ya0_ref, ya1_ref, wt_ref, x1_ref[...], ada_ref[0][5:6], g_ref, b_ref, alpha=alpha)


def _combine_call(ya, wt_tok, x1_flat, ada_l, ln_g, ln_b, alpha, seq, tok0, earlier=None):
    n_tok, d = x1_flat.shape
    ts = SEQ_TILE
    n_tiles = ya.shape[0] // (TOP_K * ts)
    tile0 = tok0 // ts
    per_seq = seq // ts
    tile = lambda i: (tile0 + i, 0)
    in_specs = [
        pl.BlockSpec((ts, PACK_W), lambda i: (i, 0)),
        pl.BlockSpec((ts, PACK_W), lambda i: (n_tiles + i, 0)),
        pl.BlockSpec((ts, TOP_K), tile),
        pl.BlockSpec((ts, d), tile),
        pl.BlockSpec((1, ADA_CHUNKS, d), lambda i: ((tile0 + i) // per_seq, 0, 0)),
        pl.BlockSpec((1, d), lambda i: (0, 0)),
        pl.BlockSpec((1, d), lambda i: (0, 0)),
    ]
    operands = [ya, ya, wt_tok, x1_flat, ada_l, ln_g, ln_b]
    aliases = {}
    if earlier is not None:
        aliases = {len(operands): 0}
        in_specs.append(pl.BlockSpec(memory_space=pl.ANY))
        operands.append(earlier)
    return pl.pallas_call(
        functools.partial(_combine_kernel, alpha=alpha),
        grid=(n_tiles,),
        in_specs=in_specs,
        out_specs=pl.BlockSpec((ts, d), tile),
        out_shape=jax.ShapeDtypeStruct((n_tok, d), F32),
        input_output_aliases=aliases,
        compiler_params=pltpu.CompilerParams(
            dimension_semantics=("arbitrary",), vmem_limit_bytes=VMEM_LIMIT),
        name="combine_ln",
    )(*operands)


def kernel(x, c, w_ada, b_ada, w_in, conv_w, conv_b, pool_w, pool_scale, sgu_ln_g, sgu_ln_b,
           sgu_w, sgu_b, w_out, ln1_g, ln1_b, w_router, b_router, w_gate, w_up, w_down,
           ln2_g, ln2_b):
    bsz, seq, d = x.shape
    depth = w_ada.shape[0]
    n_tok = bsz * seq
    alpha = (2 * depth) ** 0.25
    assert d == D_MODEL and seq % SEQ_TILE == 0 and SEQ_TILE % CHUNK == 0

    ada = _ada_call(c, w_ada, b_ada).reshape(depth, bsz, ADA_CHUNKS, d)

    w_in_b = w_in.astype(BF16)
    w_out_b = w_out.astype(BF16)
    eye_g = jnp.eye(POOL_W // HEAD_DIM, dtype=F32)
    poolw_bd = jnp.einsum('lgcd,gh->lgchd', pool_w, eye_g).reshape(depth, POOL_W, POOL_W).astype(BF16)
    n_pairs = SGU_W // LANES
    sguw_pair = sgu_w.reshape(depth, n_pairs, 2, CHUNK, CHUNK).transpose(0, 1, 3, 2, 4).reshape(
        depth, n_pairs, CHUNK, 2 * CHUNK)
    sgub_pair = jnp.repeat(sgu_b.transpose(0, 2, 1), HEAD_DIM, axis=-1).reshape(
        depth, CHUNK, n_pairs, LANES).transpose(0, 2, 1, 3)
    wr_pad = jnp.pad(w_router, ((0, 0), (0, LANES - N_EXPERTS))).astype(BF16)
    br_pad = jnp.pad(b_router, (0, LANES - N_EXPERTS)).reshape(1, LANES)

    n_steps = (n_tok * TOP_K) // ROW_BLOCK + N_EXPERTS
    experts = jnp.arange(N_EXPERTS, dtype=jnp.int32)
    tri = jnp.triu(jnp.ones((SEQ_TILE, SEQ_TILE), BF16), k=1)

    assert bsz % TOKEN_PARTS == 0
    part_seqs = bsz // TOKEN_PARTS
    part_tok = part_seqs * seq
    prev = None
    for l in range(depth):
        layer_args = (
            x, ada[l], w_in_b[l], w_out_b[l], conv_w[l], conv_b[l].reshape(1, -1), poolw_bd[l],
            pool_scale[l].reshape(1, -1), sgu_ln_g[l].reshape(1, -1), sgu_ln_b[l].reshape(1, -1),
            sguw_pair[l], sgub_pair[l], ln1_g[l].reshape(1, -1), ln1_b[l].reshape(1, -1),
            wr_pad, br_pad, tri, alpha)
        if prev is None:
            outs = _mixer_call(*layer_args)
        else:
            ya_parts, *prev_rest = prev
            outs = None
            for p in range(TOKEN_PARTS):
                outs = _mixer_call(*layer_args, prev=(ya_parts[p], *prev_rest),
                                   seq0=p * part_seqs, n_seq=part_seqs, earlier=outs)
        x1, h2p, e_idx, wts, rank, counts = outs
        sizes = counts[:, 0].astype(jnp.int32)
        seg_blocks = (sizes + ROW_BLOCK - 1) // ROW_BLOCK
        blk_end = jnp.cumsum(seg_blocks)
        blk_start = (blk_end - seg_blocks).astype(jnp.int32)
        n_used = blk_end[-1:].astype(jnp.int32)
        seg_row0 = jnp.sum(jnp.where(e_idx[:, :, None] == experts[None, None, :],
                                     (blk_start * ROW_BLOCK)[None, None, :], 0), axis=2)
        dest = (seg_row0 + rank).astype(jnp.int32).reshape(TOP_K, n_tok // SC_ROWS, SC_ROWS)

        xs = _dispatch_call(h2p, dest, n_steps * ROW_BLOCK)
        ys = _ffn_call(l, blk_start, seg_blocks.astype(jnp.int32), sizes, n_used, n_steps,
                       xs, w_gate, w_up, w_down)
        ya_parts = [_collect_call(ys, dest, p * part_tok, part_tok) for p in range(TOKEN_PARTS)]
        wt_tok = wts.T
        prev = (ya_parts, wt_tok, ada[l], ln2_g[l].reshape(1, -1), ln2_b[l].reshape(1, -1))
        x = x1
    ya_parts, wt_tok, ada_last, ln2_g_last, ln2_b_last = prev
    out = None
    for p in range(TOKEN_PARTS):
        out = _combine_call(ya_parts[p], wt_tok, x.reshape(n_tok, d), ada_last, ln2_g_last, ln2_b_last,
                            alpha, seq, p * part_tok, earlier=out)
    return out.reshape(bsz, seq, d)
```

```python
import functools

import jax
import jax.numpy as jnp
from jax import lax
from jax.experimental import pallas as pl
from jax.experimental.pallas import tpu as pltpu
from jax.experimental.pallas import tpu_sc as plsc

D_MODEL = 1024
HEAD_DIM = D_MODEL // 16
CONV_W = 6 * HEAD_DIM
POOL_W = 4 * HEAD_DIM
SGU_W = 6 * HEAD_DIM
D_IN = 3 * CONV_W + POOL_W + 2 * SGU_W
CONV_K = 3
POOL_WINDOWS = (2, 4, 8, 16)
CHUNK = 128
N_EXPERTS = 32
N_GROUPS = 4
EXPERTS_PER_GROUP = N_EXPERTS // N_GROUPS
TOP_K = 2
D_FF = D_MODEL // 2
ADA_CHUNKS = 6
LN_EPS = 1e-5

OFF_GB = 0
OFF_GC = CONV_W
OFF_P = 3 * CONV_W
OFF_U = OFF_P + POOL_W
OFF_V = OFF_U + SGU_W
YOFF_POOL = CONV_W
YOFF_SGU = CONV_W + POOL_W

LANES = 128
SUBLANES = 8
CONV_HALO = 8
POOL_HALO = 16
POOL_SLACK = 8
SEQ_TILE = 1024
SUB_BLOCK = 256
ROW_BLOCK = 2 * SUB_BLOCK
ADA_TILE = 1536
GATHER_DEPTH = 3
WEIGHT_DEPTH = 3
WEIGHT_DMA_PRIORITY = 1
VMEM_LIMIT = 60 * 1024 * 1024

PACK_W = D_MODEL // 2
SC_ROWS = 128
SC_WORKERS = 32
N_MIXER_STREAMS = 5
FIRST_PART_DIVISOR = 4

F32 = jnp.float32
BF16 = jnp.bfloat16


def _layer_norm(r, g, b):
    mu = jnp.mean(r, axis=-1, keepdims=True)
    d = r - mu
    var = jnp.mean(d * d, axis=-1, keepdims=True)
    return d * lax.rsqrt(var + LN_EPS) * g + b


def _rows(ref, start, size):
    return ref.at[pl.ds(pl.multiple_of(start, SUBLANES), size), :]


def _pack_bf16_pairs(v):
    bits = pltpu.bitcast(v, jnp.int32)
    return bits[:, 0:PACK_W] | lax.shift_right_logical(bits[:, PACK_W:2 * PACK_W], 16)


def _unpack_pairs_f32(words):
    return jnp.concatenate([pltpu.bitcast(words & jnp.int32(-65536), F32),
                            pltpu.bitcast(lax.shift_left(words, 16), F32)], axis=1)


def _unpack_bf16_pairs(words):
    first = pltpu.bitcast(words & jnp.int32(-65536), F32).astype(BF16)
    second = pltpu.bitcast(lax.shift_left(words, 16), F32).astype(BF16)
    return first, second


def _ada_kernel(c_ref, w_ref, b_ref, o_ref):
    c = c_ref[...]
    c_act = (c * jax.nn.sigmoid(c)).astype(BF16)
    col = pl.program_id(1) * ADA_TILE + lax.broadcasted_iota(jnp.int32, (1, ADA_TILE), 1)
    chunk = col // D_MODEL
    one = jnp.where((chunk == 1) | (chunk == 4), 1.0, 0.0)
    o_ref[0] = jnp.dot(c_act, w_ref[0].astype(BF16), preferred_element_type=F32) + (b_ref[0] + one)


def _ada_call(c, w_ada, b_ada):
    depth, d, n = w_ada.shape
    bsz = c.shape[0]
    return pl.pallas_call(
        _ada_kernel,
        grid=(depth, n // ADA_TILE),
        in_specs=[
            pl.BlockSpec((bsz, d), lambda l, j: (0, 0)),
            pl.BlockSpec((1, d, ADA_TILE), lambda l, j: (l, 0, j)),
            pl.BlockSpec((1, 1, ADA_TILE), lambda l, j: (l, 0, j)),
        ],
        out_specs=pl.BlockSpec((1, bsz, ADA_TILE), lambda l, j: (l, 0, j)),
        out_shape=jax.ShapeDtypeStruct((depth, bsz, n), F32),
        compiler_params=pltpu.CompilerParams(
            dimension_semantics=("arbitrary", "arbitrary"), vmem_limit_bytes=VMEM_LIMIT),
        name="ada",
    )(c, w_ada, b_ada.reshape(depth, 1, n))


def _top2_of_group(pg, sub_iota):
    big = float(EXPERTS_PER_GROUP)
    m1 = jnp.max(pg, axis=0, keepdims=True)
    i1 = jnp.min(jnp.where(pg == m1, sub_iota, big), axis=0, keepdims=True)
    rest = jnp.where(sub_iota == i1, -1.0, pg)
    m2 = jnp.max(rest, axis=0, keepdims=True)
    i2 = jnp.min(jnp.where(rest == m2, sub_iota, big), axis=0, keepdims=True)
    return m1, i1, m2, i2


def _combine_tile(ya0_ref, ya1_ref, wt_ref, x1, gate, g_ref, b_ref, *, alpha):
    wt = wt_ref[...]
    y = wt[:, 0:1] * _unpack_pairs_f32(ya0_ref[...]) + wt[:, 1:2] * _unpack_pairs_f32(ya1_ref[...])
    return _layer_norm(alpha * x1 + gate * y, g_ref[...], b_ref[...])


def _mixer_kernel(*refs, ts, alpha, nt, fused, continued):
    if fused:
        xprev_ref, ya0_ref, ya1_ref, wtp_ref, adap_ref, ln2g_ref, ln2b_ref = refs[:7]
        refs = refs[7:]
    else:
        x_ref, refs = refs[0], refs[1:]
    (ada_ref, win_ref, wout_ref, convw_ref, convb_ref, poolw_ref, pscale_ref, lng_ref, lnb_ref,
     sguw_ref, sgub_ref, ln1g_ref, ln1b_ref, wr_ref, br_ref, tri_ref) = refs[:16]
    refs = refs[16:]
    if continued:
        cnt0_ref, refs = refs[N_MIXER_STREAMS], refs[N_MIXER_STREAMS + 1:]
    (x1_ref, h2p_ref, e_ref, wt_ref, rank_ref, cnt_ref,
     gx_scr, p_scr, q_scr, r_scr, y_scr, cnt_scr) = refs

    i = pl.program_id(0)
    j = i % nt

    @pl.when(i == 0)
    def _():
        gx_scr[0:CONV_HALO, :] = jnp.zeros((CONV_HALO, CONV_W), F32)
        for scr in (p_scr, q_scr, r_scr):
            scr[0:POOL_SLACK + POOL_HALO, :] = jnp.zeros((POOL_SLACK + POOL_HALO, POOL_W), F32)
        cnt_scr[...] = cnt0_ref[...] if continued else jnp.zeros_like(cnt_scr)

    if fused:
        x = _combine_tile(ya0_ref, ya1_ref, wtp_ref, xprev_ref[...], adap_ref[0][5:6], ln2g_ref, ln2b_ref,
                          alpha=alpha)
    else:
        x = x_ref[0]
    _mixer_tile(j, x, ada_ref, win_ref, wout_ref, convw_ref, convb_ref, poolw_ref, pscale_ref,
                lng_ref, lnb_ref, sguw_ref, sgub_ref, ln1g_ref, ln1b_ref, wr_ref, br_ref, tri_ref,
                x1_ref, h2p_ref, e_ref, wt_ref, rank_ref, cnt_ref, gx_scr, p_scr, q_scr, r_scr, y_scr, cnt_scr,
                ts=ts, alpha=alpha)


def _mixer_tile(j, x, ada_ref, win_ref, wout_ref, convw_ref, convb_ref, poolw_ref, pscale_ref,
                lng_ref, lnb_ref, sguw_ref, sgub_ref, ln1g_ref, ln1b_ref, wr_ref, br_ref, tri_ref,
                x1_ref, h2p_ref, e_ref, wt_ref, rank_ref, cnt_ref, gx_scr, p_scr, q_scr, r_scr, y_scr, cnt_scr,
                *, ts, alpha):
    first_of_seq = j == 0
    ada = ada_ref[0]
    sh1, mul1, g1 = ada[0:1], ada[1:2], ada[2:3]
    sh2, mul2 = ada[3:4], ada[4:5]
    hb = (x * mul1 + sh1).astype(BF16)

    def proj(lo, hi):
        return jnp.dot(hb, win_ref[:, lo:hi], preferred_element_type=F32)

    gcxc = proj(OFF_GC, OFF_P)
    g = gcxc[:, :CONV_W] * gcxc[:, CONV_W:]
    gx_scr[CONV_HALO:CONV_HALO + ts, :] = g
    cw = convw_ref[...]
    gx_scr[0:CONV_HALO, :] = jnp.where(first_of_seq, 0.0, gx_scr[0:CONV_HALO, :])
    conv = (cw[0:1] * gx_scr[CONV_HALO - 2:CONV_HALO - 2 + ts, :]
            + cw[1:2] * gx_scr[CONV_HALO - 1:CONV_HALO - 1 + ts, :]
            + cw[2:3] * g + convb_ref[...])
    gx_scr[0:CONV_HALO, :] = g[ts - CONV_HALO:ts, :]
    y_scr[:, 0:CONV_W] = (proj(OFF_GB, OFF_GC) * conv).astype(BF16)

    p = proj(OFF_P, OFF_U)
    top = POOL_SLACK
    n_ext = POOL_HALO + ts
    p_scr[top + POOL_HALO:top + n_ext, :] = p
    p_scr[top:top + POOL_HALO, :] = jnp.where(first_of_seq, 0.0, p_scr[top:top + POOL_HALO, :])
    w2 = p_scr[top:top + n_ext, :] + p_scr[top - 1:top - 1 + n_ext, :]
    q_scr[top:top + n_ext, :] = w2
    w4 = w2 + q_scr[top - 2:top - 2 + n_ext, :]
    r_scr[top:top + n_ext, :] = w4
    w8 = w4 + r_scr[top - 4:top - 4 + n_ext, :]
    q_scr[top:top + n_ext, :] = w8
    w16 = w8 + q_scr[top - 8:top - 8 + n_ext, :]
    s2, s4 = w2[POOL_HALO:, 0:LANES], w4[POOL_HALO:, 0:LANES]
    s8, s16 = w8[POOL_HALO:, LANES:2 * LANES], w16[POOL_HALO:, LANES:2 * LANES]
    p_scr[top:top + POOL_HALO, :] = p[ts - POOL_HALO:ts, :]
    lane = lax.broadcasted_iota(jnp.int32, (ts, LANES), 1)
    tpos = (lax.broadcasted_iota(jnp.int32, (ts, LANES), 0) + (j * ts + 1)).astype(F32)
    lo_half = lane < HEAD_DIM
    cnt_a = jnp.minimum(tpos, jnp.where(lo_half, float(POOL_WINDOWS[0]), float(POOL_WINDOWS[1])))
    cnt_b = jnp.minimum(tpos, jnp.where(lo_half, float(POOL_WINDOWS[2]), float(POOL_WINDOWS[3])))
    pooled = jnp.concatenate([jnp.where(lo_half, s2, s4) / cnt_a,
                              jnp.where(lo_half, s8, s16) / cnt_b], axis=1) - p
    mixed = jnp.dot(pooled.astype(BF16), poolw_ref[...], preferred_element_type=F32)
    y_scr[:, YOFF_POOL:YOFF_POOL + POOL_W] = (mixed * pscale_ref[...]).astype(BF16)

    v = proj(OFF_V, D_IN)
    vnb = _layer_norm(v, lng_ref[...], lnb_ref[...]).astype(BF16)
    u = proj(OFF_U, OFF_V)
    row_c = lax.broadcasted_iota(jnp.int32, (CHUNK, 2 * CHUNK), 0)
    col_c = lax.broadcasted_iota(jnp.int32, (CHUNK, 2 * CHUNK), 1)
    causal = (col_c & (CHUNK - 1)) <= row_c
    lo_lanes = lax.broadcasted_iota(jnp.int32, (CHUNK, LANES), 1) < HEAD_DIM
    zero_b = jnp.zeros((CHUNK, LANES), BF16)
    for hp in range(SGU_W // LANES):
        wl = jnp.where(causal, sguw_ref[hp], 0.0).astype(BF16)
        bias = sgub_ref[hp]
        for ci in range(ts // CHUNK):
            rs = slice(ci * CHUNK, (ci + 1) * CHUNK)
            vc = vnb[rs, hp * LANES:(hp + 1) * LANES]
            rhs = jnp.concatenate([jnp.where(lo_lanes, vc, zero_b),
                                   jnp.where(lo_lanes, zero_b, vc)], axis=0)
            mixed_c = jnp.dot(wl, rhs, preferred_element_type=F32) + bias
            y_scr[rs, YOFF_SGU + hp * LANES:YOFF_SGU + (hp + 1) * LANES] = (
                u[rs, hp * LANES:(hp + 1) * LANES] * mixed_c).astype(BF16)

    yo = jnp.dot(y_scr[...], wout_ref[...], preferred_element_type=F32)
    x1 = _layer_norm(alpha * x + g1 * yo, ln1g_ref[...], ln1b_ref[...])
    x1_ref[0] = x1
    h2b = (x1 * mul2 + sh2).astype(BF16)
    h2p_ref[...] = _pack_bf16_pairs(h2b.astype(F32))

    logits = jnp.dot(h2b, wr_ref[...], preferred_element_type=F32) + br_ref[...]
    lt = logits.T[0:N_EXPERTS, :]
    ex = jnp.exp(lt - jnp.max(lt, axis=0, keepdims=True))
    probs = ex / jnp.sum(ex, axis=0, keepdims=True)
    sub_iota = lax.broadcasted_iota(jnp.int32, (EXPERTS_PER_GROUP, ts), 0).astype(F32)
    best = None
    for gi in range(N_GROUPS):
        m1, i1, m2, i2 = _top2_of_group(
            probs[gi * EXPERTS_PER_GROUP:(gi + 1) * EXPERTS_PER_GROUP, :], sub_iota)
        score = m1 + m2
        cand = (score, m1, m2, i1 + float(gi * EXPERTS_PER_GROUP), i2 + float(gi * EXPERTS_PER_GROUP))
        if best is None:
            best = cand
        else:
            better = score > best[0]
            best = tuple(jnp.where(better, c, o) for c, o in zip(cand, best))
    _, p1, p2, e1, e2 = best
    den = p1 + p2
    w1 = p1 / den
    w2 = p2 / den

    wt_ref[...] = jnp.concatenate([w1, w2], axis=0)
    e_ref[...] = jnp.concatenate([e1, e2], axis=0).astype(jnp.int32)

    ex_iota = lax.broadcasted_iota(jnp.int32, (N_EXPERTS, ts), 0).astype(F32)
    oh1 = ex_iota == e1
    oh2 = ex_iota == e2
    either = jnp.where(oh1 | oh2, 1.0, 0.0)
    seen = jnp.dot(either.astype(BF16), tri_ref[...], preferred_element_type=F32) + cnt_scr[:, 0:1]
    r1 = jnp.sum(jnp.where(oh1, seen, 0.0), axis=0, keepdims=True)
    r2 = jnp.sum(jnp.where(oh2, seen, 0.0), axis=0, keepdims=True)
    rank_ref[...] = jnp.concatenate([r1, r2], axis=0).astype(jnp.int32)
    cnt_new = cnt_scr[...] + jnp.sum(either, axis=1, keepdims=True)
    cnt_scr[...] = cnt_new
    cnt_ref[...] = cnt_new


def _mixer_call(x, ada_l, w_in, w_out, conv_w, conv_b, poolw_bd, pool_scale, ln_g, ln_b,
                sguw_pair, sgub_pair, ln1_g, ln1_b, wr_pad, br_pad, tri, alpha, prev=None,
                seq0=0, n_seq=None, earlier=None):
    bsz, s, d = x.shape
    ts = SEQ_TILE
    nt = s // ts
    n_seq = bsz if n_seq is None else n_seq
    n_tiles = n_seq * nt
    tile0 = seq0 * nt
    n_tok = bsz * s
    fused = prev is not None
    continued = earlier is not None
    const2 = lambda i: (0, 0)
    const3 = lambda i: (0, 0, 0)
    seq_of = lambda i: (seq0 + i // nt, 0, 0)
    tile3 = lambda i: (seq0 + i // nt, i % nt, 0)
    tile2 = lambda i: (tile0 + i, 0)
    by_token = lambda i: (0, tile0 + i)
    once = pl.Buffered(1)
    in_specs = [
        pl.BlockSpec((1, ADA_CHUNKS, d), seq_of),
        pl.BlockSpec((d, D_IN), const2, pipeline_mode=once),
        pl.BlockSpec((d, d), const2, pipeline_mode=once),
        pl.BlockSpec((CONV_K, CONV_W), const2),
        pl.BlockSpec((1, CONV_W), const2),
        pl.BlockSpec((POOL_W, POOL_W), const2),
        pl.BlockSpec((1, POOL_W), const2),
        pl.BlockSpec((1, SGU_W), const2),
        pl.BlockSpec((1, SGU_W), const2),
        pl.BlockSpec((SGU_W // LANES, CHUNK, 2 * CHUNK), const3),
        pl.BlockSpec((SGU_W // LANES, CHUNK, LANES), const3),
        pl.BlockSpec((1, d), const2),
        pl.BlockSpec((1, d), const2),
        pl.BlockSpec((d, LANES), const2),
        pl.BlockSpec((1, LANES), const2),
        pl.BlockSpec((ts, ts), const2, pipeline_mode=once),
    ]
    operands = [ada_l, w_in, w_out, conv_w, conv_b, poolw_bd, pool_scale, ln_g, ln_b,
                sguw_pair, sgub_pair, ln1_g, ln1_b, wr_pad, br_pad, tri]
    if fused:
        ya, wt_tok, ada_prev, ln2_g, ln2_b = prev
        in_specs = [
            pl.BlockSpec((ts, d), tile2),
            pl.BlockSpec((ts, PACK_W), lambda i: (i, 0)),
            pl.BlockSpec((ts, PACK_W), lambda i: (n_tiles + i, 0)),
            pl.BlockSpec((ts, TOP_K), tile2),
            pl.BlockSpec((1, ADA_CHUNKS, d), seq_of),
            pl.BlockSpec((1, d), const2),
            pl.BlockSpec((1, d), const2),
        ] + in_specs
        operands = [x.reshape(n_tok, d), ya, ya, wt_tok, ada_prev, ln2_g, ln2_b] + operands
    else:
        in_specs = [pl.BlockSpec((1, ts, d), tile3)] + in_specs
        operands = [x] + operands
    aliases = {}
    if continued:
        aliases = {len(operands) + k: k for k in range(N_MIXER_STREAMS)}
        in_specs = in_specs + [pl.BlockSpec(memory_space=pl.ANY)] * N_MIXER_STREAMS + [
            pl.BlockSpec((N_EXPERTS, LANES), const2)]
        operands = operands + list(earlier)
    kern = functools.partial(_mixer_kernel, ts=ts, alpha=alpha, nt=nt, fused=fused, continued=continued)
    pool_rows = POOL_SLACK + POOL_HALO + ts
    return pl.pallas_call(
        kern,
        grid=(n_tiles,),
        in_specs=in_specs,
        input_output_aliases=aliases,
        out_specs=[
            pl.BlockSpec((1, ts, d), tile3),
            pl.BlockSpec((ts, PACK_W), tile2),
            pl.BlockSpec((TOP_K, ts), by_token),
            pl.BlockSpec((TOP_K, ts), by_token),
            pl.BlockSpec((TOP_K, ts), by_token),
            pl.BlockSpec((N_EXPERTS, LANES), const2),
        ],
        out_shape=[
            jax.ShapeDtypeStruct((bsz, s, d), F32),
            jax.ShapeDtypeStruct((n_tok, PACK_W), jnp.int32),
            jax.ShapeDtypeStruct((TOP_K, n_tok), jnp.int32),
            jax.ShapeDtypeStruct((TOP_K, n_tok), F32),
            jax.ShapeDtypeStruct((TOP_K, n_tok), jnp.int32),
            jax.ShapeDtypeStruct((N_EXPERTS, LANES), F32),
        ],
        scratch_shapes=[
            pltpu.VMEM((CONV_HALO + ts, CONV_W), F32),
            pltpu.VMEM((pool_rows, POOL_W), F32),
            pltpu.VMEM((pool_rows, POOL_W), F32),
            pltpu.VMEM((pool_rows, POOL_W), F32),
            pltpu.VMEM((ts, d), BF16),
            pltpu.VMEM((N_EXPERTS, LANES), F32),
        ],
        compiler_params=pltpu.CompilerParams(
            dimension_semantics=("arbitrary",), vmem_limit_bytes=VMEM_LIMIT),
        name="mixer_router",
    )(*operands)


def _sc_worker_chunks(n_chunks):
    per_worker = n_chunks // SC_WORKERS
    worker = lax.axis_index("s") * 2 + lax.axis_index("c")
    return worker * per_worker, per_worker


def _dispatch_call(h2p, dest, n_rows):
    n_tok, width = h2p.shape
    n_chunks = n_tok // SC_ROWS
    assert n_chunks % SC_WORKERS == 0

    def body(h_hbm, dest_hbm, xs_hbm, rows_v, idx_v):
        first, per_worker = _sc_worker_chunks(n_chunks)

        @pl.loop(0, per_worker)
        def _(c):
            chunk = first + c
            pltpu.sync_copy(h_hbm.at[pl.ds(chunk * SC_ROWS, SC_ROWS)], rows_v)
            for k in range(TOP_K):
                pltpu.sync_copy(dest_hbm.at[k, chunk], idx_v.at[k])
                pltpu.sync_copy(rows_v, xs_hbm.at[idx_v.at[k]])

    return pl.kernel(
        body,
        out_type=jax.ShapeDtypeStruct((n_rows, width), jnp.int32),
        mesh=plsc.VectorSubcoreMesh(core_axis_name="c", subcore_axis_name="s"),
        scratch_types=[pltpu.VMEM((SC_ROWS, width), jnp.int32), pltpu.VMEM((TOP_K, SC_ROWS), jnp.int32)],
        name="dispatch",
    )(h2p, dest)


def _collect_call(ys, dest, tok0, n_tok):
    width = ys.shape[1]
    n_chunks = n_tok // SC_ROWS
    chunk0 = tok0 // SC_ROWS
    assert n_chunks % SC_WORKERS == 0 and tok0 % SC_ROWS == 0

    def body(ys_hbm, dest_hbm, ya_hbm, rows_v, idx_v):
        first, per_worker = _sc_worker_chunks(n_chunks)

        @pl.loop(0, per_worker)
        def _(c):
            chunk = first + c
            for k in range(TOP_K):
                pltpu.sync_copy(dest_hbm.at[k, chunk0 + chunk], idx_v.at[k])
                pltpu.sync_copy(ys_hbm.at[idx_v.at[k]], rows_v)
                pltpu.sync_copy(rows_v, ya_hbm.at[pl.ds(k * n_tok + chunk * SC_ROWS, SC_ROWS)])

    return pl.kernel(
        body,
        out_type=jax.ShapeDtypeStruct((TOP_K * n_tok, width), jnp.int32),
        mesh=plsc.VectorSubcoreMesh(core_axis_name="c", subcore_axis_name="s"),
        scratch_types=[pltpu.VMEM((SC_ROWS, width), jnp.int32), pltpu.VMEM((TOP_K, SC_ROWS), jnp.int32)],
        name="collect",
    )(ys, dest)


def _ffn_kernel(b0_ref, nb_ref, rows_ref, tot_ref, xs_ref, wg_ref, wu_ref, wd_ref, ys_ref,
                xbuf, obuf, wbuf_g, wbuf_u, wbuf_d, wg_scr, wu_scr, wd_scr, gsem, osem, wsem,
                *, layer, n_blocks):
    e = pl.program_id(0)
    n_exp = pl.num_programs(0)
    b0 = b0_ref[e]
    nb = nb_ref[e]
    n_used = tot_ref[0]

    def in_copy(g):
        slot = g % GATHER_DEPTH
        return pltpu.make_async_copy(_rows(xs_ref, g * ROW_BLOCK, ROW_BLOCK), xbuf.at[slot], gsem.at[slot])

    def out_copy(g, slot):
        return pltpu.make_async_copy(obuf.at[slot], _rows(ys_ref, g * ROW_BLOCK, ROW_BLOCK), osem.at[slot])

    def weight_copies(expert, slot):
        return [pltpu.make_async_copy(src.at[layer, expert], dst.at[slot], wsem.at[slot])
                for src, dst in ((wg_ref, wbuf_g), (wu_ref, wbuf_u), (wd_ref, wbuf_d))]

    @pl.when(e == 0)
    def _():
        for first in range(WEIGHT_DEPTH - 1):
            for cp in weight_copies(first, first):
                cp.start(priority=WEIGHT_DMA_PRIORITY)
        for ahead in range(GATHER_DEPTH - 1):
            @pl.when(ahead < n_used)
            def _():
                in_copy(ahead).start()

    @pl.when(e + (WEIGHT_DEPTH - 1) < n_exp)
    def _():
        for cp in weight_copies(e + (WEIGHT_DEPTH - 1), (e + (WEIGHT_DEPTH - 1)) % WEIGHT_DEPTH):
            cp.start(priority=WEIGHT_DMA_PRIORITY)

    wslot = e % WEIGHT_DEPTH
    for cp in weight_copies(e, wslot):
        cp.wait()

    @pl.when(nb > 0)
    def _():
        wg_scr[...] = wbuf_g[wslot].astype(BF16)
        wu_scr[...] = wbuf_u[wslot].astype(BF16)
        wd_scr[...] = wbuf_d[wslot].astype(BF16)

    def block(k, carry):
        g = b0 + k
        islot = g % GATHER_DEPTH
        oslot = g % 2

        @pl.when(g + (GATHER_DEPTH - 1) < n_used)
        def _():
            in_copy(g + (GATHER_DEPTH - 1)).start()

        in_copy(g).wait()

        @pl.when(g >= 2)
        def _():
            out_copy(g - 2, oslot).wait()

        def chain(half):
            rows = pl.ds(half * SUB_BLOCK, SUB_BLOCK)
            xb = jnp.concatenate(_unpack_bf16_pairs(xbuf[islot, rows, :]), axis=1)
            gate = jnp.dot(xb, wg_scr[...], preferred_element_type=F32)
            up = jnp.dot(xb, wu_scr[...], preferred_element_type=F32)
            act = (gate * jax.nn.sigmoid(gate) * up).astype(BF16)
            y = jnp.dot(act, wd_scr[...], preferred_element_type=F32)
            obuf[oslot, rows, :] = _pack_bf16_pairs(y.astype(BF16).astype(F32))

        both = rows_ref[e] - k * ROW_BLOCK > SUB_BLOCK

        @pl.when(both)
        def _():
            chain(0)
            chain(1)

        @pl.when(jnp.logical_not(both))
        def _():
            chain(0)
            obuf[oslot, pl.ds(SUB_BLOCK, SUB_BLOCK), :] = jnp.zeros((SUB_BLOCK, PACK_W), jnp.int32)

        out_copy(g, oslot).start()
        return carry

    lax.fori_loop(0, nb, block, 0)

    @pl.when(e == n_exp - 1)
    def _():
        @pl.when(n_used >= 2)
        def _():
            out_copy(n_used - 2, n_used % 2).wait()

        out_copy(n_used - 1, (n_used - 1) % 2).wait()
        obuf[0] = jnp.zeros((ROW_BLOCK, PACK_W), jnp.int32)

        def fill(g, carry):
            out_copy(g, 0).start()
            return carry

        def drain(g, carry):
            out_copy(g, 0).wait()
            return carry

        lax.fori_loop(n_used, n_blocks, fill, 0)
        lax.fori_loop(n_used, n_blocks, drain, 0)


def _ffn_call(layer, blk_start, seg_blocks, seg_rows, n_used, n_blocks, xs, w_gate, w_up, w_down):
    d, f = w_gate.shape[-2:]
    any_spec = pl.BlockSpec(memory_space=pl.ANY)
    return pl.pallas_call(
        functools.partial(_ffn_kernel, layer=layer, n_blocks=n_blocks),
        grid_spec=pltpu.PrefetchScalarGridSpec(
            num_scalar_prefetch=4,
            grid=(N_EXPERTS,),
            in_specs=[any_spec, any_spec, any_spec, any_spec],
            out_specs=any_spec,
            scratch_shapes=[
                pltpu.VMEM((GATHER_DEPTH, ROW_BLOCK, PACK_W), jnp.int32),
                pltpu.VMEM((2, ROW_BLOCK, PACK_W), jnp.int32),
                pltpu.VMEM((WEIGHT_DEPTH, d, f), F32),
                pltpu.VMEM((WEIGHT_DEPTH, d, f), F32),
                pltpu.VMEM((WEIGHT_DEPTH, f, d), F32),
                pltpu.VMEM((d, f), BF16),
                pltpu.VMEM((d, f), BF16),
                pltpu.VMEM((f, d), BF16),
                pltpu.SemaphoreType.DMA((GATHER_DEPTH,)),
                pltpu.SemaphoreType.DMA((2,)),
                pltpu.SemaphoreType.DMA((WEIGHT_DEPTH,)),
            ],
        ),
        out_shape=jax.ShapeDtypeStruct((n_blocks * ROW_BLOCK, PACK_W), jnp.int32),
        compiler_params=pltpu.CompilerParams(
            dimension_semantics=("arbitrary",), vmem_limit_bytes=VMEM_LIMIT),
        name="expert_ffn",
    )(blk_start, seg_blocks, seg_rows, n_used, xs, w_gate, w_up, w_down)


def _combine_kernel(ya0_ref, ya1_ref, wt_ref, x1_ref, ada_ref, g_ref, b_ref, *rest, alpha):
    o_ref = rest[-1]
    o_ref[...] = _combine_tile(ya0_ref, ya1_ref, wt_ref, x1_ref[...], ada_ref[0][5:6], g_ref, b_ref, alpha=alpha)


def _combine_call(ya, wt_tok, x1_flat, ada_l, ln_g, ln_b, alpha, seq, tok0, earlier=None):
    n_tok, d = x1_flat.shape
    ts = SEQ_TILE
    n_tiles = ya.shape[0] // (TOP_K * ts)
    tile0 = tok0 // ts
    per_seq = seq // ts
    tile = lambda i: (tile0 + i, 0)
    in_specs = [
        pl.BlockSpec((ts, PACK_W), lambda i: (i, 0)),
        pl.BlockSpec((ts, PACK_W), lambda i: (n_tiles + i, 0)),
        pl.BlockSpec((ts, TOP_K), tile),
        pl.BlockSpec((ts, d), tile),
        pl.BlockSpec((1, ADA_CHUNKS, d), lambda i: ((tile0 + i) // per_seq, 0, 0)),
        pl.BlockSpec((1, d), lambda i: (0, 0)),
        pl.BlockSpec((1, d), lambda i: (0, 0)),
    ]
    operands = [ya, ya, wt_tok, x1_flat, ada_l, ln_g, ln_b]
    aliases = {}
    if earlier is not None:
        aliases = {len(operands): 0}
        in_specs.append(pl.BlockSpec(memory_space=pl.ANY))
        operands.append(earlier)
    return pl.pallas_call(
        functools.partial(_combine_kernel, alpha=alpha),
        grid=(n_tiles,),
        in_specs=in_specs,
        out_specs=pl.BlockSpec((ts, d), tile),
        out_shape=jax.ShapeDtypeStruct((n_tok, d), F32),
        input_output_aliases=aliases,
        compiler_params=pltpu.CompilerParams(
            dimension_semantics=("arbitrary",), vmem_limit_bytes=VMEM_LIMIT),
        name="combine_ln",
    )(*operands)


def kernel(x, c, w_ada, b_ada, w_in, conv_w, conv_b, pool_w, pool_scale, sgu_ln_g, sgu_ln_b,
           sgu_w, sgu_b, w_out, ln1_g, ln1_b, w_router, b_router, w_gate, w_up, w_down,
           ln2_g, ln2_b):
    bsz, seq, d = x.shape
    depth = w_ada.shape[0]
    n_tok = bsz * seq
    alpha = (2 * depth) ** 0.25
    assert d == D_MODEL and seq % SEQ_TILE == 0 and SEQ_TILE % CHUNK == 0

    ada = _ada_call(c, w_ada, b_ada).reshape(depth, bsz, ADA_CHUNKS, d)

    w_in_b = w_in.astype(BF16)
    w_out_b = w_out.astype(BF16)
    eye_g = jnp.eye(POOL_W // HEAD_DIM, dtype=F32)
    poolw_bd = jnp.einsum('lgcd,gh->lgchd', pool_w, eye_g).reshape(depth, POOL_W, POOL_W).astype(BF16)
    n_pairs = SGU_W // LANES
    sguw_pair = sgu_w.reshape(depth, n_pairs, 2, CHUNK, CHUNK).transpose(0, 1, 3, 2, 4).reshape(
        depth, n_pairs, CHUNK, 2 * CHUNK)
    sgub_pair = jnp.repeat(sgu_b.transpose(0, 2, 1), HEAD_DIM, axis=-1).reshape(
        depth, CHUNK, n_pairs, LANES).transpose(0, 2, 1, 3)
    wr_pad = jnp.pad(w_router, ((0, 0), (0, LANES - N_EXPERTS))).astype(BF16)
    br_pad = jnp.pad(b_router, (0, LANES - N_EXPERTS)).reshape(1, LANES)

    n_steps = (n_tok * TOP_K) // ROW_BLOCK + N_EXPERTS
    experts = jnp.arange(N_EXPERTS, dtype=jnp.int32)
    tri = jnp.triu(jnp.ones((SEQ_TILE, SEQ_TILE), BF16), k=1)

    first_seqs = bsz // FIRST_PART_DIVISOR
    part_seq0 = (0, first_seqs) if first_seqs else (0,)
    part_seqs = (first_seqs, bsz - first_seqs) if first_seqs else (bsz,)
    parts = list(zip(part_seq0, part_seqs))
    prev = None
    for l in range(depth):
        layer_args = (
            x, ada[l], w_in_b[l], w_out_b[l], conv_w[l], conv_b[l].reshape(1, -1), poolw_bd[l],
            pool_scale[l].reshape(1, -1), sgu_ln_g[l].reshape(1, -1), sgu_ln_b[l].reshape(1, -1),
            sguw_pair[l], sgub_pair[l], ln1_g[l].reshape(1, -1), ln1_b[l].reshape(1, -1),
            wr_pad, br_pad, tri, alpha)
        if prev is None:
            outs = _mixer_call(*layer_args)
        else:
            ya_parts, *prev_rest = prev
            outs = None
            for ya_part, (seq0, n_seq) in zip(ya_parts, parts):
                outs = _mixer_call(*layer_args, prev=(ya_part, *prev_rest),
                                   seq0=seq0, n_seq=n_seq, earlier=outs)
        x1, h2p, e_idx, wts, rank, counts = outs
        sizes = counts[:, 0].astype(jnp.int32)
        seg_blocks = (sizes + ROW_BLOCK - 1) // ROW_BLOCK
        blk_end = jnp.cumsum(seg_blocks)
        blk_start = (blk_end - seg_blocks).astype(jnp.int32)
        n_used = blk_end[-1:].astype(jnp.int32)
        seg_row0 = jnp.sum(jnp.where(e_idx[:, :, None] == experts[None, None, :],
                                     (blk_start * ROW_BLOCK)[None, None, :], 0), axis=2)
        dest = (seg_row0 + rank).astype(jnp.int32).reshape(TOP_K, n_tok // SC_ROWS, SC_ROWS)

        xs = _dispatch_call(h2p, dest, n_steps * ROW_BLOCK)
        ys = _ffn_call(l, blk_start, seg_blocks.astype(jnp.int32), sizes, n_used, n_steps,
                       xs, w_gate, w_up, w_down)
        ya_parts = [_collect_call(ys, dest, seq0 * seq, n_seq * seq) for seq0, n_seq in parts]
        wt_tok = wts.T
        prev = (ya_parts, wt_tok, ada[l], ln2_g[l].reshape(1, -1), ln2_b[l].reshape(1, -1))
        x = x1
    ya_parts, wt_tok, ada_last, ln2_g_last, ln2_b_last = prev
    out = None
    for ya_part, (seq0, _) in zip(ya_parts, parts):
        out = _combine_call(ya_part, wt_tok, x.reshape(n_tok, d), ada_last, ln2_g_last, ln2_b_last,
                            alpha, seq, seq0 * seq, earlier=out)
    return out.reshape(bsz, seq, d)
```

```python
import functools

import jax
import jax.numpy as jnp
from jax import lax
from jax.experimental import pallas as pl
from jax.experimental.pallas import tpu as pltpu
from jax.experimental.pallas import tpu_sc as plsc

D_MODEL = 1024
HEAD_DIM = D_MODEL // 16
CONV_W = 6 * HEAD_DIM
POOL_W = 4 * HEAD_DIM
SGU_W = 6 * HEAD_DIM
D_IN = 3 * CONV_W + POOL_W + 2 * SGU_W
CONV_K = 3
POOL_WINDOWS = (2, 4, 8, 16)
CHUNK = 128
N_EXPERTS = 32
N_GROUPS = 4
EXPERTS_PER_GROUP = N_EXPERTS // N_GROUPS
TOP_K = 2
D_FF = D_MODEL // 2
ADA_CHUNKS = 6
LN_EPS = 1e-5

OFF_GB = 0
OFF_GC = CONV_W
OFF_P = 3 * CONV_W
OFF_U = OFF_P + POOL_W
OFF_V = OFF_U + SGU_W
YOFF_POOL = CONV_W
YOFF_SGU = CONV_W + POOL_W

LANES = 128
SUBLANES = 8
CONV_HALO = 8
POOL_HALO = 16
POOL_SLACK = 8
SEQ_TILE = 1024
SUB_BLOCK = 256
ROW_BLOCK = 2 * SUB_BLOCK
ADA_TILE = 1536
GATHER_DEPTH = 3
WEIGHT_DEPTH = 3
WEIGHT_DMA_PRIORITY = 1
VMEM_LIMIT = 60 * 1024 * 1024

PACK_W = D_MODEL // 2
SC_ROWS = 128
SC_WORKERS = 32
N_MIXER_STREAMS = 5
FIRST_PART_DIVISOR = 4

F32 = jnp.float32
BF16 = jnp.bfloat16


def _layer_norm(r, g, b):
    mu = jnp.mean(r, axis=-1, keepdims=True)
    d = r - mu
    var = jnp.mean(d * d, axis=-1, keepdims=True)
    return d * lax.rsqrt(var + LN_EPS) * g + b


def _rows(ref, start, size):
    return ref.at[pl.ds(pl.multiple_of(start, SUBLANES), size), :]


def _pack_bf16_pairs(v):
    bits = pltpu.bitcast(v, jnp.int32)
    return bits[:, 0:PACK_W] | lax.shift_right_logical(bits[:, PACK_W:2 * PACK_W], 16)


def _unpack_pairs_f32(words):
    return jnp.concatenate([pltpu.bitcast(words & jnp.int32(-65536), F32),
                            pltpu.bitcast(lax.shift_left(words, 16), F32)], axis=1)


def _unpack_bf16_pairs(words):
    first = pltpu.bitcast(words & jnp.int32(-65536), F32).astype(BF16)
    second = pltpu.bitcast(lax.shift_left(words, 16), F32).astype(BF16)
    return first, second


def _ada_kernel(c_ref, w_ref, b_ref, o_ref):
    c = c_ref[...]
    c_act = (c * jax.nn.sigmoid(c)).astype(BF16)
    col = pl.program_id(1) * ADA_TILE + lax.broadcasted_iota(jnp.int32, (1, ADA_TILE), 1)
    chunk = col // D_MODEL
    one = jnp.where((chunk == 1) | (chunk == 4), 1.0, 0.0)
    o_ref[0] = jnp.dot(c_act, w_ref[0].astype(BF16), preferred_element_type=F32) + (b_ref[0] + one)


def _ada_call(c, w_ada, b_ada):
    depth, d, n = w_ada.shape
    bsz = c.shape[0]
    return pl.pallas_call(
        _ada_kernel,
        grid=(depth, n // ADA_TILE),
        in_specs=[
            pl.BlockSpec((bsz, d), lambda l, j: (0, 0)),
            pl.BlockSpec((1, d, ADA_TILE), lambda l, j: (l, 0, j)),
            pl.BlockSpec((1, 1, ADA_TILE), lambda l, j: (l, 0, j)),
        ],
        out_specs=pl.BlockSpec((1, bsz, ADA_TILE), lambda l, j: (l, 0, j)),
        out_shape=jax.ShapeDtypeStruct((depth, bsz, n), F32),
        compiler_params=pltpu.CompilerParams(
            dimension_semantics=("arbitrary", "arbitrary"), vmem_limit_bytes=VMEM_LIMIT),
        name="ada",
    )(c, w_ada, b_ada.reshape(depth, 1, n))


def _top2_of_group(pg, sub_iota):
    big = float(EXPERTS_PER_GROUP)
    m1 = jnp.max(pg, axis=0, keepdims=True)
    i1 = jnp.min(jnp.where(pg == m1, sub_iota, big), axis=0, keepdims=True)
    rest = jnp.where(sub_iota == i1, -1.0, pg)
    m2 = jnp.max(rest, axis=0, keepdims=True)
    i2 = jnp.min(jnp.where(rest == m2, sub_iota, big), axis=0, keepdims=True)
    return m1, i1, m2, i2


def _combine_tile(ya0_ref, ya1_ref, wt_ref, x1, gate, g_ref, b_ref, *, alpha):
    wt = wt_ref[...]
    y = wt[:, 0:1] * _unpack_pairs_f32(ya0_ref[...]) + wt[:, 1:2] * _unpack_pairs_f32(ya1_ref[...])
    return _layer_norm(alpha * x1 + gate * y, g_ref[...], b_ref[...])


def _mixer_kernel(*refs, ts, alpha, nt, fused, continued):
    if fused:
        xprev_ref, ya0_ref, ya1_ref, wtp_ref, adap_ref, ln2g_ref, ln2b_ref = refs[:7]
        refs = refs[7:]
    else:
        x_ref, refs = refs[0], refs[1:]
    (ada_ref, win_hbm, wout_hbm, convw_ref, convb_ref, poolw_ref, pscale_ref, lng_ref, lnb_ref,
     sguw_ref, sgub_ref, ln1g_ref, ln1b_ref, wr_ref, br_ref, tri_hbm) = refs[:16]
    refs = refs[16:]
    if continued:
        cnt0_ref, refs = refs[N_MIXER_STREAMS], refs[N_MIXER_STREAMS + 1:]
    (x1_ref, h2p_ref, e_ref, wt_ref, rank_ref, cnt_ref,
     gx_scr, p_scr, q_scr, r_scr, y_scr, cnt_scr, win_ref, wout_ref, tri_ref, wsem) = refs

    i = pl.program_id(0)
    j = i % nt

    def constant_copies():
        pairs = ((win_hbm, win_ref), (wout_hbm, wout_ref), (tri_hbm, tri_ref))
        return [pltpu.make_async_copy(src, dst, wsem.at[n]) for n, (src, dst) in enumerate(pairs)]

    @pl.when(i == 0)
    def _():
        for copy in constant_copies():
            copy.start()
        gx_scr[0:CONV_HALO, :] = jnp.zeros((CONV_HALO, CONV_W), F32)
        for scr in (p_scr, q_scr, r_scr):
            scr[0:POOL_SLACK + POOL_HALO, :] = jnp.zeros((POOL_SLACK + POOL_HALO, POOL_W), F32)
        cnt_scr[...] = cnt0_ref[...] if continued else jnp.zeros_like(cnt_scr)

    if fused:
        x = _combine_tile(ya0_ref, ya1_ref, wtp_ref, xprev_ref[...], adap_ref[0][5:6], ln2g_ref, ln2b_ref,
                          alpha=alpha)
    else:
        x = x_ref[0]

    @pl.when(i == 0)
    def _():
        for copy in constant_copies():
            copy.wait()

    _mixer_tile(j, x, ada_ref, win_ref, wout_ref, convw_ref, convb_ref, poolw_ref, pscale_ref,
                lng_ref, lnb_ref, sguw_ref, sgub_ref, ln1g_ref, ln1b_ref, wr_ref, br_ref, tri_ref,
                x1_ref, h2p_ref, e_ref, wt_ref, rank_ref, cnt_ref, gx_scr, p_scr, q_scr, r_scr, y_scr, cnt_scr,
                ts=ts, alpha=alpha)


def _mixer_tile(j, x, ada_ref, win_ref, wout_ref, convw_ref, convb_ref, poolw_ref, pscale_ref,
                lng_ref, lnb_ref, sguw_ref, sgub_ref, ln1g_ref, ln1b_ref, wr_ref, br_ref, tri_ref,
                x1_ref, h2p_ref, e_ref, wt_ref, rank_ref, cnt_ref, gx_scr, p_scr, q_scr, r_scr, y_scr, cnt_scr,
                *, ts, alpha):
    first_of_seq = j == 0
    ada = ada_ref[0]
    sh1, mul1, g1 = ada[0:1], ada[1:2], ada[2:3]
    sh2, mul2 = ada[3:4], ada[4:5]
    hb = (x * mul1 + sh1).astype(BF16)

    def proj(lo, hi):
        return jnp.dot(hb, win_ref[:, lo:hi], preferred_element_type=F32)

    gcxc = proj(OFF_GC, OFF_P)
    g = gcxc[:, :CONV_W] * gcxc[:, CONV_W:]
    gx_scr[CONV_HALO:CONV_HALO + ts, :] = g
    cw = convw_ref[...]
    gx_scr[0:CONV_HALO, :] = jnp.where(first_of_seq, 0.0, gx_scr[0:CONV_HALO, :])
    conv = (cw[0:1] * gx_scr[CONV_HALO - 2:CONV_HALO - 2 + ts, :]
            + cw[1:2] * gx_scr[CONV_HALO - 1:CONV_HALO - 1 + ts, :]
            + cw[2:3] * g + convb_ref[...])
    gx_scr[0:CONV_HALO, :] = g[ts - CONV_HALO:ts, :]
    y_scr[:, 0:CONV_W] = (proj(OFF_GB, OFF_GC) * conv).astype(BF16)

    p = proj(OFF_P, OFF_U)
    top = POOL_SLACK
    n_ext = POOL_HALO + ts
    p_scr[top + POOL_HALO:top + n_ext, :] = p
    p_scr[top:top + POOL_HALO, :] = jnp.where(first_of_seq, 0.0, p_scr[top:top + POOL_HALO, :])
    w2 = p_scr[top:top + n_ext, :] + p_scr[top - 1:top - 1 + n_ext, :]
    q_scr[top:top + n_ext, :] = w2
    w4 = w2 + q_scr[top - 2:top - 2 + n_ext, :]
    r_scr[top:top + n_ext, :] = w4
    w8 = w4 + r_scr[top - 4:top - 4 + n_ext, :]
    q_scr[top:top + n_ext, :] = w8
    w16 = w8 + q_scr[top - 8:top - 8 + n_ext, :]
    s2, s4 = w2[POOL_HALO:, 0:LANES], w4[POOL_HALO:, 0:LANES]
    s8, s16 = w8[POOL_HALO:, LANES:2 * LANES], w16[POOL_HALO:, LANES:2 * LANES]
    p_scr[top:top + POOL_HALO, :] = p[ts - POOL_HALO:ts, :]
    lane = lax.broadcasted_iota(jnp.int32, (ts, LANES), 1)
    tpos = (lax.broadcasted_iota(jnp.int32, (ts, LANES), 0) + (j * ts + 1)).astype(F32)
    lo_half = lane < HEAD_DIM
    cnt_a = jnp.minimum(tpos, jnp.where(lo_half, float(POOL_WINDOWS[0]), float(POOL_WINDOWS[1])))
    cnt_b = jnp.minimum(tpos, jnp.where(lo_half, float(POOL_WINDOWS[2]), float(POOL_WINDOWS[3])))
    pooled = jnp.concatenate([jnp.where(lo_half, s2, s4) / cnt_a,
                              jnp.where(lo_half, s8, s16) / cnt_b], axis=1) - p
    mixed = jnp.dot(pooled.astype(BF16), poolw_ref[...], preferred_element_type=F32)
    y_scr[:, YOFF_POOL:YOFF_POOL + POOL_W] = (mixed * pscale_ref[...]).astype(BF16)

    v = proj(OFF_V, D_IN)
    vnb = _layer_norm(v, lng_ref[...], lnb_ref[...]).astype(BF16)
    u = proj(OFF_U, OFF_V)
    row_c = lax.broadcasted_iota(jnp.int32, (CHUNK, 2 * CHUNK), 0)
    col_c = lax.broadcasted_iota(jnp.int32, (CHUNK, 2 * CHUNK), 1)
    causal = (col_c & (CHUNK - 1)) <= row_c
    lo_lanes = lax.broadcasted_iota(jnp.int32, (CHUNK, LANES), 1) < HEAD_DIM
    zero_b = jnp.zeros((CHUNK, LANES), BF16)
    for hp in range(SGU_W // LANES):
        wl = jnp.where(causal, sguw_ref[hp], 0.0).astype(BF16)
        bias = sgub_ref[hp]
        for ci in range(ts // CHUNK):
            rs = slice(ci * CHUNK, (ci + 1) * CHUNK)
            vc = vnb[rs, hp * LANES:(hp + 1) * LANES]
            rhs = jnp.concatenate([jnp.where(lo_lanes, vc, zero_b),
                                   jnp.where(lo_lanes, zero_b, vc)], axis=0)
            mixed_c = jnp.dot(wl, rhs, preferred_element_type=F32) + bias
            y_scr[rs, YOFF_SGU + hp * LANES:YOFF_SGU + (hp + 1) * LANES] = (
                u[rs, hp * LANES:(hp + 1) * LANES] * mixed_c).astype(BF16)

    yo = jnp.dot(y_scr[...], wout_ref[...], preferred_element_type=F32)
    x1 = _layer_norm(alpha * x + g1 * yo, ln1g_ref[...], ln1b_ref[...])
    x1_ref[0] = x1
    h2b = (x1 * mul2 + sh2).astype(BF16)
    h2p_ref[...] = _pack_bf16_pairs(h2b.astype(F32))

    logits = jnp.dot(h2b, wr_ref[...], preferred_element_type=F32) + br_ref[...]
    lt = logits.T[0:N_EXPERTS, :]
    ex = jnp.exp(lt - jnp.max(lt, axis=0, keepdims=True))
    probs = ex / jnp.sum(ex, axis=0, keepdims=True)
    sub_iota = lax.broadcasted_iota(jnp.int32, (EXPERTS_PER_GROUP, ts), 0).astype(F32)
    best = None
    for gi in range(N_GROUPS):
        m1, i1, m2, i2 = _top2_of_group(
            probs[gi * EXPERTS_PER_GROUP:(gi + 1) * EXPERTS_PER_GROUP, :], sub_iota)
        score = m1 + m2
        cand = (score, m1, m2, i1 + float(gi * EXPERTS_PER_GROUP), i2 + float(gi * EXPERTS_PER_GROUP))
        if best is None:
            best = cand
        else:
            better = score > best[0]
            best = tuple(jnp.where(better, c, o) for c, o in zip(cand, best))
    _, p1, p2, e1, e2 = best
    den = p1 + p2
    w1 = p1 / den
    w2 = p2 / den

    wt_ref[...] = jnp.concatenate([w1, w2], axis=0)
    e_ref[...] = jnp.concatenate([e1, e2], axis=0).astype(jnp.int32)

    ex_iota = lax.broadcasted_iota(jnp.int32, (N_EXPERTS, ts), 0).astype(F32)
    oh1 = ex_iota == e1
    oh2 = ex_iota == e2
    either = jnp.where(oh1 | oh2, 1.0, 0.0)
    seen = jnp.dot(either.astype(BF16), tri_ref[...], preferred_element_type=F32) + cnt_scr[:, 0:1]
    r1 = jnp.sum(jnp.where(oh1, seen, 0.0), axis=0, keepdims=True)
    r2 = jnp.sum(jnp.where(oh2, seen, 0.0), axis=0, keepdims=True)
    rank_ref[...] = jnp.concatenate([r1, r2], axis=0).astype(jnp.int32)
    cnt_new = cnt_scr[...] + jnp.sum(either, axis=1, keepdims=True)
    cnt_scr[...] = cnt_new
    cnt_ref[...] = cnt_new


def _mixer_call(x, ada_l, w_in, w_out, conv_w, conv_b, poolw_bd, pool_scale, ln_g, ln_b,
                sguw_pair, sgub_pair, ln1_g, ln1_b, wr_pad, br_pad, tri, alpha, prev=None,
                seq0=0, n_seq=None, earlier=None):
    bsz, s, d = x.shape
    ts = SEQ_TILE
    nt = s // ts
    n_seq = bsz if n_seq is None else n_seq
    n_tiles = n_seq * nt
    tile0 = seq0 * nt
    n_tok = bsz * s
    fused = prev is not None
    continued = earlier is not None
    const2 = lambda i: (0, 0)
    const3 = lambda i: (0, 0, 0)
    seq_of = lambda i: (seq0 + i // nt, 0, 0)
    tile3 = lambda i: (seq0 + i // nt, i % nt, 0)
    tile2 = lambda i: (tile0 + i, 0)
    by_token = lambda i: (0, tile0 + i)
    in_hbm = pl.BlockSpec(memory_space=pl.ANY)
    in_specs = [
        pl.BlockSpec((1, ADA_CHUNKS, d), seq_of),
        in_hbm,
        in_hbm,
        pl.BlockSpec((CONV_K, CONV_W), const2),
        pl.BlockSpec((1, CONV_W), const2),
        pl.BlockSpec((POOL_W, POOL_W), const2),
        pl.BlockSpec((1, POOL_W), const2),
        pl.BlockSpec((1, SGU_W), const2),
        pl.BlockSpec((1, SGU_W), const2),
        pl.BlockSpec((SGU_W // LANES, CHUNK, 2 * CHUNK), const3),
        pl.BlockSpec((SGU_W // LANES, CHUNK, LANES), const3),
        pl.BlockSpec((1, d), const2),
        pl.BlockSpec((1, d), const2),
        pl.BlockSpec((d, LANES), const2),
        pl.BlockSpec((1, LANES), const2),
        in_hbm,
    ]
    operands = [ada_l, w_in, w_out, conv_w, conv_b, poolw_bd, pool_scale, ln_g, ln_b,
                sguw_pair, sgub_pair, ln1_g, ln1_b, wr_pad, br_pad, tri]
    if fused:
        ya, wt_tok, ada_prev, ln2_g, ln2_b = prev
        in_specs = [
            pl.BlockSpec((ts, d), tile2),
            pl.BlockSpec((ts, PACK_W), lambda i: (i, 0)),
            pl.BlockSpec((ts, PACK_W), lambda i: (n_tiles + i, 0)),
            pl.BlockSpec((ts, TOP_K), tile2),
            pl.BlockSpec((1, ADA_CHUNKS, d), seq_of),
            pl.BlockSpec((1, d), const2),
            pl.BlockSpec((1, d), const2),
        ] + in_specs
        operands = [x.reshape(n_tok, d), ya, ya, wt_tok, ada_prev, ln2_g, ln2_b] + operands
    else:
        in_specs = [pl.BlockSpec((1, ts, d), tile3)] + in_specs
        operands = [x] + operands
    aliases = {}
    if continued:
        aliases = {len(operands) + k: k for k in range(N_MIXER_STREAMS)}
        in_specs = in_specs + [pl.BlockSpec(memory_space=pl.ANY)] * N_MIXER_STREAMS + [
            pl.BlockSpec((N_EXPERTS, LANES), const2)]
        operands = operands + list(earlier)
    kern = functools.partial(_mixer_kernel, ts=ts, alpha=alpha, nt=nt, fused=fused, continued=continued)
    pool_rows = POOL_SLACK + POOL_HALO + ts
    return pl.pallas_call(
        kern,
        grid=(n_tiles,),
        in_specs=in_specs,
        input_output_aliases=aliases,
        out_specs=[
            pl.BlockSpec((1, ts, d), tile3),
            pl.BlockSpec((ts, PACK_W), tile2),
            pl.BlockSpec((TOP_K, ts), by_token),
            pl.BlockSpec((TOP_K, ts), by_token),
            pl.BlockSpec((TOP_K, ts), by_token),
            pl.BlockSpec((N_EXPERTS, LANES), const2),
        ],
        out_shape=[
            jax.ShapeDtypeStruct((bsz, s, d), F32),
            jax.ShapeDtypeStruct((n_tok, PACK_W), jnp.int32),
            jax.ShapeDtypeStruct((TOP_K, n_tok), jnp.int32),
            jax.ShapeDtypeStruct((TOP_K, n_tok), F32),
            jax.ShapeDtypeStruct((TOP_K, n_tok), jnp.int32),
            jax.ShapeDtypeStruct((N_EXPERTS, LANES), F32),
        ],
        scratch_shapes=[
            pltpu.VMEM((CONV_HALO + ts, CONV_W), F32),
            pltpu.VMEM((pool_rows, POOL_W), F32),
            pltpu.VMEM((pool_rows, POOL_W), F32),
            pltpu.VMEM((pool_rows, POOL_W), F32),
            pltpu.VMEM((ts, d), BF16),
            pltpu.VMEM((N_EXPERTS, LANES), F32),
            pltpu.VMEM((d, D_IN), BF16),
            pltpu.VMEM((d, d), BF16),
            pltpu.VMEM((ts, ts), BF16),
            pltpu.SemaphoreType.DMA((3,)),
        ],
        compiler_params=pltpu.CompilerParams(
            dimension_semantics=("arbitrary",), vmem_limit_bytes=VMEM_LIMIT),
        name="mixer_router",
    )(*operands)


def _sc_worker_chunks(n_chunks):
    per_worker = n_chunks // SC_WORKERS
    worker = lax.axis_index("s") * 2 + lax.axis_index("c")
    return worker * per_worker, per_worker


def _dispatch_call(h2p, dest, n_rows):
    n_tok, width = h2p.shape
    n_chunks = n_tok // SC_ROWS
    assert n_chunks % SC_WORKERS == 0

    def body(h_hbm, dest_hbm, xs_hbm, rows_v, idx_v):
        first, per_worker = _sc_worker_chunks(n_chunks)

        @pl.loop(0, per_worker)
        def _(c):
            chunk = first + c
            pltpu.sync_copy(h_hbm.at[pl.ds(chunk * SC_ROWS, SC_ROWS)], rows_v)
            for k in range(TOP_K):
                pltpu.sync_copy(dest_hbm.at[k, chunk], idx_v.at[k])
                pltpu.sync_copy(rows_v, xs_hbm.at[idx_v.at[k]])

    return pl.kernel(
        body,
        out_type=jax.ShapeDtypeStruct((n_rows, width), jnp.int32),
        mesh=plsc.VectorSubcoreMesh(core_axis_name="c", subcore_axis_name="s"),
        scratch_types=[pltpu.VMEM((SC_ROWS, width), jnp.int32), pltpu.VMEM((TOP_K, SC_ROWS), jnp.int32)],
        name="dispatch",
    )(h2p, dest)


def _collect_call(ys, dest, tok0, n_tok):
    width = ys.shape[1]
    n_chunks = n_tok // SC_ROWS
    chunk0 = tok0 // SC_ROWS
    assert n_chunks % SC_WORKERS == 0 and tok0 % SC_ROWS == 0

    def body(ys_hbm, dest_hbm, ya_hbm, rows_v, idx_v):
        first, per_worker = _sc_worker_chunks(n_chunks)

        @pl.loop(0, per_worker)
        def _(c):
            chunk = first + c
            for k in range(TOP_K):
                pltpu.sync_copy(dest_hbm.at[k, chunk0 + chunk], idx_v.at[k])
                pltpu.sync_copy(ys_hbm.at[idx_v.at[k]], rows_v)
                pltpu.sync_copy(rows_v, ya_hbm.at[pl.ds(k * n_tok + chunk * SC_ROWS, SC_ROWS)])

    return pl.kernel(
        body,
        out_type=jax.ShapeDtypeStruct((TOP_K * n_tok, width), jnp.int32),
        mesh=plsc.VectorSubcoreMesh(core_axis_name="c", subcore_axis_name="s"),
        scratch_types=[pltpu.VMEM((SC_ROWS, width), jnp.int32), pltpu.VMEM((TOP_K, SC_ROWS), jnp.int32)],
        name="collect",
    )(ys, dest)


def _ffn_kernel(b0_ref, nb_ref, rows_ref, tot_ref, xs_ref, wg_ref, wu_ref, wd_ref, ys_ref,
                xbuf, obuf, wbuf_g, wbuf_u, wbuf_d, wg_scr, wu_scr, wd_scr, gsem, osem, wsem,
                *, layer, n_blocks):
    e = pl.program_id(0)
    n_exp = pl.num_programs(0)
    b0 = b0_ref[e]
    nb = nb_ref[e]
    n_used = tot_ref[0]

    def in_copy(g):
        slot = g % GATHER_DEPTH
        return pltpu.make_async_copy(_rows(xs_ref, g * ROW_BLOCK, ROW_BLOCK), xbuf.at[slot], gsem.at[slot])

    def out_copy(g, slot):
        return pltpu.make_async_copy(obuf.at[slot], _rows(ys_ref, g * ROW_BLOCK, ROW_BLOCK), osem.at[slot])

    def weight_copies(expert, slot):
        return [pltpu.make_async_copy(src.at[layer, expert], dst.at[slot], wsem.at[slot])
                for src, dst in ((wg_ref, wbuf_g), (wu_ref, wbuf_u), (wd_ref, wbuf_d))]

    @pl.when(e == 0)
    def _():
        for first in range(WEIGHT_DEPTH - 1):
            for cp in weight_copies(first, first):
                cp.start(priority=WEIGHT_DMA_PRIORITY)
        for ahead in range(GATHER_DEPTH - 1):
            @pl.when(ahead < n_used)
            def _():
                in_copy(ahead).start()

    @pl.when(e + (WEIGHT_DEPTH - 1) < n_exp)
    def _():
        for cp in weight_copies(e + (WEIGHT_DEPTH - 1), (e + (WEIGHT_DEPTH - 1)) % WEIGHT_DEPTH):
            cp.start(priority=WEIGHT_DMA_PRIORITY)

    wslot = e % WEIGHT_DEPTH
    for cp in weight_copies(e, wslot):
        cp.wait()

    @pl.when(nb > 0)
    def _():
        wg_scr[...] = wbuf_g[wslot].astype(BF16)
        wu_scr[...] = wbuf_u[wslot].astype(BF16)
        wd_scr[...] = wbuf_d[wslot].astype(BF16)

    def block(k, carry):
        g = b0 + k
        islot = g % GATHER_DEPTH
        oslot = g % 2

        @pl.when(g + (GATHER_DEPTH - 1) < n_used)
        def _():
            in_copy(g + (GATHER_DEPTH - 1)).start()

        in_copy(g).wait()

        @pl.when(g >= 2)
        def _():
            out_copy(g - 2, oslot).wait()

        def chain(half):
            rows = pl.ds(half * SUB_BLOCK, SUB_BLOCK)
            xb = jnp.concatenate(_unpack_bf16_pairs(xbuf[islot, rows, :]), axis=1)
            gate = jnp.dot(xb, wg_scr[...], preferred_element_type=F32)
            up = jnp.dot(xb, wu_scr[...], preferred_element_type=F32)
            act = (gate * jax.nn.sigmoid(gate) * up).astype(BF16)
            y = jnp.dot(act, wd_scr[...], preferred_element_type=F32)
            obuf[oslot, rows, :] = _pack_bf16_pairs(y.astype(BF16).astype(F32))

        both = rows_ref[e] - k * ROW_BLOCK > SUB_BLOCK

        @pl.when(both)
        def _():
            chain(0)
            chain(1)

        @pl.when(jnp.logical_not(both))
        def _():
            chain(0)
            obuf[oslot, pl.ds(SUB_BLOCK, SUB_BLOCK), :] = jnp.zeros((SUB_BLOCK, PACK_W), jnp.int32)

        out_copy(g, oslot).start()
        return carry

    lax.fori_loop(0, nb, block, 0)

    @pl.when(e == n_exp - 1)
    def _():
        @pl.when(n_used >= 2)
        def _():
            out_copy(n_used - 2, n_used % 2).wait()

        out_copy(n_used - 1, (n_used - 1) % 2).wait()
        obuf[0] = jnp.zeros((ROW_BLOCK, PACK_W), jnp.int32)

        def fill(g, carry):
            out_copy(g, 0).start()
            return carry

        def drain(g, carry):
            out_copy(g, 0).wait()
            return carry

        lax.fori_loop(n_used, n_blocks, fill, 0)
        lax.fori_loop(n_used, n_blocks, drain, 0)


def _ffn_call(layer, blk_start, seg_blocks, seg_rows, n_used, n_blocks, xs, w_gate, w_up, w_down):
    d, f = w_gate.shape[-2:]
    any_spec = pl.BlockSpec(memory_space=pl.ANY)
    return pl.pallas_call(
        functools.partial(_ffn_kernel, layer=layer, n_blocks=n_blocks),
        grid_spec=pltpu.PrefetchScalarGridSpec(
            num_scalar_prefetch=4,
            grid=(N_EXPERTS,),
            in_specs=[any_spec, any_spec, any_spec, any_spec],
            out_specs=any_spec,
            scratch_shapes=[
                pltpu.VMEM((GATHER_DEPTH, ROW_BLOCK, PACK_W), jnp.int32),
                pltpu.VMEM((2, ROW_BLOCK, PACK_W), jnp.int32),
                pltpu.VMEM((WEIGHT_DEPTH, d, f), F32),
                pltpu.VMEM((WEIGHT_DEPTH, d, f), F32),
                pltpu.VMEM((WEIGHT_DEPTH, f, d), F32),
                pltpu.VMEM((d, f), BF16),
                pltpu.VMEM((d, f), BF16),
                pltpu.VMEM((f, d), BF16),
                pltpu.SemaphoreType.DMA((GATHER_DEPTH,)),
                pltpu.SemaphoreType.DMA((2,)),
                pltpu.SemaphoreType.DMA((WEIGHT_DEPTH,)),
            ],
        ),
        out_shape=jax.ShapeDtypeStruct((n_blocks * ROW_BLOCK, PACK_W), jnp.int32),
        compiler_params=pltpu.CompilerParams(
            dimension_semantics=("arbitrary",), vmem_limit_bytes=VMEM_LIMIT),
        name="expert_ffn",
    )(blk_start, seg_blocks, seg_rows, n_used, xs, w_gate, w_up, w_down)


def _combine_kernel(ya0_ref, ya1_ref, wt_ref, x1_ref, ada_ref, g_ref, b_ref, *rest, alpha):
    o_ref = rest[-1]
    o_ref[...] = _combine_tile(ya0_ref, ya1_ref, wt_ref, x1_ref[...], ada_ref[0][5:6], g_ref, b_ref, alpha=alpha)


def _combine_call(ya, wt_tok, x1_flat, ada_l, ln_g, ln_b, alpha, seq, tok0, earlier=None):
    n_tok, d = x1_flat.shape
    ts = SEQ_TILE
    n_tiles = ya.shape[0] // (TOP_K * ts)
    tile0 = tok0 // ts
    per_seq = seq // ts
    tile = lambda i: (tile0 + i, 0)
    in_specs = [
        pl.BlockSpec((ts, PACK_W), lambda i: (i, 0)),
        pl.BlockSpec((ts, PACK_W), lambda i: (n_tiles + i, 0)),
        pl.BlockSpec((ts, TOP_K), tile),
        pl.BlockSpec((ts, d), tile),
        pl.BlockSpec((1, ADA_CHUNKS, d), lambda i: ((tile0 + i) // per_seq, 0, 0)),
        pl.BlockSpec((1, d), lambda i: (0, 0)),
        pl.BlockSpec((1, d), lambda i: (0, 0)),
    ]
    operands = [ya, ya, wt_tok, x1_flat, ada_l, ln_g, ln_b]
    aliases = {}
    if earlier is not None:
        aliases = {len(operands): 0}
        in_specs.append(pl.BlockSpec(memory_space=pl.ANY))
        operands.append(earlier)
    return pl.pallas_call(
        functools.partial(_combine_kernel, alpha=alpha),
        grid=(n_tiles,),
        in_specs=in_specs,
        out_specs=pl.BlockSpec((ts, d), tile),
        out_shape=jax.ShapeDtypeStruct((n_tok, d), F32),
        input_output_aliases=aliases,
        compiler_params=pltpu.CompilerParams(
            dimension_semantics=("arbitrary",), vmem_limit_bytes=VMEM_LIMIT),
        name="combine_ln",
    )(*operands)


def kernel(x, c, w_ada, b_ada, w_in, conv_w, conv_b, pool_w, pool_scale, sgu_ln_g, sgu_ln_b,
           sgu_w, sgu_b, w_out, ln1_g, ln1_b, w_router, b_router, w_gate, w_up, w_down,
           ln2_g, ln2_b):
    bsz, seq, d = x.shape
    depth = w_ada.shape[0]
    n_tok = bsz * seq
    alpha = (2 * depth) ** 0.25
    assert d == D_MODEL and seq % SEQ_TILE == 0 and SEQ_TILE % CHUNK == 0

    ada = _ada_call(c, w_ada, b_ada).reshape(depth, bsz, ADA_CHUNKS, d)

    w_in_b = w_in.astype(BF16)
    w_out_b = w_out.astype(BF16)
    eye_g = jnp.eye(POOL_W // HEAD_DIM, dtype=F32)
    poolw_bd = jnp.einsum('lgcd,gh->lgchd', pool_w, eye_g).reshape(depth, POOL_W, POOL_W).astype(BF16)
    n_pairs = SGU_W // LANES
    sguw_pair = sgu_w.reshape(depth, n_pairs, 2, CHUNK, CHUNK).transpose(0, 1, 3, 2, 4).reshape(
        depth, n_pairs, CHUNK, 2 * CHUNK)
    sgub_pair = jnp.repeat(sgu_b.transpose(0, 2, 1), HEAD_DIM, axis=-1).reshape(
        depth, CHUNK, n_pairs, LANES).transpose(0, 2, 1, 3)
    wr_pad = jnp.pad(w_router, ((0, 0), (0, LANES - N_EXPERTS))).astype(BF16)
    br_pad = jnp.pad(b_router, (0, LANES - N_EXPERTS)).reshape(1, LANES)

    n_steps = (n_tok * TOP_K) // ROW_BLOCK + N_EXPERTS
    experts = jnp.arange(N_EXPERTS, dtype=jnp.int32)
    tri = jnp.triu(jnp.ones((SEQ_TILE, SEQ_TILE), BF16), k=1)

    first_seqs = bsz // FIRST_PART_DIVISOR
    part_seq0 = (0, first_seqs) if first_seqs else (0,)
    part_seqs = (first_seqs, bsz - first_seqs) if first_seqs else (bsz,)
    parts = list(zip(part_seq0, part_seqs))
    prev = None
    for l in range(depth):
        layer_args = (
            x, ada[l], w_in_b[l], w_out_b[l], conv_w[l], conv_b[l].reshape(1, -1), poolw_bd[l],
            pool_scale[l].reshape(1, -1), sgu_ln_g[l].reshape(1, -1), sgu_ln_b[l].reshape(1, -1),
            sguw_pair[l], sgub_pair[l], ln1_g[l].reshape(1, -1), ln1_b[l].reshape(1, -1),
            wr_pad, br_pad, tri, alpha)
        if prev is None:
            outs = _mixer_call(*layer_args)
        else:
            ya_parts, *prev_rest = prev
            outs = None
            for ya_part, (seq0, n_seq) in zip(ya_parts, parts):
                outs = _mixer_call(*layer_args, prev=(ya_part, *prev_rest),
                                   seq0=seq0, n_seq=n_seq, earlier=outs)
        x1, h2p, e_idx, wts, rank, counts = outs
        sizes = counts[:, 0].astype(jnp.int32)
        seg_blocks = (sizes + ROW_BLOCK - 1) // ROW_BLOCK
        blk_end = jnp.cumsum(seg_blocks)
        blk_start = (blk_end - seg_blocks).astype(jnp.int32)
        n_used = blk_end[-1:].astype(jnp.int32)
        seg_row0 = jnp.sum(jnp.where(e_idx[:, :, None] == experts[None, None, :],
                                     (blk_start * ROW_BLOCK)[None, None, :], 0), axis=2)
        dest = (seg_row0 + rank).astype(jnp.int32).reshape(TOP_K, n_tok // SC_ROWS, SC_ROWS)

        xs = _dispatch_call(h2p, dest, n_steps * ROW_BLOCK)
        ys = _ffn_call(l, blk_start, seg_blocks.astype(jnp.int32), sizes, n_used, n_steps,
                       xs, w_gate, w_up, w_down)
        ya_parts = [_collect_call(ys, dest, seq0 * seq, n_seq * seq) for seq0, n_seq in parts]
        wt_tok = wts.T
        prev = (ya_parts, wt_tok, ada[l], ln2_g[l].reshape(1, -1), ln2_b[l].reshape(1, -1))
        x = x1
    ya_parts, wt_tok, ada_last, ln2_g_last, ln2_b_last = prev
    out = None
    for ya_part, (seq0, _) in zip(ya_parts, parts):
        out = _combine_call(ya_part, wt_tok, x.reshape(n_tok, d), ada_last, ln2_g_last, ln2_b_last,
                            alpha, seq, seq0 * seq, earlier=out)
    return out.reshape(bsz, seq, d)
```

```python
import functools

import jax
import jax.numpy as jnp
from jax import lax
from jax.experimental import pallas as pl
from jax.experimental.pallas import tpu as pltpu
from jax.experimental.pallas import tpu_sc as plsc

D_MODEL = 1024
HEAD_DIM = D_MODEL // 16
CONV_W = 6 * HEAD_DIM
POOL_W = 4 * HEAD_DIM
SGU_W = 6 * HEAD_DIM
D_IN = 3 * CONV_W + POOL_W + 2 * SGU_W
CONV_K = 3
POOL_WINDOWS = (2, 4, 8, 16)
CHUNK = 128
N_EXPERTS = 32
N_GROUPS = 4
EXPERTS_PER_GROUP = N_EXPERTS // N_GROUPS
TOP_K = 2
D_FF = D_MODEL // 2
ADA_CHUNKS = 6
LN_EPS = 1e-5

OFF_GB = 0
OFF_GC = CONV_W
OFF_P = 3 * CONV_W
OFF_U = OFF_P + POOL_W
OFF_V = OFF_U + SGU_W
YOFF_POOL = CONV_W
YOFF_SGU = CONV_W + POOL_W

LANES = 128
SUBLANES = 8
CONV_HALO = 8
POOL_HALO = 16
POOL_SLACK = 8
SEQ_TILE = 1024
SUB_BLOCK = 256
ROW_BLOCK = 2 * SUB_BLOCK
ADA_TILE = 1536
GATHER_DEPTH = 3
WEIGHT_DEPTH = 3
WEIGHT_DMA_PRIORITY = 1
VMEM_LIMIT = 60 * 1024 * 1024

PACK_W = D_MODEL // 2
SC_ROWS = 128
SC_WORKERS = 32
N_MIXER_STREAMS = 5
FIRST_PART_DIVISOR = 4

F32 = jnp.float32
BF16 = jnp.bfloat16


def _layer_norm(r, g, b):
    mu = jnp.mean(r, axis=-1, keepdims=True)
    d = r - mu
    var = jnp.mean(d * d, axis=-1, keepdims=True)
    return d * lax.rsqrt(var + LN_EPS) * g + b


def _rows(ref, start, size):
    return ref.at[pl.ds(pl.multiple_of(start, SUBLANES), size), :]


def _pack_bf16_pairs(v):
    bits = pltpu.bitcast(v, jnp.int32)
    return bits[:, 0:PACK_W] | lax.shift_right_logical(bits[:, PACK_W:2 * PACK_W], 16)


def _unpack_pairs_f32(words):
    return jnp.concatenate([pltpu.bitcast(words & jnp.int32(-65536), F32),
                            pltpu.bitcast(lax.shift_left(words, 16), F32)], axis=1)


def _unpack_bf16_pairs(words):
    first = pltpu.bitcast(words & jnp.int32(-65536), F32).astype(BF16)
    second = pltpu.bitcast(lax.shift_left(words, 16), F32).astype(BF16)
    return first, second


def _ada_kernel(c_ref, w_ref, b_ref, o_ref):
    c = c_ref[...]
    c_act = (c * jax.nn.sigmoid(c)).astype(BF16)
    col = pl.program_id(1) * ADA_TILE + lax.broadcasted_iota(jnp.int32, (1, ADA_TILE), 1)
    chunk = col // D_MODEL
    one = jnp.where((chunk == 1) | (chunk == 4), 1.0, 0.0)
    o_ref[0] = jnp.dot(c_act, w_ref[0].astype(BF16), preferred_element_type=F32) + (b_ref[0] + one)


def _ada_call(c, w_ada, b_ada):
    depth, d, n = w_ada.shape
    bsz = c.shape[0]
    return pl.pallas_call(
        _ada_kernel,
        grid=(depth, n // ADA_TILE),
        in_specs=[
            pl.BlockSpec((bsz, d), lambda l, j: (0, 0)),
            pl.BlockSpec((1, d, ADA_TILE), lambda l, j: (l, 0, j)),
            pl.BlockSpec((1, 1, ADA_TILE), lambda l, j: (l, 0, j)),
        ],
        out_specs=pl.BlockSpec((1, bsz, ADA_TILE), lambda l, j: (l, 0, j)),
        out_shape=jax.ShapeDtypeStruct((depth, bsz, n), F32),
        compiler_params=pltpu.CompilerParams(
            dimension_semantics=("arbitrary", "arbitrary"), vmem_limit_bytes=VMEM_LIMIT),
        name="ada",
    )(c, w_ada, b_ada.reshape(depth, 1, n))


def _top2_of_group(pg, sub_iota):
    big = float(EXPERTS_PER_GROUP)
    m1 = jnp.max(pg, axis=0, keepdims=True)
    i1 = jnp.min(jnp.where(pg == m1, sub_iota, big), axis=0, keepdims=True)
    rest = jnp.where(sub_iota == i1, -1.0, pg)
    m2 = jnp.max(rest, axis=0, keepdims=True)
    i2 = jnp.min(jnp.where(rest == m2, sub_iota, big), axis=0, keepdims=True)
    return m1, i1, m2, i2


def _combine_tile(ya0_ref, ya1_ref, wt_ref, x1, gate, g_ref, b_ref, *, alpha):
    wt = wt_ref[...]
    y = wt[:, 0:1] * _unpack_pairs_f32(ya0_ref[...]) + wt[:, 1:2] * _unpack_pairs_f32(ya1_ref[...])
    return _layer_norm(alpha * x1 + gate * y, g_ref[...], b_ref[...])


def _mixer_kernel(*refs, ts, alpha, nt, fused, continued):
    if fused:
        xprev_ref, ya0_ref, ya1_ref, wtp_ref, adap_ref, ln2g_ref, ln2b_ref = refs[:7]
        refs = refs[7:]
    else:
        x_ref, refs = refs[0], refs[1:]
    (ada_ref, win_ref, wout_ref, convw_ref, convb_ref, poolw_ref, pscale_ref, lng_ref, lnb_ref,
     sguw_ref, sgub_ref, ln1g_ref, ln1b_ref, wr_ref, br_ref, tri_ref) = refs[:16]
    refs = refs[16:]
    if continued:
        cnt0_ref, refs = refs[N_MIXER_STREAMS], refs[N_MIXER_STREAMS + 1:]
    (x1_ref, h2p_ref, e_ref, wt_ref, rank_ref, cnt_ref,
     gx_scr, p_scr, q_scr, r_scr, y_scr, cnt_scr) = refs

    i = pl.program_id(0)
    j = i % nt

    @pl.when(i == 0)
    def _():
        gx_scr[0:CONV_HALO, :] = jnp.zeros((CONV_HALO, CONV_W), F32)
        for scr in (p_scr, q_scr, r_scr):
            scr[0:POOL_SLACK + POOL_HALO, :] = jnp.zeros((POOL_SLACK + POOL_HALO, POOL_W), F32)
        cnt_scr[...] = cnt0_ref[...] if continued else jnp.zeros_like(cnt_scr)

    if fused:
        x = _combine_tile(ya0_ref, ya1_ref, wtp_ref, xprev_ref[...], adap_ref[0][5:6], ln2g_ref, ln2b_ref,
                          alpha=alpha)
    else:
        x = x_ref[0]
    _mixer_tile(j, x, ada_ref, win_ref, wout_ref, convw_ref, convb_ref, poolw_ref, pscale_ref,
                lng_ref, lnb_ref, sguw_ref, sgub_ref, ln1g_ref, ln1b_ref, wr_ref, br_ref, tri_ref,
                x1_ref, h2p_ref, e_ref, wt_ref, rank_ref, cnt_ref, gx_scr, p_scr, q_scr, r_scr, y_scr, cnt_scr,
                ts=ts, alpha=alpha)


def _mixer_tile(j, x, ada_ref, win_ref, wout_ref, convw_ref, convb_ref, poolw_ref, pscale_ref,
                lng_ref, lnb_ref, sguw_ref, sgub_ref, ln1g_ref, ln1b_ref, wr_ref, br_ref, tri_ref,
                x1_ref, h2p_ref, e_ref, wt_ref, rank_ref, cnt_ref, gx_scr, p_scr, q_scr, r_scr, y_scr, cnt_scr,
                *, ts, alpha):
    first_of_seq = j == 0
    ada = ada_ref[0]
    sh1, mul1, g1 = ada[0:1], ada[1:2], ada[2:3]
    sh2, mul2 = ada[3:4], ada[4:5]
    hb = (x * mul1 + sh1).astype(BF16)

    def proj(lo, hi):
        return jnp.dot(hb, win_ref[:, lo:hi], preferred_element_type=F32)

    gcxc = proj(OFF_GC, OFF_P)
    g = gcxc[:, :CONV_W] * gcxc[:, CONV_W:]
    gx_scr[CONV_HALO:CONV_HALO + ts, :] = g
    cw = convw_ref[...]
    gx_scr[0:CONV_HALO, :] = jnp.where(first_of_seq, 0.0, gx_scr[0:CONV_HALO, :])
    conv = (cw[0:1] * gx_scr[CONV_HALO - 2:CONV_HALO - 2 + ts, :]
            + cw[1:2] * gx_scr[CONV_HALO - 1:CONV_HALO - 1 + ts, :]
            + cw[2:3] * g + convb_ref[...])
    gx_scr[0:CONV_HALO, :] = g[ts - CONV_HALO:ts, :]
    y_scr[:, 0:CONV_W] = (proj(OFF_GB, OFF_GC) * conv).astype(BF16)

    p = proj(OFF_P, OFF_U)
    top = POOL_SLACK
    n_ext = POOL_HALO + ts
    p_scr[top + POOL_HALO:top + n_ext, :] = p
    p_scr[top:top + POOL_HALO, :] = jnp.where(first_of_seq, 0.0, p_scr[top:top + POOL_HALO, :])
    w2 = p_scr[top:top + n_ext, :] + p_scr[top - 1:top - 1 + n_ext, :]
    q_scr[top:top + n_ext, :] = w2
    w4 = w2 + q_scr[top - 2:top - 2 + n_ext, :]
    r_scr[top:top + n_ext, :] = w4
    w8 = w4 + r_scr[top - 4:top - 4 + n_ext, :]
    q_scr[top:top + n_ext, :] = w8
    w16 = w8 + q_scr[top - 8:top - 8 + n_ext, :]
    s2, s4 = w2[POOL_HALO:, 0:LANES], w4[POOL_HALO:, 0:LANES]
    s8, s16 = w8[POOL_HALO:, LANES:2 * LANES], w16[POOL_HALO:, LANES:2 * LANES]
    p_scr[top:top + POOL_HALO, :] = p[ts - POOL_HALO:ts, :]
    lane = lax.broadcasted_iota(jnp.int32, (ts, LANES), 1)
    tpos = (lax.broadcasted_iota(jnp.int32, (ts, LANES), 0) + (j * ts + 1)).astype(F32)
    lo_half = lane < HEAD_DIM
    cnt_a = jnp.minimum(tpos, jnp.where(lo_half, float(POOL_WINDOWS[0]), float(POOL_WINDOWS[1])))
    cnt_b = jnp.minimum(tpos, jnp.where(lo_half, float(POOL_WINDOWS[2]), float(POOL_WINDOWS[3])))
    pooled = jnp.concatenate([jnp.where(lo_half, s2, s4) / cnt_a,
                              jnp.where(lo_half, s8, s16) / cnt_b], axis=1) - p
    mixed = jnp.dot(pooled.astype(BF16), poolw_ref[...], preferred_element_type=F32)
    y_scr[:, YOFF_POOL:YOFF_POOL + POOL_W] = (mixed * pscale_ref[...]).astype(BF16)

    v = proj(OFF_V, D_IN)
    vnb = _layer_norm(v, lng_ref[...], lnb_ref[...]).astype(BF16)
    u = proj(OFF_U, OFF_V)
    row_c = lax.broadcasted_iota(jnp.int32, (CHUNK, 2 * CHUNK), 0)
    col_c = lax.broadcasted_iota(jnp.int32, (CHUNK, 2 * CHUNK), 1)
    causal = (col_c & (CHUNK - 1)) <= row_c
    lo_lanes = lax.broadcasted_iota(jnp.int32, (CHUNK, LANES), 1) < HEAD_DIM
    zero_b = jnp.zeros((CHUNK, LANES), BF16)
    for hp in range(SGU_W // LANES):
        wl = jnp.where(causal, sguw_ref[hp], 0.0).astype(BF16)
        bias = sgub_ref[hp]
        for ci in range(ts // CHUNK):
            rs = slice(ci * CHUNK, (ci + 1) * CHUNK)
            vc = vnb[rs, hp * LANES:(hp + 1) * LANES]
            rhs = jnp.concatenate([jnp.where(lo_lanes, vc, zero_b),
                                   jnp.where(lo_lanes, zero_b, vc)], axis=0)
            mixed_c = jnp.dot(wl, rhs, preferred_element_type=F32) + bias
            y_scr[rs, YOFF_SGU + hp * LANES:YOFF_SGU + (hp + 1) * LANES] = (
                u[rs, hp * LANES:(hp + 1) * LANES] * mixed_c).astype(BF16)

    yo = jnp.dot(y_scr[...], wout_ref[...], preferred_element_type=F32)
    x1 = _layer_norm(alpha * x + g1 * yo, ln1g_ref[...], ln1b_ref[...])
    x1_ref[0] = x1
    h2b = (x1 * mul2 + sh2).astype(BF16)
    h2p_ref[...] = _pack_bf16_pairs(h2b.astype(F32))

    logits = jnp.dot(h2b, wr_ref[...], preferred_element_type=F32) + br_ref[...]
    lt = logits.T[0:N_EXPERTS, :]
    ex = jnp.exp(lt - jnp.max(lt, axis=0, keepdims=True))
    probs = ex / jnp.sum(ex, axis=0, keepdims=True)
    sub_iota = lax.broadcasted_iota(jnp.int32, (EXPERTS_PER_GROUP, ts), 0).astype(F32)
    best = None
    for gi in range(N_GROUPS):
        m1, i1, m2, i2 = _top2_of_group(
            probs[gi * EXPERTS_PER_GROUP:(gi + 1) * EXPERTS_PER_GROUP, :], sub_iota)
        score = m1 + m2
        cand = (score, m1, m2, i1 + float(gi * EXPERTS_PER_GROUP), i2 + float(gi * EXPERTS_PER_GROUP))
        if best is None:
            best = cand
        else:
            better = score > best[0]
            best = tuple(jnp.where(better, c, o) for c, o in zip(cand, best))
    _, p1, p2, e1, e2 = best
    den = p1 + p2
    w1 = p1 / den
    w2 = p2 / den

    wt_ref[...] = jnp.concatenate([w1, w2], axis=0)
    e_ref[...] = jnp.concatenate([e1, e2], axis=0).astype(jnp.int32)

    ex_iota = lax.broadcasted_iota(jnp.int32, (N_EXPERTS, ts), 0).astype(F32)
    oh1 = ex_iota == e1
    oh2 = ex_iota == e2
    either = jnp.where(oh1 | oh2, 1.0, 0.0)
    seen = jnp.dot(either.astype(BF16), tri_ref[...], preferred_element_type=F32) + cnt_scr[:, 0:1]
    r1 = jnp.sum(jnp.where(oh1, seen, 0.0), axis=0, keepdims=True)
    r2 = jnp.sum(jnp.where(oh2, seen, 0.0), axis=0, keepdims=True)
    rank_ref[...] = jnp.concatenate([r1, r2], axis=0).astype(jnp.int32)
    cnt_new = cnt_scr[...] + jnp.sum(either, axis=1, keepdims=True)
    cnt_scr[...] = cnt_new
    cnt_ref[...] = cnt_new


def _mixer_call(x, ada_l, w_in, w_out, conv_w, conv_b, poolw_bd, pool_scale, ln_g, ln_b,
                sguw_pair, sgub_pair, ln1_g, ln1_b, wr_pad, br_pad, tri, alpha, prev=None,
                seq0=0, n_seq=None, earlier=None):
    bsz, s, d = x.shape
    ts = SEQ_TILE
    nt = s // ts
    n_seq = bsz if n_seq is None else n_seq
    n_tiles = n_seq * nt
    tile0 = seq0 * nt
    n_tok = bsz * s
    fused = prev is not None
    continued = earlier is not None
    const2 = lambda i: (0, 0)
    const3 = lambda i: (0, 0, 0)
    seq_of = lambda i: (seq0 + i // nt, 0, 0)
    tile3 = lambda i: (seq0 + i // nt, i % nt, 0)
    tile2 = lambda i: (tile0 + i, 0)
    by_token = lambda i: (0, tile0 + i)
    once = pl.Buffered(1)
    in_specs = [
        pl.BlockSpec((1, ADA_CHUNKS, d), seq_of),
        pl.BlockSpec((d, D_IN), const2, pipeline_mode=once),
        pl.BlockSpec((d, d), const2, pipeline_mode=once),
        pl.BlockSpec((CONV_K, CONV_W), const2),
        pl.BlockSpec((1, CONV_W), const2),
        pl.BlockSpec((POOL_W, POOL_W), const2),
        pl.BlockSpec((1, POOL_W), const2),
        pl.BlockSpec((1, SGU_W), const2),
        pl.BlockSpec((1, SGU_W), const2),
        pl.BlockSpec((SGU_W // LANES, CHUNK, 2 * CHUNK), const3),
        pl.BlockSpec((SGU_W // LANES, CHUNK, LANES), const3),
        pl.BlockSpec((1, d), const2),
        pl.BlockSpec((1, d), const2),
        pl.BlockSpec((d, LANES), const2),
        pl.BlockSpec((1, LANES), const2),
        pl.BlockSpec((ts, ts), const2, pipeline_mode=once),
    ]
    operands = [ada_l, w_in, w_out, conv_w, conv_b, poolw_bd, pool_scale, ln_g, ln_b,
                sguw_pair, sgub_pair, ln1_g, ln1_b, wr_pad, br_pad, tri]
    if fused:
        ya, wt_tok, ada_prev, ln2_g, ln2_b = prev
        in_specs = [
            pl.BlockSpec((ts, d), tile2),
            pl.BlockSpec((ts, PACK_W), lambda i: (i, 0)),
            pl.BlockSpec((ts, PACK_W), lambda i: (n_tiles + i, 0)),
            pl.BlockSpec((ts, TOP_K), tile2),
            pl.BlockSpec((1, ADA_CHUNKS, d), seq_of),
            pl.BlockSpec((1, d), const2),
            pl.BlockSpec((1, d), const2),
        ] + in_specs
        operands = [x.reshape(n_tok, d), ya, ya, wt_tok, ada_prev, ln2_g, ln2_b] + operands
    else:
        in_specs = [pl.BlockSpec((1, ts, d), tile3)] + in_specs
        operands = [x] + operands
    aliases = {}
    if continued:
        aliases = {len(operands) + k: k for k in range(N_MIXER_STREAMS)}
        in_specs = in_specs + [pl.BlockSpec(memory_space=pl.ANY)] * N_MIXER_STREAMS + [
            pl.BlockSpec((N_EXPERTS, LANES), const2)]
        operands = operands + list(earlier)
    kern = functools.partial(_mixer_kernel, ts=ts, alpha=alpha, nt=nt, fused=fused, continued=continued)
    pool_rows = POOL_SLACK + POOL_HALO + ts
    return pl.pallas_call(
        kern,
        grid=(n_tiles,),
        in_specs=in_specs,
        input_output_aliases=aliases,
        out_specs=[
            pl.BlockSpec((1, ts, d), tile3),
            pl.BlockSpec((ts, PACK_W), tile2),
            pl.BlockSpec((TOP_K, ts), by_token),
            pl.BlockSpec((TOP_K, ts), by_token),
            pl.BlockSpec((TOP_K, ts), by_token),
            pl.BlockSpec((N_EXPERTS, LANES), const2),
        ],
        out_shape=[
            jax.ShapeDtypeStruct((bsz, s, d), F32),
            jax.ShapeDtypeStruct((n_tok, PACK_W), jnp.int32),
            jax.ShapeDtypeStruct((TOP_K, n_tok), jnp.int32),
            jax.ShapeDtypeStruct((TOP_K, n_tok), F32),
            jax.ShapeDtypeStruct((TOP_K, n_tok), jnp.int32),
            jax.ShapeDtypeStruct((N_EXPERTS, LANES), F32),
        ],
        scratch_shapes=[
            pltpu.VMEM((CONV_HALO + ts, CONV_W), F32),
            pltpu.VMEM((pool_rows, POOL_W), F32),
            pltpu.VMEM((pool_rows, POOL_W), F32),
            pltpu.VMEM((pool_rows, POOL_W), F32),
            pltpu.VMEM((ts, d), BF16),
            pltpu.VMEM((N_EXPERTS, LANES), F32),
        ],
        compiler_params=pltpu.CompilerParams(
            dimension_semantics=("arbitrary",), vmem_limit_bytes=VMEM_LIMIT),
        name="mixer_router",
    )(*operands)


def _sc_worker_chunks(n_chunks):
    per_worker = n_chunks // SC_WORKERS
    worker = lax.axis_index("s") * 2 + lax.axis_index("c")
    return worker * per_worker, per_worker


def _dispatch_call(h2p, dest, n_rows):
    n_tok, width = h2p.shape
    n_chunks = n_tok // SC_ROWS
    assert n_chunks % SC_WORKERS == 0

    def body(h_hbm, dest_hbm, xs_hbm, rows_v, idx_v):
        first, per_worker = _sc_worker_chunks(n_chunks)

        @pl.loop(0, per_worker)
        def _(c):
            chunk = first + c
            pltpu.sync_copy(h_hbm.at[pl.ds(chunk * SC_ROWS, SC_ROWS)], rows_v)
            for k in range(TOP_K):
                pltpu.sync_copy(dest_hbm.at[k, chunk], idx_v.at[k])
                pltpu.sync_copy(rows_v, xs_hbm.at[idx_v.at[k]])

    return pl.kernel(
        body,
        out_type=jax.ShapeDtypeStruct((n_rows, width), jnp.int32),
        mesh=plsc.VectorSubcoreMesh(core_axis_name="c", subcore_axis_name="s"),
        scratch_types=[pltpu.VMEM((SC_ROWS, width), jnp.int32), pltpu.VMEM((TOP_K, SC_ROWS), jnp.int32)],
        name="dispatch",
    )(h2p, dest)


def _collect_call(ys, dest, tok0, n_tok):
    width = ys.shape[1]
    n_chunks = n_tok // SC_ROWS
    chunk0 = tok0 // SC_ROWS
    assert n_chunks % SC_WORKERS == 0 and tok0 % SC_ROWS == 0

    def body(ys_hbm, dest_hbm, ya_hbm, rows_v, idx_v):
        first, per_worker = _sc_worker_chunks(n_chunks)

        @pl.loop(0, per_worker)
        def _(c):
            chunk = first + c
            for k in range(TOP_K):
                pltpu.sync_copy(dest_hbm.at[k, chunk0 + chunk], idx_v.at[k])
                pltpu.sync_copy(ys_hbm.at[idx_v.at[k]], rows_v)
                pltpu.sync_copy(rows_v, ya_hbm.at[pl.ds(k * n_tok + chunk * SC_ROWS, SC_ROWS)])

    return pl.kernel(
        body,
        out_type=jax.ShapeDtypeStruct((TOP_K * n_tok, width), jnp.int32),
        mesh=plsc.VectorSubcoreMesh(core_axis_name="c", subcore_axis_name="s"),
        scratch_types=[pltpu.VMEM((SC_ROWS, width), jnp.int32), pltpu.VMEM((TOP_K, SC_ROWS), jnp.int32)],
        name="collect",
    )(ys, dest)


def _ffn_kernel(b0_ref, nb_ref, rows_ref, tot_ref, xs_ref, wg_ref, wu_ref, wd_ref, ys_ref,
                xbuf, obuf, wbuf_g, wbuf_u, wbuf_d, wg_scr, wu_scr, wd_scr, gsem, osem, wsem,
                *, layer, n_blocks):
    e = pl.program_id(0)
    n_exp = pl.num_programs(0)
    b0 = b0_ref[e]
    nb = nb_ref[e]
    n_used = tot_ref[0]

    def in_copy(g):
        slot = g % GATHER_DEPTH
        return pltpu.make_async_copy(_rows(xs_ref, g * ROW_BLOCK, ROW_BLOCK), xbuf.at[slot], gsem.at[slot])

    def out_copy(g, slot):
        return pltpu.make_async_copy(obuf.at[slot], _rows(ys_ref, g * ROW_BLOCK, ROW_BLOCK), osem.at[slot])

    def weight_copies(expert, slot):
        return [pltpu.make_async_copy(src.at[layer, expert], dst.at[slot], wsem.at[slot])
                for src, dst in ((wg_ref, wbuf_g), (wu_ref, wbuf_u), (wd_ref, wbuf_d))]

    @pl.when(e == 0)
    def _():
        for first in range(WEIGHT_DEPTH - 1):
            for cp in weight_copies(first, first):
                cp.start(priority=WEIGHT_DMA_PRIORITY)
        for ahead in range(GATHER_DEPTH - 1):
            @pl.when(ahead < n_used)
            def _():
                in_copy(ahead).start()

    @pl.when(e + (WEIGHT_DEPTH - 1) < n_exp)
    def _():
        for cp in weight_copies(e + (WEIGHT_DEPTH - 1), (e + (WEIGHT_DEPTH - 1)) % WEIGHT_DEPTH):
            cp.start(priority=WEIGHT_DMA_PRIORITY)

    wslot = e % WEIGHT_DEPTH
    for cp in weight_copies(e, wslot):
        cp.wait()

    @pl.when(nb > 0)
    def _():
        wg_scr[...] = wbuf_g[wslot].astype(BF16)
        wu_scr[...] = wbuf_u[wslot].astype(BF16)
        wd_scr[...] = wbuf_d[wslot].astype(BF16)

    def block(k, carry):
        g = b0 + k
        islot = g % GATHER_DEPTH
        oslot = g % 2

        @pl.when(g + (GATHER_DEPTH - 1) < n_used)
        def _():
            in_copy(g + (GATHER_DEPTH - 1)).start()

        in_copy(g).wait()

        @pl.when(g >= 2)
        def _():
            out_copy(g - 2, oslot).wait()

        def chain(half):
            rows = pl.ds(half * SUB_BLOCK, SUB_BLOCK)
            xb = jnp.concatenate(_unpack_bf16_pairs(xbuf[islot, rows, :]), axis=1)
            gate = jnp.dot(xb, wg_scr[...], preferred_element_type=F32)
            up = jnp.dot(xb, wu_scr[...], preferred_element_type=F32)
            act = (gate * jax.nn.sigmoid(gate) * up).astype(BF16)
            y = jnp.dot(act, wd_scr[...], preferred_element_type=F32)
            obuf[oslot, rows, :] = _pack_bf16_pairs(y.astype(BF16).astype(F32))

        both = rows_ref[e] - k * ROW_BLOCK > SUB_BLOCK

        @pl.when(both)
        def _():
            chain(0)
            chain(1)

        @pl.when(jnp.logical_not(both))
        def _():
            chain(0)
            obuf[oslot, pl.ds(SUB_BLOCK, SUB_BLOCK), :] = jnp.zeros((SUB_BLOCK, PACK_W), jnp.int32)

        out_copy(g, oslot).start()
        return carry

    lax.fori_loop(0, nb, block, 0)

    @pl.when(e == n_exp - 1)
    def _():
        @pl.when(n_used >= 2)
        def _():
            out_copy(n_used - 2, n_used % 2).wait()

        out_copy(n_used - 1, (n_used - 1) % 2).wait()


def _ffn_call(layer, blk_start, seg_blocks, seg_rows, n_used, n_blocks, xs, w_gate, w_up, w_down):
    d, f = w_gate.shape[-2:]
    any_spec = pl.BlockSpec(memory_space=pl.ANY)
    return pl.pallas_call(
        functools.partial(_ffn_kernel, layer=layer, n_blocks=n_blocks),
        grid_spec=pltpu.PrefetchScalarGridSpec(
            num_scalar_prefetch=4,
            grid=(N_EXPERTS,),
            in_specs=[any_spec, any_spec, any_spec, any_spec],
            out_specs=any_spec,
            scratch_shapes=[
                pltpu.VMEM((GATHER_DEPTH, ROW_BLOCK, PACK_W), jnp.int32),
                pltpu.VMEM((2, ROW_BLOCK, PACK_W), jnp.int32),
                pltpu.VMEM((WEIGHT_DEPTH, d, f), F32),
                pltpu.VMEM((WEIGHT_DEPTH, d, f), F32),
                pltpu.VMEM((WEIGHT_DEPTH, f, d), F32),
                pltpu.VMEM((d, f), BF16),
                pltpu.VMEM((d, f), BF16),
                pltpu.VMEM((f, d), BF16),
                pltpu.SemaphoreType.DMA((GATHER_DEPTH,)),
                pltpu.SemaphoreType.DMA((2,)),
                pltpu.SemaphoreType.DMA((WEIGHT_DEPTH,)),
            ],
        ),
        out_shape=jax.ShapeDtypeStruct((n_blocks * ROW_BLOCK, PACK_W), jnp.int32),
        compiler_params=pltpu.CompilerParams(
            dimension_semantics=("arbitrary",), vmem_limit_bytes=VMEM_LIMIT),
        name="expert_ffn",
    )(blk_start, seg_blocks, seg_rows, n_used, xs, w_gate, w_up, w_down)


def _combine_kernel(ya0_ref, ya1_ref, wt_ref, x1_ref, ada_ref, g_ref, b_ref, *rest, alpha):
    o_ref = rest[-1]
    o_ref[...] = _combine_tile(ya0_ref, ya1_ref, wt_ref, x1_ref[...], ada_ref[0][5:6], g_ref, b_ref, alpha=alpha)


def _combine_call(ya, wt_tok, x1_flat, ada_l, ln_g, ln_b, alpha, seq, tok0, earlier=None):
    n_tok, d = x1_flat.shape
    ts = SEQ_TILE
    n_tiles = ya.shape[0] // (TOP_K * ts)
    tile0 = tok0 // ts
    per_seq = seq // ts
    tile = lambda i: (tile0 + i, 0)
    in_specs = [
        pl.BlockSpec((ts, PACK_W), lambda i: (i, 0)),
        pl.BlockSpec((ts, PACK_W), lambda i: (n_tiles + i, 0)),
        pl.BlockSpec((ts, TOP_K), tile),
        pl.BlockSpec((ts, d), tile),
        pl.BlockSpec((1, ADA_CHUNKS, d), lambda i: ((tile0 + i) // per_seq, 0, 0)),
        pl.BlockSpec((1, d), lambda i: (0, 0)),
        pl.BlockSpec((1, d), lambda i: (0, 0)),
    ]
    operands = [ya, ya, wt_tok, x1_flat, ada_l, ln_g, ln_b]
    aliases = {}
    if earlier is not None:
        aliases = {len(operands): 0}
        in_specs.append(pl.BlockSpec(memory_space=pl.ANY))
        operands.append(earlier)
    return pl.pallas_call(
        functools.partial(_combine_kernel, alpha=alpha),
        grid=(n_tiles,),
        in_specs=in_specs,
        out_specs=pl.BlockSpec((ts, d), tile),
        out_shape=jax.ShapeDtypeStruct((n_tok, d), F32),
        input_output_aliases=aliases,
        compiler_params=pltpu.CompilerParams(
            dimension_semantics=("arbitrary",), vmem_limit_bytes=VMEM_LIMIT),
        name="combine_ln",
    )(*operands)


def kernel(x, c, w_ada, b_ada, w_in, conv_w, conv_b, pool_w, pool_scale, sgu_ln_g, sgu_ln_b,
           sgu_w, sgu_b, w_out, ln1_g, ln1_b, w_router, b_router, w_gate, w_up, w_down,
           ln2_g, ln2_b):
    bsz, seq, d = x.shape
    depth = w_ada.shape[0]
    n_tok = bsz * seq
    alpha = (2 * depth) ** 0.25
    assert d == D_MODEL and seq % SEQ_TILE == 0 and SEQ_TILE % CHUNK == 0

    ada = _ada_call(c, w_ada, b_ada).reshape(depth, bsz, ADA_CHUNKS, d)

    w_in_b = w_in.astype(BF16)
    w_out_b = w_out.astype(BF16)
    eye_g = jnp.eye(POOL_W // HEAD_DIM, dtype=F32)
    poolw_bd = jnp.einsum('lgcd,gh->lgchd', pool_w, eye_g).reshape(depth, POOL_W, POOL_W).astype(BF16)
    n_pairs = SGU_W // LANES
    sguw_pair = sgu_w.reshape(depth, n_pairs, 2, CHUNK, CHUNK).transpose(0, 1, 3, 2, 4).reshape(
        depth, n_pairs, CHUNK, 2 * CHUNK)
    sgub_pair = jnp.repeat(sgu_b.transpose(0, 2, 1), HEAD_DIM, axis=-1).reshape(
        depth, CHUNK, n_pairs, LANES).transpose(0, 2, 1, 3)
    wr_pad = jnp.pad(w_router, ((0, 0), (0, LANES - N_EXPERTS))).astype(BF16)
    br_pad = jnp.pad(b_router, (0, LANES - N_EXPERTS)).reshape(1, LANES)

    n_steps = (n_tok * TOP_K) // ROW_BLOCK + N_EXPERTS
    experts = jnp.arange(N_EXPERTS, dtype=jnp.int32)
    tri = jnp.triu(jnp.ones((SEQ_TILE, SEQ_TILE), BF16), k=1)

    first_seqs = bsz // FIRST_PART_DIVISOR
    part_seq0 = (0, first_seqs) if first_seqs else (0,)
    part_seqs = (first_seqs, bsz - first_seqs) if first_seqs else (bsz,)
    parts = list(zip(part_seq0, part_seqs))
    prev = None
    for l in range(depth):
        layer_args = (
            x, ada[l], w_in_b[l], w_out_b[l], conv_w[l], conv_b[l].reshape(1, -1), poolw_bd[l],
            pool_scale[l].reshape(1, -1), sgu_ln_g[l].reshape(1, -1), sgu_ln_b[l].reshape(1, -1),
            sguw_pair[l], sgub_pair[l], ln1_g[l].reshape(1, -1), ln1_b[l].reshape(1, -1),
            wr_pad, br_pad, tri, alpha)
        if prev is None:
            outs = _mixer_call(*layer_args)
        else:
            ya_parts, *prev_rest = prev
            outs = None
            for ya_part, (seq0, n_seq) in zip(ya_parts, parts):
                outs = _mixer_call(*layer_args, prev=(ya_part, *prev_rest),
                                   seq0=seq0, n_seq=n_seq, earlier=outs)
        x1, h2p, e_idx, wts, rank, counts = outs
        sizes = counts[:, 0].astype(jnp.int32)
        seg_blocks = (sizes + ROW_BLOCK - 1) // ROW_BLOCK
        blk_end = jnp.cumsum(seg_blocks)
        blk_start = (blk_end - seg_blocks).astype(jnp.int32)
        n_used = blk_end[-1:].astype(jnp.int32)
        seg_row0 = jnp.sum(jnp.where(e_idx[:, :, None] == experts[None, None, :],
                                     (blk_start * ROW_BLOCK)[None, None, :], 0), axis=2)
        dest = (seg_row0 + rank).astype(jnp.int32).reshape(TOP_K, n_tok // SC_ROWS, SC_ROWS)

        xs = _dispatch_call(h2p, dest, n_steps * ROW_BLOCK)
        ys = _ffn_call(l, blk_start, seg_blocks.astype(jnp.int32), sizes, n_used, n_steps,
                       xs, w_gate, w_up, w_down)
        ya_parts = [_collect_call(ys, dest, seq0 * seq, n_seq * seq) for seq0, n_seq in parts]
        wt_tok = wts.T
        prev = (ya_parts, wt_tok, ada[l], ln2_g[l].reshape(1, -1), ln2_b[l].reshape(1, -1))
        x = x1
    ya_parts, wt_tok, ada_last, ln2_g_last, ln2_b_last = prev
    out = None
    for ya_part, (seq0, _) in zip(ya_parts, parts):
        out = _combine_call(ya_part, wt_tok, x.reshape(n_tok, d), ada_last, ln2_g_last, ln2_b_last,
                            alpha, seq, seq0 * seq, earlier=out)
    return out.reshape(bsz, seq, d)
```

```python
import functools

import jax
import jax.numpy as jnp
from jax import lax
from jax.experimental import pallas as pl
from jax.experimental.pallas import tpu as pltpu
from jax.experimental.pallas import tpu_sc as plsc

D_MODEL = 1024
HEAD_DIM = D_MODEL // 16
CONV_W = 6 * HEAD_DIM
POOL_W = 4 * HEAD_DIM
SGU_W = 6 * HEAD_DIM
D_IN = 3 * CONV_W + POOL_W + 2 * SGU_W
CONV_K = 3
POOL_WINDOWS = (2, 4, 8, 16)
CHUNK = 128
N_EXPERTS = 32
N_GROUPS = 4
EXPERTS_PER_GROUP = N_EXPERTS // N_GROUPS
TOP_K = 2
D_FF = D_MODEL // 2
ADA_CHUNKS = 6
LN_EPS = 1e-5

OFF_GB = 0
OFF_GC = CONV_W
OFF_P = 3 * CONV_W
OFF_U = OFF_P + POOL_W
OFF_V = OFF_U + SGU_W
YOFF_POOL = CONV_W
YOFF_SGU = CONV_W + POOL_W

LANES = 128
SUBLANES = 8
CONV_HALO = 8
POOL_HALO = 16
POOL_SLACK = 8
SEQ_TILE = 1024
SUB_BLOCK = 256
ROW_BLOCK = 2 * SUB_BLOCK
ADA_TILE = 1536
GATHER_DEPTH = 3
WEIGHT_DEPTH = 3
WEIGHT_DMA_PRIORITY = 1
VMEM_LIMIT = 60 * 1024 * 1024

PACK_W = D_MODEL // 2
SC_ROWS = 128
SC_WORKERS = 32
N_MIXER_STREAMS = 5
FIRST_PART_DIVISOR = 4

F32 = jnp.float32
BF16 = jnp.bfloat16


def _layer_norm(r, g, b):
    mu = jnp.mean(r, axis=-1, keepdims=True)
    d = r - mu
    var = jnp.mean(d * d, axis=-1, keepdims=True)
    return d * lax.rsqrt(var + LN_EPS) * g + b


def _rows(ref, start, size):
    return ref.at[pl.ds(pl.multiple_of(start, SUBLANES), size), :]


def _pack_bf16_pairs(v):
    bits = pltpu.bitcast(v, jnp.int32)
    return bits[:, 0:PACK_W] | lax.shift_right_logical(bits[:, PACK_W:2 * PACK_W], 16)


def _unpack_pairs_f32(words):
    return jnp.concatenate([pltpu.bitcast(words & jnp.int32(-65536), F32),
                            pltpu.bitcast(lax.shift_left(words, 16), F32)], axis=1)


def _unpack_bf16_pairs(words):
    first = pltpu.bitcast(words & jnp.int32(-65536), F32).astype(BF16)
    second = pltpu.bitcast(lax.shift_left(words, 16), F32).astype(BF16)
    return first, second


def _ada_kernel(c_ref, w_ref, b_ref, o_ref):
    c = c_ref[...]
    c_act = (c * jax.nn.sigmoid(c)).astype(BF16)
    col = pl.program_id(1) * ADA_TILE + lax.broadcasted_iota(jnp.int32, (1, ADA_TILE), 1)
    chunk = col // D_MODEL
    one = jnp.where((chunk == 1) | (chunk == 4), 1.0, 0.0)
    o_ref[0] = jnp.dot(c_act, w_ref[0].astype(BF16), preferred_element_type=F32) + (b_ref[0] + one)


def _ada_call(c, w_ada, b_ada):
    depth, d, n = w_ada.shape
    bsz = c.shape[0]
    return pl.pallas_call(
        _ada_kernel,
        grid=(depth, n // ADA_TILE),
        in_specs=[
            pl.BlockSpec((bsz, d), lambda l, j: (0, 0)),
            pl.BlockSpec((1, d, ADA_TILE), lambda l, j: (l, 0, j)),
            pl.BlockSpec((1, 1, ADA_TILE), lambda l, j: (l, 0, j)),
        ],
        out_specs=pl.BlockSpec((1, bsz, ADA_TILE), lambda l, j: (l, 0, j)),
        out_shape=jax.ShapeDtypeStruct((depth, bsz, n), F32),
        compiler_params=pltpu.CompilerParams(
            dimension_semantics=("arbitrary", "arbitrary"), vmem_limit_bytes=VMEM_LIMIT),
        name="ada",
    )(c, w_ada, b_ada.reshape(depth, 1, n))


def _top2_of_group(pg, sub_iota):
    big = float(EXPERTS_PER_GROUP)
    m1 = jnp.max(pg, axis=0, keepdims=True)
    i1 = jnp.min(jnp.where(pg == m1, sub_iota, big), axis=0, keepdims=True)
    rest = jnp.where(sub_iota == i1, -1.0, pg)
    m2 = jnp.max(rest, axis=0, keepdims=True)
    i2 = jnp.min(jnp.where(rest == m2, sub_iota, big), axis=0, keepdims=True)
    return m1, i1, m2, i2


def _combine_tile(ya0_ref, ya1_ref, wt_ref, x1, gate, g_ref, b_ref, *, alpha):
    wt = wt_ref[...]
    y = wt[:, 0:1] * _unpack_pairs_f32(ya0_ref[...]) + wt[:, 1:2] * _unpack_pairs_f32(ya1_ref[...])
    return _layer_norm(alpha * x1 + gate * y, g_ref[...], b_ref[...])


def _mixer_kernel(*refs, ts, alpha, nt, fused, continued):
    if fused:
        xprev_ref, ya0_ref, ya1_ref, wtp_ref, adap_ref, ln2g_ref, ln2b_ref = refs[:7]
        refs = refs[7:]
    else:
        x_ref, refs = refs[0], refs[1:]
    (ada_ref, win_ref, wout_ref, convw_ref, convb_ref, poolw_ref, pscale_ref, lng_ref, lnb_ref,
     sguw_ref, sgub_ref, ln1g_ref, ln1b_ref, wr_ref, br_ref, tri_ref) = refs[:16]
    refs = refs[16:]
    if continued:
        cnt0_ref, refs = refs[N_MIXER_STREAMS], refs[N_MIXER_STREAMS + 1:]
    (x1_ref, h2p_ref, e_ref, wt_ref, rank_ref, cnt_ref,
     gx_scr, p_scr, q_scr, r_scr, y_scr, cnt_scr) = refs

    i = pl.program_id(0)
    j = i % nt

    @pl.when(i == 0)
    def _():
        gx_scr[0:CONV_HALO, :] = jnp.zeros((CONV_HALO, CONV_W), F32)
        for scr in (p_scr, q_scr, r_scr):
            scr[0:POOL_SLACK + POOL_HALO, :] = jnp.zeros((POOL_SLACK + POOL_HALO, POOL_W), F32)
        cnt_scr[...] = cnt0_ref[...] if continued else jnp.zeros_like(cnt_scr)

    if fused:
        x = _combine_tile(ya0_ref, ya1_ref, wtp_ref, xprev_ref[...], adap_ref[0][5:6], ln2g_ref, ln2b_ref,
                          alpha=alpha)
    else:
        x = x_ref[0]
    _mixer_tile(j, x, ada_ref, win_ref, wout_ref, convw_ref, convb_ref, poolw_ref, pscale_ref,
                lng_ref, lnb_ref, sguw_ref, sgub_ref, ln1g_ref, ln1b_ref, wr_ref, br_ref, tri_ref,
                x1_ref, h2p_ref, e_ref, wt_ref, rank_ref, cnt_ref, gx_scr, p_scr, q_scr, r_scr, y_scr, cnt_scr,
                ts=ts, alpha=alpha)


def _mixer_tile(j, x, ada_ref, win_ref, wout_ref, convw_ref, convb_ref, poolw_ref, pscale_ref,
                lng_ref, lnb_ref, sguw_ref, sgub_ref, ln1g_ref, ln1b_ref, wr_ref, br_ref, tri_ref,
                x1_ref, h2p_ref, e_ref, wt_ref, rank_ref, cnt_ref, gx_scr, p_scr, q_scr, r_scr, y_scr, cnt_scr,
                *, ts, alpha):
    first_of_seq = j == 0
    ada = ada_ref[0]
    sh1, mul1, g1 = ada[0:1], ada[1:2], ada[2:3]
    sh2, mul2 = ada[3:4], ada[4:5]
    hb = (x * mul1 + sh1).astype(BF16)

    def proj(lo, hi):
        return jnp.dot(hb, win_ref[:, lo:hi], preferred_element_type=F32)

    gcxc = proj(OFF_GC, OFF_P)
    g = gcxc[:, :CONV_W] * gcxc[:, CONV_W:]
    gx_scr[CONV_HALO:CONV_HALO + ts, :] = g
    cw = convw_ref[...]
    gx_scr[0:CONV_HALO, :] = jnp.where(first_of_seq, 0.0, gx_scr[0:CONV_HALO, :])
    conv = (cw[0:1] * gx_scr[CONV_HALO - 2:CONV_HALO - 2 + ts, :]
            + cw[1:2] * gx_scr[CONV_HALO - 1:CONV_HALO - 1 + ts, :]
            + cw[2:3] * g + convb_ref[...])
    gx_scr[0:CONV_HALO, :] = g[ts - CONV_HALO:ts, :]
    y_scr[:, 0:CONV_W] = (proj(OFF_GB, OFF_GC) * conv).astype(BF16)

    p = proj(OFF_P, OFF_U)
    top = POOL_SLACK
    n_ext = POOL_HALO + ts
    p_scr[top + POOL_HALO:top + n_ext, :] = p
    p_scr[top:top + POOL_HALO, :] = jnp.where(first_of_seq, 0.0, p_scr[top:top + POOL_HALO, :])
    w2 = p_scr[top:top + n_ext, :] + p_scr[top - 1:top - 1 + n_ext, :]
    q_scr[top:top + n_ext, :] = w2
    w4 = w2 + q_scr[top - 2:top - 2 + n_ext, :]
    r_scr[top:top + n_ext, :] = w4
    w8 = w4 + r_scr[top - 4:top - 4 + n_ext, :]
    q_scr[top:top + n_ext, :] = w8
    w16 = w8 + q_scr[top - 8:top - 8 + n_ext, :]
    s2, s4 = w2[POOL_HALO:, 0:LANES], w4[POOL_HALO:, 0:LANES]
    s8, s16 = w8[POOL_HALO:, LANES:2 * LANES], w16[POOL_HALO:, LANES:2 * LANES]
    p_scr[top:top + POOL_HALO, :] = p[ts - POOL_HALO:ts, :]
    lane = lax.broadcasted_iota(jnp.int32, (ts, LANES), 1)
    tpos = (lax.broadcasted_iota(jnp.int32, (ts, LANES), 0) + (j * ts + 1)).astype(F32)
    lo_half = lane < HEAD_DIM
    cnt_a = jnp.minimum(tpos, jnp.where(lo_half, float(POOL_WINDOWS[0]), float(POOL_WINDOWS[1])))
    cnt_b = jnp.minimum(tpos, jnp.where(lo_half, float(POOL_WINDOWS[2]), float(POOL_WINDOWS[3])))
    pooled = jnp.concatenate([jnp.where(lo_half, s2, s4) / cnt_a,
                              jnp.where(lo_half, s8, s16) / cnt_b], axis=1) - p
    mixed = jnp.dot(pooled.astype(BF16), poolw_ref[...], preferred_element_type=F32)
    y_scr[:, YOFF_POOL:YOFF_POOL + POOL_W] = (mixed * pscale_ref[...]).astype(BF16)

    v = proj(OFF_V, D_IN)
    vnb = _layer_norm(v, lng_ref[...], lnb_ref[...]).astype(BF16)
    u = proj(OFF_U, OFF_V)
    row_c = lax.broadcasted_iota(jnp.int32, (CHUNK, 2 * CHUNK), 0)
    col_c = lax.broadcasted_iota(jnp.int32, (CHUNK, 2 * CHUNK), 1)
    causal = (col_c & (CHUNK - 1)) <= row_c
    lo_lanes = lax.broadcasted_iota(jnp.int32, (CHUNK, LANES), 1) < HEAD_DIM
    zero_b = jnp.zeros((CHUNK, LANES), BF16)
    for hp in range(SGU_W // LANES):
        wl = jnp.where(causal, sguw_ref[hp], 0.0).astype(BF16)
        bias = sgub_ref[hp]
        for ci in range(ts // CHUNK):
            rs = slice(ci * CHUNK, (ci + 1) * CHUNK)
            vc = vnb[rs, hp * LANES:(hp + 1) * LANES]
            rhs = jnp.concatenate([jnp.where(lo_lanes, vc, zero_b),
                                   jnp.where(lo_lanes, zero_b, vc)], axis=0)
            mixed_c = jnp.dot(wl, rhs, preferred_element_type=F32) + bias
            y_scr[rs, YOFF_SGU + hp * LANES:YOFF_SGU + (hp + 1) * LANES] = (
                u[rs, hp * LANES:(hp + 1) * LANES] * mixed_c).astype(BF16)

    yo = jnp.dot(y_scr[...], wout_ref[...], preferred_element_type=F32)
    x1 = _layer_norm(alpha * x + g1 * yo, ln1g_ref[...], ln1b_ref[...])
    x1_ref[0] = x1
    h2b = (x1 * mul2 + sh2).astype(BF16)
    h2p_ref[...] = _pack_bf16_pairs(h2b.astype(F32))

    logits = jnp.dot(h2b, wr_ref[...], preferred_element_type=F32) + br_ref[...]
    lt = logits.T[0:N_EXPERTS, :]
    ex = jnp.exp(lt - jnp.max(lt, axis=0, keepdims=True))
    probs = ex / jnp.sum(ex, axis=0, keepdims=True)
    sub_iota = lax.broadcasted_iota(jnp.int32, (EXPERTS_PER_GROUP, ts), 0).astype(F32)
    best = None
    for gi in range(N_GROUPS):
        m1, i1, m2, i2 = _top2_of_group(
            probs[gi * EXPERTS_PER_GROUP:(gi + 1) * EXPERTS_PER_GROUP, :], sub_iota)
        score = m1 + m2
        cand = (score, m1, m2, i1 + float(gi * EXPERTS_PER_GROUP), i2 + float(gi * EXPERTS_PER_GROUP))
        if best is None:
            best = cand
        else:
            better = score > best[0]
            best = tuple(jnp.where(better, c, o) for c, o in zip(cand, best))
    _, p1, p2, e1, e2 = best
    den = p1 + p2
    w1 = p1 / den
    w2 = p2 / den

    wt_ref[...] = jnp.concatenate([w1, w2], axis=0)
    e_ref[...] = jnp.concatenate([e1, e2], axis=0).astype(jnp.int32)

    ex_iota = lax.broadcasted_iota(jnp.int32, (N_EXPERTS, ts), 0).astype(F32)
    oh1 = ex_iota == e1
    oh2 = ex_iota == e2
    either = jnp.where(oh1 | oh2, 1.0, 0.0)
    seen = jnp.dot(either.astype(BF16), tri_ref[...], preferred_element_type=F32) + cnt_scr[:, 0:1]
    r1 = jnp.sum(jnp.where(oh1, seen, 0.0), axis=0, keepdims=True)
    r2 = jnp.sum(jnp.where(oh2, seen, 0.0), axis=0, keepdims=True)
    rank_ref[...] = jnp.concatenate([r1, r2], axis=0).astype(jnp.int32)
    cnt_new = cnt_scr[...] + jnp.sum(either, axis=1, keepdims=True)
    cnt_scr[...] = cnt_new
    cnt_ref[...] = cnt_new


def _mixer_call(x, ada_l, w_in, w_out, conv_w, conv_b, poolw_bd, pool_scale, ln_g, ln_b,
                sguw_pair, sgub_pair, ln1_g, ln1_b, wr_pad, br_pad, tri, alpha, prev=None,
                seq0=0, n_seq=None, earlier=None):
    bsz, s, d = x.shape
    ts = SEQ_TILE
    nt = s // ts
    n_seq = bsz if n_seq is None else n_seq
    n_tiles = n_seq * nt
    tile0 = seq0 * nt
    n_tok = bsz * s
    fused = prev is not None
    continued = earlier is not None
    const2 = lambda i: (0, 0)
    const3 = lambda i: (0, 0, 0)
    seq_of = lambda i: (seq0 + i // nt, 0, 0)
    tile3 = lambda i: (seq0 + i // nt, i % nt, 0)
    tile2 = lambda i: (tile0 + i, 0)
    by_token = lambda i: (0, tile0 + i)
    once = pl.Buffered(1)
    in_specs = [
        pl.BlockSpec((1, ADA_CHUNKS, d), seq_of),
        pl.BlockSpec((d, D_IN), const2, pipeline_mode=once),
        pl.BlockSpec((d, d), const2, pipeline_mode=once),
        pl.BlockSpec((CONV_K, CONV_W), const2),
        pl.BlockSpec((1, CONV_W), const2),
        pl.BlockSpec((POOL_W, POOL_W), const2),
        pl.BlockSpec((1, POOL_W), const2),
        pl.BlockSpec((1, SGU_W), const2),
        pl.BlockSpec((1, SGU_W), const2),
        pl.BlockSpec((SGU_W // LANES, CHUNK, 2 * CHUNK), const3),
        pl.BlockSpec((SGU_W // LANES, CHUNK, LANES), const3),
        pl.BlockSpec((1, d), const2),
        pl.BlockSpec((1, d), const2),
        pl.BlockSpec((d, LANES), const2),
        pl.BlockSpec((1, LANES), const2),
        pl.BlockSpec((ts, ts), const2, pipeline_mode=once),
    ]
    operands = [ada_l, w_in, w_out, conv_w, conv_b, poolw_bd, pool_scale, ln_g, ln_b,
                sguw_pair, sgub_pair, ln1_g, ln1_b, wr_pad, br_pad, tri]
    if fused:
        ya, wt_tok, ada_prev, ln2_g, ln2_b = prev
        in_specs = [
            pl.BlockSpec((ts, d), tile2),
            pl.BlockSpec((ts, PACK_W), lambda i: (i, 0)),
            pl.BlockSpec((ts, PACK_W), lambda i: (n_tiles + i, 0)),
            pl.BlockSpec((ts, TOP_K), tile2),
            pl.BlockSpec((1, ADA_CHUNKS, d), seq_of),
            pl.BlockSpec((1, d), const2),
            pl.BlockSpec((1, d), const2),
        ] + in_specs
        operands = [x.reshape(n_tok, d), ya, ya, wt_tok, ada_prev, ln2_g, ln2_b] + operands
    else:
        in_specs = [pl.BlockSpec((1, ts, d), tile3)] + in_specs
        operands = [x] + operands
    aliases = {}
    if continued:
        aliases = {len(operands) + k: k for k in range(N_MIXER_STREAMS)}
        in_specs = in_specs + [pl.BlockSpec(memory_space=pl.ANY)] * N_MIXER_STREAMS + [
            pl.BlockSpec((N_EXPERTS, LANES), const2)]
        operands = operands + list(earlier)
    kern = functools.partial(_mixer_kernel, ts=ts, alpha=alpha, nt=nt, fused=fused, continued=continued)
    pool_rows = POOL_SLACK + POOL_HALO + ts
    return pl.pallas_call(
        kern,
        grid=(n_tiles,),
        in_specs=in_specs,
        input_output_aliases=aliases,
        out_specs=[
            pl.BlockSpec((1, ts, d), tile3),
            pl.BlockSpec((ts, PACK_W), tile2),
            pl.BlockSpec((TOP_K, ts), by_token),
            pl.BlockSpec((TOP_K, ts), by_token),
            pl.BlockSpec((TOP_K, ts), by_token),
            pl.BlockSpec((N_EXPERTS, LANES), const2),
        ],
        out_shape=[
            jax.ShapeDtypeStruct((bsz, s, d), F32),
            jax.ShapeDtypeStruct((n_tok, PACK_W), jnp.int32),
            jax.ShapeDtypeStruct((TOP_K, n_tok), jnp.int32),
            jax.ShapeDtypeStruct((TOP_K, n_tok), F32),
            jax.ShapeDtypeStruct((TOP_K, n_tok), jnp.int32),
            jax.ShapeDtypeStruct((N_EXPERTS, LANES), F32),
        ],
        scratch_shapes=[
            pltpu.VMEM((CONV_HALO + ts, CONV_W), F32),
            pltpu.VMEM((pool_rows, POOL_W), F32),
            pltpu.VMEM((pool_rows, POOL_W), F32),
            pltpu.VMEM((pool_rows, POOL_W), F32),
            pltpu.VMEM((ts, d), BF16),
            pltpu.VMEM((N_EXPERTS, LANES), F32),
        ],
        compiler_params=pltpu.CompilerParams(
            dimension_semantics=("arbitrary",), vmem_limit_bytes=VMEM_LIMIT),
        name="mixer_router",
    )(*operands)


def _sc_worker_chunks(n_chunks):
    per_worker = n_chunks // SC_WORKERS
    worker = lax.axis_index("s") * 2 + lax.axis_index("c")
    return worker * per_worker, per_worker


def _dispatch_call(h2p, dest, n_rows):
    n_tok, width = h2p.shape
    n_chunks = n_tok // SC_ROWS
    assert n_chunks % SC_WORKERS == 0

    def body(h_hbm, dest_hbm, xs_hbm, rows_v, idx_v):
        first, per_worker = _sc_worker_chunks(n_chunks)

        @pl.loop(0, per_worker)
        def _(c):
            chunk = first + c
            pltpu.sync_copy(h_hbm.at[pl.ds(chunk * SC_ROWS, SC_ROWS)], rows_v)
            for k in range(TOP_K):
                pltpu.sync_copy(dest_hbm.at[k, chunk], idx_v.at[k])
                pltpu.sync_copy(rows_v, xs_hbm.at[idx_v.at[k]])

    return pl.kernel(
        body,
        out_type=jax.ShapeDtypeStruct((n_rows, width), jnp.int32),
        mesh=plsc.VectorSubcoreMesh(core_axis_name="c", subcore_axis_name="s"),
        scratch_types=[pltpu.VMEM((SC_ROWS, width), jnp.int32), pltpu.VMEM((TOP_K, SC_ROWS), jnp.int32)],
        name="dispatch",
    )(h2p, dest)


def _collect_call(ys, dest, tok0, n_tok):
    width = ys.shape[1]
    n_chunks = n_tok // SC_ROWS
    chunk0 = tok0 // SC_ROWS
    assert n_chunks % SC_WORKERS == 0 and tok0 % SC_ROWS == 0

    def body(ys_hbm, dest_hbm, ya_hbm, rows_v, idx_v):
        first, per_worker = _sc_worker_chunks(n_chunks)

        @pl.loop(0, per_worker)
        def _(c):
            chunk = first + c
            for k in range(TOP_K):
                pltpu.sync_copy(dest_hbm.at[k, chunk0 + chunk], idx_v.at[k])
                pltpu.sync_copy(ys_hbm.at[idx_v.at[k]], rows_v)
                pltpu.sync_copy(rows_v, ya_hbm.at[pl.ds(k * n_tok + chunk * SC_ROWS, SC_ROWS)])

    return pl.kernel(
        body,
        out_type=jax.ShapeDtypeStruct((TOP_K * n_tok, width), jnp.int32),
        mesh=plsc.VectorSubcoreMesh(core_axis_name="c", subcore_axis_name="s"),
        scratch_types=[pltpu.VMEM((SC_ROWS, width), jnp.int32), pltpu.VMEM((TOP_K, SC_ROWS), jnp.int32)],
        name="collect",
    )(ys, dest)


def _ffn_kernel(b0_ref, nb_ref, rows_ref, tot_ref, xs_ref, wg_ref, wu_ref, wd_ref, ys_ref,
                xbuf, obuf, wbuf_g, wbuf_u, wbuf_d, wg_scr, wu_scr, wd_scr, gsem, osem, wsem,
                *, layer):
    e = pl.program_id(0)
    n_exp = pl.num_programs(0)
    b0 = b0_ref[e]
    nb = nb_ref[e]
    n_used = tot_ref[0]

    def in_copy(g):
        slot = g % GATHER_DEPTH
        return pltpu.make_async_copy(_rows(xs_ref, g * ROW_BLOCK, ROW_BLOCK), xbuf.at[slot], gsem.at[slot])

    def out_copy(g, slot):
        return pltpu.make_async_copy(obuf.at[slot], _rows(ys_ref, g * ROW_BLOCK, ROW_BLOCK), osem.at[slot])

    def weight_copies(expert, slot):
        return [pltpu.make_async_copy(src.at[layer, expert], dst.at[slot], wsem.at[slot])
                for src, dst in ((wg_ref, wbuf_g), (wu_ref, wbuf_u), (wd_ref, wbuf_d))]

    @pl.when(e == 0)
    def _():
        for first in range(WEIGHT_DEPTH - 1):
            for cp in weight_copies(first, first):
                cp.start(priority=WEIGHT_DMA_PRIORITY)
        for ahead in range(GATHER_DEPTH - 1):
            @pl.when(ahead < n_used)
            def _():
                in_copy(ahead).start()

    @pl.when(e + (WEIGHT_DEPTH - 1) < n_exp)
    def _():
        for cp in weight_copies(e + (WEIGHT_DEPTH - 1), (e + (WEIGHT_DEPTH - 1)) % WEIGHT_DEPTH):
            cp.start(priority=WEIGHT_DMA_PRIORITY)

    wslot = e % WEIGHT_DEPTH
    for cp in weight_copies(e, wslot):
        cp.wait()

    @pl.when(nb > 0)
    def _():
        wg_scr[...] = wbuf_g[wslot].astype(BF16)
        wu_scr[...] = wbuf_u[wslot].astype(BF16)
        wd_scr[...] = wbuf_d[wslot].astype(BF16)

    def block(k, carry):
        g = b0 + k
        islot = g % GATHER_DEPTH
        oslot = g % 2

        @pl.when(g + (GATHER_DEPTH - 1) < n_used)
        def _():
            in_copy(g + (GATHER_DEPTH - 1)).start()

        in_copy(g).wait()

        @pl.when(g >= 2)
        def _():
            out_copy(g - 2, oslot).wait()

        def chain(half):
            rows = pl.ds(half * SUB_BLOCK, SUB_BLOCK)
            xb = jnp.concatenate(_unpack_bf16_pairs(xbuf[islot, rows, :]), axis=1)
            gate = jnp.dot(xb, wg_scr[...], preferred_element_type=F32)
            up = jnp.dot(xb, wu_scr[...], preferred_element_type=F32)
            act = (gate * jax.nn.sigmoid(gate) * up).astype(BF16)
            y = jnp.dot(act, wd_scr[...], preferred_element_type=F32)
            obuf[oslot, rows, :] = _pack_bf16_pairs(y.astype(BF16).astype(F32))

        both = rows_ref[e] - k * ROW_BLOCK > SUB_BLOCK

        @pl.when(both)
        def _():
            chain(0)
            chain(1)

        @pl.when(jnp.logical_not(both))
        def _():
            chain(0)
            obuf[oslot, pl.ds(SUB_BLOCK, SUB_BLOCK), :] = jnp.zeros((SUB_BLOCK, PACK_W), jnp.int32)

        out_copy(g, oslot).start()
        return carry

    lax.fori_loop(0, nb, block, 0)

    @pl.when(e == n_exp - 1)
    def _():
        @pl.when(n_used >= 2)
        def _():
            out_copy(n_used - 2, n_used % 2).wait()

        out_copy(n_used - 1, (n_used - 1) % 2).wait()


def _ffn_call(layer, blk_start, seg_blocks, seg_rows, n_used, n_blocks, xs, w_gate, w_up, w_down):
    d, f = w_gate.shape[-2:]
    any_spec = pl.BlockSpec(memory_space=pl.ANY)
    return pl.pallas_call(
        functools.partial(_ffn_kernel, layer=layer),
        grid_spec=pltpu.PrefetchScalarGridSpec(
            num_scalar_prefetch=4,
            grid=(N_EXPERTS,),
            in_specs=[any_spec, any_spec, any_spec, any_spec],
            out_specs=any_spec,
            scratch_shapes=[
                pltpu.VMEM((GATHER_DEPTH, ROW_BLOCK, PACK_W), jnp.int32),
                pltpu.VMEM((2, ROW_BLOCK, PACK_W), jnp.int32),
                pltpu.VMEM((WEIGHT_DEPTH, d, f), F32),
                pltpu.VMEM((WEIGHT_DEPTH, d, f), F32),
                pltpu.VMEM((WEIGHT_DEPTH, f, d), F32),
                pltpu.VMEM((d, f), BF16),
                pltpu.VMEM((d, f), BF16),
                pltpu.VMEM((f, d), BF16),
                pltpu.SemaphoreType.DMA((GATHER_DEPTH,)),
                pltpu.SemaphoreType.DMA((2,)),
                pltpu.SemaphoreType.DMA((WEIGHT_DEPTH,)),
            ],
        ),
        out_shape=jax.ShapeDtypeStruct((n_blocks * ROW_BLOCK, PACK_W), jnp.int32),
        compiler_params=pltpu.CompilerParams(
            dimension_semantics=("arbitrary",), vmem_limit_bytes=VMEM_LIMIT),
        name="expert_ffn",
    )(blk_start, seg_blocks, seg_rows, n_used, xs, w_gate, w_up, w_down)


def _combine_kernel(ya0_ref, ya1_ref, wt_ref, x1_ref, ada_ref, g_ref, b_ref, *rest, alpha):
    o_ref = rest[-1]
    o_ref[...] = _combine_tile(ya0_ref, ya1_ref, wt_ref, x1_ref[...], ada_ref[0][5:6], g_ref, b_ref, alpha=alpha)


def _combine_call(ya, wt_tok, x1_flat, ada_l, ln_g, ln_b, alpha, seq, tok0, earlier=None):
    n_tok, d = x1_flat.shape
    ts = SEQ_TILE
    n_tiles = ya.shape[0] // (TOP_K * ts)
    tile0 = tok0 // ts
    per_seq = seq // ts
    tile = lambda i: (tile0 + i, 0)
    in_specs = [
        pl.BlockSpec((ts, PACK_W), lambda i: (i, 0)),
        pl.BlockSpec((ts, PACK_W), lambda i: (n_tiles + i, 0)),
        pl.BlockSpec((ts, TOP_K), tile),
        pl.BlockSpec((ts, d), tile),
        pl.BlockSpec((1, ADA_CHUNKS, d), lambda i: ((tile0 + i) // per_seq, 0, 0)),
        pl.BlockSpec((1, d), lambda i: (0, 0)),
        pl.BlockSpec((1, d), lambda i: (0, 0)),
    ]
    operands = [ya, ya, wt_tok, x1_flat, ada_l, ln_g, ln_b]
    aliases = {}
    if earlier is not None:
        aliases = {len(operands): 0}
        in_specs.append(pl.BlockSpec(memory_space=pl.ANY))
        operands.append(earlier)
    return pl.pallas_call(
        functools.partial(_combine_kernel, alpha=alpha),
        grid=(n_tiles,),
        in_specs=in_specs,
        out_specs=pl.BlockSpec((ts, d), tile),
        out_shape=jax.ShapeDtypeStruct((n_tok, d), F32),
        input_output_aliases=aliases,
        compiler_params=pltpu.CompilerParams(
            dimension_semantics=("arbitrary",), vmem_limit_bytes=VMEM_LIMIT),
        name="combine_ln",
    )(*operands)


def kernel(x, c, w_ada, b_ada, w_in, conv_w, conv_b, pool_w, pool_scale, sgu_ln_g, sgu_ln_b,
           sgu_w, sgu_b, w_out, ln1_g, ln1_b, w_router, b_router, w_gate, w_up, w_down,
           ln2_g, ln2_b):
    bsz, seq, d = x.shape
    depth = w_ada.shape[0]
    n_tok = bsz * seq
    alpha = (2 * depth) ** 0.25
    assert d == D_MODEL and seq % SEQ_TILE == 0 and SEQ_TILE % CHUNK == 0

    ada = _ada_call(c, w_ada, b_ada).reshape(depth, bsz, ADA_CHUNKS, d)

    w_in_b = w_in.astype(BF16)
    w_out_b = w_out.astype(BF16)
    eye_g = jnp.eye(POOL_W // HEAD_DIM, dtype=F32)
    poolw_bd = jnp.einsum('lgcd,gh->lgchd', pool_w, eye_g).reshape(depth, POOL_W, POOL_W).astype(BF16)
    n_pairs = SGU_W // LANES
    sguw_pair = sgu_w.reshape(depth, n_pairs, 2, CHUNK, CHUNK).transpose(0, 1, 3, 2, 4).reshape(
        depth, n_pairs, CHUNK, 2 * CHUNK)
    sgub_pair = jnp.repeat(sgu_b.transpose(0, 2, 1), HEAD_DIM, axis=-1).reshape(
        depth, CHUNK, n_pairs, LANES).transpose(0, 2, 1, 3)
    wr_pad = jnp.pad(w_router, ((0, 0), (0, LANES - N_EXPERTS))).astype(BF16)
    br_pad = jnp.pad(b_router, (0, LANES - N_EXPERTS)).reshape(1, LANES)

    n_steps = (n_tok * TOP_K) // ROW_BLOCK + N_EXPERTS
    experts = jnp.arange(N_EXPERTS, dtype=jnp.int32)
    tri = jnp.triu(jnp.ones((SEQ_TILE, SEQ_TILE), BF16), k=1)

    first_seqs = bsz // FIRST_PART_DIVISOR
    part_seq0 = (0, first_seqs) if first_seqs else (0,)
    part_seqs = (first_seqs, bsz - first_seqs) if first_seqs else (bsz,)
    parts = list(zip(part_seq0, part_seqs))
    prev = None
    for l in range(depth):
        layer_args = (
            x, ada[l], w_in_b[l], w_out_b[l], conv_w[l], conv_b[l].reshape(1, -1), poolw_bd[l],
            pool_scale[l].reshape(1, -1), sgu_ln_g[l].reshape(1, -1), sgu_ln_b[l].reshape(1, -1),
            sguw_pair[l], sgub_pair[l], ln1_g[l].reshape(1, -1), ln1_b[l].reshape(1, -1),
            wr_pad, br_pad, tri, alpha)
        if prev is None:
            outs = _mixer_call(*layer_args)
        else:
            ya_parts, *prev_rest = prev
            outs = None
            for ya_part, (seq0, n_seq) in zip(ya_parts, parts):
                outs = _mixer_call(*layer_args, prev=(ya_part, *prev_rest),
                                   seq0=seq0, n_seq=n_seq, earlier=outs)
        x1, h2p, e_idx, wts, rank, counts = outs
        sizes = counts[:, 0].astype(jnp.int32)
        seg_blocks = (sizes + ROW_BLOCK - 1) // ROW_BLOCK
        blk_end = jnp.cumsum(seg_blocks)
        blk_start = (blk_end - seg_blocks).astype(jnp.int32)
        n_used = blk_end[-1:].astype(jnp.int32)
        seg_row0 = jnp.sum(jnp.where(e_idx[:, :, None] == experts[None, None, :],
                                     (blk_start * ROW_BLOCK)[None, None, :], 0), axis=2)
        dest = (seg_row0 + rank).astype(jnp.int32).reshape(TOP_K, n_tok // SC_ROWS, SC_ROWS)

        xs = _dispatch_call(h2p, dest, n_steps * ROW_BLOCK)
        ys = _ffn_call(l, blk_start, seg_blocks.astype(jnp.int32), sizes, n_used, n_steps,
                       xs, w_gate, w_up, w_down)
        ya_parts = [_collect_call(ys, dest, seq0 * seq, n_seq * seq) for seq0, n_seq in parts]
        wt_tok = wts.T
        prev = (ya_parts, wt_tok, ada[l], ln2_g[l].reshape(1, -1), ln2_b[l].reshape(1, -1))
        x = x1
    ya_parts, wt_tok, ada_last, ln2_g_last, ln2_b_last = prev
    out = None
    for ya_part, (seq0, _) in zip(ya_parts, parts):
        out = _combine_call(ya_part, wt_tok, x.reshape(n_tok, d), ada_last, ln2_g_last, ln2_b_last,
                            alpha, seq, seq0 * seq, earlier=out)
    return out.reshape(bsz, seq, d)
```
